```python
import math
import jax, jax.numpy as jnp
from jax import lax
import numpy as np

D_MODEL = 2048
BATCH = 4
SEQ = 2048
DEPTH = 1

CTX_LEN = 256
GRID_W = 64
EPS = 1e-6
S5_WIDTH = D_MODEL // 2
S5_GROUP = 16
S5_GROUPS = S5_WIDTH // S5_GROUP
S5_STATE = 64
MLA_HEADS = 8
QK_NOPE = 128
QK_ROPE = 64
V_DIM = 128
Q_RANK = 512
KV_RANK = 256
ROPE_BASE = 10000.0
Q_BLOCK = 128
ATTN_SCALE = (QK_NOPE + QK_ROPE) ** -0.5
N_BRANCH = 2
D_FF = -(-8 * D_MODEL // (3 * 256)) * 256
IN_COLS = S5_WIDTH + Q_RANK + KV_RANK + QK_ROPE + N_BRANCH * D_MODEL

kernel_name = 'hybrid_s5_mla_dit_block'


def rmsnorm(x, g):
    xf = x.astype(jnp.float32)
    y = xf * lax.rsqrt(jnp.mean(xf * xf, axis=-1, keepdims=True) + EPS)
    return (y * g.astype(jnp.float32)).astype(x.dtype)


def ada(cvec, w_mod, b_mod):
    m = jax.nn.silu(cvec) @ w_mod + b_mod
    return m.reshape(m.shape[:-1] + (6, D_MODEL))


def rope2d_tables(n_tokens):
    rows = n_tokens // GRID_W
    row = jnp.repeat(jnp.arange(rows, dtype=jnp.float32), GRID_W)
    col = jnp.tile(jnp.arange(GRID_W, dtype=jnp.float32), rows)
    n_freq = QK_ROPE // 4
    inv = ROPE_BASE ** (-jnp.arange(n_freq, dtype=jnp.float32) / n_freq)
    ang = jnp.stack([row[:, None] * inv, col[:, None] * inv], axis=1)
    return jnp.cos(ang), jnp.sin(ang)


def apply_rope2d(x, cos, sin):
    xs = x.reshape(x.shape[:-1] + (2, 2, QK_ROPE // 4))
    x1, x2 = xs[..., 0, :], xs[..., 1, :]
    c = cos[None, :, None].astype(x.dtype)
    s = sin[None, :, None].astype(x.dtype)
    out = jnp.stack([x1 * c - x2 * s, x2 * c + x1 * s], axis=-2)
    return out.reshape(x.shape)


def split_in(h):
    o = S5_WIDTH
    u = h[..., :o]
    cq = h[..., o:o + Q_RANK]
    o += Q_RANK
    ckv = h[..., o:o + KV_RANK]
    o += KV_RANK
    kr = h[..., o:o + QK_ROPE]
    o += QK_ROPE
    return u, cq, ckv, kr, h[..., o:]


def s5_discretize(a_re, a_im, log_dt, b_re, b_im):
    f32 = jnp.float32
    dt = jnp.exp(log_dt.astype(f32))[:, None]
    lr, li = a_re.astype(f32), a_im.astype(f32)
    mag = jnp.exp(lr * dt)
    ab_re, ab_im = mag * jnp.cos(li * dt), mag * jnp.sin(li * dt)
    den = lr * lr + li * li
    nr, ni = ab_re - 1.0, ab_im
    co_re = (nr * lr + ni * li) / den
    co_im = (ni * lr - nr * li) / den
    br, bi = b_re.astype(f32), b_im.astype(f32)
    bb_re = co_re[..., None] * br - co_im[..., None] * bi
    bb_im = co_re[..., None] * bi + co_im[..., None] * br
    return ab_re, ab_im, bb_re, bb_im


def _ssm_combine(e1, e2):
    a1r, a1i, b1r, b1i = e1
    a2r, a2i, b2r, b2i = e2
    return (a2r * a1r - a2i * a1i, a2r * a1i + a2i * a1r,
            a2r * b1r - a2i * b1i + b2r, a2r * b1i + a2i * b1r + b2i)


def s5_scan(u, disc, h0, reverse):
    ab_re, ab_im, bb_re, bb_im = disc
    bu_re = jnp.einsum('blgp,gnp->blgn', u, bb_re)
    bu_im = jnp.einsum('blgp,gnp->blgn', u, bb_im)
    if h0 is not None:
        idx = -1 if reverse else 0
        h_re, h_im = h0
        bu_re = bu_re.at[:, idx].add(ab_re * h_re - ab_im * h_im)
        bu_im = bu_im.at[:, idx].add(ab_re * h_im + ab_im * h_re)
    a_re = jnp.broadcast_to(ab_re, bu_re.shape)
    a_im = jnp.broadcast_to(ab_im, bu_re.shape)
    _, _, h_re, h_im = lax.associative_scan(_ssm_combine, (a_re, a_im, bu_re, bu_im),
                                            reverse=reverse, axis=1)
    return h_re, h_im


def s5_readout(h, c_re, c_im):
    h_re, h_im = h
    return (jnp.einsum('blgn,gpn->blgp', h_re, c_re)
            - jnp.einsum('blgn,gpn->blgp', h_im, c_im))


def s5_mixer(u_ctx, u_lat, p, need_ctx_out):
    f32 = jnp.float32
    B, L = u_lat.shape[:2]
    Lc = u_ctx.shape[1]
    uc = u_ctx.astype(f32).reshape(B, Lc, S5_GROUPS, S5_GROUP)
    ul = u_lat.astype(f32).reshape(B, L, S5_GROUPS, S5_GROUP)
    d_skip = p['s5_d'].astype(f32)
    y_lat = d_skip * ul
    y_ctx = d_skip * uc if need_ctx_out else None
    for d, rev in enumerate((False, True)):
        disc = s5_discretize(p['s5_a_re'][d], p['s5_a_im'][d], p['s5_log_dt'][d],
                             p['s5_b_re'][d], p['s5_b_im'][d])
        c_re, c_im = p['s5_c_re'][d].astype(f32), p['s5_c_im'][d].astype(f32)
        hc = s5_scan(uc, disc, None, rev)
        last = 0 if rev else -1
        hl = s5_scan(ul, disc, (hc[0][:, last], hc[1][:, last]), rev)
        y_lat = y_lat + s5_readout(hl, c_re, c_im)
        if need_ctx_out:
            y_ctx = y_ctx + s5_readout(hc, c_re, c_im)
    y_lat = y_lat.reshape(B, L, S5_WIDTH).astype(u_lat.dtype)
    if need_ctx_out:
        y_ctx = y_ctx.reshape(B, Lc, S5_WIDTH).astype(u_ctx.dtype)
    return y_lat, y_ctx


def mla_qkv(cq, ckv, kr, p, rope):
    B, L = cq.shape[:2]
    q = (rmsnorm(cq, p['q_norm']) @ p['w_uq']).reshape(B, L, MLA_HEADS, QK_NOPE + QK_ROPE)
    kv = (rmsnorm(ckv, p['kv_norm']) @ p['w_ukv']).reshape(B, L, MLA_HEADS, QK_NOPE + V_DIM)
    q_nope, q_rope = q[..., :QK_NOPE], q[..., QK_NOPE:]
    k_nope, v = kv[..., :QK_NOPE], kv[..., QK_NOPE:]
    k_rope = kr[:, :, None, :]
    if rope is not None:
        cos, sin = rope
        q_rope = apply_rope2d(q_rope, cos, sin)
        k_rope = apply_rope2d(k_rope, cos, sin)
    q = jnp.concatenate([q_nope, q_rope], axis=-1)
    k = jnp.concatenate([k_nope, jnp.broadcast_to(k_rope, (B, L, MLA_HEADS, QK_ROPE))], axis=-1)
    return q, k, v


def attend(q, k, v):
    s = jnp.einsum('bqhd,bkhd->bhqk', q, k, preferred_element_type=jnp.float32) * ATTN_SCALE
    pr = jax.nn.softmax(s, axis=-1).astype(v.dtype)
    return jnp.einsum('bhqk,bkhd->bqhd', pr, v)


def blocked_attend(q, k, v):
    B, L, H, dk = q.shape
    nb = L // Q_BLOCK
    qb = q.reshape(B, nb, Q_BLOCK, H, dk).transpose(1, 0, 2, 3, 4)
    ob = lax.map(lambda qi: attend(qi, k, v), qb)
    return ob.transpose(1, 0, 2, 3, 4).reshape(B, L, H, v.shape[-1])


def merge_branches(y5, o_mla, gate_cols, p):
    z = jax.nn.gelu(y5)
    a, b = jnp.split(z @ p['w_glu'], 2, axis=-1)
    br_s5 = a * jax.nn.sigmoid(b)
    br_mla = o_mla.reshape(o_mla.shape[:2] + (MLA_HEADS * V_DIM,)) @ p['w_mla_o']
    g_s5, g_mla = jnp.split(jax.nn.sigmoid(gate_cols), 2, axis=-1)
    return (g_s5 * br_s5 + g_mla * br_mla) @ p['w_out']


def swiglu(h, p):
    a, b = jnp.split(h @ p['w_ffn_in'], 2, axis=-1)
    return (jax.nn.silu(a) * b) @ p['w_ffn_out']


def layer(x, xc, m_lat, m_ctx, cos, sin, p, need_ctx_out):
    sh1, sc1, g1, sh2, sc2, g2 = (m_lat[..., i, :] for i in range(6))
    csh1, csc1, cg1, csh2, csc2, cg2 = (m_ctx[..., i, :] for i in range(6))
    hl = (rmsnorm(x, p['norm1']) * (1.0 + sc1) + sh1) @ p['w_in']
    hc = (rmsnorm(xc, p['norm1']) * (1.0 + csc1) + csh1) @ p['w_in']
    ul, cql, ckvl, krl, gl = split_in(hl)
    uc, cqc, ckvc, krc, gc = split_in(hc)
    y5_lat, y5_ctx = s5_mixer(uc, ul, p, need_ctx_out)
    qc, kc, vc = mla_qkv(cqc, ckvc, krc, p, None)
    ql, kl, vl = mla_qkv(cql, ckvl, krl, p, (cos, sin))
    k_all = jnp.concatenate([kl, kc], axis=1)
    v_all = jnp.concatenate([vl, vc], axis=1)
    ol = blocked_attend(ql, k_all, v_all)
    x = x + g1 * merge_branches(y5_lat, ol, gl, p)
    x = x + g2 * swiglu(rmsnorm(x, p['norm2']) * (1.0 + sc2) + sh2, p)
    if need_ctx_out:
        oc = attend(qc, kc, vc)
        xc = xc + cg1 * merge_branches(y5_ctx, oc, gc, p)
        xc = xc + cg2 * swiglu(rmsnorm(xc, p['norm2']) * (1.0 + csc2) + csh2, p)
    return x, xc


def setup_inputs(seed: int = 0) -> dict:
    key = jax.random.key(seed)
    ks = jax.random.split(key, 32)
    f32 = jnp.float32

    def nrm(k, shape, scale):
        return jax.random.normal(k, shape, f32) * scale

    G, N, P = S5_GROUPS, S5_STATE, S5_GROUP
    n_idx = jnp.arange(N, dtype=f32)
    return {
        'x': nrm(ks[0], (BATCH, SEQ, D_MODEL), 1.0),
        'c': nrm(ks[1], (BATCH, D_MODEL), 1.0),
        'ctx': nrm(ks[2], (BATCH, CTX_LEN, D_MODEL), 1.0),
        'c_ctx': nrm(ks[3], (D_MODEL,), 1.0),
        'w_mod': nrm(ks[4], (DEPTH, D_MODEL, 6 * D_MODEL), 0.3 * D_MODEL ** -0.5),
        'b_mod': nrm(ks[5], (DEPTH, 6 * D_MODEL), 0.02),
        'norm1': 1.0 + nrm(ks[6], (DEPTH, D_MODEL), 0.01),
        'norm2': 1.0 + nrm(ks[7], (DEPTH, D_MODEL), 0.01),
        'w_in': nrm(ks[8], (DEPTH, D_MODEL, IN_COLS), D_MODEL ** -0.5),
        's5_a_re': -0.5 + nrm(ks[9], (DEPTH, 2, G, N), 0.01),
        's5_a_im': math.pi * n_idx + nrm(ks[10], (DEPTH, 2, G, N), 0.01),
        's5_log_dt': jax.random.uniform(ks[11], (DEPTH, 2, G), f32, math.log(1e-3), math.log(1e-1)),
        's5_b_re': nrm(ks[12], (DEPTH, 2, G, N, P), (2 * P) ** -0.5),
        's5_b_im': nrm(ks[13], (DEPTH, 2, G, N, P), (2 * P) ** -0.5),
        's5_c_re': nrm(ks[14], (DEPTH, 2, G, P, N), N ** -0.5),
        's5_c_im': nrm(ks[15], (DEPTH, 2, G, P, N), N ** -0.5),
        's5_d': nrm(ks[16], (DEPTH, G, P), 0.5),
        'w_glu': nrm(ks[17], (DEPTH, S5_WIDTH, 2 * D_MODEL), S5_WIDTH ** -0.5),
        'q_norm': 1.0 + nrm(ks[18], (DEPTH, Q_RANK), 0.01),
        'kv_norm': 1.0 + nrm(ks[19], (DEPTH, KV_RANK), 0.01),
        'w_uq': nrm(ks[20], (DEPTH, Q_RANK, MLA_HEADS * (QK_NOPE + QK_ROPE)), Q_RANK ** -0.5),
        'w_ukv': nrm(ks[21], (DEPTH, KV_RANK, MLA_HEADS * (QK_NOPE + V_DIM)), KV_RANK ** -0.5),
        'w_mla_o': nrm(ks[22], (DEPTH, MLA_HEADS * V_DIM, D_MODEL), (MLA_HEADS * V_DIM) ** -0.5),
        'w_out': nrm(ks[23], (DEPTH, D_MODEL, D_MODEL), D_MODEL ** -0.5),
        'w_ffn_in': nrm(ks[24], (DEPTH, D_MODEL, 2 * D_FF), D_MODEL ** -0.5),
        'w_ffn_out': nrm(ks[25], (DEPTH, D_FF, D_MODEL), D_FF ** -0.5),
        'norm_f': 1.0 + nrm(ks[26], (D_MODEL,), 0.01),
    }


def reference(x, c, ctx, c_ctx, w_mod, b_mod, norm1, norm2, w_in, s5_a_re, s5_a_im, s5_log_dt,
              s5_b_re, s5_b_im, s5_c_re, s5_c_im, s5_d, w_glu, q_norm, kv_norm, w_uq, w_ukv,
              w_mla_o, w_out, w_ffn_in, w_ffn_out, norm_f):
    cos, sin = rope2d_tables(x.shape[1])
    xc = ctx
    for l in range(DEPTH):
        p = {
            'norm1': norm1[l], 'norm2': norm2[l], 'w_in': w_in[l],
            's5_a_re': s5_a_re[l], 's5_a_im': s5_a_im[l], 's5_log_dt': s5_log_dt[l],
            's5_b_re': s5_b_re[l], 's5_b_im': s5_b_im[l], 's5_c_re': s5_c_re[l], 's5_c_im': s5_c_im[l],
            's5_d': s5_d[l], 'w_glu': w_glu[l], 'q_norm': q_norm[l], 'kv_norm': kv_norm[l],
            'w_uq': w_uq[l], 'w_ukv': w_ukv[l], 'w_mla_o': w_mla_o[l], 'w_out': w_out[l],
            'w_ffn_in': w_ffn_in[l], 'w_ffn_out': w_ffn_out[l],
        }
        m_lat = ada(c, w_mod[l], b_mod[l])[:, None]
        m_ctx = ada(c_ctx, w_mod[l], b_mod[l])
        x, xc = layer(x, xc, m_lat, m_ctx, cos, sin, p, l < DEPTH - 1)
    return rmsnorm(x, norm_f)
```

```python
import functools
import math

import jax
import jax.numpy as jnp
from jax import lax
from jax.experimental import pallas as pl
from jax.experimental.pallas import tpu as pltpu

F32 = jnp.float32
BF16 = jnp.bfloat16

D_MODEL = 2048
GRID_W = 64
EPS = 1e-6
S5_WIDTH = D_MODEL // 2
S5_GROUP = 16
S5_GROUPS = S5_WIDTH // S5_GROUP
S5_STATE = 64
S5_CHUNK = 16
S5_PAIRS = S5_GROUPS // 2
MLA_HEADS = 8
QK_NOPE = 128
QK_ROPE = 64
V_DIM = 128
Q_RANK = 512
KV_RANK = 256
ROPE_BASE = 10000.0
ATTN_SCALE = (QK_NOPE + QK_ROPE) ** -0.5
D_FF = -(-8 * D_MODEL // (3 * 256)) * 256

VMEM_LIMIT_BYTES = 56 * 1024 * 1024


def _cparams(n_axes):
    return pltpu.CompilerParams(dimension_semantics=("arbitrary",) * n_axes,
                                vmem_limit_bytes=VMEM_LIMIT_BYTES)


def _rms(x, g):
    return x * lax.rsqrt(jnp.mean(x * x, axis=-1, keepdims=True) + EPS) * g


def _mod_kernel(cv_ref, w_ref, b_ref, o_ref):
    s = jax.nn.silu(cv_ref[...]).astype(BF16)
    o_ref[...] = jnp.dot(s, w_ref[...].astype(BF16), preferred_element_type=F32) + b_ref[...]


def _modulation(cv, w_mod, b_mod):
    n = w_mod.shape[1]
    tn = 1536
    return pl.pallas_call(
        _mod_kernel,
        grid=(n // tn,),
        in_specs=[pl.BlockSpec((8, D_MODEL), lambda j: (0, 0)),
                  pl.BlockSpec((D_MODEL, tn), lambda j: (0, j)),
                  pl.BlockSpec((1, tn), lambda j: (0, j))],
        out_specs=pl.BlockSpec((8, tn), lambda j: (0, j)),
        out_shape=jax.ShapeDtypeStruct((8, n), F32),
        compiler_params=_cparams(1),
        name="mod",
    )(cv, w_mod, b_mod.reshape(1, n))


def _norm_mod_kernel(x_ref, g_ref, sc_ref, sh_ref, o_ref):
    y = _rms(x_ref[...], g_ref[...])
    o_ref[...] = (y * (1.0 + sc_ref[0]) + sh_ref[0]).astype(o_ref.dtype)


def _norm_mod(x2d, gain, scale, shift, rows_per_batch, tm, out_dtype=BF16):
    t = x2d.shape[0]
    tpb = rows_per_batch // tm
    return pl.pallas_call(
        _norm_mod_kernel,
        grid=(t // tm,),
        in_specs=[pl.BlockSpec((tm, D_MODEL), lambda i: (i, 0)),
                  pl.BlockSpec((1, D_MODEL), lambda i: (0, 0)),
                  pl.BlockSpec((1, 1, D_MODEL), lambda i: (i // tpb, 0, 0)),
                  pl.BlockSpec((1, 1, D_MODEL), lambda i: (i // tpb, 0, 0))],
        out_specs=pl.BlockSpec((tm, D_MODEL), lambda i: (i, 0)),
        out_shape=jax.ShapeDtypeStruct((t, D_MODEL), out_dtype),
        compiler_params=_cparams(1),
        name="norm_mod",
    )(x2d, gain.reshape(1, D_MODEL), scale, shift)


def _fused_mm(a, weights, epilogue, extras, outs, *, tm, tn, nj, name):
    t, k = a.shape
    ni = t // tm
    nw, ne, no = len(weights), len(extras), len(outs)
    need_cast = [w.dtype != BF16 for w, _ in weights]

    def kernel(*refs):
        a_ref = refs[0]
        w_refs = refs[1:1 + nw]
        e_refs = refs[1 + nw:1 + nw + ne]
        o_refs = refs[1 + nw + ne:1 + nw + ne + no]
        s_refs = list(refs[1 + nw + ne + no:])
        wb = []
        for idx in range(nw):
            if need_cast[idx]:
                s_ref = s_refs.pop(0)

                @pl.when(pl.program_id(1) == 0)
                def _(w_ref=w_refs[idx], s_ref=s_ref):
                    s_ref[...] = w_ref[...].astype(BF16)

                wb.append(s_ref)
            else:
                wb.append(w_refs[idx])
        av = a_ref[...]
        accs = [jnp.dot(av, w[...], preferred_element_type=F32) for w in wb]
        epilogue(accs, e_refs, o_refs)

    in_specs = [pl.BlockSpec((tm, k), lambda j, i: (i, 0))]
    for _, off in weights:
        in_specs.append(pl.BlockSpec((k, tn), functools.partial(lambda j, i, off: (0, off + j), off=off)))
    in_specs += [spec for _, spec in extras]
    scratch = [pltpu.VMEM((k, tn), BF16) for c in need_cast if c]
    res = pl.pallas_call(
        kernel,
        grid=(nj, ni),
        in_specs=in_specs,
        out_specs=[spec for _, spec in outs],
        out_shape=[sds for sds, _ in outs],
        scratch_shapes=scratch,
        compiler_params=_cparams(2),
        name=name,
    )(a, *[w for w, _ in weights], *[e for e, _ in extras])
    return res


def _proj_cast(xn, w, off, n, tm, tn, name, act=None):
    def epi(accs, e_refs, o_refs):
        v = accs[0]
        if act is not None:
            v = act(v)
        o_refs[0][...] = v.astype(BF16)

    t = xn.shape[0]
    return _fused_mm(xn, [(w, off)], epi, [],
                     [(jax.ShapeDtypeStruct((t, n), BF16), pl.BlockSpec((tm, tn), lambda j, i: (i, j)))],
                     tm=tm, tn=tn, nj=n // tn, name=name)[0]


def _q_path(xn, w_in, q_norm, wq2, cos_q, sin_q, tm, seq):
    nr = MLA_HEADS * QK_ROPE
    nn = MLA_HEADS * QK_NOPE

    def epi(accs, e_refs, o_refs):
        qn_ref, w2_ref, cos_ref, sin_ref = e_refs
        cq = _rms(accs[0], qn_ref[...]).astype(BF16)
        q = jnp.dot(cq, w2_ref[...], preferred_element_type=F32)
        rope = q[:, nn:nn + nr] * cos_ref[...] + q[:, nn + nr:] * sin_ref[...]
        o_refs[0][:, :nn] = (q[:, :nn] * ATTN_SCALE).astype(BF16)
        o_refs[0][:, nn:] = (rope * ATTN_SCALE).astype(BF16)

    t = xn.shape[0]
    pos_tiles = seq // tm
    extras = [(q_norm.reshape(1, Q_RANK), pl.BlockSpec((1, Q_RANK), lambda j, i: (0, 0))),
              (wq2, pl.BlockSpec(wq2.shape, lambda j, i: (0, 0))),
              (cos_q, pl.BlockSpec((tm, nr), lambda j, i: (i % pos_tiles, 0))),
              (sin_q, pl.BlockSpec((tm, nr), lambda j, i: (i % pos_tiles, 0)))]
    outs = [(jax.ShapeDtypeStruct((t, nn + nr), BF16), pl.BlockSpec((tm, nn + nr), lambda j, i: (i, 0)))]
    return _fused_mm(xn, [(w_in, S5_WIDTH // Q_RANK)], epi, extras, outs,
                     tm=tm, tn=Q_RANK, nj=1, name="q_path")[0]


def _kv_path(xn, w_kv1, kv_norm, w_ukv_bf, cos_k, sin_k, tm, seq):
    nkv = w_ukv_bf.shape[1]
    rope = cos_k is not None

    def epi(accs, e_refs, o_refs):
        acc = accs[0]
        ckv = _rms(acc[:, :KV_RANK], e_refs[0][...]).astype(BF16)
        o_refs[0][...] = jnp.dot(ckv, e_refs[1][...], preferred_element_type=F32).astype(BF16)
        kr = acc[:, KV_RANK:KV_RANK + QK_ROPE]
        if rope:
            kr = kr * e_refs[2][...] + acc[:, KV_RANK + QK_ROPE:] * e_refs[3][...]
        o_refs[1][...] = kr.astype(BF16)

    t = xn.shape[0]
    pos_tiles = seq // tm
    extras = [(kv_norm.reshape(1, KV_RANK), pl.BlockSpec((1, KV_RANK), lambda j, i: (0, 0))),
              (w_ukv_bf, pl.BlockSpec(w_ukv_bf.shape, lambda j, i: (0, 0)))]
    if rope:
        extras += [(cos_k, pl.BlockSpec((tm, QK_ROPE), lambda j, i: (i % pos_tiles, 0))),
                   (sin_k, pl.BlockSpec((tm, QK_ROPE), lambda j, i: (i % pos_tiles, 0)))]
    outs = [(jax.ShapeDtypeStruct((t, nkv), BF16), pl.BlockSpec((tm, nkv), lambda j, i: (i, 0))),
            (jax.ShapeDtypeStruct((t, QK_ROPE), BF16), pl.BlockSpec((tm, QK_ROPE), lambda j, i: (i, 0)))]
    return _fused_mm(xn, [(w_kv1, 0)], epi, extras, outs,
                     tm=tm, tn=w_kv1.shape[1], nj=1, name="kv_path")


def _attn_kernel(qn_ref, qr_ref, kvl_ref, kvc_ref, krl_ref, krc_ref, o_ref, k_scr, v_scr, *, seq, ctx):
    dk = QK_NOPE + QK_ROPE

    @pl.when(pl.program_id(2) == 0)
    def _():
        for h in range(2):
            base = h * (QK_NOPE + V_DIM)
            k_scr[h, :seq, :QK_NOPE] = kvl_ref[:, base:base + QK_NOPE]
            k_scr[h, seq:, :QK_NOPE] = kvc_ref[:, base:base + QK_NOPE]
            k_scr[h, :seq, QK_NOPE:dk] = krl_ref[...]
            k_scr[h, seq:, QK_NOPE:dk] = krc_ref[...]
            v_scr[h, :seq, :] = kvl_ref[:, base + QK_NOPE:base + QK_NOPE + V_DIM]
            v_scr[h, seq:, :] = kvc_ref[:, base + QK_NOPE:base + QK_NOPE + V_DIM]

    for h in range(2):
        q = jnp.concatenate([qn_ref[:, h * QK_NOPE:(h + 1) * QK_NOPE],
                             qr_ref[:, h * QK_ROPE:(h + 1) * QK_ROPE]], axis=1)
        s = lax.dot_general(q, k_scr[h], (((1,), (1,)), ((), ())), preferred_element_type=F32)
        m = jnp.max(s, axis=-1, keepdims=True)
        p = jnp.exp(s - m)
        l = jnp.sum(p, axis=-1, keepdims=True)
        o = jnp.dot(p.astype(BF16), v_scr[h], preferred_element_type=F32)
        o_ref[:, h * V_DIM:(h + 1) * V_DIM] = (o / l).astype(o_ref.dtype)


def _attention(q, kv_lat, kv_ctx, kr_lat, kr_ctx, batch, seq, ctx, tq):
    nq = seq // tq
    nn_blocks = MLA_HEADS * QK_NOPE // (2 * QK_NOPE)
    dk = QK_NOPE + QK_ROPE
    hw = 2 * (QK_NOPE + V_DIM)
    return pl.pallas_call(
        functools.partial(_attn_kernel, seq=seq, ctx=ctx),
        grid=(batch, MLA_HEADS // 2, nq),
        in_specs=[pl.BlockSpec((tq, 2 * QK_NOPE), lambda b, hp, qi: (b * nq + qi, hp)),
                  pl.BlockSpec((tq, 2 * QK_ROPE), lambda b, hp, qi: (b * nq + qi, 2 * nn_blocks + hp)),
                  pl.BlockSpec((seq, hw), lambda b, hp, qi: (b, hp)),
                  pl.BlockSpec((ctx, hw), lambda b, hp, qi: (b, hp)),
                  pl.BlockSpec((seq, QK_ROPE), lambda b, hp, qi: (b, 0)),
                  pl.BlockSpec((ctx, QK_ROPE), lambda b, hp, qi: (b, 0))],
        out_specs=pl.BlockSpec((tq, 2 * V_DIM), lambda b, hp, qi: (b * nq + qi, hp)),
        out_shape=jax.ShapeDtypeStruct((batch * seq, MLA_HEADS * V_DIM), BF16),
        scratch_shapes=[pltpu.VMEM((2, seq + ctx, dk), BF16), pltpu.VMEM((2, seq + ctx, V_DIM), BF16)],
        compiler_params=_cparams(3),
        name="attention",
    )(q, q, kv_lat, kv_ctx, kr_lat, kr_ctx)


def _s5_weights(a_re, a_im, log_dt, b_re, b_im, c_re, c_im, reverse):
    tc = S5_CHUNK
    dt = jnp.exp(log_dt.astype(F32))[:, None]
    lr, li = a_re.astype(F32), a_im.astype(F32)
    mag = jnp.exp(lr * dt)
    ab_re, ab_im = mag * jnp.cos(li * dt), mag * jnp.sin(li * dt)
    den = lr * lr + li * li
    nr, ni = ab_re - 1.0, ab_im
    co_re = (nr * lr + ni * li) / den
    co_im = (ni * lr - nr * li) / den
    br, bi = b_re.astype(F32), b_im.astype(F32)
    bb_re = co_re[..., None] * br - co_im[..., None] * bi
    bb_im = co_re[..., None] * bi + co_im[..., None] * br
    pw_re, pw_im = [jnp.ones_like(ab_re)], [jnp.zeros_like(ab_re)]
    for _ in range(tc):
        pr, pi = pw_re[-1], pw_im[-1]
        pw_re.append(pr * ab_re - pi * ab_im)
        pw_im.append(pr * ab_im + pi * ab_re)
    pw_re, pw_im = jnp.stack(pw_re), jnp.stack(pw_im)
    cr, ci = c_re.astype(F32), c_im.astype(F32)
    ca_re = cr[None] * pw_re[:, :, None, :] - ci[None] * pw_im[:, :, None, :]
    ca_im = cr[None] * pw_im[:, :, None, :] + ci[None] * pw_re[:, :, None, :]
    hp = lax.Precision.HIGHEST
    kern = (jnp.einsum('tgpn,gnq->tgpq', ca_re[:tc], bb_re, precision=hp)
            - jnp.einsum('tgpn,gnq->tgpq', ca_im[:tc], bb_im, precision=hp))
    r = jnp.arange(tc)
    lag = (r[:, None] - r[None, :]) if reverse else (r[None, :] - r[:, None])
    toep = jnp.where((lag >= 0)[:, :, None, None, None], kern[jnp.clip(lag, 0, tc - 1)], 0.0)
    toep = toep.transpose(2, 0, 4, 1, 3).reshape(S5_GROUPS, tc * S5_GROUP, tc * S5_GROUP)
    kst = r if reverse else tc - 1 - r
    wst_re = pw_re[kst][:, :, :, None] * bb_re[None] - pw_im[kst][:, :, :, None] * bb_im[None]
    wst_im = pw_re[kst][:, :, :, None] * bb_im[None] + pw_im[kst][:, :, :, None] * bb_re[None]
    wst_re = wst_re.transpose(1, 0, 3, 2).reshape(S5_GROUPS, tc * S5_GROUP, S5_STATE)
    wst_im = wst_im.transpose(1, 0, 3, 2).reshape(S5_GROUPS, tc * S5_GROUP, S5_STATE)
    kro = tc - r if reverse else r + 1
    wo_re = ca_re[kro].transpose(1, 3, 0, 2).reshape(S5_GROUPS, S5_STATE, tc * S5_GROUP)
    wo_im = (-ca_im[kro]).transpose(1, 3, 0, 2).reshape(S5_GROUPS, S5_STATE, tc * S5_GROUP)

    def pair_rows(w):
        g, k, n = w.shape
        w = w.reshape(g // 2, 2, k, n)
        z = jnp.zeros_like(w[:, 0])
        return jnp.concatenate([jnp.concatenate([w[:, 0], z], axis=2),
                                jnp.concatenate([z, w[:, 1]], axis=2)], axis=1)

    a_t_re = pw_re[tc].reshape(S5_PAIRS, 1, 2 * S5_STATE)
    a_t_im = pw_im[tc].reshape(S5_PAIRS, 1, 2 * S5_STATE)
    return (toep.astype(BF16), pair_rows(wst_re).astype(BF16), pair_rows(wst_im).astype(BF16),
            pair_rows(wo_re).astype(BF16), pair_rows(wo_im).astype(BF16), a_t_re, a_t_im)


def _s5_kernel(up_ref, dsk_ref, *refs, batch, n_ctx, n_lat):
    wrefs, (o_ref, sre, sim, hre, him) = refs[:14], refs[14:]
    w = S5_CHUNK * S5_GROUP
    rows_ctx = n_ctx * batch
    up = up_ref[0]
    ul = up[rows_ctx:]
    y = ul.astype(F32) * dsk_ref[0]
    for d in range(2):
        toep_ref, wsr, wsi, wor, woi, atr, ati = wrefs[7 * d:7 * d + 7]
        sre[...] = jnp.dot(up, wsr[0], preferred_element_type=F32)
        sim[...] = jnp.dot(up, wsi[0], preferred_element_type=F32)
        ar, ai = atr[0], ati[0]
        ctx_chunks = list(range(n_ctx))
        lat_chunks = list(range(n_ctx, n_ctx + n_lat))
        order = (ctx_chunks[::-1] + lat_chunks[::-1]) if d else (ctx_chunks + lat_chunks)
        h_re = jnp.zeros((batch, 2 * S5_STATE), F32)
        h_im = jnp.zeros((batch, 2 * S5_STATE), F32)
        for c in order:
            rows = slice(c * batch, (c + 1) * batch)
            hre[rows, :] = h_re
            him[rows, :] = h_im
            n_re = ar * h_re - ai * h_im + sre[rows, :]
            n_im = ar * h_im + ai * h_re + sim[rows, :]
            h_re, h_im = n_re, n_im
        y = y + jnp.dot(hre[rows_ctx:, :].astype(BF16), wor[0], preferred_element_type=F32)
        y = y + jnp.dot(him[rows_ctx:, :].astype(BF16), woi[0], preferred_element_type=F32)
        y = y + jnp.concatenate(
            [jnp.dot(ul[:, :w], toep_ref[0], preferred_element_type=F32),
             jnp.dot(ul[:, w:], toep_ref[1], preferred_element_type=F32)], axis=1)
    o_ref[0] = y.astype(o_ref.dtype)


def _s5_mixer(u_lat, u_ctx, params, batch, seq, ctx):
    tc = S5_CHUNK
    n_lat, n_ctx = seq // tc, ctx // tc

    def to_pairs(u, nchunk):
        v = u.reshape(batch, nchunk, tc, S5_PAIRS, 2, S5_GROUP).transpose(3, 1, 0, 4, 2, 5)
        return v.reshape(S5_PAIRS, nchunk * batch, 2 * tc * S5_GROUP)

    up = jnp.concatenate([to_pairs(u_ctx, n_ctx), to_pairs(u_lat, n_lat)], axis=1)
    rows_all, rows_lat = (n_ctx + n_lat) * batch, n_lat * batch
    wide = 2 * tc * S5_GROUP
    dsk = jnp.broadcast_to(params['s5_d'].astype(F32).reshape(S5_PAIRS, 2, 1, S5_GROUP),
                           (S5_PAIRS, 2, tc, S5_GROUP)).reshape(S5_PAIRS, 1, wide)
    weights = []
    specs = []
    for d, rev in enumerate((False, True)):
        wd = _s5_weights(params['s5_a_re'][d], params['s5_a_im'][d], params['s5_log_dt'][d],
                         params['s5_b_re'][d], params['s5_b_im'][d], params['s5_c_re'][d],
                         params['s5_c_im'][d], rev)
        weights += list(wd)
        specs += [pl.BlockSpec((2, wide // 2, wide // 2), lambda k: (k, 0, 0)),
                  pl.BlockSpec((1, wide, 2 * S5_STATE), lambda k: (k, 0, 0)),
                  pl.BlockSpec((1, wide, 2 * S5_STATE), lambda k: (k, 0, 0)),
                  pl.BlockSpec((1, 2 * S5_STATE, wide), lambda k: (k, 0, 0)),
                  pl.BlockSpec((1, 2 * S5_STATE, wide), lambda k: (k, 0, 0)),
                  pl.BlockSpec((1, 1, 2 * S5_STATE), lambda k: (k, 0, 0)),
                  pl.BlockSpec((1, 1, 2 * S5_STATE), lambda k: (k, 0, 0))]
    yp = pl.pallas_call(
        functools.partial(_s5_kernel, batch=batch, n_ctx=n_ctx, n_lat=n_lat),
        grid=(S5_PAIRS,),
        in_specs=[pl.BlockSpec((1, rows_all, wide), lambda k: (k, 0, 0)),
                  pl.BlockSpec((1, 1, wide), lambda k: (k, 0, 0))] + specs,
        out_specs=pl.BlockSpec((1, rows_lat, wide), lambda k: (k, 0, 0)),
        out_shape=jax.ShapeDtypeStruct((S5_PAIRS, rows_lat, wide), BF16),
        scratch_shapes=[pltpu.VMEM((rows_all, 2 * S5_STATE), F32) for _ in range(4)],
        compiler_params=_cparams(1),
        name="s5",
    )(up, dsk, *weights)
    y = yp.reshape(S5_PAIRS, n_lat, batch, 2, tc, S5_GROUP).transpose(2, 1, 4, 0, 3, 5)
    return y.reshape(batch * seq, S5_WIDTH)


def _merge_kernel(y5_ref, o_ref_in, gs_ref, gm_ref, wa_ref, wb_ref, wm_ref, out_ref, wa_s, wb_s, wm_s):
    @pl.when(pl.program_id(1) == 0)
    def _():
        wa_s[...] = wa_ref[...].astype(BF16)
        wb_s[...] = wb_ref[...].astype(BF16)
        wm_s[...] = wm_ref[...].astype(BF16)

    z = jax.nn.gelu(y5_ref[...].astype(F32)).astype(BF16)
    a = jnp.dot(z, wa_s[...], preferred_element_type=F32)
    b = jnp.dot(z, wb_s[...], preferred_element_type=F32)
    mla = jnp.dot(o_ref_in[...], wm_s[...], preferred_element_type=F32)
    merged = gs_ref[...].astype(F32) * (a * jax.nn.sigmoid(b)) + gm_ref[...].astype(F32) * mla
    out_ref[...] = merged.astype(out_ref.dtype)


def _merge(y5, o_mla, gates, w_glu, w_mla_o, tm, tn):
    t = y5.shape[0]
    nj = D_MODEL // tn
    return pl.pallas_call(
        _merge_kernel,
        grid=(nj, t // tm),
        in_specs=[pl.BlockSpec((tm, S5_WIDTH), lambda j, i: (i, 0)),
                  pl.BlockSpec((tm, MLA_HEADS * V_DIM), lambda j, i: (i, 0)),
                  pl.BlockSpec((tm, tn), lambda j, i: (i, j)),
                  pl.BlockSpec((tm, tn), lambda j, i: (i, nj + j)),
                  pl.BlockSpec((S5_WIDTH, tn), lambda j, i: (0, j)),
                  pl.BlockSpec((S5_WIDTH, tn), lambda j, i: (0, nj + j)),
                  pl.BlockSpec((MLA_HEADS * V_DIM, tn), lambda j, i: (0, j))],
        out_specs=pl.BlockSpec((tm, tn), lambda j, i: (i, j)),
        out_shape=jax.ShapeDtypeStruct((t, D_MODEL), BF16),
        scratch_shapes=[pltpu.VMEM((S5_WIDTH, tn), BF16), pltpu.VMEM((S5_WIDTH, tn), BF16),
                        pltpu.VMEM((MLA_HEADS * V_DIM, tn), BF16)],
        compiler_params=_cparams(2),
        name="merge",
    )(y5, o_mla, gates, gates, w_glu, w_glu, w_mla_o)


def _resid_mm(a, w, x_res, gate, rows_per_batch, tm, tn, name):
    def epi(accs, e_refs, o_refs):
        o_refs[0][...] = e_refs[0][...] + e_refs[1][0] * accs[0]

    t = a.shape[0]
    n = w.shape[1]
    tpb = rows_per_batch // tm
    extras = [(x_res, pl.BlockSpec((tm, tn), lambda j, i: (i, j))),
              (gate, pl.BlockSpec((1, 1, tn), lambda j, i: (i // tpb, 0, j)))]
    outs = [(jax.ShapeDtypeStruct((t, n), F32), pl.BlockSpec((tm, tn), lambda j, i: (i, j)))]
    return _fused_mm(a, [(w, 0)], epi, extras, outs, tm=tm, tn=tn, nj=n // tn, name=name)[0]


def _ffn_in(xn, w_ffn_in, tm, tn):
    def epi(accs, e_refs, o_refs):
        o_refs[0][...] = (jax.nn.silu(accs[0]) * accs[1]).astype(BF16)

    t = xn.shape[0]
    nj = D_FF // tn
    outs = [(jax.ShapeDtypeStruct((t, D_FF), BF16), pl.BlockSpec((tm, tn), lambda j, i: (i, j)))]
    return _fused_mm(xn, [(w_ffn_in, 0), (w_ffn_in, nj)], epi, [], outs, tm=tm, tn=tn, nj=nj, name="ffn_in")[0]


def _final_norm_kernel(x_ref, g_ref, o_ref):
    o_ref[...] = _rms(x_ref[...], g_ref[...])


def _final_norm(x2d, g, tm):
    t = x2d.shape[0]
    return pl.pallas_call(
        _final_norm_kernel,
        grid=(t // tm,),
        in_specs=[pl.BlockSpec((tm, D_MODEL), lambda i: (i, 0)),
                  pl.BlockSpec((1, D_MODEL), lambda i: (0, 0))],
        out_specs=pl.BlockSpec((tm, D_MODEL), lambda i: (i, 0)),
        out_shape=jax.ShapeDtypeStruct((t, D_MODEL), F32),
        compiler_params=_cparams(1),
        name="final_norm",
    )(x2d, g.reshape(1, D_MODEL))


def _rope_rot_cols(w):
    k = w.shape[0]
    ws = w.reshape(k, -1, 2, 2, QK_ROPE // 4)
    return jnp.stack([-ws[:, :, :, 1, :], ws[:, :, :, 0, :]], axis=3).reshape(k, -1)


def _rope_tables(n_tokens):
    rows = n_tokens // GRID_W
    row = jnp.repeat(jnp.arange(rows, dtype=F32), GRID_W)
    col = jnp.tile(jnp.arange(GRID_W, dtype=F32), rows)
    n_freq = QK_ROPE // 4
    inv = ROPE_BASE ** (-jnp.arange(n_freq, dtype=F32) / n_freq)
    ang = jnp.stack([row[:, None] * inv, col[:, None] * inv], axis=1)
    cos = jnp.broadcast_to(jnp.cos(ang)[:, :, None, :], (n_tokens, 2, 2, n_freq)).reshape(n_tokens, QK_ROPE)
    sin = jnp.broadcast_to(jnp.sin(ang)[:, :, None, :], (n_tokens, 2, 2, n_freq)).reshape(n_tokens, QK_ROPE)
    return cos, sin


def kernel(x, c, ctx, c_ctx, w_mod, b_mod, norm1, norm2, w_in, s5_a_re, s5_a_im, s5_log_dt, s5_b_re, s5_b_im,
           s5_c_re, s5_c_im, s5_d, w_glu, q_norm, kv_norm, w_uq, w_ukv, w_mla_o, w_out, w_ffn_in, w_ffn_out,
           norm_f):
    batch, seq, _ = x.shape
    n_ctx = ctx.shape[1]
    assert w_mod.shape[0] == 1, "single-layer block"
    p = dict(s5_a_re=s5_a_re[0], s5_a_im=s5_a_im[0], s5_log_dt=s5_log_dt[0], s5_b_re=s5_b_re[0],
             s5_b_im=s5_b_im[0], s5_c_re=s5_c_re[0], s5_c_im=s5_c_im[0], s5_d=s5_d[0])
    w_in0 = w_in[0]

    cv = jnp.concatenate([c, c_ctx[None], jnp.zeros((8 - batch - 1, D_MODEL), F32)], axis=0)
    m = _modulation(cv, w_mod[0], b_mod[0]).reshape(8, 6, 1, D_MODEL)
    m_lat = m[:batch]
    m_ctx = m[batch:batch + 1]

    x2d = x.reshape(batch * seq, D_MODEL)
    c2d = ctx.reshape(batch * n_ctx, D_MODEL)
    xn = _norm_mod(x2d, norm1[0], m_lat[:, 1], m_lat[:, 0], seq, 512)
    cn = _norm_mod(c2d, norm1[0], m_ctx[:, 1], m_ctx[:, 0], batch * n_ctx, 256)

    kv_lo = S5_WIDTH + Q_RANK
    w_kr = w_in0[:, kv_lo + KV_RANK:kv_lo + KV_RANK + QK_ROPE]
    w_kv1 = jnp.concatenate([w_in0[:, kv_lo:kv_lo + KV_RANK], w_kr, _rope_rot_cols(w_kr)], axis=1)
    w_gate = w_in0[:, kv_lo + KV_RANK + QK_ROPE:].astype(BF16)
    wq = w_uq[0].reshape(Q_RANK, MLA_HEADS, QK_NOPE + QK_ROPE)
    wq_rope = wq[:, :, QK_NOPE:].reshape(Q_RANK, MLA_HEADS * QK_ROPE)
    wq2 = jnp.concatenate([wq[:, :, :QK_NOPE].reshape(Q_RANK, MLA_HEADS * QK_NOPE), wq_rope,
                           _rope_rot_cols(wq_rope)], axis=1).astype(BF16)
    w_ukv_bf = w_ukv[0].astype(BF16)
    cos, sin = _rope_tables(seq)
    cos_q, sin_q = jnp.tile(cos, (1, MLA_HEADS)), jnp.tile(sin, (1, MLA_HEADS))

    u_lat = _proj_cast(xn, w_in0, 0, S5_WIDTH, 1024, 512, "u_lat")
    u_ctx = _proj_cast(cn, w_in0, 0, S5_WIDTH, 256, 512, "u_ctx")
    q = _q_path(xn, w_in0, q_norm[0], wq2, cos_q, sin_q, 1024, seq)
    kv_lat, kr_lat = _kv_path(xn, w_kv1, kv_norm[0], w_ukv_bf, cos, sin, 1024, seq)
    kv_ctx, kr_ctx = _kv_path(cn, w_kv1, kv_norm[0], w_ukv_bf, None, None, 256, n_ctx)
    gates = _proj_cast(xn, w_gate, 0, 2 * D_MODEL, 1024, 512, "gates", act=jax.nn.sigmoid)

    y5 = _s5_mixer(u_lat, u_ctx, p, batch, seq, n_ctx)
    o_mla = _attention(q, kv_lat, kv_ctx, kr_lat, kr_ctx, batch, seq, n_ctx, 512)

    merged = _merge(y5, o_mla, gates, w_glu[0], w_mla_o[0], 1024, 512)
    x1 = _resid_mm(merged, w_out[0], x2d, m_lat[:, 2], seq, 1024, 512, "out_proj")
    xn2 = _norm_mod(x1, norm2[0], m_lat[:, 4], m_lat[:, 3], seq, 512)
    hid = _ffn_in(xn2, w_ffn_in[0], 1024, 512)
    x2 = _resid_mm(hid, w_ffn_out[0], x1, m_lat[:, 5], seq, 512, 512, "ffn_out")
    return _final_norm(x2, norm_f, 512).reshape(batch, seq, D_MODEL)
```

```python
import functools
import math

import jax
import jax.numpy as jnp
from jax import lax
from jax.experimental import pallas as pl
from jax.experimental.pallas import tpu as pltpu

F32 = jnp.float32
BF16 = jnp.bfloat16

D_MODEL = 2048
GRID_W = 64
EPS = 1e-6
S5_WIDTH = D_MODEL // 2
S5_GROUP = 16
S5_GROUPS = S5_WIDTH // S5_GROUP
S5_STATE = 64
S5_CHUNK = 16
S5_PAIRS = S5_GROUPS // 2
MLA_HEADS = 8
QK_NOPE = 128
QK_ROPE = 64
V_DIM = 128
Q_RANK = 512
KV_RANK = 256
ROPE_BASE = 10000.0
ATTN_SCALE = (QK_NOPE + QK_ROPE) ** -0.5
D_FF = -(-8 * D_MODEL // (3 * 256)) * 256

VMEM_LIMIT_BYTES = 56 * 1024 * 1024


def _cparams(n_axes):
    return pltpu.CompilerParams(dimension_semantics=("arbitrary",) * n_axes,
                                vmem_limit_bytes=VMEM_LIMIT_BYTES)


def _rms(x, g):
    return x * lax.rsqrt(jnp.mean(x * x, axis=-1, keepdims=True) + EPS) * g


def _mod_kernel(cv_ref, w_ref, b_ref, o_ref):
    s = jax.nn.silu(cv_ref[...]).astype(BF16)
    o_ref[...] = jnp.dot(s, w_ref[...].astype(BF16), preferred_element_type=F32) + b_ref[...]


def _modulation(cv, w_mod, b_mod):
    n = w_mod.shape[1]
    tn = 1536
    return pl.pallas_call(
        _mod_kernel,
        grid=(n // tn,),
        in_specs=[pl.BlockSpec((8, D_MODEL), lambda j: (0, 0)),
                  pl.BlockSpec((D_MODEL, tn), lambda j: (0, j)),
                  pl.BlockSpec((1, tn), lambda j: (0, j))],
        out_specs=pl.BlockSpec((8, tn), lambda j: (0, j)),
        out_shape=jax.ShapeDtypeStruct((8, n), F32),
        compiler_params=_cparams(1),
        name="mod",
    )(cv, w_mod, b_mod.reshape(1, n))


def _norm_mod_kernel(x_ref, g_ref, sc_ref, sh_ref, o_ref):
    y = _rms(x_ref[...], g_ref[...])
    o_ref[...] = (y * (1.0 + sc_ref[0]) + sh_ref[0]).astype(o_ref.dtype)


def _norm_mod(x2d, gain, scale, shift, rows_per_batch, tm, out_dtype=BF16):
    t = x2d.shape[0]
    tpb = rows_per_batch // tm
    return pl.pallas_call(
        _norm_mod_kernel,
        grid=(t // tm,),
        in_specs=[pl.BlockSpec((tm, D_MODEL), lambda i: (i, 0)),
                  pl.BlockSpec((1, D_MODEL), lambda i: (0, 0)),
                  pl.BlockSpec((1, 1, D_MODEL), lambda i: (i // tpb, 0, 0)),
                  pl.BlockSpec((1, 1, D_MODEL), lambda i: (i // tpb, 0, 0))],
        out_specs=pl.BlockSpec((tm, D_MODEL), lambda i: (i, 0)),
        out_shape=jax.ShapeDtypeStruct((t, D_MODEL), out_dtype),
        compiler_params=_cparams(1),
        name="norm_mod",
    )(x2d, gain.reshape(1, D_MODEL), scale, shift)


def _fused_mm(a, weights, epilogue, extras, outs, *, tm, tn, nj, name):
    t, k = a.shape
    ni = t // tm
    nw, ne, no = len(weights), len(extras), len(outs)
    need_cast = [w.dtype != BF16 for w, _ in weights]

    def kernel(*refs):
        a_ref = refs[0]
        w_refs = refs[1:1 + nw]
        e_refs = refs[1 + nw:1 + nw + ne]
        o_refs = refs[1 + nw + ne:1 + nw + ne + no]
        s_refs = list(refs[1 + nw + ne + no:])
        wb = []
        for idx in range(nw):
            if need_cast[idx]:
                s_ref = s_refs.pop(0)

                @pl.when(pl.program_id(1) == 0)
                def _(w_ref=w_refs[idx], s_ref=s_ref):
                    s_ref[...] = w_ref[...].astype(BF16)

                wb.append(s_ref)
            else:
                wb.append(w_refs[idx])
        av = a_ref[...]
        accs = [jnp.dot(av, w[...], preferred_element_type=F32) for w in wb]
        epilogue(accs, e_refs, o_refs)

    in_specs = [pl.BlockSpec((tm, k), lambda j, i: (i, 0))]
    for _, off in weights:
        in_specs.append(pl.BlockSpec((k, tn), functools.partial(lambda j, i, off: (0, off + j), off=off)))
    in_specs += [spec for _, spec in extras]
    scratch = [pltpu.VMEM((k, tn), BF16) for c in need_cast if c]
    res = pl.pallas_call(
        kernel,
        grid=(nj, ni),
        in_specs=in_specs,
        out_specs=[spec for _, spec in outs],
        out_shape=[sds for sds, _ in outs],
        scratch_shapes=scratch,
        compiler_params=_cparams(2),
        name=name,
    )(a, *[w for w, _ in weights], *[e for e, _ in extras])
    return res


def _proj_cast(xn, w, off, n, tm, tn, name, act=None, out_dtype=BF16):
    def epi(accs, e_refs, o_refs):
        v = accs[0]
        if act is not None:
            v = act(v)
        o_refs[0][...] = v.astype(out_dtype)

    t = xn.shape[0]
    return _fused_mm(xn, [(w, off)], epi, [],
                     [(jax.ShapeDtypeStruct((t, n), out_dtype), pl.BlockSpec((tm, tn), lambda j, i: (i, j)))],
                     tm=tm, tn=tn, nj=n // tn, name=name)[0]


def _q_path(xn, w_in, q_norm, wq2, cos_q, sin_q, tm, seq):
    nr = MLA_HEADS * QK_ROPE
    nn = MLA_HEADS * QK_NOPE

    def epi(accs, e_refs, o_refs):
        qn_ref, w2_ref, cos_ref, sin_ref = e_refs
        cq = _rms(accs[0], qn_ref[...]).astype(BF16)
        q = jnp.dot(cq, w2_ref[...], preferred_element_type=F32)
        rope = q[:, nn:nn + nr] * cos_ref[...] + q[:, nn + nr:] * sin_ref[...]
        o_refs[0][:, :nn] = (q[:, :nn] * ATTN_SCALE).astype(BF16)
        o_refs[0][:, nn:] = (rope * ATTN_SCALE).astype(BF16)

    t = xn.shape[0]
    pos_tiles = seq // tm
    extras = [(q_norm.reshape(1, Q_RANK), pl.BlockSpec((1, Q_RANK), lambda j, i: (0, 0))),
              (wq2, pl.BlockSpec(wq2.shape, lambda j, i: (0, 0))),
              (cos_q, pl.BlockSpec((tm, nr), lambda j, i: (i % pos_tiles, 0))),
              (sin_q, pl.BlockSpec((tm, nr), lambda j, i: (i % pos_tiles, 0)))]
    outs = [(jax.ShapeDtypeStruct((t, nn + nr), BF16), pl.BlockSpec((tm, nn + nr), lambda j, i: (i, 0)))]
    return _fused_mm(xn, [(w_in, S5_WIDTH // Q_RANK)], epi, extras, outs,
                     tm=tm, tn=Q_RANK, nj=1, name="q_path")[0]


def _kv_path(xn, w_kv1, kv_norm, w_ukv_bf, cos_k, sin_k, tm, seq):
    nkv = w_ukv_bf.shape[1]
    rope = cos_k is not None

    def epi(accs, e_refs, o_refs):
        acc = accs[0]
        ckv = _rms(acc[:, :KV_RANK], e_refs[0][...]).astype(BF16)
        o_refs[0][...] = jnp.dot(ckv, e_refs[1][...], preferred_element_type=F32).astype(BF16)
        kr = acc[:, KV_RANK:KV_RANK + QK_ROPE]
        if rope:
            kr = kr * e_refs[2][...] + acc[:, KV_RANK + QK_ROPE:] * e_refs[3][...]
        o_refs[1][...] = kr.astype(BF16)

    t = xn.shape[0]
    pos_tiles = seq // tm
    extras = [(kv_norm.reshape(1, KV_RANK), pl.BlockSpec((1, KV_RANK), lambda j, i: (0, 0))),
              (w_ukv_bf, pl.BlockSpec(w_ukv_bf.shape, lambda j, i: (0, 0)))]
    if rope:
        extras += [(cos_k, pl.BlockSpec((tm, QK_ROPE), lambda j, i: (i % pos_tiles, 0))),
                   (sin_k, pl.BlockSpec((tm, QK_ROPE), lambda j, i: (i % pos_tiles, 0)))]
    outs = [(jax.ShapeDtypeStruct((t, nkv), BF16), pl.BlockSpec((tm, nkv), lambda j, i: (i, 0))),
            (jax.ShapeDtypeStruct((t, QK_ROPE), BF16), pl.BlockSpec((tm, QK_ROPE), lambda j, i: (i, 0)))]
    return _fused_mm(xn, [(w_kv1, 0)], epi, extras, outs,
                     tm=tm, tn=w_kv1.shape[1], nj=1, name="kv_path")


def _attn_kernel(qn_ref, qr_ref, kvl_ref, kvc_ref, krl_ref, krc_ref, o_ref, k_scr, v_scr, *, seq, ctx):
    dk = QK_NOPE + QK_ROPE

    @pl.when(pl.program_id(2) == 0)
    def _():
        for h in range(2):
            base = h * (QK_NOPE + V_DIM)
            k_scr[h, :seq, :QK_NOPE] = kvl_ref[:, base:base + QK_NOPE]
            k_scr[h, seq:, :QK_NOPE] = kvc_ref[:, base:base + QK_NOPE]
            k_scr[h, :seq, QK_NOPE:dk] = krl_ref[...]
            k_scr[h, seq:, QK_NOPE:dk] = krc_ref[...]
            v_scr[h, :seq, :] = kvl_ref[:, base + QK_NOPE:base + QK_NOPE + V_DIM]
            v_scr[h, seq:, :] = kvc_ref[:, base + QK_NOPE:base + QK_NOPE + V_DIM]

    for h in range(2):
        q = jnp.concatenate([qn_ref[:, h * QK_NOPE:(h + 1) * QK_NOPE],
                             qr_ref[:, h * QK_ROPE:(h + 1) * QK_ROPE]], axis=1)
        s = lax.dot_general(q, k_scr[h], (((1,), (1,)), ((), ())), preferred_element_type=F32)
        m = jnp.max(s, axis=-1, keepdims=True)
        p = jnp.exp(s - m)
        l = jnp.sum(p, axis=-1, keepdims=True)
        o = jnp.dot(p.astype(BF16), v_scr[h], preferred_element_type=F32)
        o_ref[:, h * V_DIM:(h + 1) * V_DIM] = (o / l).astype(o_ref.dtype)


def _attention(q, kv_lat, kv_ctx, kr_lat, kr_ctx, batch, seq, ctx, tq):
    nq = seq // tq
    nn_blocks = MLA_HEADS * QK_NOPE // (2 * QK_NOPE)
    dk = QK_NOPE + QK_ROPE
    hw = 2 * (QK_NOPE + V_DIM)
    return pl.pallas_call(
        functools.partial(_attn_kernel, seq=seq, ctx=ctx),
        grid=(batch, MLA_HEADS // 2, nq),
        in_specs=[pl.BlockSpec((tq, 2 * QK_NOPE), lambda b, hp, qi: (b * nq + qi, hp)),
                  pl.BlockSpec((tq, 2 * QK_ROPE), lambda b, hp, qi: (b * nq + qi, 2 * nn_blocks + hp)),
                  pl.BlockSpec((seq, hw), lambda b, hp, qi: (b, hp)),
                  pl.BlockSpec((ctx, hw), lambda b, hp, qi: (b, hp)),
                  pl.BlockSpec((seq, QK_ROPE), lambda b, hp, qi: (b, 0)),
                  pl.BlockSpec((ctx, QK_ROPE), lambda b, hp, qi: (b, 0))],
        out_specs=pl.BlockSpec((tq, 2 * V_DIM), lambda b, hp, qi: (b * nq + qi, hp)),
        out_shape=jax.ShapeDtypeStruct((batch * seq, MLA_HEADS * V_DIM), BF16),
        scratch_shapes=[pltpu.VMEM((2, seq + ctx, dk), BF16), pltpu.VMEM((2, seq + ctx, V_DIM), BF16)],
        compiler_params=_cparams(3),
        name="attention",
    )(q, q, kv_lat, kv_ctx, kr_lat, kr_ctx)


def _s5_weights(a_re, a_im, log_dt, b_re, b_im, c_re, c_im, reverse):
    tc = S5_CHUNK
    dt = jnp.exp(log_dt.astype(F32))[:, None]
    lr, li = a_re.astype(F32), a_im.astype(F32)
    mag = jnp.exp(lr * dt)
    ab_re, ab_im = mag * jnp.cos(li * dt), mag * jnp.sin(li * dt)
    den = lr * lr + li * li
    nr, ni = ab_re - 1.0, ab_im
    co_re = (nr * lr + ni * li) / den
    co_im = (ni * lr - nr * li) / den
    br, bi = b_re.astype(F32), b_im.astype(F32)
    bb_re = co_re[..., None] * br - co_im[..., None] * bi
    bb_im = co_re[..., None] * bi + co_im[..., None] * br
    pw_re, pw_im = [jnp.ones_like(ab_re)], [jnp.zeros_like(ab_re)]
    for _ in range(tc):
        pr, pi = pw_re[-1], pw_im[-1]
        pw_re.append(pr * ab_re - pi * ab_im)
        pw_im.append(pr * ab_im + pi * ab_re)
    pw_re, pw_im = jnp.stack(pw_re), jnp.stack(pw_im)
    cr, ci = c_re.astype(F32), c_im.astype(F32)
    ca_re = cr[None] * pw_re[:, :, None, :] - ci[None] * pw_im[:, :, None, :]
    ca_im = cr[None] * pw_im[:, :, None, :] + ci[None] * pw_re[:, :, None, :]
    hp = lax.Precision.HIGHEST
    kern = (jnp.einsum('tgpn,gnq->tgpq', ca_re[:tc], bb_re, precision=hp)
            - jnp.einsum('tgpn,gnq->tgpq', ca_im[:tc], bb_im, precision=hp))
    r = jnp.arange(tc)
    lag = (r[:, None] - r[None, :]) if reverse else (r[None, :] - r[:, None])
    toep = jnp.where((lag >= 0)[:, :, None, None, None], kern[jnp.clip(lag, 0, tc - 1)], 0.0)
    toep = toep.transpose(2, 0, 4, 1, 3).reshape(S5_GROUPS, tc * S5_GROUP, tc * S5_GROUP)
    kst = r if reverse else tc - 1 - r
    wst_re = pw_re[kst][:, :, :, None] * bb_re[None] - pw_im[kst][:, :, :, None] * bb_im[None]
    wst_im = pw_re[kst][:, :, :, None] * bb_im[None] + pw_im[kst][:, :, :, None] * bb_re[None]
    wst_re = wst_re.transpose(1, 0, 3, 2).reshape(S5_GROUPS, tc * S5_GROUP, S5_STATE)
    wst_im = wst_im.transpose(1, 0, 3, 2).reshape(S5_GROUPS, tc * S5_GROUP, S5_STATE)
    kro = tc - r if reverse else r + 1
    wo_re = ca_re[kro].transpose(1, 3, 0, 2).reshape(S5_GROUPS, S5_STATE, tc * S5_GROUP)
    wo_im = (-ca_im[kro]).transpose(1, 3, 0, 2).reshape(S5_GROUPS, S5_STATE, tc * S5_GROUP)

    def pair_rows(w):
        g, k, n = w.shape
        w = w.reshape(g // 2, 2, k, n)
        z = jnp.zeros_like(w[:, 0])
        return jnp.concatenate([jnp.concatenate([w[:, 0], z], axis=2),
                                jnp.concatenate([z, w[:, 1]], axis=2)], axis=1)

    a_t_re = pw_re[tc].reshape(S5_PAIRS, 1, 2 * S5_STATE)
    a_t_im = pw_im[tc].reshape(S5_PAIRS, 1, 2 * S5_STATE)
    return (toep.astype(BF16), pair_rows(wst_re).astype(BF16), pair_rows(wst_im).astype(BF16),
            pair_rows(wo_re).astype(BF16), pair_rows(wo_im).astype(BF16), a_t_re, a_t_im)


S5_STEP_PAIRS = 4
SUBLANES = 8


def _s5_state_rows(batch, n_lat, n_ctx):
    def pitch(n):
        p = -(-n // SUBLANES)
        return SUBLANES * (p + 1 - p % 2)

    lat_pitch, ctx_pitch = pitch(n_lat), pitch(n_ctx)
    ctx_base = batch * lat_pitch
    return lat_pitch, ctx_pitch, ctx_base, ctx_base + batch * ctx_pitch


def _s5_kernel(ul_ref, uc_ref, dsk_ref, *refs, batch, seq, ctx):
    wrefs, o_ref = refs[:14], refs[14]
    wscr, zscr, upscr, sre, sim, hre, him = refs[15:]
    tc = S5_CHUNK
    n_lat, n_ctx = seq // tc, ctx // tc
    rows_lat = n_lat * batch
    lat_pitch, ctx_pitch, ctx_base, _ = _s5_state_rows(batch, n_lat, n_ctx)
    w = tc * S5_GROUP
    lanes = S5_STEP_PAIRS * 2 * S5_GROUP

    def scatter_tiles(xt, r, col0):
        for kk in range(S5_STEP_PAIRS):
            for j in range(2):
                ch = (2 * kk + j) * S5_GROUP
                wscr[kk, j * w + r * S5_GROUP:j * w + (r + 1) * S5_GROUP, col0:col0 + lanes] = xt[ch:ch + S5_GROUP, :]

    for r in range(tc):
        for b in range(batch):
            x = ul_ref[pl.ds(b * seq + r, n_lat, stride=tc), :]
            scatter_tiles(x.T, r, b * n_lat)
        xc = jnp.concatenate([uc_ref[pl.ds(b * ctx + r, n_ctx, stride=tc), :] for b in range(batch)]
                             + [jnp.zeros((lanes - batch * n_ctx, lanes), F32)], axis=0)
        scatter_tiles(xc.T, r, rows_lat)

    chains = [(kk, d) for kk in range(S5_STEP_PAIRS) for d in range(2)]
    for kk in range(S5_STEP_PAIRS):
        upscr[kk] = wscr[kk].T.astype(BF16)
        for d in range(2):
            for dst, wi in ((sre, 1), (sim, 2)):
                s = jnp.dot(upscr[kk], wrefs[7 * d + wi][kk], preferred_element_type=F32)
                for b in range(batch):
                    dst[kk, d, b * lat_pitch:b * lat_pitch + n_lat, :] = s[b * n_lat:(b + 1) * n_lat]
                    dst[kk, d, ctx_base + b * ctx_pitch:ctx_base + b * ctx_pitch + n_ctx, :] = (
                        s[rows_lat + b * n_ctx:rows_lat + (b + 1) * n_ctx])

    ctx_rows = [pl.ds(ctx_base + c, batch, stride=ctx_pitch) for c in range(n_ctx)]
    lat_rows = [pl.ds(c, batch, stride=lat_pitch) for c in range(n_lat)]
    order = (ctx_rows + lat_rows, ctx_rows[::-1] + lat_rows[::-1])
    coef = {(kk, d): (wrefs[7 * d + 5][kk], wrefs[7 * d + 6][kk]) for kk, d in chains}
    state = {ch: (jnp.zeros((batch, 2 * S5_STATE), F32), jnp.zeros((batch, 2 * S5_STATE), F32)) for ch in chains}
    for t in range(n_ctx + n_lat):
        for kk, d in chains:
            rows = order[d][t]
            (ar, ai), (h_re, h_im) = coef[kk, d], state[kk, d]
            hre[kk, d, rows, :] = h_re
            him[kk, d, rows, :] = h_im
            state[kk, d] = (ar * h_re - ai * h_im + sre[kk, d, rows, :],
                            ar * h_im + ai * h_re + sim[kk, d, rows, :])

    for kk in range(S5_STEP_PAIRS):
        ul = upscr[kk, :rows_lat, :]
        y = ul.astype(F32) * dsk_ref[kk]
        for d in range(2):
            toep_ref, wor, woi = wrefs[7 * d], wrefs[7 * d + 3], wrefs[7 * d + 4]
            h_r = jnp.concatenate([hre[kk, d, b * lat_pitch:b * lat_pitch + n_lat, :] for b in range(batch)], axis=0)
            h_i = jnp.concatenate([him[kk, d, b * lat_pitch:b * lat_pitch + n_lat, :] for b in range(batch)], axis=0)
            y = y + jnp.dot(h_r.astype(BF16), wor[kk], preferred_element_type=F32)
            y = y + jnp.dot(h_i.astype(BF16), woi[kk], preferred_element_type=F32)
            y = y + jnp.concatenate(
                [jnp.dot(ul[:, :w], toep_ref[2 * kk], preferred_element_type=F32),
                 jnp.dot(ul[:, w:], toep_ref[2 * kk + 1], preferred_element_type=F32)], axis=1)
        yt = y.T
        for b in range(batch):
            for s in range(tc):
                for j in range(2):
                    ch = (2 * kk + j) * S5_GROUP
                    zscr[b * tc + s, ch:ch + S5_GROUP, :] = yt[j * w + s * S5_GROUP:j * w + (s + 1) * S5_GROUP,
                                                               b * n_lat:(b + 1) * n_lat]

    for b in range(batch):
        for s in range(tc):
            o_ref[pl.ds(b * seq + s, n_lat, stride=tc), :] = zscr[b * tc + s].T


def _s5_mixer(u_lat, u_ctx, params, batch, seq, ctx):
    tc = S5_CHUNK
    n_lat, n_ctx = seq // tc, ctx // tc
    sp = S5_STEP_PAIRS
    lanes = sp * 2 * S5_GROUP
    assert n_lat == lanes and batch * n_ctx <= lanes
    rows_lat = n_lat * batch
    rows_all = rows_lat + lanes
    state_rows = _s5_state_rows(batch, n_lat, n_ctx)[3]
    wide = 2 * tc * S5_GROUP
    dsk = jnp.broadcast_to(params['s5_d'].astype(F32).reshape(S5_PAIRS, 2, 1, S5_GROUP),
                           (S5_PAIRS, 2, tc, S5_GROUP)).reshape(S5_PAIRS, 1, wide)
    weights = []
    specs = []
    for d, rev in enumerate((False, True)):
        wd = _s5_weights(params['s5_a_re'][d], params['s5_a_im'][d], params['s5_log_dt'][d],
                         params['s5_b_re'][d], params['s5_b_im'][d], params['s5_c_re'][d],
                         params['s5_c_im'][d], rev)
        weights += list(wd)
        specs += [pl.BlockSpec((2 * sp, wide // 2, wide // 2), lambda k: (k, 0, 0)),
                  pl.BlockSpec((sp, wide, 2 * S5_STATE), lambda k: (k, 0, 0)),
                  pl.BlockSpec((sp, wide, 2 * S5_STATE), lambda k: (k, 0, 0)),
                  pl.BlockSpec((sp, 2 * S5_STATE, wide), lambda k: (k, 0, 0)),
                  pl.BlockSpec((sp, 2 * S5_STATE, wide), lambda k: (k, 0, 0)),
                  pl.BlockSpec((sp, 1, 2 * S5_STATE), lambda k: (k, 0, 0)),
                  pl.BlockSpec((sp, 1, 2 * S5_STATE), lambda k: (k, 0, 0))]
    return pl.pallas_call(
        functools.partial(_s5_kernel, batch=batch, seq=seq, ctx=ctx),
        grid=(S5_PAIRS // sp,),
        in_specs=[pl.BlockSpec((batch * seq, lanes), lambda k: (0, k)),
                  pl.BlockSpec((batch * ctx, lanes), lambda k: (0, k)),
                  pl.BlockSpec((sp, 1, wide), lambda k: (k, 0, 0))] + specs,
        out_specs=pl.BlockSpec((batch * seq, lanes), lambda k: (0, k)),
        out_shape=jax.ShapeDtypeStruct((batch * seq, S5_WIDTH), F32),
        scratch_shapes=[pltpu.VMEM((sp, wide, rows_all), F32),
                        pltpu.VMEM((batch * tc, lanes, n_lat), F32),
                        pltpu.VMEM((sp, rows_all, wide), BF16)]
                       + [pltpu.VMEM((sp, 2, state_rows, 2 * S5_STATE), F32) for _ in range(4)],
        compiler_params=_cparams(1),
        name="s5",
    )(u_lat, u_ctx, dsk, *weights)


def _merge_kernel(y5_ref, o_ref_in, gs_ref, gm_ref, wa_ref, wb_ref, wm_ref, out_ref, wa_s, wb_s, wm_s):
    @pl.when(pl.program_id(1) == 0)
    def _():
        wa_s[...] = wa_ref[...].astype(BF16)
        wb_s[...] = wb_ref[...].astype(BF16)
        wm_s[...] = wm_ref[...].astype(BF16)

    z = jax.nn.gelu(y5_ref[...].astype(F32)).astype(BF16)
    a = jnp.dot(z, wa_s[...], preferred_element_type=F32)
    b = jnp.dot(z, wb_s[...], preferred_element_type=F32)
    mla = jnp.dot(o_ref_in[...], wm_s[...], preferred_element_type=F32)
    merged = gs_ref[...].astype(F32) * (a * jax.nn.sigmoid(b)) + gm_ref[...].astype(F32) * mla
    out_ref[...] = merged.astype(out_ref.dtype)


def _merge(y5, o_mla, gates, w_glu, w_mla_o, tm, tn):
    t = y5.shape[0]
    nj = D_MODEL // tn
    return pl.pallas_call(
        _merge_kernel,
        grid=(nj, t // tm),
        in_specs=[pl.BlockSpec((tm, S5_WIDTH), lambda j, i: (i, 0)),
                  pl.BlockSpec((tm, MLA_HEADS * V_DIM), lambda j, i: (i, 0)),
                  pl.BlockSpec((tm, tn), lambda j, i: (i, j)),
                  pl.BlockSpec((tm, tn), lambda j, i: (i, nj + j)),
                  pl.BlockSpec((S5_WIDTH, tn), lambda j, i: (0, j)),
                  pl.BlockSpec((S5_WIDTH, tn), lambda j, i: (0, nj + j)),
                  pl.BlockSpec((MLA_HEADS * V_DIM, tn), lambda j, i: (0, j))],
        out_specs=pl.BlockSpec((tm, tn), lambda j, i: (i, j)),
        out_shape=jax.ShapeDtypeStruct((t, D_MODEL), BF16),
        scratch_shapes=[pltpu.VMEM((S5_WIDTH, tn), BF16), pltpu.VMEM((S5_WIDTH, tn), BF16),
                        pltpu.VMEM((MLA_HEADS * V_DIM, tn), BF16)],
        compiler_params=_cparams(2),
        name="merge",
    )(y5, o_mla, gates, gates, w_glu, w_glu, w_mla_o)


def _resid_mm(a, w, x_res, gate, rows_per_batch, tm, tn, name):
    def epi(accs, e_refs, o_refs):
        o_refs[0][...] = e_refs[0][...] + e_refs[1][0] * accs[0]

    t = a.shape[0]
    n = w.shape[1]
    tpb = rows_per_batch // tm
    extras = [(x_res, pl.BlockSpec((tm, tn), lambda j, i: (i, j))),
              (gate, pl.BlockSpec((1, 1, tn), lambda j, i: (i // tpb, 0, j)))]
    outs = [(jax.ShapeDtypeStruct((t, n), F32), pl.BlockSpec((tm, tn), lambda j, i: (i, j)))]
    return _fused_mm(a, [(w, 0)], epi, extras, outs, tm=tm, tn=tn, nj=n // tn, name=name)[0]


def _ffn_in(xn, w_ffn_in, tm, tn):
    def epi(accs, e_refs, o_refs):
        o_refs[0][...] = (jax.nn.silu(accs[0]) * accs[1]).astype(BF16)

    t = xn.shape[0]
    nj = D_FF // tn
    outs = [(jax.ShapeDtypeStruct((t, D_FF), BF16), pl.BlockSpec((tm, tn), lambda j, i: (i, j)))]
    return _fused_mm(xn, [(w_ffn_in, 0), (w_ffn_in, nj)], epi, [], outs, tm=tm, tn=tn, nj=nj, name="ffn_in")[0]


def _final_norm_kernel(x_ref, g_ref, o_ref):
    o_ref[...] = _rms(x_ref[...], g_ref[...])


def _final_norm(x2d, g, tm):
    t = x2d.shape[0]
    return pl.pallas_call(
        _final_norm_kernel,
        grid=(t // tm,),
        in_specs=[pl.BlockSpec((tm, D_MODEL), lambda i: (i, 0)),
                  pl.BlockSpec((1, D_MODEL), lambda i: (0, 0))],
        out_specs=pl.BlockSpec((tm, D_MODEL), lambda i: (i, 0)),
        out_shape=jax.ShapeDtypeStruct((t, D_MODEL), F32),
        compiler_params=_cparams(1),
        name="final_norm",
    )(x2d, g.reshape(1, D_MODEL))


def _rope_rot_cols(w):
    k = w.shape[0]
    ws = w.reshape(k, -1, 2, 2, QK_ROPE // 4)
    return jnp.stack([-ws[:, :, :, 1, :], ws[:, :, :, 0, :]], axis=3).reshape(k, -1)


def _rope_tables(n_tokens):
    rows = n_tokens // GRID_W
    row = jnp.repeat(jnp.arange(rows, dtype=F32), GRID_W)
    col = jnp.tile(jnp.arange(GRID_W, dtype=F32), rows)
    n_freq = QK_ROPE // 4
    inv = ROPE_BASE ** (-jnp.arange(n_freq, dtype=F32) / n_freq)
    ang = jnp.stack([row[:, None] * inv, col[:, None] * inv], axis=1)
    cos = jnp.broadcast_to(jnp.cos(ang)[:, :, None, :], (n_tokens, 2, 2, n_freq)).reshape(n_tokens, QK_ROPE)
    sin = jnp.broadcast_to(jnp.sin(ang)[:, :, None, :], (n_tokens, 2, 2, n_freq)).reshape(n_tokens, QK_ROPE)
    return cos, sin


def kernel(x, c, ctx, c_ctx, w_mod, b_mod, norm1, norm2, w_in, s5_a_re, s5_a_im, s5_log_dt, s5_b_re, s5_b_im,
           s5_c_re, s5_c_im, s5_d, w_glu, q_norm, kv_norm, w_uq, w_ukv, w_mla_o, w_out, w_ffn_in, w_ffn_out,
           norm_f):
    batch, seq, _ = x.shape
    n_ctx = ctx.shape[1]
    assert w_mod.shape[0] == 1, "single-layer block"
    p = dict(s5_a_re=s5_a_re[0], s5_a_im=s5_a_im[0], s5_log_dt=s5_log_dt[0], s5_b_re=s5_b_re[0],
             s5_b_im=s5_b_im[0], s5_c_re=s5_c_re[0], s5_c_im=s5_c_im[0], s5_d=s5_d[0])
    w_in0 = w_in[0]

    cv = jnp.concatenate([c, c_ctx[None], jnp.zeros((8 - batch - 1, D_MODEL), F32)], axis=0)
    m = _modulation(cv, w_mod[0], b_mod[0]).reshape(8, 6, 1, D_MODEL)
    m_lat = m[:batch]
    m_ctx = m[batch:batch + 1]

    x2d = x.reshape(batch * seq, D_MODEL)
    c2d = ctx.reshape(batch * n_ctx, D_MODEL)
    xn = _norm_mod(x2d, norm1[0], m_lat[:, 1], m_lat[:, 0], seq, 512)
    cn = _norm_mod(c2d, norm1[0], m_ctx[:, 1], m_ctx[:, 0], batch * n_ctx, 256)

    kv_lo = S5_WIDTH + Q_RANK
    w_kr = w_in0[:, kv_lo + KV_RANK:kv_lo + KV_RANK + QK_ROPE]
    w_kv1 = jnp.concatenate([w_in0[:, kv_lo:kv_lo + KV_RANK], w_kr, _rope_rot_cols(w_kr)], axis=1)
    w_gate = w_in0[:, kv_lo + KV_RANK + QK_ROPE:].astype(BF16)
    wq = w_uq[0].reshape(Q_RANK, MLA_HEADS, QK_NOPE + QK_ROPE)
    wq_rope = wq[:, :, QK_NOPE:].reshape(Q_RANK, MLA_HEADS * QK_ROPE)
    wq2 = jnp.concatenate([wq[:, :, :QK_NOPE].reshape(Q_RANK, MLA_HEADS * QK_NOPE), wq_rope,
                           _rope_rot_cols(wq_rope)], axis=1).astype(BF16)
    w_ukv_bf = w_ukv[0].astype(BF16)
    cos, sin = _rope_tables(seq)
    cos_q, sin_q = jnp.tile(cos, (1, MLA_HEADS)), jnp.tile(sin, (1, MLA_HEADS))

    u_lat = _proj_cast(xn, w_in0, 0, S5_WIDTH, 1024, 512, "u_lat", out_dtype=F32)
    u_ctx = _proj_cast(cn, w_in0, 0, S5_WIDTH, 256, 512, "u_ctx", out_dtype=F32)
    q = _q_path(xn, w_in0, q_norm[0], wq2, cos_q, sin_q, 1024, seq)
    kv_lat, kr_lat = _kv_path(xn, w_kv1, kv_norm[0], w_ukv_bf, cos, sin, 1024, seq)
    kv_ctx, kr_ctx = _kv_path(cn, w_kv1, kv_norm[0], w_ukv_bf, None, None, 256, n_ctx)
    gates = _proj_cast(xn, w_gate, 0, 2 * D_MODEL, 1024, 512, "gates", act=jax.nn.sigmoid)

    y5 = _s5_mixer(u_lat, u_ctx, p, batch, seq, n_ctx)
    o_mla = _attention(q, kv_lat, kv_ctx, kr_lat, kr_ctx, batch, seq, n_ctx, 512)

    merged = _merge(y5, o_mla, gates, w_glu[0], w_mla_o[0], 1024, 512)
    x1 = _resid_mm(merged, w_out[0], x2d, m_lat[:, 2], seq, 1024, 512, "out_proj")
    xn2 = _norm_mod(x1, norm2[0], m_lat[:, 4], m_lat[:, 3], seq, 512)
    hid = _ffn_in(xn2, w_ffn_in[0], 1024, 512)
    x2 = _resid_mm(hid, w_ffn_out[0], x1, m_lat[:, 5], seq, 512, 512, "ffn_out")
    return _final_norm(x2, norm_f, 512).reshape(batch, seq, D_MODEL)
```

```python
import functools
import math

import jax
import jax.numpy as jnp
from jax import lax
from jax.experimental import pallas as pl
from jax.experimental.pallas import tpu as pltpu

F32 = jnp.float32
BF16 = jnp.bfloat16

D_MODEL = 2048
GRID_W = 64
EPS = 1e-6
S5_WIDTH = D_MODEL // 2
S5_GROUP = 16
S5_GROUPS = S5_WIDTH // S5_GROUP
S5_STATE = 64
S5_CHUNK = 16
S5_PAIRS = S5_GROUPS // 2
MLA_HEADS = 8
QK_NOPE = 128
QK_ROPE = 64
V_DIM = 128
Q_RANK = 512
KV_RANK = 256
ROPE_BASE = 10000.0
ATTN_SCALE = (QK_NOPE + QK_ROPE) ** -0.5
D_FF = -(-8 * D_MODEL // (3 * 256)) * 256

VMEM_LIMIT_BYTES = 56 * 1024 * 1024


def _cparams(n_axes):
    return pltpu.CompilerParams(dimension_semantics=("arbitrary",) * n_axes,
                                vmem_limit_bytes=VMEM_LIMIT_BYTES)


def _rms(x, g):
    return x * lax.rsqrt(jnp.mean(x * x, axis=-1, keepdims=True) + EPS) * g


def _mod_kernel(cv_ref, w_ref, b_ref, o_ref):
    s = jax.nn.silu(cv_ref[...]).astype(BF16)
    o_ref[...] = jnp.dot(s, w_ref[...].astype(BF16), preferred_element_type=F32) + b_ref[...]


def _modulation(cv, w_mod, b_mod):
    n = w_mod.shape[1]
    tn = 1536
    return pl.pallas_call(
        _mod_kernel,
        grid=(n // tn,),
        in_specs=[pl.BlockSpec((8, D_MODEL), lambda j: (0, 0)),
                  pl.BlockSpec((D_MODEL, tn), lambda j: (0, j)),
                  pl.BlockSpec((1, tn), lambda j: (0, j))],
        out_specs=pl.BlockSpec((8, tn), lambda j: (0, j)),
        out_shape=jax.ShapeDtypeStruct((8, n), F32),
        compiler_params=_cparams(1),
        name="mod",
    )(cv, w_mod, b_mod.reshape(1, n))


def _norm_mod_kernel(x_ref, g_ref, sc_ref, sh_ref, o_ref):
    y = _rms(x_ref[...], g_ref[...])
    o_ref[...] = (y * (1.0 + sc_ref[0]) + sh_ref[0]).astype(o_ref.dtype)


def _norm_mod(x2d, gain, scale, shift, rows_per_batch, tm, out_dtype=BF16):
    t = x2d.shape[0]
    tpb = rows_per_batch // tm
    return pl.pallas_call(
        _norm_mod_kernel,
        grid=(t // tm,),
        in_specs=[pl.BlockSpec((tm, D_MODEL), lambda i: (i, 0)),
                  pl.BlockSpec((1, D_MODEL), lambda i: (0, 0)),
                  pl.BlockSpec((1, 1, D_MODEL), lambda i: (i // tpb, 0, 0)),
                  pl.BlockSpec((1, 1, D_MODEL), lambda i: (i // tpb, 0, 0))],
        out_specs=pl.BlockSpec((tm, D_MODEL), lambda i: (i, 0)),
        out_shape=jax.ShapeDtypeStruct((t, D_MODEL), out_dtype),
        compiler_params=_cparams(1),
        name="norm_mod",
    )(x2d, gain.reshape(1, D_MODEL), scale, shift)


def _fused_mm(a, weights, epilogue, extras, outs, *, tm, tn, nj, name):
    t, k = a.shape
    ni = t // tm
    nw, ne, no = len(weights), len(extras), len(outs)
    need_cast = [w.dtype != BF16 for w, _ in weights]

    def kernel(*refs):
        a_ref = refs[0]
        w_refs = refs[1:1 + nw]
        e_refs = refs[1 + nw:1 + nw + ne]
        o_refs = refs[1 + nw + ne:1 + nw + ne + no]
        s_refs = list(refs[1 + nw + ne + no:])
        wb = []
        for idx in range(nw):
            if need_cast[idx]:
                s_ref = s_refs.pop(0)

                @pl.when(pl.program_id(1) == 0)
                def _(w_ref=w_refs[idx], s_ref=s_ref):
                    s_ref[...] = w_ref[...].astype(BF16)

                wb.append(s_ref)
            else:
                wb.append(w_refs[idx])
        av = a_ref[...]
        accs = [jnp.dot(av, w[...], preferred_element_type=F32) for w in wb]
        epilogue(accs, e_refs, o_refs)

    in_specs = [pl.BlockSpec((tm, k), lambda j, i: (i, 0))]
    for _, off in weights:
        in_specs.append(pl.BlockSpec((k, tn), functools.partial(lambda j, i, off: (0, off + j), off=off)))
    in_specs += [spec for _, spec in extras]
    scratch = [pltpu.VMEM((k, tn), BF16) for c in need_cast if c]
    res = pl.pallas_call(
        kernel,
        grid=(nj, ni),
        in_specs=in_specs,
        out_specs=[spec for _, spec in outs],
        out_shape=[sds for sds, _ in outs],
        scratch_shapes=scratch,
        compiler_params=_cparams(2),
        name=name,
    )(a, *[w for w, _ in weights], *[e for e, _ in extras])
    return res


def _proj_cast(xn, w, off, n, tm, tn, name, act=None, out_dtype=BF16):
    def epi(accs, e_refs, o_refs):
        v = accs[0]
        if act is not None:
            v = act(v)
        o_refs[0][...] = v.astype(out_dtype)

    t = xn.shape[0]
    return _fused_mm(xn, [(w, off)], epi, [],
                     [(jax.ShapeDtypeStruct((t, n), out_dtype), pl.BlockSpec((tm, tn), lambda j, i: (i, j)))],
                     tm=tm, tn=tn, nj=n // tn, name=name)[0]


def _q_path(xn, w_in, q_norm, wq2, cos_q, sin_q, tm, seq):
    nr = MLA_HEADS * QK_ROPE
    nn = MLA_HEADS * QK_NOPE

    def epi(accs, e_refs, o_refs):
        qn_ref, w2_ref, cos_ref, sin_ref = e_refs
        cq = _rms(accs[0], qn_ref[...]).astype(BF16)
        q = jnp.dot(cq, w2_ref[...], preferred_element_type=F32)
        rope = q[:, nn:nn + nr] * cos_ref[...] + q[:, nn + nr:] * sin_ref[...]
        o_refs[0][:, :nn] = (q[:, :nn] * ATTN_SCALE).astype(BF16)
        o_refs[0][:, nn:] = (rope * ATTN_SCALE).astype(BF16)

    t = xn.shape[0]
    pos_tiles = seq // tm
    extras = [(q_norm.reshape(1, Q_RANK), pl.BlockSpec((1, Q_RANK), lambda j, i: (0, 0))),
              (wq2, pl.BlockSpec(wq2.shape, lambda j, i: (0, 0))),
              (cos_q, pl.BlockSpec((tm, nr), lambda j, i: (i % pos_tiles, 0))),
              (sin_q, pl.BlockSpec((tm, nr), lambda j, i: (i % pos_tiles, 0)))]
    outs = [(jax.ShapeDtypeStruct((t, nn + nr), BF16), pl.BlockSpec((tm, nn + nr), lambda j, i: (i, 0)))]
    return _fused_mm(xn, [(w_in, S5_WIDTH // Q_RANK)], epi, extras, outs,
                     tm=tm, tn=Q_RANK, nj=1, name="q_path")[0]


def _kv_path(xn, w_kv1, kv_norm, w_ukv_bf, cos_k, sin_k, tm, seq):
    nkv = w_ukv_bf.shape[1]
    rope = cos_k is not None

    def epi(accs, e_refs, o_refs):
        acc = accs[0]
        ckv = _rms(acc[:, :KV_RANK], e_refs[0][...]).astype(BF16)
        o_refs[0][...] = jnp.dot(ckv, e_refs[1][...], preferred_element_type=F32).astype(BF16)
        kr = acc[:, KV_RANK:KV_RANK + QK_ROPE]
        if rope:
            kr = kr * e_refs[2][...] + acc[:, KV_RANK + QK_ROPE:] * e_refs[3][...]
        o_refs[1][...] = kr.astype(BF16)

    t = xn.shape[0]
    pos_tiles = seq // tm
    extras = [(kv_norm.reshape(1, KV_RANK), pl.BlockSpec((1, KV_RANK), lambda j, i: (0, 0))),
              (w_ukv_bf, pl.BlockSpec(w_ukv_bf.shape, lambda j, i: (0, 0)))]
    if rope:
        extras += [(cos_k, pl.BlockSpec((tm, QK_ROPE), lambda j, i: (i % pos_tiles, 0))),
                   (sin_k, pl.BlockSpec((tm, QK_ROPE), lambda j, i: (i % pos_tiles, 0)))]
    outs = [(jax.ShapeDtypeStruct((t, nkv), BF16), pl.BlockSpec((tm, nkv), lambda j, i: (i, 0))),
            (jax.ShapeDtypeStruct((t, QK_ROPE), BF16), pl.BlockSpec((tm, QK_ROPE), lambda j, i: (i, 0)))]
    return _fused_mm(xn, [(w_kv1, 0)], epi, extras, outs,
                     tm=tm, tn=w_kv1.shape[1], nj=1, name="kv_path")


def _attn_kernel(qn_ref, qr_ref, kvl_ref, kvc_ref, krl_ref, krc_ref, o_ref, k_scr, v_scr, *, seq, ctx):
    dk = QK_NOPE + QK_ROPE

    @pl.when(pl.program_id(2) == 0)
    def _():
        for h in range(2):
            base = h * (QK_NOPE + V_DIM)
            k_scr[h, :seq, :QK_NOPE] = kvl_ref[:, base:base + QK_NOPE]
            k_scr[h, seq:, :QK_NOPE] = kvc_ref[:, base:base + QK_NOPE]
            k_scr[h, :seq, QK_NOPE:dk] = krl_ref[...]
            k_scr[h, seq:, QK_NOPE:dk] = krc_ref[...]
            v_scr[h, :seq, :] = kvl_ref[:, base + QK_NOPE:base + QK_NOPE + V_DIM]
            v_scr[h, seq:, :] = kvc_ref[:, base + QK_NOPE:base + QK_NOPE + V_DIM]

    for h in range(2):
        q = jnp.concatenate([qn_ref[:, h * QK_NOPE:(h + 1) * QK_NOPE],
                             qr_ref[:, h * QK_ROPE:(h + 1) * QK_ROPE]], axis=1)
        s = lax.dot_general(q, k_scr[h], (((1,), (1,)), ((), ())), preferred_element_type=F32)
        m = jnp.max(s, axis=-1, keepdims=True)
        p = jnp.exp(s - m)
        l = jnp.sum(p, axis=-1, keepdims=True)
        o = jnp.dot(p.astype(BF16), v_scr[h], preferred_element_type=F32)
        o_ref[:, h * V_DIM:(h + 1) * V_DIM] = (o / l).astype(o_ref.dtype)


def _attention(q, kv_lat, kv_ctx, kr_lat, kr_ctx, batch, seq, ctx, tq):
    nq = seq // tq
    nn_blocks = MLA_HEADS * QK_NOPE // (2 * QK_NOPE)
    dk = QK_NOPE + QK_ROPE
    hw = 2 * (QK_NOPE + V_DIM)
    return pl.pallas_call(
        functools.partial(_attn_kernel, seq=seq, ctx=ctx),
        grid=(batch, MLA_HEADS // 2, nq),
        in_specs=[pl.BlockSpec((tq, 2 * QK_NOPE), lambda b, hp, qi: (b * nq + qi, hp)),
                  pl.BlockSpec((tq, 2 * QK_ROPE), lambda b, hp, qi: (b * nq + qi, 2 * nn_blocks + hp)),
                  pl.BlockSpec((seq, hw), lambda b, hp, qi: (b, hp)),
                  pl.BlockSpec((ctx, hw), lambda b, hp, qi: (b, hp)),
                  pl.BlockSpec((seq, QK_ROPE), lambda b, hp, qi: (b, 0)),
                  pl.BlockSpec((ctx, QK_ROPE), lambda b, hp, qi: (b, 0))],
        out_specs=pl.BlockSpec((tq, 2 * V_DIM), lambda b, hp, qi: (b * nq + qi, hp)),
        out_shape=jax.ShapeDtypeStruct((batch * seq, MLA_HEADS * V_DIM), BF16),
        scratch_shapes=[pltpu.VMEM((2, seq + ctx, dk), BF16), pltpu.VMEM((2, seq + ctx, V_DIM), BF16)],
        compiler_params=_cparams(3),
        name="attention",
    )(q, q, kv_lat, kv_ctx, kr_lat, kr_ctx)


def _s5_prep_kernel(are_ref, aim_ref, ldt_ref, bre_ref, bim_ref, cre_ref, cim_ref,
                    toep_ref, wsr_ref, wsi_ref, wor_ref, woi_ref, atr_ref, ati_ref):
    tc, g = S5_CHUNK, S5_GROUP
    w = tc * g
    hp = lax.Precision.HIGHEST
    lane = lax.broadcasted_iota(jnp.int32, (g, 2 * S5_STATE), 1)
    in_group = (lane < S5_STATE, lane >= S5_STATE)
    lane_w = lax.broadcasted_iota(jnp.int32, (g, w), 1)
    nt = (((1,), (1,)), ((), ()))
    toep_rows = [[jnp.zeros((g, w), F32) for _ in range(tc)] for _ in range(2)]
    for d in range(2):
        lr, li = are_ref[d, 0], aim_ref[d, 0]
        dt = jnp.exp(ldt_ref[d, 0])
        mag = jnp.exp(lr * dt)
        ab_re, ab_im = mag * jnp.cos(li * dt), mag * jnp.sin(li * dt)
        den = lr * lr + li * li
        nr, ni = ab_re - 1.0, ab_im
        co_re = (nr * lr + ni * li) / den
        co_im = (ni * lr - nr * li) / den
        br, bi = bre_ref[d, 0], bim_ref[d, 0]
        bb_re = co_re * br - co_im * bi
        bb_im = co_re * bi + co_im * br
        pw = [(jnp.ones_like(ab_re), jnp.zeros_like(ab_re))]
        for _ in range(tc):
            pr, pi = pw[-1]
            pw.append((pr * ab_re - pi * ab_im, pr * ab_im + pi * ab_re))
        cr, ci = cre_ref[d, 0], cim_ref[d, 0]
        ca = [(cr * pr - ci * pi, cr * pi + ci * pr) for pr, pi in pw]
        taus = list(range(tc))[::-1] if d else list(range(tc))
        y_re = jnp.concatenate([ca[t][0] for t in taus], axis=0)
        y_im = jnp.concatenate([ca[t][1] for t in taus], axis=0)
        for j in range(2):
            x_re = jnp.where(in_group[j], bb_re, 0.0)
            x_im = jnp.where(in_group[j], bb_im, 0.0)
            kt = (lax.dot_general(x_re, y_re, nt, precision=hp, preferred_element_type=F32)
                  - lax.dot_general(x_im, y_im, nt, precision=hp, preferred_element_type=F32))
            for r in range(tc):
                sh = (r + 1) * g if d else r * g
                blk = pltpu.roll(kt, sh % w, 1) if sh % w else kt
                keep = (lane_w < sh) if d else (lane_w >= sh)
                toep_rows[j][r] = toep_rows[j][r] + jnp.where(keep, blk, 0.0)
        for r in range(tc):
            pr, pi = pw[r] if d else pw[tc - 1 - r]
            w_re = bb_re * pr - bb_im * pi
            w_im = bb_re * pi + bb_im * pr
            car, cai = ca[tc - r] if d else ca[r + 1]
            for j in range(2):
                rows = slice(j * w + r * g, j * w + (r + 1) * g)
                wsr_ref[d, 0, rows, :] = jnp.where(in_group[j], w_re, 0.0).astype(BF16)
                wsi_ref[d, 0, rows, :] = jnp.where(in_group[j], w_im, 0.0).astype(BF16)
                wor_ref[d, 0, rows, :] = jnp.where(in_group[j], car, 0.0).astype(BF16)
                woi_ref[d, 0, rows, :] = jnp.where(in_group[j], -cai, 0.0).astype(BF16)
        atr_ref[d, 0] = pw[tc][0]
        ati_ref[d, 0] = pw[tc][1]
    for j in range(2):
        toep_ref[0, j] = jnp.concatenate(toep_rows[j], axis=0).astype(BF16)


def _s5_prep(params):
    tc = S5_CHUNK
    wide = 2 * tc * S5_GROUP
    sl = 2 * S5_STATE

    def pair_lanes(v):
        return v.astype(F32).reshape(2, S5_PAIRS, 1, sl)

    def pair_rows(v):
        rows = v.shape[2]
        return v.astype(F32).reshape(2, S5_PAIRS, 2, rows, S5_STATE).transpose(0, 1, 3, 2, 4).reshape(
            2, S5_PAIRS, rows, sl)

    ldt = jnp.broadcast_to(params['s5_log_dt'].astype(F32)[:, :, None], (2, S5_GROUPS, S5_STATE))
    ins = [pair_lanes(params['s5_a_re']), pair_lanes(params['s5_a_im']), pair_lanes(ldt),
           pair_rows(params['s5_b_re'].transpose(0, 1, 3, 2)), pair_rows(params['s5_b_im'].transpose(0, 1, 3, 2)),
           pair_rows(params['s5_c_re']), pair_rows(params['s5_c_im'])]
    vec_spec = pl.BlockSpec((2, 1, 1, sl), lambda k: (0, k, 0, 0))
    mat_spec = pl.BlockSpec((2, 1, S5_GROUP, sl), lambda k: (0, k, 0, 0))
    w_spec = pl.BlockSpec((2, 1, wide, sl), lambda k: (0, k, 0, 0))
    w_sds = jax.ShapeDtypeStruct((2, S5_PAIRS, wide, sl), BF16)
    a_sds = jax.ShapeDtypeStruct((2, S5_PAIRS, 1, sl), F32)
    return pl.pallas_call(
        _s5_prep_kernel,
        grid=(S5_PAIRS,),
        in_specs=[vec_spec] * 3 + [mat_spec] * 4,
        out_specs=[pl.BlockSpec((1, 2, wide // 2, wide // 2), lambda k: (k, 0, 0, 0))] + [w_spec] * 4 + [vec_spec] * 2,
        out_shape=[jax.ShapeDtypeStruct((S5_PAIRS, 2, wide // 2, wide // 2), BF16)] + [w_sds] * 4 + [a_sds] * 2,
        compiler_params=_cparams(1),
        name="s5_prep",
    )(*ins)


S5_STEP_PAIRS = 4
SUBLANES = 8


def _s5_state_rows(batch, n_lat, n_ctx):
    def pitch(n):
        p = -(-n // SUBLANES)
        return SUBLANES * (p + 1 - p % 2)

    lat_pitch, ctx_pitch = pitch(n_lat), pitch(n_ctx)
    ctx_base = batch * lat_pitch
    return lat_pitch, ctx_pitch, ctx_base, ctx_base + batch * ctx_pitch


def _s5_kernel(ul_ref, uc_ref, dsk_ref, *refs, batch, seq, ctx):
    toep_ref, wsr, wsi, wor, woi, atr, ati = refs[:7]
    o_ref = refs[7]
    wscr, zscr, upscr, sre, sim, hre, him = refs[8:]
    tc = S5_CHUNK
    n_lat, n_ctx = seq // tc, ctx // tc
    rows_lat = n_lat * batch
    lat_pitch, ctx_pitch, ctx_base, _ = _s5_state_rows(batch, n_lat, n_ctx)
    w = tc * S5_GROUP
    lanes = S5_STEP_PAIRS * 2 * S5_GROUP

    def scatter_tiles(xt, r, col0):
        for kk in range(S5_STEP_PAIRS):
            for j in range(2):
                ch = (2 * kk + j) * S5_GROUP
                wscr[kk, j * w + r * S5_GROUP:j * w + (r + 1) * S5_GROUP, col0:col0 + lanes] = xt[ch:ch + S5_GROUP, :]

    for r in range(tc):
        for b in range(batch):
            x = ul_ref[pl.ds(b * seq + r, n_lat, stride=tc), :]
            scatter_tiles(x.T, r, b * n_lat)
        xc = jnp.concatenate([uc_ref[pl.ds(b * ctx + r, n_ctx, stride=tc), :] for b in range(batch)]
                             + [jnp.zeros((lanes - batch * n_ctx, lanes), F32)], axis=0)
        scatter_tiles(xc.T, r, rows_lat)

    chains = [(kk, d) for kk in range(S5_STEP_PAIRS) for d in range(2)]
    for kk in range(S5_STEP_PAIRS):
        upscr[kk] = wscr[kk].T.astype(BF16)
        for d in range(2):
            for dst, wst in ((sre, wsr), (sim, wsi)):
                s = jnp.dot(upscr[kk], wst[d, kk], preferred_element_type=F32)
                for b in range(batch):
                    dst[kk, d, b * lat_pitch:b * lat_pitch + n_lat, :] = s[b * n_lat:(b + 1) * n_lat]
                    dst[kk, d, ctx_base + b * ctx_pitch:ctx_base + b * ctx_pitch + n_ctx, :] = (
                        s[rows_lat + b * n_ctx:rows_lat + (b + 1) * n_ctx])

    ctx_rows = [pl.ds(ctx_base + c, batch, stride=ctx_pitch) for c in range(n_ctx)]
    lat_rows = [pl.ds(c, batch, stride=lat_pitch) for c in range(n_lat)]
    order = (ctx_rows + lat_rows, ctx_rows[::-1] + lat_rows[::-1])
    coef = {(kk, d): (atr[d, kk], ati[d, kk]) for kk, d in chains}
    state = {ch: (jnp.zeros((batch, 2 * S5_STATE), F32), jnp.zeros((batch, 2 * S5_STATE), F32)) for ch in chains}
    for t in range(n_ctx + n_lat):
        for kk, d in chains:
            rows = order[d][t]
            (ar, ai), (h_re, h_im) = coef[kk, d], state[kk, d]
            hre[kk, d, rows, :] = h_re
            him[kk, d, rows, :] = h_im
            state[kk, d] = (ar * h_re - ai * h_im + sre[kk, d, rows, :],
                            ar * h_im + ai * h_re + sim[kk, d, rows, :])

    for kk in range(S5_STEP_PAIRS):
        ul = upscr[kk, :rows_lat, :]
        y = ul.astype(F32) * dsk_ref[kk]
        y = y + jnp.concatenate(
            [jnp.dot(ul[:, :w], toep_ref[kk, 0], preferred_element_type=F32),
             jnp.dot(ul[:, w:], toep_ref[kk, 1], preferred_element_type=F32)], axis=1)
        nt = (((1,), (1,)), ((), ()))
        for d in range(2):
            h_r = jnp.concatenate([hre[kk, d, b * lat_pitch:b * lat_pitch + n_lat, :] for b in range(batch)], axis=0)
            h_i = jnp.concatenate([him[kk, d, b * lat_pitch:b * lat_pitch + n_lat, :] for b in range(batch)], axis=0)
            y = y + lax.dot_general(h_r.astype(BF16), wor[d, kk], nt, preferred_element_type=F32)
            y = y + lax.dot_general(h_i.astype(BF16), woi[d, kk], nt, preferred_element_type=F32)
        yt = y.T
        for b in range(batch):
            for s in range(tc):
                for j in range(2):
                    ch = (2 * kk + j) * S5_GROUP
                    zscr[b * tc + s, ch:ch + S5_GROUP, :] = yt[j * w + s * S5_GROUP:j * w + (s + 1) * S5_GROUP,
                                                               b * n_lat:(b + 1) * n_lat]

    for b in range(batch):
        for s in range(tc):
            o_ref[pl.ds(b * seq + s, n_lat, stride=tc), :] = zscr[b * tc + s].T


def _s5_mixer(u_lat, u_ctx, params, batch, seq, ctx):
    tc = S5_CHUNK
    n_lat, n_ctx = seq // tc, ctx // tc
    sp = S5_STEP_PAIRS
    lanes = sp * 2 * S5_GROUP
    assert n_lat == lanes and batch * n_ctx <= lanes
    rows_lat = n_lat * batch
    rows_all = rows_lat + lanes
    state_rows = _s5_state_rows(batch, n_lat, n_ctx)[3]
    wide = 2 * tc * S5_GROUP
    dsk = jnp.broadcast_to(params['s5_d'].astype(F32).reshape(S5_PAIRS, 2, 1, S5_GROUP),
                           (S5_PAIRS, 2, tc, S5_GROUP)).reshape(S5_PAIRS, 1, wide)
    weights = _s5_prep(params)
    w_spec = pl.BlockSpec((2, sp, wide, 2 * S5_STATE), lambda k: (0, k, 0, 0))
    a_spec = pl.BlockSpec((2, sp, 1, 2 * S5_STATE), lambda k: (0, k, 0, 0))
    specs = [pl.BlockSpec((sp, 2, wide // 2, wide // 2), lambda k: (k, 0, 0, 0))] + [w_spec] * 4 + [a_spec] * 2
    return pl.pallas_call(
        functools.partial(_s5_kernel, batch=batch, seq=seq, ctx=ctx),
        grid=(S5_PAIRS // sp,),
        in_specs=[pl.BlockSpec((batch * seq, lanes), lambda k: (0, k)),
                  pl.BlockSpec((batch * ctx, lanes), lambda k: (0, k)),
                  pl.BlockSpec((sp, 1, wide), lambda k: (k, 0, 0))] + specs,
        out_specs=pl.BlockSpec((batch * seq, lanes), lambda k: (0, k)),
        out_shape=jax.ShapeDtypeStruct((batch * seq, S5_WIDTH), F32),
        scratch_shapes=[pltpu.VMEM((sp, wide, rows_all), F32),
                        pltpu.VMEM((batch * tc, lanes, n_lat), F32),
                        pltpu.VMEM((sp, rows_all, wide), BF16)]
                       + [pltpu.VMEM((sp, 2, state_rows, 2 * S5_STATE), F32) for _ in range(4)],
        compiler_params=_cparams(1),
        name="s5",
    )(u_lat, u_ctx, dsk, *weights)


def _merge_kernel(y5_ref, o_ref_in, gs_ref, gm_ref, wa_ref, wb_ref, wm_ref, out_ref, wa_s, wb_s, wm_s):
    @pl.when(pl.program_id(1) == 0)
    def _():
        wa_s[...] = wa_ref[...].astype(BF16)
        wb_s[...] = wb_ref[...].astype(BF16)
        wm_s[...] = wm_ref[...].astype(BF16)

    z = jax.nn.gelu(y5_ref[...].astype(F32)).astype(BF16)
    a = jnp.dot(z, wa_s[...], preferred_element_type=F32)
    b = jnp.dot(z, wb_s[...], preferred_element_type=F32)
    mla = jnp.dot(o_ref_in[...], wm_s[...], preferred_element_type=F32)
    merged = gs_ref[...].astype(F32) * (a * jax.nn.sigmoid(b)) + gm_ref[...].astype(F32) * mla
    out_ref[...] = merged.astype(out_ref.dtype)


def _merge(y5, o_mla, gates, w_glu, w_mla_o, tm, tn):
    t = y5.shape[0]
    nj = D_MODEL // tn
    return pl.pallas_call(
        _merge_kernel,
        grid=(nj, t // tm),
        in_specs=[pl.BlockSpec((tm, S5_WIDTH), lambda j, i: (i, 0)),
                  pl.BlockSpec((tm, MLA_HEADS * V_DIM), lambda j, i: (i, 0)),
                  pl.BlockSpec((tm, tn), lambda j, i: (i, j)),
                  pl.BlockSpec((tm, tn), lambda j, i: (i, nj + j)),
                  pl.BlockSpec((S5_WIDTH, tn), lambda j, i: (0, j)),
                  pl.BlockSpec((S5_WIDTH, tn), lambda j, i: (0, nj + j)),
                  pl.BlockSpec((MLA_HEADS * V_DIM, tn), lambda j, i: (0, j))],
        out_specs=pl.BlockSpec((tm, tn), lambda j, i: (i, j)),
        out_shape=jax.ShapeDtypeStruct((t, D_MODEL), BF16),
        scratch_shapes=[pltpu.VMEM((S5_WIDTH, tn), BF16), pltpu.VMEM((S5_WIDTH, tn), BF16),
                        pltpu.VMEM((MLA_HEADS * V_DIM, tn), BF16)],
        compiler_params=_cparams(2),
        name="merge",
    )(y5, o_mla, gates, gates, w_glu, w_glu, w_mla_o)


def _resid_mm(a, w, x_res, gate, rows_per_batch, tm, tn, name):
    def epi(accs, e_refs, o_refs):
        o_refs[0][...] = e_refs[0][...] + e_refs[1][0] * accs[0]

    t = a.shape[0]
    n = w.shape[1]
    tpb = rows_per_batch // tm
    extras = [(x_res, pl.BlockSpec((tm, tn), lambda j, i: (i, j))),
              (gate, pl.BlockSpec((1, 1, tn), lambda j, i: (i // tpb, 0, j)))]
    outs = [(jax.ShapeDtypeStruct((t, n), F32), pl.BlockSpec((tm, tn), lambda j, i: (i, j)))]
    return _fused_mm(a, [(w, 0)], epi, extras, outs, tm=tm, tn=tn, nj=n // tn, name=name)[0]


def _ffn_in(xn, w_ffn_in, tm, tn):
    def epi(accs, e_refs, o_refs):
        o_refs[0][...] = (jax.nn.silu(accs[0]) * accs[1]).astype(BF16)

    t = xn.shape[0]
    nj = D_FF // tn
    outs = [(jax.ShapeDtypeStruct((t, D_FF), BF16), pl.BlockSpec((tm, tn), lambda j, i: (i, j)))]
    return _fused_mm(xn, [(w_ffn_in, 0), (w_ffn_in, nj)], epi, [], outs, tm=tm, tn=tn, nj=nj, name="ffn_in")[0]


def _final_norm_kernel(x_ref, g_ref, o_ref):
    o_ref[...] = _rms(x_ref[...], g_ref[...])


def _final_norm(x2d, g, tm):
    t = x2d.shape[0]
    return pl.pallas_call(
        _final_norm_kernel,
        grid=(t // tm,),
        in_specs=[pl.BlockSpec((tm, D_MODEL), lambda i: (i, 0)),
                  pl.BlockSpec((1, D_MODEL), lambda i: (0, 0))],
        out_specs=pl.BlockSpec((tm, D_MODEL), lambda i: (i, 0)),
        out_shape=jax.ShapeDtypeStruct((t, D_MODEL), F32),
        compiler_params=_cparams(1),
        name="final_norm",
    )(x2d, g.reshape(1, D_MODEL))


def _rope_rot_cols(w):
    k = w.shape[0]
    ws = w.reshape(k, -1, 2, 2, QK_ROPE // 4)
    return jnp.stack([-ws[:, :, :, 1, :], ws[:, :, :, 0, :]], axis=3).reshape(k, -1)


def _rope_tables(n_tokens):
    rows = n_tokens // GRID_W
    row = jnp.repeat(jnp.arange(rows, dtype=F32), GRID_W)
    col = jnp.tile(jnp.arange(GRID_W, dtype=F32), rows)
    n_freq = QK_ROPE // 4
    inv = ROPE_BASE ** (-jnp.arange(n_freq, dtype=F32) / n_freq)
    ang = jnp.stack([row[:, None] * inv, col[:, None] * inv], axis=1)
    cos = jnp.broadcast_to(jnp.cos(ang)[:, :, None, :], (n_tokens, 2, 2, n_freq)).reshape(n_tokens, QK_ROPE)
    sin = jnp.broadcast_to(jnp.sin(ang)[:, :, None, :], (n_tokens, 2, 2, n_freq)).reshape(n_tokens, QK_ROPE)
    return cos, sin


def kernel(x, c, ctx, c_ctx, w_mod, b_mod, norm1, norm2, w_in, s5_a_re, s5_a_im, s5_log_dt, s5_b_re, s5_b_im,
           s5_c_re, s5_c_im, s5_d, w_glu, q_norm, kv_norm, w_uq, w_ukv, w_mla_o, w_out, w_ffn_in, w_ffn_out,
           norm_f):
    batch, seq, _ = x.shape
    n_ctx = ctx.shape[1]
    assert w_mod.shape[0] == 1, "single-layer block"
    p = dict(s5_a_re=s5_a_re[0], s5_a_im=s5_a_im[0], s5_log_dt=s5_log_dt[0], s5_b_re=s5_b_re[0],
             s5_b_im=s5_b_im[0], s5_c_re=s5_c_re[0], s5_c_im=s5_c_im[0], s5_d=s5_d[0])
    w_in0 = w_in[0]

    cv = jnp.concatenate([c, c_ctx[None], jnp.zeros((8 - batch - 1, D_MODEL), F32)], axis=0)
    m = _modulation(cv, w_mod[0], b_mod[0]).reshape(8, 6, 1, D_MODEL)
    m_lat = m[:batch]
    m_ctx = m[batch:batch + 1]

    x2d = x.reshape(batch * seq, D_MODEL)
    c2d = ctx.reshape(batch * n_ctx, D_MODEL)
    xn = _norm_mod(x2d, norm1[0], m_lat[:, 1], m_lat[:, 0], seq, 512)
    cn = _norm_mod(c2d, norm1[0], m_ctx[:, 1], m_ctx[:, 0], batch * n_ctx, 256)

    kv_lo = S5_WIDTH + Q_RANK
    w_kr = w_in0[:, kv_lo + KV_RANK:kv_lo + KV_RANK + QK_ROPE]
    w_kv1 = jnp.concatenate([w_in0[:, kv_lo:kv_lo + KV_RANK], w_kr, _rope_rot_cols(w_kr)], axis=1)
    w_gate = w_in0[:, kv_lo + KV_RANK + QK_ROPE:].astype(BF16)
    wq = w_uq[0].reshape(Q_RANK, MLA_HEADS, QK_NOPE + QK_ROPE)
    wq_rope = wq[:, :, QK_NOPE:].reshape(Q_RANK, MLA_HEADS * QK_ROPE)
    wq2 = jnp.concatenate([wq[:, :, :QK_NOPE].reshape(Q_RANK, MLA_HEADS * QK_NOPE), wq_rope,
                           _rope_rot_cols(wq_rope)], axis=1).astype(BF16)
    w_ukv_bf = w_ukv[0].astype(BF16)
    cos, sin = _rope_tables(seq)
    cos_q, sin_q = jnp.tile(cos, (1, MLA_HEADS)), jnp.tile(sin, (1, MLA_HEADS))

    u_lat = _proj_cast(xn, w_in0, 0, S5_WIDTH, 1024, 512, "u_lat", out_dtype=F32)
    u_ctx = _proj_cast(cn, w_in0, 0, S5_WIDTH, 256, 512, "u_ctx", out_dtype=F32)
    q = _q_path(xn, w_in0, q_norm[0], wq2, cos_q, sin_q, 1024, seq)
    kv_lat, kr_lat = _kv_path(xn, w_kv1, kv_norm[0], w_ukv_bf, cos, sin, 1024, seq)
    kv_ctx, kr_ctx = _kv_path(cn, w_kv1, kv_norm[0], w_ukv_bf, None, None, 256, n_ctx)
    gates = _proj_cast(xn, w_gate, 0, 2 * D_MODEL, 1024, 512, "gates", act=jax.nn.sigmoid)

    y5 = _s5_mixer(u_lat, u_ctx, p, batch, seq, n_ctx)
    o_mla = _attention(q, kv_lat, kv_ctx, kr_lat, kr_ctx, batch, seq, n_ctx, 512)

    merged = _merge(y5, o_mla, gates, w_glu[0], w_mla_o[0], 1024, 512)
    x1 = _resid_mm(merged, w_out[0], x2d, m_lat[:, 2], seq, 1024, 512, "out_proj")
    xn2 = _norm_mod(x1, norm2[0], m_lat[:, 4], m_lat[:, 3], seq, 512)
    hid = _ffn_in(xn2, w_ffn_in[0], 1024, 512)
    x2 = _resid_mm(hid, w_ffn_out[0], x1, m_lat[:, 5], seq, 512, 512, "ffn_out")
    return _final_norm(x2, norm_f, 512).reshape(batch, seq, D_MODEL)
```

```python
import functools
import math

import jax
import jax.numpy as jnp
from jax import lax
from jax.experimental import pallas as pl
from jax.experimental.pallas import tpu as pltpu

F32 = jnp.float32
BF16 = jnp.bfloat16

D_MODEL = 2048
GRID_W = 64
EPS = 1e-6
S5_WIDTH = D_MODEL // 2
S5_GROUP = 16
S5_GROUPS = S5_WIDTH // S5_GROUP
S5_STATE = 64
S5_CHUNK = 16
S5_PAIRS = S5_GROUPS // 2
MLA_HEADS = 8
QK_NOPE = 128
QK_ROPE = 64
V_DIM = 128
Q_RANK = 512
KV_RANK = 256
ROPE_BASE = 10000.0
ATTN_SCALE = (QK_NOPE + QK_ROPE) ** -0.5
D_FF = -(-8 * D_MODEL // (3 * 256)) * 256

VMEM_LIMIT_BYTES = 56 * 1024 * 1024


def _cparams(n_axes):
    return pltpu.CompilerParams(dimension_semantics=("arbitrary",) * n_axes,
                                vmem_limit_bytes=VMEM_LIMIT_BYTES)


def _rms(x, g):
    return x * lax.rsqrt(jnp.mean(x * x, axis=-1, keepdims=True) + EPS) * g


def _mod_kernel(cv_ref, w_ref, b_ref, o_ref):
    s = jax.nn.silu(cv_ref[...]).astype(BF16)
    o_ref[...] = jnp.dot(s, w_ref[...].astype(BF16), preferred_element_type=F32) + b_ref[...]


def _modulation(cv, w_mod, b_mod):
    n = w_mod.shape[1]
    tn = 1536
    return pl.pallas_call(
        _mod_kernel,
        grid=(n // tn,),
        in_specs=[pl.BlockSpec((8, D_MODEL), lambda j: (0, 0)),
                  pl.BlockSpec((D_MODEL, tn), lambda j: (0, j)),
                  pl.BlockSpec((1, tn), lambda j: (0, j))],
        out_specs=pl.BlockSpec((8, tn), lambda j: (0, j)),
        out_shape=jax.ShapeDtypeStruct((8, n), F32),
        compiler_params=_cparams(1),
        name="mod",
    )(cv, w_mod, b_mod.reshape(1, n))


def _norm_mod_kernel(x_ref, g_ref, sc_ref, sh_ref, o_ref):
    y = _rms(x_ref[...], g_ref[...])
    o_ref[...] = (y * (1.0 + sc_ref[0]) + sh_ref[0]).astype(o_ref.dtype)


def _norm_mod(x2d, gain, scale, shift, rows_per_batch, tm, out_dtype=BF16):
    t = x2d.shape[0]
    tpb = rows_per_batch // tm
    return pl.pallas_call(
        _norm_mod_kernel,
        grid=(t // tm,),
        in_specs=[pl.BlockSpec((tm, D_MODEL), lambda i: (i, 0)),
                  pl.BlockSpec((1, D_MODEL), lambda i: (0, 0)),
                  pl.BlockSpec((1, 1, D_MODEL), lambda i: (i // tpb, 0, 0)),
                  pl.BlockSpec((1, 1, D_MODEL), lambda i: (i // tpb, 0, 0))],
        out_specs=pl.BlockSpec((tm, D_MODEL), lambda i: (i, 0)),
        out_shape=jax.ShapeDtypeStruct((t, D_MODEL), out_dtype),
        compiler_params=_cparams(1),
        name="norm_mod",
    )(x2d, gain.reshape(1, D_MODEL), scale, shift)


def _fused_mm(a, weights, epilogue, extras, outs, *, tm, tn, nj, name):
    t, k = a.shape
    ni = t // tm
    nw, ne, no = len(weights), len(extras), len(outs)
    need_cast = [w.dtype != BF16 for w, _ in weights]

    def kernel(*refs):
        a_ref = refs[0]
        w_refs = refs[1:1 + nw]
        e_refs = refs[1 + nw:1 + nw + ne]
        o_refs = refs[1 + nw + ne:1 + nw + ne + no]
        s_refs = list(refs[1 + nw + ne + no:])
        wb = []
        for idx in range(nw):
            if need_cast[idx]:
                s_ref = s_refs.pop(0)

                @pl.when(pl.program_id(1) == 0)
                def _(w_ref=w_refs[idx], s_ref=s_ref):
                    s_ref[...] = w_ref[...].astype(BF16)

                wb.append(s_ref)
            else:
                wb.append(w_refs[idx])
        av = a_ref[...]
        accs = [jnp.dot(av, w[...], preferred_element_type=F32) for w in wb]
        epilogue(accs, e_refs, o_refs)

    in_specs = [pl.BlockSpec((tm, k), lambda j, i: (i, 0))]
    for _, off in weights:
        in_specs.append(pl.BlockSpec((k, tn), functools.partial(lambda j, i, off: (0, off + j), off=off)))
    in_specs += [spec for _, spec in extras]
    scratch = [pltpu.VMEM((k, tn), BF16) for c in need_cast if c]
    res = pl.pallas_call(
        kernel,
        grid=(nj, ni),
        in_specs=in_specs,
        out_specs=[spec for _, spec in outs],
        out_shape=[sds for sds, _ in outs],
        scratch_shapes=scratch,
        compiler_params=_cparams(2),
        name=name,
    )(a, *[w for w, _ in weights], *[e for e, _ in extras])
    return res


def _proj_cast(xn, w, off, n, tm, tn, name, act=None, out_dtype=BF16):
    def epi(accs, e_refs, o_refs):
        v = accs[0]
        if act is not None:
            v = act(v)
        o_refs[0][...] = v.astype(out_dtype)

    t = xn.shape[0]
    return _fused_mm(xn, [(w, off)], epi, [],
                     [(jax.ShapeDtypeStruct((t, n), out_dtype), pl.BlockSpec((tm, tn), lambda j, i: (i, j)))],
                     tm=tm, tn=tn, nj=n // tn, name=name)[0]


GATE_W_BLOCK = 256


def _gates(xn, w_in, col0, tm, tn):
    t, k = xn.shape
    n = 2 * D_MODEL
    blk = GATE_W_BLOCK
    base_blk, shift = divmod(col0, blk)
    per_tile = tn // blk
    nb = per_tile + 1

    def kernel(a_ref, *refs):
        w_refs, o_ref, s_ref = refs[:nb], refs[nb], refs[nb + 1]

        @pl.when(pl.program_id(1) == 0)
        def _():
            window = jnp.concatenate([w[...] for w in w_refs], axis=1)
            s_ref[...] = window[:, shift:shift + tn].astype(BF16)

        acc = jnp.dot(a_ref[...], s_ref[...], preferred_element_type=F32)
        o_ref[...] = jax.nn.sigmoid(acc).astype(BF16)

    w_specs = [pl.BlockSpec((k, blk), functools.partial(lambda j, i, m: (0, base_blk + j * per_tile + m), m=m))
               for m in range(nb)]
    return pl.pallas_call(
        kernel,
        grid=(n // tn, t // tm),
        in_specs=[pl.BlockSpec((tm, k), lambda j, i: (i, 0))] + w_specs,
        out_specs=pl.BlockSpec((tm, tn), lambda j, i: (i, j)),
        out_shape=jax.ShapeDtypeStruct((t, n), BF16),
        scratch_shapes=[pltpu.VMEM((k, tn), BF16)],
        compiler_params=_cparams(2),
        name="gates",
    )(xn, *([w_in] * nb))


def _q_path(xn, w_in, q_norm, wq2, cos_q, sin_q, tm, seq):
    nr = MLA_HEADS * QK_ROPE
    nn = MLA_HEADS * QK_NOPE

    def epi(accs, e_refs, o_refs):
        qn_ref, w2_ref, cos_ref, sin_ref = e_refs
        cq = _rms(accs[0], qn_ref[...]).astype(BF16)
        q = jnp.dot(cq, w2_ref[...], preferred_element_type=F32)
        rope = q[:, nn:nn + nr] * cos_ref[...] + q[:, nn + nr:] * sin_ref[...]
        o_refs[0][:, :nn] = (q[:, :nn] * ATTN_SCALE).astype(BF16)
        o_refs[0][:, nn:] = (rope * ATTN_SCALE).astype(BF16)

    t = xn.shape[0]
    pos_tiles = seq // tm
    extras = [(q_norm.reshape(1, Q_RANK), pl.BlockSpec((1, Q_RANK), lambda j, i: (0, 0))),
              (wq2, pl.BlockSpec(wq2.shape, lambda j, i: (0, 0))),
              (cos_q, pl.BlockSpec((tm, nr), lambda j, i: (i % pos_tiles, 0))),
              (sin_q, pl.BlockSpec((tm, nr), lambda j, i: (i % pos_tiles, 0)))]
    outs = [(jax.ShapeDtypeStruct((t, nn + nr), BF16), pl.BlockSpec((tm, nn + nr), lambda j, i: (i, 0)))]
    return _fused_mm(xn, [(w_in, S5_WIDTH // Q_RANK)], epi, extras, outs,
                     tm=tm, tn=Q_RANK, nj=1, name="q_path")[0]


def _kv_path(xn, w_kv1, kv_norm, w_ukv_bf, cos_k, sin_k, tm, seq):
    nkv = w_ukv_bf.shape[1]
    rope = cos_k is not None

    def epi(accs, e_refs, o_refs):
        acc = accs[0]
        ckv = _rms(acc[:, :KV_RANK], e_refs[0][...]).astype(BF16)
        o_refs[0][...] = jnp.dot(ckv, e_refs[1][...], preferred_element_type=F32).astype(BF16)
        kr = acc[:, KV_RANK:KV_RANK + QK_ROPE]
        if rope:
            kr = kr * e_refs[2][...] + acc[:, KV_RANK + QK_ROPE:] * e_refs[3][...]
        o_refs[1][...] = kr.astype(BF16)

    t = xn.shape[0]
    pos_tiles = seq // tm
    extras = [(kv_norm.reshape(1, KV_RANK), pl.BlockSpec((1, KV_RANK), lambda j, i: (0, 0))),
              (w_ukv_bf, pl.BlockSpec(w_ukv_bf.shape, lambda j, i: (0, 0)))]
    if rope:
        extras += [(cos_k, pl.BlockSpec((tm, QK_ROPE), lambda j, i: (i % pos_tiles, 0))),
                   (sin_k, pl.BlockSpec((tm, QK_ROPE), lambda j, i: (i % pos_tiles, 0)))]
    outs = [(jax.ShapeDtypeStruct((t, nkv), BF16), pl.BlockSpec((tm, nkv), lambda j, i: (i, 0))),
            (jax.ShapeDtypeStruct((t, QK_ROPE), BF16), pl.BlockSpec((tm, QK_ROPE), lambda j, i: (i, 0)))]
    return _fused_mm(xn, [(w_kv1, 0)], epi, extras, outs,
                     tm=tm, tn=w_kv1.shape[1], nj=1, name="kv_path")


ATTN_SUB_ROWS = 512


def _attn_kernel(qn_ref, qr_ref, kvl_ref, kvc_ref, krl_ref, krc_ref, o_ref, k_scr, v_scr, *, seq, ctx):
    dk = QK_NOPE + QK_ROPE

    @pl.when(pl.program_id(2) == 0)
    def _():
        for h in range(2):
            base = h * (QK_NOPE + V_DIM)
            k_scr[h, :seq, :QK_NOPE] = kvl_ref[:, base:base + QK_NOPE]
            k_scr[h, seq:, :QK_NOPE] = kvc_ref[:, base:base + QK_NOPE]
            k_scr[h, :seq, QK_NOPE:dk] = krl_ref[...]
            k_scr[h, seq:, QK_NOPE:dk] = krc_ref[...]
            v_scr[h, :seq, :V_DIM] = kvl_ref[:, base + QK_NOPE:base + QK_NOPE + V_DIM]
            v_scr[h, seq:, :V_DIM] = kvc_ref[:, base + QK_NOPE:base + QK_NOPE + V_DIM]
            ones_col = lax.broadcasted_iota(jnp.int32, (seq + ctx, V_DIM), 1) == 0
            v_scr[h, :, V_DIM:] = jnp.where(ones_col, 1.0, 0.0).astype(BF16)

    for r0 in range(0, qn_ref.shape[0], ATTN_SUB_ROWS):
        rows = slice(r0, r0 + ATTN_SUB_ROWS)
        for h in range(2):
            q = jnp.concatenate([qn_ref[rows, h * QK_NOPE:(h + 1) * QK_NOPE],
                                 qr_ref[rows, h * QK_ROPE:(h + 1) * QK_ROPE]], axis=1)
            s = lax.dot_general(q, k_scr[h], (((1,), (1,)), ((), ())), preferred_element_type=F32)
            m = jnp.max(s, axis=-1, keepdims=True)
            p = jnp.exp((s - m).astype(BF16))
            ol = jnp.dot(p, v_scr[h], preferred_element_type=F32)
            o_ref[rows, h * V_DIM:(h + 1) * V_DIM] = (ol[:, :V_DIM] / ol[:, V_DIM:V_DIM + 1]).astype(o_ref.dtype)


def _attention(q, kv_lat, kv_ctx, kr_lat, kr_ctx, batch, seq, ctx, tq):
    nq = seq // tq
    nn_blocks = MLA_HEADS * QK_NOPE // (2 * QK_NOPE)
    dk = QK_NOPE + QK_ROPE
    hw = 2 * (QK_NOPE + V_DIM)
    return pl.pallas_call(
        functools.partial(_attn_kernel, seq=seq, ctx=ctx),
        grid=(batch, MLA_HEADS // 2, nq),
        in_specs=[pl.BlockSpec((tq, 2 * QK_NOPE), lambda b, hp, qi: (b * nq + qi, hp)),
                  pl.BlockSpec((tq, 2 * QK_ROPE), lambda b, hp, qi: (b * nq + qi, 2 * nn_blocks + hp)),
                  pl.BlockSpec((seq, hw), lambda b, hp, qi: (b, hp)),
                  pl.BlockSpec((ctx, hw), lambda b, hp, qi: (b, hp)),
                  pl.BlockSpec((seq, QK_ROPE), lambda b, hp, qi: (b, 0)),
                  pl.BlockSpec((ctx, QK_ROPE), lambda b, hp, qi: (b, 0))],
        out_specs=pl.BlockSpec((tq, 2 * V_DIM), lambda b, hp, qi: (b * nq + qi, hp)),
        out_shape=jax.ShapeDtypeStruct((batch * seq, MLA_HEADS * V_DIM), BF16),
        scratch_shapes=[pltpu.VMEM((2, seq + ctx, dk), BF16), pltpu.VMEM((2, seq + ctx, 2 * V_DIM), BF16)],
        compiler_params=_cparams(3),
        name="attention",
    )(q, q, kv_lat, kv_ctx, kr_lat, kr_ctx)


def _s5_prep_kernel(are_ref, aim_ref, ldt_ref, bre_ref, bim_ref, cre_ref, cim_ref,
                    toep_ref, wsr_ref, wsi_ref, wor_ref, woi_ref, atr_ref, ati_ref):
    tc, g = S5_CHUNK, S5_GROUP
    w = tc * g
    hp = lax.Precision.HIGHEST
    lane = lax.broadcasted_iota(jnp.int32, (g, 2 * S5_STATE), 1)
    in_group = (lane < S5_STATE, lane >= S5_STATE)
    lane_w = lax.broadcasted_iota(jnp.int32, (g, w), 1)
    nt = (((1,), (1,)), ((), ()))
    toep_rows = [[jnp.zeros((g, w), F32) for _ in range(tc)] for _ in range(2)]
    for d in range(2):
        lr, li = are_ref[d, 0], aim_ref[d, 0]
        dt = jnp.exp(ldt_ref[d, 0])
        mag = jnp.exp(lr * dt)
        ab_re, ab_im = mag * jnp.cos(li * dt), mag * jnp.sin(li * dt)
        den = lr * lr + li * li
        nr, ni = ab_re - 1.0, ab_im
        co_re = (nr * lr + ni * li) / den
        co_im = (ni * lr - nr * li) / den
        br, bi = bre_ref[d, 0], bim_ref[d, 0]
        bb_re = co_re * br - co_im * bi
        bb_im = co_re * bi + co_im * br
        pw = [(jnp.ones_like(ab_re), jnp.zeros_like(ab_re))]
        for _ in range(tc):
            pr, pi = pw[-1]
            pw.append((pr * ab_re - pi * ab_im, pr * ab_im + pi * ab_re))
        cr, ci = cre_ref[d, 0], cim_ref[d, 0]
        ca = [(cr * pr - ci * pi, cr * pi + ci * pr) for pr, pi in pw]
        taus = list(range(tc))[::-1] if d else list(range(tc))
        y_re = jnp.concatenate([ca[t][0] for t in taus], axis=0)
        y_im = jnp.concatenate([ca[t][1] for t in taus], axis=0)
        for j in range(2):
            x_re = jnp.where(in_group[j], bb_re, 0.0)
            x_im = jnp.where(in_group[j], bb_im, 0.0)
            kt = (lax.dot_general(x_re, y_re, nt, precision=hp, preferred_element_type=F32)
                  - lax.dot_general(x_im, y_im, nt, precision=hp, preferred_element_type=F32))
            for r in range(tc):
                sh = (r + 1) * g if d else r * g
                blk = pltpu.roll(kt, sh % w, 1) if sh % w else kt
                keep = (lane_w < sh) if d else (lane_w >= sh)
                toep_rows[j][r] = toep_rows[j][r] + jnp.where(keep, blk, 0.0)
        for r in range(tc):
            pr, pi = pw[r] if d else pw[tc - 1 - r]
            w_re = bb_re * pr - bb_im * pi
            w_im = bb_re * pi + bb_im * pr
            car, cai = ca[tc - r] if d else ca[r + 1]
            for j in range(2):
                rows = slice(j * w + r * g, j * w + (r + 1) * g)
                wsr_ref[d, 0, rows, :] = jnp.where(in_group[j], w_re, 0.0).astype(BF16)
                wsi_ref[d, 0, rows, :] = jnp.where(in_group[j], w_im, 0.0).astype(BF16)
                wor_ref[d, 0, rows, :] = jnp.where(in_group[j], car, 0.0).astype(BF16)
                woi_ref[d, 0, rows, :] = jnp.where(in_group[j], -cai, 0.0).astype(BF16)
        atr_ref[d, 0] = pw[tc][0]
        ati_ref[d, 0] = pw[tc][1]
    for j in range(2):
        toep_ref[0, j] = jnp.concatenate(toep_rows[j], axis=0).astype(BF16)


def _s5_prep(params):
    tc = S5_CHUNK
    wide = 2 * tc * S5_GROUP
    sl = 2 * S5_STATE

    def pair_lanes(v):
        return v.astype(F32).reshape(2, S5_PAIRS, 1, sl)

    def pair_rows(v):
        rows = v.shape[2]
        return v.astype(F32).reshape(2, S5_PAIRS, 2, rows, S5_STATE).transpose(0, 1, 3, 2, 4).reshape(
            2, S5_PAIRS, rows, sl)

    ldt = jnp.broadcast_to(params['s5_log_dt'].astype(F32)[:, :, None], (2, S5_GROUPS, S5_STATE))
    ins = [pair_lanes(params['s5_a_re']), pair_lanes(params['s5_a_im']), pair_lanes(ldt),
           pair_rows(params['s5_b_re'].transpose(0, 1, 3, 2)), pair_rows(params['s5_b_im'].transpose(0, 1, 3, 2)),
           pair_rows(params['s5_c_re']), pair_rows(params['s5_c_im'])]
    vec_spec = pl.BlockSpec((2, 1, 1, sl), lambda k: (0, k, 0, 0))
    mat_spec = pl.BlockSpec((2, 1, S5_GROUP, sl), lambda k: (0, k, 0, 0))
    w_spec = pl.BlockSpec((2, 1, wide, sl), lambda k: (0, k, 0, 0))
    w_sds = jax.ShapeDtypeStruct((2, S5_PAIRS, wide, sl), BF16)
    a_sds = jax.ShapeDtypeStruct((2, S5_PAIRS, 1, sl), F32)
    return pl.pallas_call(
        _s5_prep_kernel,
        grid=(S5_PAIRS,),
        in_specs=[vec_spec] * 3 + [mat_spec] * 4,
        out_specs=[pl.BlockSpec((1, 2, wide // 2, wide // 2), lambda k: (k, 0, 0, 0))] + [w_spec] * 4 + [vec_spec] * 2,
        out_shape=[jax.ShapeDtypeStruct((S5_PAIRS, 2, wide // 2, wide // 2), BF16)] + [w_sds] * 4 + [a_sds] * 2,
        compiler_params=_cparams(1),
        name="s5_prep",
    )(*ins)


S5_STEP_PAIRS = 4
SUBLANES = 8


def _s5_state_rows(batch, n_lat, n_ctx):
    def pitch(n):
        p = -(-n // SUBLANES)
        return SUBLANES * (p + 1 - p % 2)

    lat_pitch, ctx_pitch = pitch(n_lat), pitch(n_ctx)
    ctx_base = batch * lat_pitch
    return lat_pitch, ctx_pitch, ctx_base, ctx_base + batch * ctx_pitch


def _s5_kernel(ul_ref, uc_ref, dsk_ref, *refs, batch, seq, ctx):
    toep_ref, wsr, wsi, wor, woi, atr, ati = refs[:7]
    o_ref = refs[7]
    wscr, zscr, upscr, sre, sim, hre, him = refs[8:]
    tc = S5_CHUNK
    n_lat, n_ctx = seq // tc, ctx // tc
    rows_lat = n_lat * batch
    lat_pitch, ctx_pitch, ctx_base, _ = _s5_state_rows(batch, n_lat, n_ctx)
    w = tc * S5_GROUP
    lanes = S5_STEP_PAIRS * 2 * S5_GROUP

    def scatter_tiles(xt, r, col0):
        for kk in range(S5_STEP_PAIRS):
            for j in range(2):
                ch = (2 * kk + j) * S5_GROUP
                wscr[kk, j * w + r * S5_GROUP:j * w + (r + 1) * S5_GROUP, col0:col0 + lanes] = xt[ch:ch + S5_GROUP, :]

    for r in range(tc):
        for b in range(batch):
            x = ul_ref[pl.ds(b * seq + r, n_lat, stride=tc), :]
            scatter_tiles(x.T, r, b * n_lat)
        xc = jnp.concatenate([uc_ref[pl.ds(b * ctx + r, n_ctx, stride=tc), :] for b in range(batch)]
                             + [jnp.zeros((lanes - batch * n_ctx, lanes), F32)], axis=0)
        scatter_tiles(xc.T, r, rows_lat)

    chains = [(kk, d) for kk in range(S5_STEP_PAIRS) for d in range(2)]
    for kk in range(S5_STEP_PAIRS):
        upscr[kk] = wscr[kk].T.astype(BF16)
        for d in range(2):
            for dst, wst in ((sre, wsr), (sim, wsi)):
                s = jnp.dot(upscr[kk], wst[d, kk], preferred_element_type=F32)
                for b in range(batch):
                    dst[kk, d, b * lat_pitch:b * lat_pitch + n_lat, :] = s[b * n_lat:(b + 1) * n_lat]
                    dst[kk, d, ctx_base + b * ctx_pitch:ctx_base + b * ctx_pitch + n_ctx, :] = (
                        s[rows_lat + b * n_ctx:rows_lat + (b + 1) * n_ctx])

    ctx_rows = [pl.ds(ctx_base + c, batch, stride=ctx_pitch) for c in range(n_ctx)]
    lat_rows = [pl.ds(c, batch, stride=lat_pitch) for c in range(n_lat)]
    order = (ctx_rows + lat_rows, ctx_rows[::-1] + lat_rows[::-1])
    coef = {(kk, d): (atr[d, kk], ati[d, kk]) for kk, d in chains}
    state = {ch: (jnp.zeros((batch, 2 * S5_STATE), F32), jnp.zeros((batch, 2 * S5_STATE), F32)) for ch in chains}
    for t in range(n_ctx + n_lat):
        for kk, d in chains:
            rows = order[d][t]
            (ar, ai), (h_re, h_im) = coef[kk, d], state[kk, d]
            hre[kk, d, rows, :] = h_re
            him[kk, d, rows, :] = h_im
            state[kk, d] = (ar * h_re - ai * h_im + sre[kk, d, rows, :],
                            ar * h_im + ai * h_re + sim[kk, d, rows, :])

    for kk in range(S5_STEP_PAIRS):
        ul = upscr[kk, :rows_lat, :]
        y = ul.astype(F32) * dsk_ref[kk]
        y = y + jnp.concatenate(
            [jnp.dot(ul[:, :w], toep_ref[kk, 0], preferred_element_type=F32),
             jnp.dot(ul[:, w:], toep_ref[kk, 1], preferred_element_type=F32)], axis=1)
        nt = (((1,), (1,)), ((), ()))
        for d in range(2):
            h_r = jnp.concatenate([hre[kk, d, b * lat_pitch:b * lat_pitch + n_lat, :] for b in range(batch)], axis=0)
            h_i = jnp.concatenate([him[kk, d, b * lat_pitch:b * lat_pitch + n_lat, :] for b in range(batch)], axis=0)
            y = y + lax.dot_general(h_r.astype(BF16), wor[d, kk], nt, preferred_element_type=F32)
            y = y + lax.dot_general(h_i.astype(BF16), woi[d, kk], nt, preferred_element_type=F32)
        yt = y.T
        for b in range(batch):
            for s in range(tc):
                for j in range(2):
                    ch = (2 * kk + j) * S5_GROUP
                    zscr[b * tc + s, ch:ch + S5_GROUP, :] = yt[j * w + s * S5_GROUP:j * w + (s + 1) * S5_GROUP,
                                                               b * n_lat:(b + 1) * n_lat]

    for b in range(batch):
        for s in range(tc):
            o_ref[pl.ds(b * seq + s, n_lat, stride=tc), :] = zscr[b * tc + s].T


def _s5_mixer(u_lat, u_ctx, params, batch, seq, ctx):
    tc = S5_CHUNK
    n_lat, n_ctx = seq // tc, ctx // tc
    sp = S5_STEP_PAIRS
    lanes = sp * 2 * S5_GROUP
    assert n_lat == lanes and batch * n_ctx <= lanes
    rows_lat = n_lat * batch
    rows_all = rows_lat + lanes
    state_rows = _s5_state_rows(batch, n_lat, n_ctx)[3]
    wide = 2 * tc * S5_GROUP
    dsk = jnp.broadcast_to(params['s5_d'].astype(F32).reshape(S5_PAIRS, 2, 1, S5_GROUP),
                           (S5_PAIRS, 2, tc, S5_GROUP)).reshape(S5_PAIRS, 1, wide)
    weights = _s5_prep(params)
    w_spec = pl.BlockSpec((2, sp, wide, 2 * S5_STATE), lambda k: (0, k, 0, 0))
    a_spec = pl.BlockSpec((2, sp, 1, 2 * S5_STATE), lambda k: (0, k, 0, 0))
    specs = [pl.BlockSpec((sp, 2, wide // 2, wide // 2), lambda k: (k, 0, 0, 0))] + [w_spec] * 4 + [a_spec] * 2
    return pl.pallas_call(
        functools.partial(_s5_kernel, batch=batch, seq=seq, ctx=ctx),
        grid=(S5_PAIRS // sp,),
        in_specs=[pl.BlockSpec((batch * seq, lanes), lambda k: (0, k)),
                  pl.BlockSpec((batch * ctx, lanes), lambda k: (0, k)),
                  pl.BlockSpec((sp, 1, wide), lambda k: (k, 0, 0))] + specs,
        out_specs=pl.BlockSpec((batch * seq, lanes), lambda k: (0, k)),
        out_shape=jax.ShapeDtypeStruct((batch * seq, S5_WIDTH), F32),
        scratch_shapes=[pltpu.VMEM((sp, wide, rows_all), F32),
                        pltpu.VMEM((batch * tc, lanes, n_lat), F32),
                        pltpu.VMEM((sp, rows_all, wide), BF16)]
                       + [pltpu.VMEM((sp, 2, state_rows, 2 * S5_STATE), F32) for _ in range(4)],
        compiler_params=_cparams(1),
        name="s5",
    )(u_lat, u_ctx, dsk, *weights)


def _merge_kernel(y5_ref, o_ref_in, gs_ref, gm_ref, wa_ref, wb_ref, wm_ref, out_ref, wa_s, wb_s, wm_s):
    @pl.when(pl.program_id(1) == 0)
    def _():
        wa_s[...] = wa_ref[...].astype(BF16)
        wb_s[...] = wb_ref[...].astype(BF16)
        wm_s[...] = wm_ref[...].astype(BF16)

    z = jax.nn.gelu(y5_ref[...].astype(F32)).astype(BF16)
    a = jnp.dot(z, wa_s[...], preferred_element_type=F32)
    b = jnp.dot(z, wb_s[...], preferred_element_type=F32)
    mla = jnp.dot(o_ref_in[...], wm_s[...], preferred_element_type=F32)
    merged = gs_ref[...].astype(F32) * (a * jax.nn.sigmoid(b)) + gm_ref[...].astype(F32) * mla
    out_ref[...] = merged.astype(out_ref.dtype)


def _merge(y5, o_mla, gates, w_glu, w_mla_o, tm, tn):
    t = y5.shape[0]
    nj = D_MODEL // tn
    return pl.pallas_call(
        _merge_kernel,
        grid=(nj, t // tm),
        in_specs=[pl.BlockSpec((tm, S5_WIDTH), lambda j, i: (i, 0)),
                  pl.BlockSpec((tm, MLA_HEADS * V_DIM), lambda j, i: (i, 0)),
                  pl.BlockSpec((tm, tn), lambda j, i: (i, j)),
                  pl.BlockSpec((tm, tn), lambda j, i: (i, nj + j)),
                  pl.BlockSpec((S5_WIDTH, tn), lambda j, i: (0, j)),
                  pl.BlockSpec((S5_WIDTH, tn), lambda j, i: (0, nj + j)),
                  pl.BlockSpec((MLA_HEADS * V_DIM, tn), lambda j, i: (0, j))],
        out_specs=pl.BlockSpec((tm, tn), lambda j, i: (i, j)),
        out_shape=jax.ShapeDtypeStruct((t, D_MODEL), BF16),
        scratch_shapes=[pltpu.VMEM((S5_WIDTH, tn), BF16), pltpu.VMEM((S5_WIDTH, tn), BF16),
                        pltpu.VMEM((MLA_HEADS * V_DIM, tn), BF16)],
        compiler_params=_cparams(2),
        name="merge",
    )(y5, o_mla, gates, gates, w_glu, w_glu, w_mla_o)


def _resid_mm(a, w, x_res, gate, rows_per_batch, tm, tn, name):
    def epi(accs, e_refs, o_refs):
        o_refs[0][...] = e_refs[0][...] + e_refs[1][0] * accs[0]

    t = a.shape[0]
    n = w.shape[1]
    tpb = rows_per_batch // tm
    extras = [(x_res, pl.BlockSpec((tm, tn), lambda j, i: (i, j))),
              (gate, pl.BlockSpec((1, 1, tn), lambda j, i: (i // tpb, 0, j)))]
    outs = [(jax.ShapeDtypeStruct((t, n), F32), pl.BlockSpec((tm, tn), lambda j, i: (i, j)))]
    return _fused_mm(a, [(w, 0)], epi, extras, outs, tm=tm, tn=tn, nj=n // tn, name=name)[0]


def _out_proj_norm_kernel(a_ref, w_ref, x_ref, g1_ref, n2_ref, sc_ref, sh_ref, x1_ref, xn_ref, w_s):
    @pl.when(pl.program_id(0) == 0)
    def _():
        w_s[...] = w_ref[...].astype(BF16)

    x1 = x_ref[...] + g1_ref[0] * jnp.dot(a_ref[...], w_s[...], preferred_element_type=F32)
    x1_ref[...] = x1
    xn_ref[...] = (_rms(x1, n2_ref[...]) * (1.0 + sc_ref[0]) + sh_ref[0]).astype(xn_ref.dtype)


def _out_proj_norm(merged, w_out, x_res, gate, gain2, scale2, shift2, rows_per_batch, tm):
    t, k = merged.shape
    tpb = rows_per_batch // tm
    row = pl.BlockSpec((tm, D_MODEL), lambda i: (i, 0))
    per_batch = pl.BlockSpec((1, 1, D_MODEL), lambda i: (i // tpb, 0, 0))
    return pl.pallas_call(
        _out_proj_norm_kernel,
        grid=(t // tm,),
        in_specs=[pl.BlockSpec((tm, k), lambda i: (i, 0)),
                  pl.BlockSpec((k, D_MODEL), lambda i: (0, 0), pipeline_mode=pl.Buffered(1)),
                  row, per_batch, pl.BlockSpec((1, D_MODEL), lambda i: (0, 0)), per_batch, per_batch],
        out_specs=[row, row],
        out_shape=[jax.ShapeDtypeStruct((t, D_MODEL), F32), jax.ShapeDtypeStruct((t, D_MODEL), BF16)],
        scratch_shapes=[pltpu.VMEM((k, D_MODEL), BF16)],
        compiler_params=_cparams(1),
        name="out_proj",
    )(merged, w_out, x_res, gate, gain2.reshape(1, D_MODEL), scale2, shift2)


def _ffn_in(xn, w_ffn_in, tm, tn):
    def epi(accs, e_refs, o_refs):
        o_refs[0][...] = (jax.nn.silu(accs[0]) * accs[1]).astype(BF16)

    t = xn.shape[0]
    nj = D_FF // tn
    outs = [(jax.ShapeDtypeStruct((t, D_FF), BF16), pl.BlockSpec((tm, tn), lambda j, i: (i, j)))]
    return _fused_mm(xn, [(w_ffn_in, 0), (w_ffn_in, nj)], epi, [], outs, tm=tm, tn=tn, nj=nj, name="ffn_in")[0]


def _final_norm_kernel(x_ref, g_ref, o_ref):
    o_ref[...] = _rms(x_ref[...], g_ref[...])


def _final_norm(x2d, g, tm):
    t = x2d.shape[0]
    return pl.pallas_call(
        _final_norm_kernel,
        grid=(t // tm,),
        in_specs=[pl.BlockSpec((tm, D_MODEL), lambda i: (i, 0)),
                  pl.BlockSpec((1, D_MODEL), lambda i: (0, 0))],
        out_specs=pl.BlockSpec((tm, D_MODEL), lambda i: (i, 0)),
        out_shape=jax.ShapeDtypeStruct((t, D_MODEL), F32),
        compiler_params=_cparams(1),
        name="final_norm",
    )(x2d, g.reshape(1, D_MODEL))


def _rope_rot_cols(w):
    k = w.shape[0]
    ws = w.reshape(k, -1, 2, 2, QK_ROPE // 4)
    return jnp.stack([-ws[:, :, :, 1, :], ws[:, :, :, 0, :]], axis=3).reshape(k, -1)


def _rope_tables(n_tokens):
    rows = n_tokens // GRID_W
    row = jnp.repeat(jnp.arange(rows, dtype=F32), GRID_W)
    col = jnp.tile(jnp.arange(GRID_W, dtype=F32), rows)
    n_freq = QK_ROPE // 4
    inv = ROPE_BASE ** (-jnp.arange(n_freq, dtype=F32) / n_freq)
    ang = jnp.stack([row[:, None] * inv, col[:, None] * inv], axis=1)
    cos = jnp.broadcast_to(jnp.cos(ang)[:, :, None, :], (n_tokens, 2, 2, n_freq)).reshape(n_tokens, QK_ROPE)
    sin = jnp.broadcast_to(jnp.sin(ang)[:, :, None, :], (n_tokens, 2, 2, n_freq)).reshape(n_tokens, QK_ROPE)
    return cos, sin


def kernel(x, c, ctx, c_ctx, w_mod, b_mod, norm1, norm2, w_in, s5_a_re, s5_a_im, s5_log_dt, s5_b_re, s5_b_im,
           s5_c_re, s5_c_im, s5_d, w_glu, q_norm, kv_norm, w_uq, w_ukv, w_mla_o, w_out, w_ffn_in, w_ffn_out,
           norm_f):
    batch, seq, _ = x.shape
    n_ctx = ctx.shape[1]
    assert w_mod.shape[0] == 1, "single-layer block"
    p = dict(s5_a_re=s5_a_re[0], s5_a_im=s5_a_im[0], s5_log_dt=s5_log_dt[0], s5_b_re=s5_b_re[0],
             s5_b_im=s5_b_im[0], s5_c_re=s5_c_re[0], s5_c_im=s5_c_im[0], s5_d=s5_d[0])
    w_in0 = w_in[0]

    cv = jnp.concatenate([c, c_ctx[None], jnp.zeros((8 - batch - 1, D_MODEL), F32)], axis=0)
    m = _modulation(cv, w_mod[0], b_mod[0]).reshape(8, 6, 1, D_MODEL)
    m_lat = m[:batch]
    m_ctx = m[batch:batch + 1]

    x2d = x.reshape(batch * seq, D_MODEL)
    c2d = ctx.reshape(batch * n_ctx, D_MODEL)
    xn = _norm_mod(x2d, norm1[0], m_lat[:, 1], m_lat[:, 0], seq, 512)
    cn = _norm_mod(c2d, norm1[0], m_ctx[:, 1], m_ctx[:, 0], batch * n_ctx, 256)

    kv_lo = S5_WIDTH + Q_RANK
    w_kr = w_in0[:, kv_lo + KV_RANK:kv_lo + KV_RANK + QK_ROPE]
    w_kv1 = jnp.concatenate([w_in0[:, kv_lo:kv_lo + KV_RANK], w_kr, _rope_rot_cols(w_kr)], axis=1)
    wq = w_uq[0].reshape(Q_RANK, MLA_HEADS, QK_NOPE + QK_ROPE)
    wq_rope = wq[:, :, QK_NOPE:].reshape(Q_RANK, MLA_HEADS * QK_ROPE)
    wq2 = jnp.concatenate([wq[:, :, :QK_NOPE].reshape(Q_RANK, MLA_HEADS * QK_NOPE), wq_rope,
                           _rope_rot_cols(wq_rope)], axis=1).astype(BF16)
    w_ukv_bf = w_ukv[0].astype(BF16)
    cos, sin = _rope_tables(seq)
    cos_q, sin_q = jnp.tile(cos, (1, MLA_HEADS)), jnp.tile(sin, (1, MLA_HEADS))

    u_lat = _proj_cast(xn, w_in0, 0, S5_WIDTH, 1024, 1024, "u_lat", out_dtype=F32)
    u_ctx = _proj_cast(cn, w_in0, 0, S5_WIDTH, 256, 512, "u_ctx", out_dtype=F32)
    q = _q_path(xn, w_in0, q_norm[0], wq2, cos_q, sin_q, 1024, seq)
    kv_lat, kr_lat = _kv_path(xn, w_kv1, kv_norm[0], w_ukv_bf, cos, sin, 1024, seq)
    kv_ctx, kr_ctx = _kv_path(cn, w_kv1, kv_norm[0], w_ukv_bf, None, None, 256, n_ctx)
    gates = _gates(xn, w_in0, kv_lo + KV_RANK + QK_ROPE, 2048, 512)

    y5 = _s5_mixer(u_lat, u_ctx, p, batch, seq, n_ctx)
    o_mla = _attention(q, kv_lat, kv_ctx, kr_lat, kr_ctx, batch, seq, n_ctx, 2048)

    merged = _merge(y5, o_mla, gates, w_glu[0], w_mla_o[0], 1024, 512)
    x1, xn2 = _out_proj_norm(merged, w_out[0], x2d, m_lat[:, 2], norm2[0], m_lat[:, 4], m_lat[:, 3], seq, 512)
    hid = _ffn_in(xn2, w_ffn_in[0], 1024, 512)
    x2 = _resid_mm(hid, w_ffn_out[0], x1, m_lat[:, 5], seq, 512, 512, "ffn_out")
    return _final_norm(x2, norm_f, 512).reshape(batch, seq, D_MODEL)
```

```python
import functools
import math

import jax
import jax.numpy as jnp
from jax import lax
from jax.experimental import pallas as pl
from jax.experimental.pallas import tpu as pltpu

F32 = jnp.float32
BF16 = jnp.bfloat16

D_MODEL = 2048
GRID_W = 64
EPS = 1e-6
S5_WIDTH = D_MODEL // 2
S5_GROUP = 16
S5_GROUPS = S5_WIDTH // S5_GROUP
S5_STATE = 64
S5_CHUNK = 16
S5_PAIRS = S5_GROUPS // 2
MLA_HEADS = 8
QK_NOPE = 128
QK_ROPE = 64
V_DIM = 128
Q_RANK = 512
KV_RANK = 256
ROPE_BASE = 10000.0
ATTN_SCALE = (QK_NOPE + QK_ROPE) ** -0.5
D_FF = -(-8 * D_MODEL // (3 * 256)) * 256

VMEM_LIMIT_BYTES = 56 * 1024 * 1024
SUBLANES = 8


def _cparams(n_axes):
    return pltpu.CompilerParams(dimension_semantics=("arbitrary",) * n_axes,
                                vmem_limit_bytes=VMEM_LIMIT_BYTES)


def _rms(x, g):
    return x * lax.rsqrt(jnp.mean(x * x, axis=-1, keepdims=True) + EPS) * g


def _mod_kernel(cv_ref, w_ref, b_ref, o_ref):
    s = jax.nn.silu(cv_ref[...]).astype(BF16)
    o_ref[...] = jnp.dot(s, w_ref[...].astype(BF16), preferred_element_type=F32) + b_ref[...]


def _modulation(cv, w_mod, b_mod):
    n = w_mod.shape[1]
    tn = 1536
    return pl.pallas_call(
        _mod_kernel,
        grid=(n // tn,),
        in_specs=[pl.BlockSpec((8, D_MODEL), lambda j: (0, 0)),
                  pl.BlockSpec((D_MODEL, tn), lambda j: (0, j)),
                  pl.BlockSpec((1, tn), lambda j: (0, j))],
        out_specs=pl.BlockSpec((8, tn), lambda j: (0, j)),
        out_shape=jax.ShapeDtypeStruct((8, n), F32),
        compiler_params=_cparams(1),
        name="mod",
    )(cv, w_mod, b_mod.reshape(1, n))


def _norm_mod_kernel(x_ref, g_ref, sc_ref, sh_ref, o_ref):
    y = _rms(x_ref[...], g_ref[...])
    o_ref[...] = (y * (1.0 + sc_ref[0]) + sh_ref[0]).astype(o_ref.dtype)


def _norm_mod(x2d, gain, scale, shift, rows_per_batch, tm, out_dtype=BF16):
    t = x2d.shape[0]
    tpb = rows_per_batch // tm
    return pl.pallas_call(
        _norm_mod_kernel,
        grid=(t // tm,),
        in_specs=[pl.BlockSpec((tm, D_MODEL), lambda i: (i, 0)),
                  pl.BlockSpec((1, D_MODEL), lambda i: (0, 0)),
                  pl.BlockSpec((1, 1, D_MODEL), lambda i: (i // tpb, 0, 0)),
                  pl.BlockSpec((1, 1, D_MODEL), lambda i: (i // tpb, 0, 0))],
        out_specs=pl.BlockSpec((tm, D_MODEL), lambda i: (i, 0)),
        out_shape=jax.ShapeDtypeStruct((t, D_MODEL), out_dtype),
        compiler_params=_cparams(1),
        name="norm_mod",
    )(x2d, gain.reshape(1, D_MODEL), scale, shift)


def _fused_mm(a, weights, epilogue, extras, outs, *, tm, nj, name):
    t, k = a.shape
    ni = t // tm
    nw, ne, no = len(weights), len(extras), len(outs)
    need_cast = [w.dtype != BF16 for w, _, _, _ in weights]
    nt = (((1,), (1,)), ((), ()))

    def kernel(*refs):
        a_ref = refs[0]
        w_refs = refs[1:1 + nw]
        e_refs = refs[1 + nw:1 + nw + ne]
        o_refs = refs[1 + nw + ne:1 + nw + ne + no]
        s_refs = list(refs[1 + nw + ne + no:])
        staged = {idx: s_refs.pop(0) for idx in range(nw) if need_cast[idx]}
        if staged:
            @pl.when(pl.program_id(1) == 0)
            def _():
                for idx, s_ref in staged.items():
                    s_ref[...] = w_refs[idx][...].astype(BF16)

        av = a_ref[...]
        accs = []
        for idx in range(nw):
            w_ref = staged.get(idx, w_refs[idx])
            if weights[idx][2] == "kn":
                accs.append(jnp.dot(av, w_ref[...], preferred_element_type=F32))
            else:
                accs.append(lax.dot_general(av, w_ref[...], nt, preferred_element_type=F32))
        epilogue(accs, e_refs, o_refs)

    in_specs = [pl.BlockSpec((tm, k), lambda j, i: (i, 0))]
    scratch = []
    for (w, off, layout, width), cast in zip(weights, need_cast):
        if layout == "kn":
            shape = (k, width)
            in_specs.append(pl.BlockSpec(shape, functools.partial(lambda j, i, off: (0, off + j), off=off)))
        else:
            shape = (width, k)
            in_specs.append(pl.BlockSpec(
                (pl.Element(width), pl.Element(k)),
                functools.partial(lambda j, i, off, width: (pl.multiple_of(off + j * width, SUBLANES), 0),
                                  off=off, width=width)))
        if cast:
            scratch.append(pltpu.VMEM(shape, BF16))
    in_specs += [spec for _, spec in extras]
    return pl.pallas_call(
        kernel,
        grid=(nj, ni),
        in_specs=in_specs,
        out_specs=[spec for _, spec in outs],
        out_shape=[sds for sds, _ in outs],
        scratch_shapes=scratch,
        compiler_params=_cparams(2),
        name=name,
    )(a, *[w[0] for w in weights], *[e for e, _ in extras])


def _proj_cast(xn, w_in_t, row0, n, tm, tn, name, act=None, out_dtype=BF16):
    def epi(accs, e_refs, o_refs):
        v = accs[0]
        if act is not None:
            v = act(v)
        o_refs[0][...] = v.astype(out_dtype)

    t = xn.shape[0]
    return _fused_mm(xn, [(w_in_t, row0, "nk", tn)], epi, [],
                     [(jax.ShapeDtypeStruct((t, n), out_dtype), pl.BlockSpec((tm, tn), lambda j, i: (i, j)))],
                     tm=tm, nj=n // tn, name=name)[0]


def _q_path(xn, w_in_t, q_norm, wq2, cos_q, sin_q, tm, seq):
    nr = MLA_HEADS * QK_ROPE
    nn = MLA_HEADS * QK_NOPE

    def epi(accs, e_refs, o_refs):
        qn_ref, w2_ref, cos_ref, sin_ref = e_refs
        cq = _rms(accs[0], qn_ref[...]).astype(BF16)
        q = jnp.dot(cq, w2_ref[...], preferred_element_type=F32)
        rope = q[:, nn:nn + nr] * cos_ref[...] + q[:, nn + nr:] * sin_ref[...]
        o_refs[0][:, :nn] = (q[:, :nn] * ATTN_SCALE).astype(BF16)
        o_refs[0][:, nn:] = (rope * ATTN_SCALE).astype(BF16)

    t = xn.shape[0]
    pos_tiles = seq // tm
    extras = [(q_norm.reshape(1, Q_RANK), pl.BlockSpec((1, Q_RANK), lambda j, i: (0, 0))),
              (wq2, pl.BlockSpec(wq2.shape, lambda j, i: (0, 0))),
              (cos_q, pl.BlockSpec((tm, nr), lambda j, i: (i % pos_tiles, 0))),
              (sin_q, pl.BlockSpec((tm, nr), lambda j, i: (i % pos_tiles, 0)))]
    outs = [(jax.ShapeDtypeStruct((t, nn + nr), BF16), pl.BlockSpec((tm, nn + nr), lambda j, i: (i, 0)))]
    return _fused_mm(xn, [(w_in_t, S5_WIDTH, "nk", Q_RANK)], epi, extras, outs, tm=tm, nj=1, name="q_path")[0]


def _kv_path(xn, w_in_t, w_kr_rot_t, kv_norm, w_ukv_bf, cos_k, sin_k, tm, seq):
    nkv = w_ukv_bf.shape[1]
    rope = cos_k is not None

    def epi(accs, e_refs, o_refs):
        acc = accs[0]
        ckv = _rms(acc[:, :KV_RANK], e_refs[0][...]).astype(BF16)
        o_refs[0][...] = jnp.dot(ckv, e_refs[1][...], preferred_element_type=F32).astype(BF16)
        kr = acc[:, KV_RANK:]
        if rope:
            kr = kr * e_refs[2][...] + accs[1] * e_refs[3][...]
        o_refs[1][...] = kr.astype(BF16)

    t = xn.shape[0]
    pos_tiles = seq // tm
    extras = [(kv_norm.reshape(1, KV_RANK), pl.BlockSpec((1, KV_RANK), lambda j, i: (0, 0))),
              (w_ukv_bf, pl.BlockSpec(w_ukv_bf.shape, lambda j, i: (0, 0)))]
    weights = [(w_in_t, S5_WIDTH + Q_RANK, "nk", KV_RANK + QK_ROPE)]
    if rope:
        weights.append((w_kr_rot_t, 0, "nk", QK_ROPE))
        extras += [(cos_k, pl.BlockSpec((tm, QK_ROPE), lambda j, i: (i % pos_tiles, 0))),
                   (sin_k, pl.BlockSpec((tm, QK_ROPE), lambda j, i: (i % pos_tiles, 0)))]
    outs = [(jax.ShapeDtypeStruct((t, nkv), BF16), pl.BlockSpec((tm, nkv), lambda j, i: (i, 0))),
            (jax.ShapeDtypeStruct((t, QK_ROPE), BF16), pl.BlockSpec((tm, QK_ROPE), lambda j, i: (i, 0)))]
    return _fused_mm(xn, weights, epi, extras, outs, tm=tm, nj=1, name="kv_path")


ATTN_SUB_ROWS = 512


def _attn_kernel(qn_ref, qr_ref, kvl_ref, kvc_ref, krl_ref, krc_ref, o_ref, k_scr, v_scr, *, seq, ctx):
    dk = QK_NOPE + QK_ROPE

    @pl.when(pl.program_id(2) == 0)
    def _():
        for h in range(2):
            base = h * (QK_NOPE + V_DIM)
            k_scr[h, :seq, :QK_NOPE] = kvl_ref[:, base:base + QK_NOPE]
            k_scr[h, seq:, :QK_NOPE] = kvc_ref[:, base:base + QK_NOPE]
            k_scr[h, :seq, QK_NOPE:dk] = krl_ref[...]
            k_scr[h, seq:, QK_NOPE:dk] = krc_ref[...]
            v_scr[h, :seq, :V_DIM] = kvl_ref[:, base + QK_NOPE:base + QK_NOPE + V_DIM]
            v_scr[h, seq:, :V_DIM] = kvc_ref[:, base + QK_NOPE:base + QK_NOPE + V_DIM]
            ones_col = lax.broadcasted_iota(jnp.int32, (seq + ctx, V_DIM), 1) == 0
            v_scr[h, :, V_DIM:] = jnp.where(ones_col, 1.0, 0.0).astype(BF16)

    for r0 in range(0, qn_ref.shape[0], ATTN_SUB_ROWS):
        rows = slice(r0, r0 + ATTN_SUB_ROWS)
        for h in range(2):
            q = jnp.concatenate([qn_ref[rows, h * QK_NOPE:(h + 1) * QK_NOPE],
                                 qr_ref[rows, h * QK_ROPE:(h + 1) * QK_ROPE]], axis=1)
            s = lax.dot_general(q, k_scr[h], (((1,), (1,)), ((), ())), preferred_element_type=F32)
            m = jnp.max(s, axis=-1, keepdims=True)
            p = jnp.exp((s - m).astype(BF16))
            ol = jnp.dot(p, v_scr[h], preferred_element_type=F32)
            o_ref[rows, h * V_DIM:(h + 1) * V_DIM] = (ol[:, :V_DIM] / ol[:, V_DIM:V_DIM + 1]).astype(o_ref.dtype)


def _attention(q, kv_lat, kv_ctx, kr_lat, kr_ctx, batch, seq, ctx, tq):
    nq = seq // tq
    nn_blocks = MLA_HEADS * QK_NOPE // (2 * QK_NOPE)
    dk = QK_NOPE + QK_ROPE
    hw = 2 * (QK_NOPE + V_DIM)
    return pl.pallas_call(
        functools.partial(_attn_kernel, seq=seq, ctx=ctx),
        grid=(batch, MLA_HEADS // 2, nq),
        in_specs=[pl.BlockSpec((tq, 2 * QK_NOPE), lambda b, hp, qi: (b * nq + qi, hp)),
                  pl.BlockSpec((tq, 2 * QK_ROPE), lambda b, hp, qi: (b * nq + qi, 2 * nn_blocks + hp)),
                  pl.BlockSpec((seq, hw), lambda b, hp, qi: (b, hp)),
                  pl.BlockSpec((ctx, hw), lambda b, hp, qi: (b, hp)),
                  pl.BlockSpec((seq, QK_ROPE), lambda b, hp, qi: (b, 0)),
                  pl.BlockSpec((ctx, QK_ROPE), lambda b, hp, qi: (b, 0))],
        out_specs=pl.BlockSpec((tq, 2 * V_DIM), lambda b, hp, qi: (b * nq + qi, hp)),
        out_shape=jax.ShapeDtypeStruct((batch * seq, MLA_HEADS * V_DIM), BF16),
        scratch_shapes=[pltpu.VMEM((2, seq + ctx, dk), BF16), pltpu.VMEM((2, seq + ctx, 2 * V_DIM), BF16)],
        compiler_params=_cparams(3),
        name="attention",
    )(q, q, kv_lat, kv_ctx, kr_lat, kr_ctx)


def _s5_prep_kernel(are_ref, aim_ref, ldt_ref, bre_ref, bim_ref, cre_ref, cim_ref,
                    toep_ref, wsr_ref, wsi_ref, wor_ref, woi_ref, atr_ref, ati_ref):
    tc, g = S5_CHUNK, S5_GROUP
    w = tc * g
    hp = lax.Precision.HIGHEST
    lane = lax.broadcasted_iota(jnp.int32, (g, 2 * S5_STATE), 1)
    in_group = (lane < S5_STATE, lane >= S5_STATE)
    lane_w = lax.broadcasted_iota(jnp.int32, (g, w), 1)
    nt = (((1,), (1,)), ((), ()))
    toep_rows = [[jnp.zeros((g, w), F32) for _ in range(tc)] for _ in range(2)]
    for d in range(2):
        lr, li = are_ref[d, 0], aim_ref[d, 0]
        dt = jnp.exp(ldt_ref[d, 0])
        mag = jnp.exp(lr * dt)
        ab_re, ab_im = mag * jnp.cos(li * dt), mag * jnp.sin(li * dt)
        den = lr * lr + li * li
        nr, ni = ab_re - 1.0, ab_im
        co_re = (nr * lr + ni * li) / den
        co_im = (ni * lr - nr * li) / den
        br, bi = bre_ref[d, 0], bim_ref[d, 0]
        bb_re = co_re * br - co_im * bi
        bb_im = co_re * bi + co_im * br
        pw = [(jnp.ones_like(ab_re), jnp.zeros_like(ab_re))]
        for _ in range(tc):
            pr, pi = pw[-1]
            pw.append((pr * ab_re - pi * ab_im, pr * ab_im + pi * ab_re))
        cr, ci = cre_ref[d, 0], cim_ref[d, 0]
        ca = [(cr * pr - ci * pi, cr * pi + ci * pr) for pr, pi in pw]
        taus = list(range(tc))[::-1] if d else list(range(tc))
        y_re = jnp.concatenate([ca[t][0] for t in taus], axis=0)
        y_im = jnp.concatenate([ca[t][1] for t in taus], axis=0)
        for j in range(2):
            x_re = jnp.where(in_group[j], bb_re, 0.0)
            x_im = jnp.where(in_group[j], bb_im, 0.0)
            kt = (lax.dot_general(x_re, y_re, nt, precision=hp, preferred_element_type=F32)
                  - lax.dot_general(x_im, y_im, nt, precision=hp, preferred_element_type=F32))
            for r in range(tc):
                sh = (r + 1) * g if d else r * g
                blk = pltpu.roll(kt, sh % w, 1) if sh % w else kt
                keep = (lane_w < sh) if d else (lane_w >= sh)
                toep_rows[j][r] = toep_rows[j][r] + jnp.where(keep, blk, 0.0)
        for r in range(tc):
            pr, pi = pw[r] if d else pw[tc - 1 - r]
            w_re = bb_re * pr - bb_im * pi
            w_im = bb_re * pi + bb_im * pr
            car, cai = ca[tc - r] if d else ca[r + 1]
            for j in range(2):
                rows = slice(j * w + r * g, j * w + (r + 1) * g)
                wsr_ref[d, 0, rows, :] = jnp.where(in_group[j], w_re, 0.0).astype(BF16)
                wsi_ref[d, 0, rows, :] = jnp.where(in_group[j], w_im, 0.0).astype(BF16)
                wor_ref[d, 0, rows, :] = jnp.where(in_group[j], car, 0.0).astype(BF16)
                woi_ref[d, 0, rows, :] = jnp.where(in_group[j], -cai, 0.0).astype(BF16)
        atr_ref[d, 0] = pw[tc][0]
        ati_ref[d, 0] = pw[tc][1]
    for j in range(2):
        toep_ref[0, j] = jnp.concatenate(toep_rows[j], axis=0).astype(BF16)


def _s5_prep(params):
    tc = S5_CHUNK
    wide = 2 * tc * S5_GROUP
    sl = 2 * S5_STATE

    def pair_lanes(v):
        return v.astype(F32).reshape(2, S5_PAIRS, 1, sl)

    def pair_rows(v):
        rows = v.shape[2]
        return v.astype(F32).reshape(2, S5_PAIRS, 2, rows, S5_STATE).transpose(0, 1, 3, 2, 4).reshape(
            2, S5_PAIRS, rows, sl)

    ldt = jnp.broadcast_to(params['s5_log_dt'].astype(F32)[:, :, None], (2, S5_GROUPS, S5_STATE))
    ins = [pair_lanes(params['s5_a_re']), pair_lanes(params['s5_a_im']), pair_lanes(ldt),
           pair_rows(params['s5_b_re'].transpose(0, 1, 3, 2)), pair_rows(params['s5_b_im'].transpose(0, 1, 3, 2)),
           pair_rows(params['s5_c_re']), pair_rows(params['s5_c_im'])]
    vec_spec = pl.BlockSpec((2, 1, 1, sl), lambda k: (0, k, 0, 0))
    mat_spec = pl.BlockSpec((2, 1, S5_GROUP, sl), lambda k: (0, k, 0, 0))
    w_spec = pl.BlockSpec((2, 1, wide, sl), lambda k: (0, k, 0, 0))
    w_sds = jax.ShapeDtypeStruct((2, S5_PAIRS, wide, sl), BF16)
    a_sds = jax.ShapeDtypeStruct((2, S5_PAIRS, 1, sl), F32)
    return pl.pallas_call(
        _s5_prep_kernel,
        grid=(S5_PAIRS,),
        in_specs=[vec_spec] * 3 + [mat_spec] * 4,
        out_specs=[pl.BlockSpec((1, 2, wide // 2, wide // 2), lambda k: (k, 0, 0, 0))] + [w_spec] * 4 + [vec_spec] * 2,
        out_shape=[jax.ShapeDtypeStruct((S5_PAIRS, 2, wide // 2, wide // 2), BF16)] + [w_sds] * 4 + [a_sds] * 2,
        compiler_params=_cparams(1),
        name="s5_prep",
    )(*ins)


S5_STEP_PAIRS = 4
def _s5_state_rows(batch, n_lat, n_ctx):
    def pitch(n):
        p = -(-n // SUBLANES)
        return SUBLANES * (p + 1 - p % 2)

    lat_pitch, ctx_pitch = pitch(n_lat), pitch(n_ctx)
    ctx_base = batch * lat_pitch
    return lat_pitch, ctx_pitch, ctx_base, ctx_base + batch * ctx_pitch


def _s5_kernel(ul_ref, uc_ref, dsk_ref, *refs, batch, seq, ctx):
    toep_ref, wsr, wsi, wor, woi, atr, ati = refs[:7]
    o_ref = refs[7]
    wscr, zscr, upscr, sre, sim, hre, him = refs[8:]
    tc = S5_CHUNK
    n_lat, n_ctx = seq // tc, ctx // tc
    rows_lat = n_lat * batch
    lat_pitch, ctx_pitch, ctx_base, _ = _s5_state_rows(batch, n_lat, n_ctx)
    w = tc * S5_GROUP
    lanes = S5_STEP_PAIRS * 2 * S5_GROUP

    def scatter_tiles(xt, r, col0):
        for kk in range(S5_STEP_PAIRS):
            for j in range(2):
                ch = (2 * kk + j) * S5_GROUP
                wscr[kk, j * w + r * S5_GROUP:j * w + (r + 1) * S5_GROUP, col0:col0 + lanes] = xt[ch:ch + S5_GROUP, :]

    for r in range(tc):
        for b in range(batch):
            x = ul_ref[pl.ds(b * seq + r, n_lat, stride=tc), :]
            scatter_tiles(x.T, r, b * n_lat)
        xc = jnp.concatenate([uc_ref[pl.ds(b * ctx + r, n_ctx, stride=tc), :] for b in range(batch)]
                             + [jnp.zeros((lanes - batch * n_ctx, lanes), F32)], axis=0)
        scatter_tiles(xc.T, r, rows_lat)

    chains = [(kk, d) for kk in range(S5_STEP_PAIRS) for d in range(2)]
    for kk in range(S5_STEP_PAIRS):
        upscr[kk] = wscr[kk].T.astype(BF16)
        for d in range(2):
            for dst, wst in ((sre, wsr), (sim, wsi)):
                s = jnp.dot(upscr[kk], wst[d, kk], preferred_element_type=F32)
                for b in range(batch):
                    dst[kk, d, b * lat_pitch:b * lat_pitch + n_lat, :] = s[b * n_lat:(b + 1) * n_lat]
                    dst[kk, d, ctx_base + b * ctx_pitch:ctx_base + b * ctx_pitch + n_ctx, :] = (
                        s[rows_lat + b * n_ctx:rows_lat + (b + 1) * n_ctx])

    ctx_rows = [pl.ds(ctx_base + c, batch, stride=ctx_pitch) for c in range(n_ctx)]
    lat_rows = [pl.ds(c, batch, stride=lat_pitch) for c in range(n_lat)]
    order = (ctx_rows + lat_rows, ctx_rows[::-1] + lat_rows[::-1])
    coef = {(kk, d): (atr[d, kk], ati[d, kk]) for kk, d in chains}
    state = {ch: (jnp.zeros((batch, 2 * S5_STATE), F32), jnp.zeros((batch, 2 * S5_STATE), F32)) for ch in chains}
    for t in range(n_ctx + n_lat):
        for kk, d in chains:
            rows = order[d][t]
            (ar, ai), (h_re, h_im) = coef[kk, d], state[kk, d]
            hre[kk, d, rows, :] = h_re
            him[kk, d, rows, :] = h_im
            state[kk, d] = (ar * h_re - ai * h_im + sre[kk, d, rows, :],
                            ar * h_im + ai * h_re + sim[kk, d, rows, :])

    for kk in range(S5_STEP_PAIRS):
        ul = upscr[kk, :rows_lat, :]
        y = ul.astype(F32) * dsk_ref[kk]
        y = y + jnp.concatenate(
            [jnp.dot(ul[:, :w], toep_ref[kk, 0], preferred_element_type=F32),
             jnp.dot(ul[:, w:], toep_ref[kk, 1], preferred_element_type=F32)], axis=1)
        nt = (((1,), (1,)), ((), ()))
        for d in range(2):
            h_r = jnp.concatenate([hre[kk, d, b * lat_pitch:b * lat_pitch + n_lat, :] for b in range(batch)], axis=0)
            h_i = jnp.concatenate([him[kk, d, b * lat_pitch:b * lat_pitch + n_lat, :] for b in range(batch)], axis=0)
            y = y + lax.dot_general(h_r.astype(BF16), wor[d, kk], nt, preferred_element_type=F32)
            y = y + lax.dot_general(h_i.astype(BF16), woi[d, kk], nt, preferred_element_type=F32)
        yt = y.T
        for b in range(batch):
            for s in range(tc):
                for j in range(2):
                    ch = (2 * kk + j) * S5_GROUP
                    zscr[b * tc + s, ch:ch + S5_GROUP, :] = yt[j * w + s * S5_GROUP:j * w + (s + 1) * S5_GROUP,
                                                               b * n_lat:(b + 1) * n_lat]

    for b in range(batch):
        for s in range(tc):
            o_ref[pl.ds(b * seq + s, n_lat, stride=tc), :] = zscr[b * tc + s].T


def _s5_mixer(u_lat, u_ctx, params, batch, seq, ctx):
    tc = S5_CHUNK
    n_lat, n_ctx = seq // tc, ctx // tc
    sp = S5_STEP_PAIRS
    lanes = sp * 2 * S5_GROUP
    assert n_lat == lanes and batch * n_ctx <= lanes
    rows_lat = n_lat * batch
    rows_all = rows_lat + lanes
    state_rows = _s5_state_rows(batch, n_lat, n_ctx)[3]
    wide = 2 * tc * S5_GROUP
    dsk = jnp.broadcast_to(params['s5_d'].astype(F32).reshape(S5_PAIRS, 2, 1, S5_GROUP),
                           (S5_PAIRS, 2, tc, S5_GROUP)).reshape(S5_PAIRS, 1, wide)
    weights = _s5_prep(params)
    w_spec = pl.BlockSpec((2, sp, wide, 2 * S5_STATE), lambda k: (0, k, 0, 0))
    a_spec = pl.BlockSpec((2, sp, 1, 2 * S5_STATE), lambda k: (0, k, 0, 0))
    specs = [pl.BlockSpec((sp, 2, wide // 2, wide // 2), lambda k: (k, 0, 0, 0))] + [w_spec] * 4 + [a_spec] * 2
    return pl.pallas_call(
        functools.partial(_s5_kernel, batch=batch, seq=seq, ctx=ctx),
        grid=(S5_PAIRS // sp,),
        in_specs=[pl.BlockSpec((batch * seq, lanes), lambda k: (0, k)),
                  pl.BlockSpec((batch * ctx, lanes), lambda k: (0, k)),
                  pl.BlockSpec((sp, 1, wide), lambda k: (k, 0, 0))] + specs,
        out_specs=pl.BlockSpec((batch * seq, lanes), lambda k: (0, k)),
        out_shape=jax.ShapeDtypeStruct((batch * seq, S5_WIDTH), F32),
        scratch_shapes=[pltpu.VMEM((sp, wide, rows_all), F32),
                        pltpu.VMEM((batch * tc, lanes, n_lat), F32),
                        pltpu.VMEM((sp, rows_all, wide), BF16)]
                       + [pltpu.VMEM((sp, 2, state_rows, 2 * S5_STATE), F32) for _ in range(4)],
        compiler_params=_cparams(1),
        name="s5",
    )(u_lat, u_ctx, dsk, *weights)


def _merge_kernel(y5_ref, o_ref_in, gs_ref, gm_ref, wa_ref, wb_ref, wm_ref, out_ref, wa_s, wb_s, wm_s):
    @pl.when(pl.program_id(1) == 0)
    def _():
        wa_s[...] = wa_ref[...].astype(BF16)
        wb_s[...] = wb_ref[...].astype(BF16)
        wm_s[...] = wm_ref[...].astype(BF16)

    z = jax.nn.gelu(y5_ref[...].astype(F32)).astype(BF16)
    a = jnp.dot(z, wa_s[...], preferred_element_type=F32)
    b = jnp.dot(z, wb_s[...], preferred_element_type=F32)
    mla = jnp.dot(o_ref_in[...], wm_s[...], preferred_element_type=F32)
    merged = gs_ref[...].astype(F32) * (a * jax.nn.sigmoid(b)) + gm_ref[...].astype(F32) * mla
    out_ref[...] = merged.astype(out_ref.dtype)


def _merge(y5, o_mla, gates, w_glu, w_mla_o, tm, tn):
    t = y5.shape[0]
    nj = D_MODEL // tn
    return pl.pallas_call(
        _merge_kernel,
        grid=(nj, t // tm),
        in_specs=[pl.BlockSpec((tm, S5_WIDTH), lambda j, i: (i, 0)),
                  pl.BlockSpec((tm, MLA_HEADS * V_DIM), lambda j, i: (i, 0)),
                  pl.BlockSpec((tm, tn), lambda j, i: (i, j)),
                  pl.BlockSpec((tm, tn), lambda j, i: (i, nj + j)),
                  pl.BlockSpec((S5_WIDTH, tn), lambda j, i: (0, j)),
                  pl.BlockSpec((S5_WIDTH, tn), lambda j, i: (0, nj + j)),
                  pl.BlockSpec((MLA_HEADS * V_DIM, tn), lambda j, i: (0, j))],
        out_specs=pl.BlockSpec((tm, tn), lambda j, i: (i, j)),
        out_shape=jax.ShapeDtypeStruct((t, D_MODEL), BF16),
        scratch_shapes=[pltpu.VMEM((S5_WIDTH, tn), BF16), pltpu.VMEM((S5_WIDTH, tn), BF16),
                        pltpu.VMEM((MLA_HEADS * V_DIM, tn), BF16)],
        compiler_params=_cparams(2),
        name="merge",
    )(y5, o_mla, gates, gates, w_glu, w_glu, w_mla_o)


def _resid_mm(a, w, x_res, gate, rows_per_batch, tm, tn, name):
    def epi(accs, e_refs, o_refs):
        o_refs[0][...] = e_refs[0][...] + e_refs[1][0] * accs[0]

    t = a.shape[0]
    n = w.shape[1]
    tpb = rows_per_batch // tm
    extras = [(x_res, pl.BlockSpec((tm, tn), lambda j, i: (i, j))),
              (gate, pl.BlockSpec((1, 1, tn), lambda j, i: (i // tpb, 0, j)))]
    outs = [(jax.ShapeDtypeStruct((t, n), F32), pl.BlockSpec((tm, tn), lambda j, i: (i, j)))]
    return _fused_mm(a, [(w, 0, "kn", tn)], epi, extras, outs, tm=tm, nj=n // tn, name=name)[0]


def _out_proj_norm_kernel(a_ref, w_ref, x_ref, g1_ref, n2_ref, sc_ref, sh_ref, x1_ref, xn_ref, w_s):
    @pl.when(pl.program_id(0) == 0)
    def _():
        w_s[...] = w_ref[...].astype(BF16)

    x1 = x_ref[...] + g1_ref[0] * jnp.dot(a_ref[...], w_s[...], preferred_element_type=F32)
    x1_ref[...] = x1
    xn_ref[...] = (_rms(x1, n2_ref[...]) * (1.0 + sc_ref[0]) + sh_ref[0]).astype(xn_ref.dtype)


def _out_proj_norm(merged, w_out, x_res, gate, gain2, scale2, shift2, rows_per_batch, tm):
    t, k = merged.shape
    tpb = rows_per_batch // tm
    row = pl.BlockSpec((tm, D_MODEL), lambda i: (i, 0))
    per_batch = pl.BlockSpec((1, 1, D_MODEL), lambda i: (i // tpb, 0, 0))
    return pl.pallas_call(
        _out_proj_norm_kernel,
        grid=(t // tm,),
        in_specs=[pl.BlockSpec((tm, k), lambda i: (i, 0)),
                  pl.BlockSpec((k, D_MODEL), lambda i: (0, 0), pipeline_mode=pl.Buffered(1)),
                  row, per_batch, pl.BlockSpec((1, D_MODEL), lambda i: (0, 0)), per_batch, per_batch],
        out_specs=[row, row],
        out_shape=[jax.ShapeDtypeStruct((t, D_MODEL), F32), jax.ShapeDtypeStruct((t, D_MODEL), BF16)],
        scratch_shapes=[pltpu.VMEM((k, D_MODEL), BF16)],
        compiler_params=_cparams(1),
        name="out_proj",
    )(merged, w_out, x_res, gate, gain2.reshape(1, D_MODEL), scale2, shift2)


def _ffn_in(xn, w_ffn_in, tm, tn):
    def epi(accs, e_refs, o_refs):
        o_refs[0][...] = (jax.nn.silu(accs[0]) * accs[1]).astype(BF16)

    t = xn.shape[0]
    nj = D_FF // tn
    outs = [(jax.ShapeDtypeStruct((t, D_FF), BF16), pl.BlockSpec((tm, tn), lambda j, i: (i, j)))]
    return _fused_mm(xn, [(w_ffn_in, 0, "kn", tn), (w_ffn_in, nj, "kn", tn)], epi, [], outs,
                     tm=tm, nj=nj, name="ffn_in")[0]


def _final_norm_kernel(x_ref, g_ref, o_ref):
    o_ref[...] = _rms(x_ref[...], g_ref[...])


def _final_norm(x2d, g, tm):
    t = x2d.shape[0]
    return pl.pallas_call(
        _final_norm_kernel,
        grid=(t // tm,),
        in_specs=[pl.BlockSpec((tm, D_MODEL), lambda i: (i, 0)),
                  pl.BlockSpec((1, D_MODEL), lambda i: (0, 0))],
        out_specs=pl.BlockSpec((tm, D_MODEL), lambda i: (i, 0)),
        out_shape=jax.ShapeDtypeStruct((t, D_MODEL), F32),
        compiler_params=_cparams(1),
        name="final_norm",
    )(x2d, g.reshape(1, D_MODEL))


def _rope_rot_cols(w):
    k = w.shape[0]
    ws = w.reshape(k, -1, 2, 2, QK_ROPE // 4)
    return jnp.stack([-ws[:, :, :, 1, :], ws[:, :, :, 0, :]], axis=3).reshape(k, -1)


def _rope_tables(n_tokens):
    rows = n_tokens // GRID_W
    row = jnp.repeat(jnp.arange(rows, dtype=F32), GRID_W)
    col = jnp.tile(jnp.arange(GRID_W, dtype=F32), rows)
    n_freq = QK_ROPE // 4
    inv = ROPE_BASE ** (-jnp.arange(n_freq, dtype=F32) / n_freq)
    ang = jnp.stack([row[:, None] * inv, col[:, None] * inv], axis=1)
    cos = jnp.broadcast_to(jnp.cos(ang)[:, :, None, :], (n_tokens, 2, 2, n_freq)).reshape(n_tokens, QK_ROPE)
    sin = jnp.broadcast_to(jnp.sin(ang)[:, :, None, :], (n_tokens, 2, 2, n_freq)).reshape(n_tokens, QK_ROPE)
    return cos, sin


def kernel(x, c, ctx, c_ctx, w_mod, b_mod, norm1, norm2, w_in, s5_a_re, s5_a_im, s5_log_dt, s5_b_re, s5_b_im,
           s5_c_re, s5_c_im, s5_d, w_glu, q_norm, kv_norm, w_uq, w_ukv, w_mla_o, w_out, w_ffn_in, w_ffn_out,
           norm_f):
    batch, seq, _ = x.shape
    n_ctx = ctx.shape[1]
    assert w_mod.shape[0] == 1, "single-layer block"
    p = dict(s5_a_re=s5_a_re[0], s5_a_im=s5_a_im[0], s5_log_dt=s5_log_dt[0], s5_b_re=s5_b_re[0],
             s5_b_im=s5_b_im[0], s5_c_re=s5_c_re[0], s5_c_im=s5_c_im[0], s5_d=s5_d[0])
    w_in_t = w_in.reshape(w_in.shape[1:]).T

    cv = jnp.concatenate([c, c_ctx[None], jnp.zeros((8 - batch - 1, D_MODEL), F32)], axis=0)
    m = _modulation(cv, w_mod[0], b_mod[0]).reshape(8, 6, 1, D_MODEL)
    m_lat = m[:batch]
    m_ctx = m[batch:batch + 1]

    x2d = x.reshape(batch * seq, D_MODEL)
    c2d = ctx.reshape(batch * n_ctx, D_MODEL)
    xn = _norm_mod(x2d, norm1[0], m_lat[:, 1], m_lat[:, 0], seq, 512)
    cn = _norm_mod(c2d, norm1[0], m_ctx[:, 1], m_ctx[:, 0], batch * n_ctx, 256)

    kv_lo = S5_WIDTH + Q_RANK
    w_kr_rot_t = _rope_rot_cols(w_in_t[kv_lo + KV_RANK:kv_lo + KV_RANK + QK_ROPE].T).T
    wq = w_uq[0].reshape(Q_RANK, MLA_HEADS, QK_NOPE + QK_ROPE)
    wq_rope = wq[:, :, QK_NOPE:].reshape(Q_RANK, MLA_HEADS * QK_ROPE)
    wq2 = jnp.concatenate([wq[:, :, :QK_NOPE].reshape(Q_RANK, MLA_HEADS * QK_NOPE), wq_rope,
                           _rope_rot_cols(wq_rope)], axis=1).astype(BF16)
    w_ukv_bf = w_ukv[0].astype(BF16)
    cos, sin = _rope_tables(seq)
    cos_q, sin_q = jnp.tile(cos, (1, MLA_HEADS)), jnp.tile(sin, (1, MLA_HEADS))

    u_lat = _proj_cast(xn, w_in_t, 0, S5_WIDTH, 1024, 1024, "u_lat", out_dtype=F32)
    u_ctx = _proj_cast(cn, w_in_t, 0, S5_WIDTH, 256, 512, "u_ctx", out_dtype=F32)
    q = _q_path(xn, w_in_t, q_norm[0], wq2, cos_q, sin_q, 1024, seq)
    kv_lat, kr_lat = _kv_path(xn, w_in_t, w_kr_rot_t, kv_norm[0], w_ukv_bf, cos, sin, 1024, seq)
    kv_ctx, kr_ctx = _kv_path(cn, w_in_t, None, kv_norm[0], w_ukv_bf, None, None, 256, n_ctx)
    gates = _proj_cast(xn, w_in_t, kv_lo + KV_RANK + QK_ROPE, 2 * D_MODEL, 2048, 512, "gates",
                       act=jax.nn.sigmoid)

    y5 = _s5_mixer(u_lat, u_ctx, p, batch, seq, n_ctx)
    o_mla = _attention(q, kv_lat, kv_ctx, kr_lat, kr_ctx, batch, seq, n_ctx, 2048)

    merged = _merge(y5, o_mla, gates, w_glu[0], w_mla_o[0], 1024, 512)
    x1, xn2 = _out_proj_norm(merged, w_out[0], x2d, m_lat[:, 2], norm2[0], m_lat[:, 4], m_lat[:, 3], seq, 512)
    hid = _ffn_in(xn2, w_ffn_in[0], 1024, 512)
    x2 = _resid_mm(hid, w_ffn_out[0], x1, m_lat[:, 5], seq, 512, 512, "ffn_out")
    return _final_norm(x2, norm_f, 512).reshape(batch, seq, D_MODEL)
```

```python
import functools
import math

import jax
import jax.numpy as jnp
from jax import lax
from jax.experimental import pallas as pl
from jax.experimental.pallas import tpu as pltpu

F32 = jnp.float32
BF16 = jnp.bfloat16

D_MODEL = 2048
GRID_W = 64
EPS = 1e-6
S5_WIDTH = D_MODEL // 2
S5_GROUP = 16
S5_GROUPS = S5_WIDTH // S5_GROUP
S5_STATE = 64
S5_CHUNK = 16
S5_PAIRS = S5_GROUPS // 2
MLA_HEADS = 8
QK_NOPE = 128
QK_ROPE = 64
V_DIM = 128
Q_RANK = 512
KV_RANK = 256
ROPE_BASE = 10000.0
ATTN_SCALE = (QK_NOPE + QK_ROPE) ** -0.5
D_FF = -(-8 * D_MODEL // (3 * 256)) * 256

VMEM_LIMIT_BYTES = 56 * 1024 * 1024
SUBLANES = 8


def _cparams(n_axes):
    return pltpu.CompilerParams(dimension_semantics=("arbitrary",) * n_axes,
                                vmem_limit_bytes=VMEM_LIMIT_BYTES)


def _rms(x, g):
    return x * lax.rsqrt(jnp.mean(x * x, axis=-1, keepdims=True) + EPS) * g


def _mod_kernel(cv_ref, w_ref, b_ref, o_ref):
    s = jax.nn.silu(cv_ref[...]).astype(BF16)
    o_ref[...] = jnp.dot(s, w_ref[...].astype(BF16), preferred_element_type=F32) + b_ref[...]


def _modulation(cv, w_mod, b_mod):
    n = w_mod.shape[1]
    tn = 1536
    return pl.pallas_call(
        _mod_kernel,
        grid=(n // tn,),
        in_specs=[pl.BlockSpec((8, D_MODEL), lambda j: (0, 0)),
                  pl.BlockSpec((D_MODEL, tn), lambda j: (0, j)),
                  pl.BlockSpec((1, tn), lambda j: (0, j))],
        out_specs=pl.BlockSpec((8, tn), lambda j: (0, j)),
        out_shape=jax.ShapeDtypeStruct((8, n), F32),
        compiler_params=_cparams(1),
        name="mod",
    )(cv, w_mod, b_mod.reshape(1, n))


def _norm_mod_kernel(x_ref, g_ref, sc_ref, sh_ref, o_ref):
    y = _rms(x_ref[...], g_ref[...])
    o_ref[...] = (y * (1.0 + sc_ref[0]) + sh_ref[0]).astype(o_ref.dtype)


def _norm_mod(x2d, gain, scale, shift, rows_per_batch, tm, out_dtype=BF16):
    t = x2d.shape[0]
    tpb = rows_per_batch // tm
    return pl.pallas_call(
        _norm_mod_kernel,
        grid=(t // tm,),
        in_specs=[pl.BlockSpec((tm, D_MODEL), lambda i: (i, 0)),
                  pl.BlockSpec((1, D_MODEL), lambda i: (0, 0)),
                  pl.BlockSpec((1, 1, D_MODEL), lambda i: (i // tpb, 0, 0)),
                  pl.BlockSpec((1, 1, D_MODEL), lambda i: (i // tpb, 0, 0))],
        out_specs=pl.BlockSpec((tm, D_MODEL), lambda i: (i, 0)),
        out_shape=jax.ShapeDtypeStruct((t, D_MODEL), out_dtype),
        compiler_params=_cparams(1),
        name="norm_mod",
    )(x2d, gain.reshape(1, D_MODEL), scale, shift)


def _fused_mm(a, weights, epilogue, extras, outs, *, tm, nj, name):
    t, k = a.shape
    ni = t // tm
    nw, ne, no = len(weights), len(extras), len(outs)
    need_cast = [w.dtype != BF16 for w, _, _, _ in weights]
    nt = (((1,), (1,)), ((), ()))

    def kernel(*refs):
        a_ref = refs[0]
        w_refs = refs[1:1 + nw]
        e_refs = refs[1 + nw:1 + nw + ne]
        o_refs = refs[1 + nw + ne:1 + nw + ne + no]
        s_refs = list(refs[1 + nw + ne + no:])
        staged = {idx: s_refs.pop(0) for idx in range(nw) if need_cast[idx]}
        if staged:
            @pl.when(pl.program_id(1) == 0)
            def _():
                for idx, s_ref in staged.items():
                    s_ref[...] = w_refs[idx][...].astype(BF16)

        av = a_ref[...]
        accs = []
        for idx in range(nw):
            w_ref = staged.get(idx, w_refs[idx])
            if weights[idx][2] == "kn":
                accs.append(jnp.dot(av, w_ref[...], preferred_element_type=F32))
            else:
                accs.append(lax.dot_general(av, w_ref[...], nt, preferred_element_type=F32))
        epilogue(accs, e_refs, o_refs)

    in_specs = [pl.BlockSpec((tm, k), lambda j, i: (i, 0))]
    scratch = []
    for (w, off, layout, width), cast in zip(weights, need_cast):
        if layout == "kn":
            shape = (k, width)
            in_specs.append(pl.BlockSpec(shape, functools.partial(lambda j, i, off: (0, off + j), off=off)))
        else:
            shape = (width, k)
            in_specs.append(pl.BlockSpec(
                (pl.Element(width), pl.Element(k)),
                functools.partial(lambda j, i, off, width: (pl.multiple_of(off + j * width, SUBLANES), 0),
                                  off=off, width=width)))
        if cast:
            scratch.append(pltpu.VMEM(shape, BF16))
    in_specs += [spec for _, spec in extras]
    return pl.pallas_call(
        kernel,
        grid=(nj, ni),
        in_specs=in_specs,
        out_specs=[spec for _, spec in outs],
        out_shape=[sds for sds, _ in outs],
        scratch_shapes=scratch,
        compiler_params=_cparams(2),
        name=name,
    )(a, *[w[0] for w in weights], *[e for e, _ in extras])


def _proj_cast(xn, w_in_t, row0, n, tm, tn, name, act=None, out_dtype=BF16):
    def epi(accs, e_refs, o_refs):
        v = accs[0]
        if act is not None:
            v = act(v)
        o_refs[0][...] = v.astype(out_dtype)

    t = xn.shape[0]
    return _fused_mm(xn, [(w_in_t, row0, "nk", tn)], epi, [],
                     [(jax.ShapeDtypeStruct((t, n), out_dtype), pl.BlockSpec((tm, tn), lambda j, i: (i, j)))],
                     tm=tm, nj=n // tn, name=name)[0]


def _q_path(xn, w_in_t, q_norm, wq2, cos_q, sin_q, tm, seq):
    nr = MLA_HEADS * QK_ROPE
    nn = MLA_HEADS * QK_NOPE

    def epi(accs, e_refs, o_refs):
        qn_ref, w2_ref, cos_ref, sin_ref = e_refs
        cq = _rms(accs[0], qn_ref[...]).astype(BF16)
        q = jnp.dot(cq, w2_ref[...], preferred_element_type=F32)
        rope = q[:, nn:nn + nr] * cos_ref[...] + q[:, nn + nr:] * sin_ref[...]
        o_refs[0][:, :nn] = (q[:, :nn] * ATTN_SCALE).astype(BF16)
        o_refs[0][:, nn:] = (rope * ATTN_SCALE).astype(BF16)

    t = xn.shape[0]
    pos_tiles = seq // tm
    extras = [(q_norm.reshape(1, Q_RANK), pl.BlockSpec((1, Q_RANK), lambda j, i: (0, 0))),
              (wq2, pl.BlockSpec(wq2.shape, lambda j, i: (0, 0))),
              (cos_q, pl.BlockSpec((tm, nr), lambda j, i: (i % pos_tiles, 0))),
              (sin_q, pl.BlockSpec((tm, nr), lambda j, i: (i % pos_tiles, 0)))]
    outs = [(jax.ShapeDtypeStruct((t, nn + nr), BF16), pl.BlockSpec((tm, nn + nr), lambda j, i: (i, 0)))]
    return _fused_mm(xn, [(w_in_t, S5_WIDTH, "nk", Q_RANK)], epi, extras, outs, tm=tm, nj=1, name="q_path")[0]


def _kv_path(xn, w_in_t, w_kr_rot_t, kv_norm, w_ukv_bf, cos_k, sin_k, tm, seq):
    nkv = w_ukv_bf.shape[1]
    rope = cos_k is not None

    def epi(accs, e_refs, o_refs):
        acc = accs[0]
        ckv = _rms(acc[:, :KV_RANK], e_refs[0][...]).astype(BF16)
        o_refs[0][...] = jnp.dot(ckv, e_refs[1][...], preferred_element_type=F32).astype(BF16)
        kr = acc[:, KV_RANK:]
        if rope:
            kr = kr * e_refs[2][...] + accs[1] * e_refs[3][...]
        o_refs[1][...] = kr.astype(BF16)

    t = xn.shape[0]
    pos_tiles = seq // tm
    extras = [(kv_norm.reshape(1, KV_RANK), pl.BlockSpec((1, KV_RANK), lambda j, i: (0, 0))),
              (w_ukv_bf, pl.BlockSpec(w_ukv_bf.shape, lambda j, i: (0, 0)))]
    weights = [(w_in_t, S5_WIDTH + Q_RANK, "nk", KV_RANK + QK_ROPE)]
    if rope:
        weights.append((w_kr_rot_t, 0, "nk", QK_ROPE))
        extras += [(cos_k, pl.BlockSpec((tm, QK_ROPE), lambda j, i: (i % pos_tiles, 0))),
                   (sin_k, pl.BlockSpec((tm, QK_ROPE), lambda j, i: (i % pos_tiles, 0)))]
    outs = [(jax.ShapeDtypeStruct((t, nkv), BF16), pl.BlockSpec((tm, nkv), lambda j, i: (i, 0))),
            (jax.ShapeDtypeStruct((t, QK_ROPE), BF16), pl.BlockSpec((tm, QK_ROPE), lambda j, i: (i, 0)))]
    return _fused_mm(xn, weights, epi, extras, outs, tm=tm, nj=1, name="kv_path")


ATTN_SUB_ROWS = 512


def _attn_kernel(qn_ref, qr_ref, kvl_ref, kvc_ref, krl_ref, krc_ref, o_ref, k_scr, v_scr, *, seq, ctx):
    dk = QK_NOPE + QK_ROPE

    @pl.when(pl.program_id(2) == 0)
    def _():
        for h in range(2):
            base = h * (QK_NOPE + V_DIM)
            k_scr[h, :seq, :QK_NOPE] = kvl_ref[:, base:base + QK_NOPE]
            k_scr[h, seq:, :QK_NOPE] = kvc_ref[:, base:base + QK_NOPE]
            k_scr[h, :seq, QK_NOPE:dk] = krl_ref[...]
            k_scr[h, seq:, QK_NOPE:dk] = krc_ref[...]
            v_scr[h, :seq, :V_DIM] = kvl_ref[:, base + QK_NOPE:base + QK_NOPE + V_DIM]
            v_scr[h, seq:, :V_DIM] = kvc_ref[:, base + QK_NOPE:base + QK_NOPE + V_DIM]
            ones_col = lax.broadcasted_iota(jnp.int32, (seq + ctx, V_DIM), 1) == 0
            v_scr[h, :, V_DIM:] = jnp.where(ones_col, 1.0, 0.0).astype(BF16)

    for r0 in range(0, qn_ref.shape[0], ATTN_SUB_ROWS):
        rows = slice(r0, r0 + ATTN_SUB_ROWS)
        for h in range(2):
            q = jnp.concatenate([qn_ref[rows, h * QK_NOPE:(h + 1) * QK_NOPE],
                                 qr_ref[rows, h * QK_ROPE:(h + 1) * QK_ROPE]], axis=1)
            s = lax.dot_general(q, k_scr[h], (((1,), (1,)), ((), ())), preferred_element_type=F32)
            m = jnp.max(s, axis=-1, keepdims=True)
            p = jnp.exp((s - m).astype(BF16))
            ol = jnp.dot(p, v_scr[h], preferred_element_type=F32)
            o_ref[rows, h * V_DIM:(h + 1) * V_DIM] = (ol[:, :V_DIM] / ol[:, V_DIM:V_DIM + 1]).astype(o_ref.dtype)


def _attention(q, kv_lat, kv_ctx, kr_lat, kr_ctx, batch, seq, ctx, tq):
    nq = seq // tq
    nn_blocks = MLA_HEADS * QK_NOPE // (2 * QK_NOPE)
    dk = QK_NOPE + QK_ROPE
    hw = 2 * (QK_NOPE + V_DIM)
    return pl.pallas_call(
        functools.partial(_attn_kernel, seq=seq, ctx=ctx),
        grid=(batch, MLA_HEADS // 2, nq),
        in_specs=[pl.BlockSpec((tq, 2 * QK_NOPE), lambda b, hp, qi: (b * nq + qi, hp)),
                  pl.BlockSpec((tq, 2 * QK_ROPE), lambda b, hp, qi: (b * nq + qi, 2 * nn_blocks + hp)),
                  pl.BlockSpec((seq, hw), lambda b, hp, qi: (b, hp)),
                  pl.BlockSpec((ctx, hw), lambda b, hp, qi: (b, hp)),
                  pl.BlockSpec((seq, QK_ROPE), lambda b, hp, qi: (b, 0)),
                  pl.BlockSpec((ctx, QK_ROPE), lambda b, hp, qi: (b, 0))],
        out_specs=pl.BlockSpec((tq, 2 * V_DIM), lambda b, hp, qi: (b * nq + qi, hp)),
        out_shape=jax.ShapeDtypeStruct((batch * seq, MLA_HEADS * V_DIM), BF16),
        scratch_shapes=[pltpu.VMEM((2, seq + ctx, dk), BF16), pltpu.VMEM((2, seq + ctx, 2 * V_DIM), BF16)],
        compiler_params=_cparams(3),
        name="attention",
    )(q, q, kv_lat, kv_ctx, kr_lat, kr_ctx)


def _s5_prep_kernel(are_ref, aim_ref, ldt_ref, bre_ref, bim_ref, cre_ref, cim_ref,
                    toep_ref, wsr_ref, wsi_ref, wor_ref, woi_ref, atr_ref, ati_ref):
    tc, g = S5_CHUNK, S5_GROUP
    w = tc * g
    lane = lax.broadcasted_iota(jnp.int32, (g, 2 * S5_STATE), 1)
    in_group = (lane < S5_STATE, lane >= S5_STATE)
    lane_w = lax.broadcasted_iota(jnp.int32, (g, w), 1)
    nt = (((1,), (1,)), ((), ()))
    toep_rows = [[jnp.zeros((g, w), F32) for _ in range(tc)] for _ in range(2)]
    for d in range(2):
        lr, li = are_ref[d, 0], aim_ref[d, 0]
        dt = jnp.exp(ldt_ref[d, 0])
        mag = jnp.exp(lr * dt)
        ab_re, ab_im = mag * jnp.cos(li * dt), mag * jnp.sin(li * dt)
        den = lr * lr + li * li
        nr, ni = ab_re - 1.0, ab_im
        co_re = (nr * lr + ni * li) / den
        co_im = (ni * lr - nr * li) / den
        br, bi = bre_ref[d, 0], bim_ref[d, 0]
        bb_re = co_re * br - co_im * bi
        bb_im = co_re * bi + co_im * br
        pw = [(jnp.ones_like(ab_re), jnp.zeros_like(ab_re))]
        for _ in range(tc):
            pr, pi = pw[-1]
            pw.append((pr * ab_re - pi * ab_im, pr * ab_im + pi * ab_re))
        cr, ci = cre_ref[d, 0], cim_ref[d, 0]
        ca = [(cr * pr - ci * pi, cr * pi + ci * pr) for pr, pi in pw]
        taus = list(range(tc))[::-1] if d else list(range(tc))
        y_re = jnp.concatenate([ca[t][0] for t in taus], axis=0).astype(BF16)
        y_im = jnp.concatenate([ca[t][1] for t in taus], axis=0).astype(BF16)
        for j in range(2):
            x_re = jnp.where(in_group[j], bb_re, 0.0)
            x_im = jnp.where(in_group[j], bb_im, 0.0)
            kt = (lax.dot_general(x_re.astype(BF16), y_re, nt, preferred_element_type=F32)
                  - lax.dot_general(x_im.astype(BF16), y_im, nt, preferred_element_type=F32))
            for r in range(tc):
                sh = (r + 1) * g if d else r * g
                blk = pltpu.roll(kt, sh % w, 1) if sh % w else kt
                keep = (lane_w < sh) if d else (lane_w >= sh)
                toep_rows[j][r] = toep_rows[j][r] + jnp.where(keep, blk, 0.0)
        for r in range(tc):
            pr, pi = pw[r] if d else pw[tc - 1 - r]
            w_re = bb_re * pr - bb_im * pi
            w_im = bb_re * pi + bb_im * pr
            car, cai = ca[tc - r] if d else ca[r + 1]
            for j in range(2):
                rows = slice(j * w + r * g, j * w + (r + 1) * g)
                wsr_ref[d, 0, rows, :] = jnp.where(in_group[j], w_re, 0.0).astype(BF16)
                wsi_ref[d, 0, rows, :] = jnp.where(in_group[j], w_im, 0.0).astype(BF16)
                wor_ref[d, 0, rows, :] = jnp.where(in_group[j], car, 0.0).astype(BF16)
                woi_ref[d, 0, rows, :] = jnp.where(in_group[j], -cai, 0.0).astype(BF16)
        atr_ref[d, 0] = pw[tc][0]
        ati_ref[d, 0] = pw[tc][1]
    for j in range(2):
        toep_ref[0, j] = jnp.concatenate(toep_rows[j], axis=0).astype(BF16)


def _s5_prep(params):
    tc = S5_CHUNK
    wide = 2 * tc * S5_GROUP
    sl = 2 * S5_STATE

    def pair_lanes(v):
        return v.astype(F32).reshape(2, S5_PAIRS, 1, sl)

    def pair_rows(v):
        rows = v.shape[2]
        return v.astype(F32).reshape(2, S5_PAIRS, 2, rows, S5_STATE).transpose(0, 1, 3, 2, 4).reshape(
            2, S5_PAIRS, rows, sl)

    ldt = jnp.broadcast_to(params['s5_log_dt'].astype(F32)[:, :, None], (2, S5_GROUPS, S5_STATE))
    ins = [pair_lanes(params['s5_a_re']), pair_lanes(params['s5_a_im']), pair_lanes(ldt),
           pair_rows(params['s5_b_re'].transpose(0, 1, 3, 2)), pair_rows(params['s5_b_im'].transpose(0, 1, 3, 2)),
           pair_rows(params['s5_c_re']), pair_rows(params['s5_c_im'])]
    vec_spec = pl.BlockSpec((2, 1, 1, sl), lambda k: (0, k, 0, 0))
    mat_spec = pl.BlockSpec((2, 1, S5_GROUP, sl), lambda k: (0, k, 0, 0))
    w_spec = pl.BlockSpec((2, 1, wide, sl), lambda k: (0, k, 0, 0))
    w_sds = jax.ShapeDtypeStruct((2, S5_PAIRS, wide, sl), BF16)
    a_sds = jax.ShapeDtypeStruct((2, S5_PAIRS, 1, sl), F32)
    return pl.pallas_call(
        _s5_prep_kernel,
        grid=(S5_PAIRS,),
        in_specs=[vec_spec] * 3 + [mat_spec] * 4,
        out_specs=[pl.BlockSpec((1, 2, wide // 2, wide // 2), lambda k: (k, 0, 0, 0))] + [w_spec] * 4 + [vec_spec] * 2,
        out_shape=[jax.ShapeDtypeStruct((S5_PAIRS, 2, wide // 2, wide // 2), BF16)] + [w_sds] * 4 + [a_sds] * 2,
        compiler_params=_cparams(1),
        name="s5_prep",
    )(*ins)


S5_STEP_PAIRS = 4
def _s5_state_rows(batch, n_lat, n_ctx):
    def pitch(n):
        p = -(-n // SUBLANES)
        return SUBLANES * (p + 1 - p % 2)

    lat_pitch, ctx_pitch = pitch(n_lat), pitch(n_ctx)
    ctx_base = batch * lat_pitch
    return lat_pitch, ctx_pitch, ctx_base, ctx_base + batch * ctx_pitch


def _s5_kernel(ul_ref, uc_ref, dsk_ref, *refs, batch, seq, ctx):
    toep_ref, wsr, wsi, wor, woi, atr, ati = refs[:7]
    o_ref = refs[7]
    wscr, zscr, upscr, sre, sim, hre, him = refs[8:]
    tc = S5_CHUNK
    n_lat, n_ctx = seq // tc, ctx // tc
    rows_lat = n_lat * batch
    lat_pitch, ctx_pitch, ctx_base, _ = _s5_state_rows(batch, n_lat, n_ctx)
    w = tc * S5_GROUP
    lanes = S5_STEP_PAIRS * 2 * S5_GROUP

    def scatter_tiles(xt, r, col0):
        for kk in range(S5_STEP_PAIRS):
            for j in range(2):
                ch = (2 * kk + j) * S5_GROUP
                wscr[kk, j * w + r * S5_GROUP:j * w + (r + 1) * S5_GROUP, col0:col0 + lanes] = xt[ch:ch + S5_GROUP, :]

    for r in range(tc):
        for b in range(batch):
            x = ul_ref[pl.ds(b * seq + r, n_lat, stride=tc), :]
            scatter_tiles(x.T, r, b * n_lat)
        xc = jnp.concatenate([uc_ref[pl.ds(b * ctx + r, n_ctx, stride=tc), :] for b in range(batch)]
                             + [jnp.zeros((lanes - batch * n_ctx, lanes), F32)], axis=0)
        scatter_tiles(xc.T, r, rows_lat)

    chains = [(kk, d) for kk in range(S5_STEP_PAIRS) for d in range(2)]
    for kk in range(S5_STEP_PAIRS):
        upscr[kk] = wscr[kk].T.astype(BF16)
        for d in range(2):
            for dst, wst in ((sre, wsr), (sim, wsi)):
                s = jnp.dot(upscr[kk], wst[d, kk], preferred_element_type=F32)
                for b in range(batch):
                    dst[kk, d, b * lat_pitch:b * lat_pitch + n_lat, :] = s[b * n_lat:(b + 1) * n_lat]
                    dst[kk, d, ctx_base + b * ctx_pitch:ctx_base + b * ctx_pitch + n_ctx, :] = (
                        s[rows_lat + b * n_ctx:rows_lat + (b + 1) * n_ctx])

    ctx_rows = [pl.ds(ctx_base + c, batch, stride=ctx_pitch) for c in range(n_ctx)]
    lat_rows = [pl.ds(c, batch, stride=lat_pitch) for c in range(n_lat)]
    order = (ctx_rows + lat_rows, ctx_rows[::-1] + lat_rows[::-1])
    coef = {(kk, d): (atr[d, kk], ati[d, kk]) for kk, d in chains}
    state = {ch: (jnp.zeros((batch, 2 * S5_STATE), F32), jnp.zeros((batch, 2 * S5_STATE), F32)) for ch in chains}
    for t in range(n_ctx + n_lat):
        for kk, d in chains:
            rows = order[d][t]
            (ar, ai), (h_re, h_im) = coef[kk, d], state[kk, d]
            hre[kk, d, rows, :] = h_re
            him[kk, d, rows, :] = h_im
            state[kk, d] = (ar * h_re - ai * h_im + sre[kk, d, rows, :],
                            ar * h_im + ai * h_re + sim[kk, d, rows, :])

    for kk in range(S5_STEP_PAIRS):
        ul = upscr[kk, :rows_lat, :]
        y = ul.astype(F32) * dsk_ref[kk]
        y = y + jnp.concatenate(
            [jnp.dot(ul[:, :w], toep_ref[kk, 0], preferred_element_type=F32),
             jnp.dot(ul[:, w:], toep_ref[kk, 1], preferred_element_type=F32)], axis=1)
        nt = (((1,), (1,)), ((), ()))
        for d in range(2):
            h_r = jnp.concatenate([hre[kk, d, b * lat_pitch:b * lat_pitch + n_lat, :] for b in range(batch)], axis=0)
            h_i = jnp.concatenate([him[kk, d, b * lat_pitch:b * lat_pitch + n_lat, :] for b in range(batch)], axis=0)
            y = y + lax.dot_general(h_r.astype(BF16), wor[d, kk], nt, preferred_element_type=F32)
            y = y + lax.dot_general(h_i.astype(BF16), woi[d, kk], nt, preferred_element_type=F32)
        yt = y.T
        for b in range(batch):
            for s in range(tc):
                for j in range(2):
                    ch = (2 * kk + j) * S5_GROUP
                    zscr[b * tc + s, ch:ch + S5_GROUP, :] = yt[j * w + s * S5_GROUP:j * w + (s + 1) * S5_GROUP,
                                                               b * n_lat:(b + 1) * n_lat]

    for b in range(batch):
        for s in range(tc):
            o_ref[pl.ds(b * seq + s, n_lat, stride=tc), :] = jax.nn.gelu(zscr[b * tc + s].T)


def _s5_mixer(u_lat, u_ctx, params, batch, seq, ctx):
    tc = S5_CHUNK
    n_lat, n_ctx = seq // tc, ctx // tc
    sp = S5_STEP_PAIRS
    lanes = sp * 2 * S5_GROUP
    assert n_lat == lanes and batch * n_ctx <= lanes
    rows_lat = n_lat * batch
    rows_all = rows_lat + lanes
    state_rows = _s5_state_rows(batch, n_lat, n_ctx)[3]
    wide = 2 * tc * S5_GROUP
    dsk = jnp.broadcast_to(params['s5_d'].astype(F32).reshape(S5_PAIRS, 2, 1, S5_GROUP),
                           (S5_PAIRS, 2, tc, S5_GROUP)).reshape(S5_PAIRS, 1, wide)
    weights = _s5_prep(params)
    w_spec = pl.BlockSpec((2, sp, wide, 2 * S5_STATE), lambda k: (0, k, 0, 0))
    a_spec = pl.BlockSpec((2, sp, 1, 2 * S5_STATE), lambda k: (0, k, 0, 0))
    specs = [pl.BlockSpec((sp, 2, wide // 2, wide // 2), lambda k: (k, 0, 0, 0))] + [w_spec] * 4 + [a_spec] * 2
    return pl.pallas_call(
        functools.partial(_s5_kernel, batch=batch, seq=seq, ctx=ctx),
        grid=(S5_PAIRS // sp,),
        in_specs=[pl.BlockSpec((batch * seq, lanes), lambda k: (0, k)),
                  pl.BlockSpec((batch * ctx, lanes), lambda k: (0, k)),
                  pl.BlockSpec((sp, 1, wide), lambda k: (k, 0, 0))] + specs,
        out_specs=pl.BlockSpec((batch * seq, lanes), lambda k: (0, k)),
        out_shape=jax.ShapeDtypeStruct((batch * seq, S5_WIDTH), F32),
        scratch_shapes=[pltpu.VMEM((sp, wide, rows_all), F32),
                        pltpu.VMEM((batch * tc, lanes, n_lat), F32),
                        pltpu.VMEM((sp, rows_all, wide), BF16)]
                       + [pltpu.VMEM((sp, 2, state_rows, 2 * S5_STATE), F32) for _ in range(4)],
        compiler_params=_cparams(1),
        name="s5",
    )(u_lat, u_ctx, dsk, *weights)


def _merge_kernel(y5_ref, o_ref_in, gs_ref, gm_ref, wa_ref, wb_ref, wm_ref, out_ref, wa_s, wb_s, wm_s):
    @pl.when(pl.program_id(1) == 0)
    def _():
        wa_s[...] = wa_ref[...].astype(BF16)
        wb_s[...] = wb_ref[...].astype(BF16)
        wm_s[...] = wm_ref[...].astype(BF16)

    z = y5_ref[...].astype(BF16)
    a = jnp.dot(z, wa_s[...], preferred_element_type=F32)
    b = jnp.dot(z, wb_s[...], preferred_element_type=F32)
    mla = jnp.dot(o_ref_in[...], wm_s[...], preferred_element_type=F32)
    merged = gs_ref[...].astype(F32) * (a * jax.nn.sigmoid(b)) + gm_ref[...].astype(F32) * mla
    out_ref[...] = merged.astype(out_ref.dtype)


def _merge(y5, o_mla, gates, w_glu, w_mla_o, tm, tn):
    t = y5.shape[0]
    nj = D_MODEL // tn
    return pl.pallas_call(
        _merge_kernel,
        grid=(nj, t // tm),
        in_specs=[pl.BlockSpec((tm, S5_WIDTH), lambda j, i: (i, 0)),
                  pl.BlockSpec((tm, MLA_HEADS * V_DIM), lambda j, i: (i, 0)),
                  pl.BlockSpec((tm, tn), lambda j, i: (i, j)),
                  pl.BlockSpec((tm, tn), lambda j, i: (i, nj + j)),
                  pl.BlockSpec((S5_WIDTH, tn), lambda j, i: (0, j)),
                  pl.BlockSpec((S5_WIDTH, tn), lambda j, i: (0, nj + j)),
                  pl.BlockSpec((MLA_HEADS * V_DIM, tn), lambda j, i: (0, j))],
        out_specs=pl.BlockSpec((tm, tn), lambda j, i: (i, j)),
        out_shape=jax.ShapeDtypeStruct((t, D_MODEL), BF16),
        scratch_shapes=[pltpu.VMEM((S5_WIDTH, tn), BF16), pltpu.VMEM((S5_WIDTH, tn), BF16),
                        pltpu.VMEM((MLA_HEADS * V_DIM, tn), BF16)],
        compiler_params=_cparams(2),
        name="merge",
    )(y5, o_mla, gates, gates, w_glu, w_glu, w_mla_o)


def _resid_mm(a, w, x_res, gate, rows_per_batch, tm, tn, name):
    def epi(accs, e_refs, o_refs):
        o_refs[0][...] = e_refs[0][...] + e_refs[1][0] * accs[0]

    t = a.shape[0]
    n = w.shape[1]
    tpb = rows_per_batch // tm
    extras = [(x_res, pl.BlockSpec((tm, tn), lambda j, i: (i, j))),
              (gate, pl.BlockSpec((1, 1, tn), lambda j, i: (i // tpb, 0, j)))]
    outs = [(jax.ShapeDtypeStruct((t, n), F32), pl.BlockSpec((tm, tn), lambda j, i: (i, j)))]
    return _fused_mm(a, [(w, 0, "kn", tn)], epi, extras, outs, tm=tm, nj=n // tn, name=name)[0]


def _out_proj_norm_kernel(a_ref, w_ref, x_ref, g1_ref, n2_ref, sc_ref, sh_ref, x1_ref, xn_ref, w_s):
    @pl.when(pl.program_id(0) == 0)
    def _():
        w_s[...] = w_ref[...].astype(BF16)

    x1 = x_ref[...] + g1_ref[0] * jnp.dot(a_ref[...], w_s[...], preferred_element_type=F32)
    x1_ref[...] = x1
    xn_ref[...] = (_rms(x1, n2_ref[...]) * (1.0 + sc_ref[0]) + sh_ref[0]).astype(xn_ref.dtype)


def _out_proj_norm(merged, w_out, x_res, gate, gain2, scale2, shift2, rows_per_batch, tm):
    t, k = merged.shape
    tpb = rows_per_batch // tm
    row = pl.BlockSpec((tm, D_MODEL), lambda i: (i, 0))
    per_batch = pl.BlockSpec((1, 1, D_MODEL), lambda i: (i // tpb, 0, 0))
    return pl.pallas_call(
        _out_proj_norm_kernel,
        grid=(t // tm,),
        in_specs=[pl.BlockSpec((tm, k), lambda i: (i, 0)),
                  pl.BlockSpec((k, D_MODEL), lambda i: (0, 0), pipeline_mode=pl.Buffered(1)),
                  row, per_batch, pl.BlockSpec((1, D_MODEL), lambda i: (0, 0)), per_batch, per_batch],
        out_specs=[row, row],
        out_shape=[jax.ShapeDtypeStruct((t, D_MODEL), F32), jax.ShapeDtypeStruct((t, D_MODEL), BF16)],
        scratch_shapes=[pltpu.VMEM((k, D_MODEL), BF16)],
        compiler_params=_cparams(1),
        name="out_proj",
    )(merged, w_out, x_res, gate, gain2.reshape(1, D_MODEL), scale2, shift2)


def _ffn_in(xn, w_ffn_in, tm, tn):
    def epi(accs, e_refs, o_refs):
        o_refs[0][...] = (jax.nn.silu(accs[0]) * accs[1]).astype(BF16)

    t = xn.shape[0]
    nj = D_FF // tn
    outs = [(jax.ShapeDtypeStruct((t, D_FF), BF16), pl.BlockSpec((tm, tn), lambda j, i: (i, j)))]
    return _fused_mm(xn, [(w_ffn_in, 0, "kn", tn), (w_ffn_in, nj, "kn", tn)], epi, [], outs,
                     tm=tm, nj=nj, name="ffn_in")[0]


def _final_norm_kernel(x_ref, g_ref, o_ref):
    o_ref[...] = _rms(x_ref[...], g_ref[...])


def _final_norm(x2d, g, tm):
    t = x2d.shape[0]
    return pl.pallas_call(
        _final_norm_kernel,
        grid=(t // tm,),
        in_specs=[pl.BlockSpec((tm, D_MODEL), lambda i: (i, 0)),
                  pl.BlockSpec((1, D_MODEL), lambda i: (0, 0))],
        out_specs=pl.BlockSpec((tm, D_MODEL), lambda i: (i, 0)),
        out_shape=jax.ShapeDtypeStruct((t, D_MODEL), F32),
        compiler_params=_cparams(1),
        name="final_norm",
    )(x2d, g.reshape(1, D_MODEL))


def _rope_rot_cols(w):
    k = w.shape[0]
    ws = w.reshape(k, -1, 2, 2, QK_ROPE // 4)
    return jnp.stack([-ws[:, :, :, 1, :], ws[:, :, :, 0, :]], axis=3).reshape(k, -1)


def _rope_tables(n_tokens):
    rows = n_tokens // GRID_W
    row = jnp.repeat(jnp.arange(rows, dtype=F32), GRID_W)
    col = jnp.tile(jnp.arange(GRID_W, dtype=F32), rows)
    n_freq = QK_ROPE // 4
    inv = ROPE_BASE ** (-jnp.arange(n_freq, dtype=F32) / n_freq)
    ang = jnp.stack([row[:, None] * inv, col[:, None] * inv], axis=1)
    cos = jnp.broadcast_to(jnp.cos(ang)[:, :, None, :], (n_tokens, 2, 2, n_freq)).reshape(n_tokens, QK_ROPE)
    sin = jnp.broadcast_to(jnp.sin(ang)[:, :, None, :], (n_tokens, 2, 2, n_freq)).reshape(n_tokens, QK_ROPE)
    return cos, sin


def kernel(x, c, ctx, c_ctx, w_mod, b_mod, norm1, norm2, w_in, s5_a_re, s5_a_im, s5_log_dt, s5_b_re, s5_b_im,
           s5_c_re, s5_c_im, s5_d, w_glu, q_norm, kv_norm, w_uq, w_ukv, w_mla_o, w_out, w_ffn_in, w_ffn_out,
           norm_f):
    batch, seq, _ = x.shape
    n_ctx = ctx.shape[1]
    assert w_mod.shape[0] == 1, "single-layer block"
    p = dict(s5_a_re=s5_a_re[0], s5_a_im=s5_a_im[0], s5_log_dt=s5_log_dt[0], s5_b_re=s5_b_re[0],
             s5_b_im=s5_b_im[0], s5_c_re=s5_c_re[0], s5_c_im=s5_c_im[0], s5_d=s5_d[0])
    w_in_t = w_in.reshape(w_in.shape[1:]).T

    cv = jnp.concatenate([c, c_ctx[None], jnp.zeros((8 - batch - 1, D_MODEL), F32)], axis=0)
    m = _modulation(cv, w_mod[0], b_mod[0]).reshape(8, 6, 1, D_MODEL)
    m_lat = m[:batch]
    m_ctx = m[batch:batch + 1]

    x2d = x.reshape(batch * seq, D_MODEL)
    c2d = ctx.reshape(batch * n_ctx, D_MODEL)
    xn = _norm_mod(x2d, norm1[0], m_lat[:, 1], m_lat[:, 0], seq, 512)
    cn = _norm_mod(c2d, norm1[0], m_ctx[:, 1], m_ctx[:, 0], batch * n_ctx, 256)

    kv_lo = S5_WIDTH + Q_RANK
    w_kr_rot_t = _rope_rot_cols(w_in_t[kv_lo + KV_RANK:kv_lo + KV_RANK + QK_ROPE].T).T
    wq = w_uq[0].reshape(Q_RANK, MLA_HEADS, QK_NOPE + QK_ROPE)
    wq_rope = wq[:, :, QK_NOPE:].reshape(Q_RANK, MLA_HEADS * QK_ROPE)
    wq2 = jnp.concatenate([wq[:, :, :QK_NOPE].reshape(Q_RANK, MLA_HEADS * QK_NOPE), wq_rope,
                           _rope_rot_cols(wq_rope)], axis=1).astype(BF16)
    w_ukv_bf = w_ukv[0].astype(BF16)
    cos, sin = _rope_tables(seq)
    cos_q, sin_q = jnp.tile(cos, (1, MLA_HEADS)), jnp.tile(sin, (1, MLA_HEADS))

    u_lat = _proj_cast(xn, w_in_t, 0, S5_WIDTH, 1024, 1024, "u_lat", out_dtype=F32)
    u_ctx = _proj_cast(cn, w_in_t, 0, S5_WIDTH, 256, 512, "u_ctx", out_dtype=F32)
    q = _q_path(xn, w_in_t, q_norm[0], wq2, cos_q, sin_q, 1024, seq)
    kv_lat, kr_lat = _kv_path(xn, w_in_t, w_kr_rot_t, kv_norm[0], w_ukv_bf, cos, sin, 1024, seq)
    kv_ctx, kr_ctx = _kv_path(cn, w_in_t, None, kv_norm[0], w_ukv_bf, None, None, 256, n_ctx)
    gates = _proj_cast(xn, w_in_t, kv_lo + KV_RANK + QK_ROPE, 2 * D_MODEL, 2048, 512, "gates",
                       act=jax.nn.sigmoid)

    y5 = _s5_mixer(u_lat, u_ctx, p, batch, seq, n_ctx)
    o_mla = _attention(q, kv_lat, kv_ctx, kr_lat, kr_ctx, batch, seq, n_ctx, 2048)

    merged = _merge(y5, o_mla, gates, w_glu[0], w_mla_o[0], 1024, 512)
    x1, xn2 = _out_proj_norm(merged, w_out[0], x2d, m_lat[:, 2], norm2[0], m_lat[:, 4], m_lat[:, 3], seq, 512)
    hid = _ffn_in(xn2, w_ffn_in[0], 2048, 512)
    x2 = _resid_mm(hid, w_ffn_out[0], x1, m_lat[:, 5], seq, 512, 512, "ffn_out")
    return _final_norm(x2, norm_f, 512).reshape(batch, seq, D_MODEL)
```

```python
import functools
import math

import jax
import jax.numpy as jnp
from jax import lax
from jax.experimental import pallas as pl
from jax.experimental.pallas import tpu as pltpu

F32 = jnp.float32
BF16 = jnp.bfloat16

D_MODEL = 2048
GRID_W = 64
EPS = 1e-6
S5_WIDTH = D_MODEL // 2
S5_GROUP = 16
S5_GROUPS = S5_WIDTH // S5_GROUP
S5_STATE = 64
S5_CHUNK = 16
S5_PAIRS = S5_GROUPS // 2
MLA_HEADS = 8
QK_NOPE = 128
QK_ROPE = 64
V_DIM = 128
Q_RANK = 512
KV_RANK = 256
ROPE_BASE = 10000.0
ATTN_SCALE = (QK_NOPE + QK_ROPE) ** -0.5
D_FF = -(-8 * D_MODEL // (3 * 256)) * 256

VMEM_LIMIT_BYTES = 56 * 1024 * 1024
SUBLANES = 8


def _cparams(n_axes):
    return pltpu.CompilerParams(dimension_semantics=("arbitrary",) * n_axes,
                                vmem_limit_bytes=VMEM_LIMIT_BYTES)


def _rms(x, g):
    return x * lax.rsqrt(jnp.mean(x * x, axis=-1, keepdims=True) + EPS) * g


def _mod_kernel(cv_ref, w_ref, b_ref, o_ref):
    s = jax.nn.silu(cv_ref[...]).astype(BF16)
    o_ref[...] = jnp.dot(s, w_ref[...].astype(BF16), preferred_element_type=F32) + b_ref[...]


def _modulation(cv, w_mod, b_mod):
    n = w_mod.shape[1]
    tn = 1536
    return pl.pallas_call(
        _mod_kernel,
        grid=(n // tn,),
        in_specs=[pl.BlockSpec((8, D_MODEL), lambda j: (0, 0)),
                  pl.BlockSpec((D_MODEL, tn), lambda j: (0, j)),
                  pl.BlockSpec((1, tn), lambda j: (0, j))],
        out_specs=pl.BlockSpec((8, tn), lambda j: (0, j)),
        out_shape=jax.ShapeDtypeStruct((8, n), F32),
        compiler_params=_cparams(1),
        name="mod",
    )(cv, w_mod, b_mod.reshape(1, n))


def _norm_mod_kernel(x_ref, g_ref, sc_ref, sh_ref, o_ref):
    y = _rms(x_ref[...], g_ref[...])
    o_ref[...] = (y * (1.0 + sc_ref[0]) + sh_ref[0]).astype(o_ref.dtype)


def _norm_mod(x2d, gain, scale, shift, rows_per_batch, tm, out_dtype=BF16):
    t = x2d.shape[0]
    tpb = rows_per_batch // tm
    return pl.pallas_call(
        _norm_mod_kernel,
        grid=(t // tm,),
        in_specs=[pl.BlockSpec((tm, D_MODEL), lambda i: (i, 0)),
                  pl.BlockSpec((1, D_MODEL), lambda i: (0, 0)),
                  pl.BlockSpec((1, 1, D_MODEL), lambda i: (i // tpb, 0, 0)),
                  pl.BlockSpec((1, 1, D_MODEL), lambda i: (i // tpb, 0, 0))],
        out_specs=pl.BlockSpec((tm, D_MODEL), lambda i: (i, 0)),
        out_shape=jax.ShapeDtypeStruct((t, D_MODEL), out_dtype),
        compiler_params=_cparams(1),
        name="norm_mod",
    )(x2d, gain.reshape(1, D_MODEL), scale, shift)


def _fused_mm(a, weights, epilogue, extras, outs, *, tm, nj, name, weight_buffers=2):
    t, k = a.shape
    ni = t // tm
    nw, ne, no = len(weights), len(extras), len(outs)
    need_cast = [w.dtype != BF16 for w, _, _, _ in weights]
    nt = (((1,), (1,)), ((), ()))

    def kernel(*refs):
        a_ref = refs[0]
        w_refs = refs[1:1 + nw]
        e_refs = refs[1 + nw:1 + nw + ne]
        o_refs = refs[1 + nw + ne:1 + nw + ne + no]
        s_refs = list(refs[1 + nw + ne + no:])
        staged = {idx: s_refs.pop(0) for idx in range(nw) if need_cast[idx]}
        if staged:
            @pl.when(pl.program_id(1) == 0)
            def _():
                for idx, s_ref in staged.items():
                    s_ref[...] = w_refs[idx][...].astype(BF16)

        av = a_ref[...]
        accs = []
        for idx in range(nw):
            w_ref = staged.get(idx, w_refs[idx])
            if weights[idx][2] == "kn":
                accs.append(jnp.dot(av, w_ref[...], preferred_element_type=F32))
            else:
                accs.append(lax.dot_general(av, w_ref[...], nt, preferred_element_type=F32))
        epilogue(accs, e_refs, o_refs)

    in_specs = [pl.BlockSpec((tm, k), lambda j, i: (i, 0))]
    scratch = []
    for (w, off, layout, width), cast in zip(weights, need_cast):
        if layout == "kn":
            shape = (k, width)
            mode = {} if weight_buffers == 2 else {"pipeline_mode": pl.Buffered(weight_buffers)}
            in_specs.append(pl.BlockSpec(shape, functools.partial(lambda j, i, off: (0, off + j), off=off), **mode))
        else:
            shape = (width, k)
            in_specs.append(pl.BlockSpec(
                (pl.Element(width), pl.Element(k)),
                functools.partial(lambda j, i, off, width: (pl.multiple_of(off + j * width, SUBLANES), 0),
                                  off=off, width=width)))
        if cast:
            scratch.append(pltpu.VMEM(shape, BF16))
    in_specs += [spec for _, spec in extras]
    return pl.pallas_call(
        kernel,
        grid=(nj, ni),
        in_specs=in_specs,
        out_specs=[spec for _, spec in outs],
        out_shape=[sds for sds, _ in outs],
        scratch_shapes=scratch,
        compiler_params=_cparams(2),
        name=name,
    )(a, *[w[0] for w in weights], *[e for e, _ in extras])


def _proj_cast(xn, w_in_t, row0, n, tm, tn, name, act=None, out_dtype=BF16):
    def epi(accs, e_refs, o_refs):
        v = accs[0]
        if act is not None:
            v = act(v)
        o_refs[0][...] = v.astype(out_dtype)

    t = xn.shape[0]
    return _fused_mm(xn, [(w_in_t, row0, "nk", tn)], epi, [],
                     [(jax.ShapeDtypeStruct((t, n), out_dtype), pl.BlockSpec((tm, tn), lambda j, i: (i, j)))],
                     tm=tm, nj=n // tn, name=name)[0]


def _q_path(xn, w_in_t, q_norm, wq2, cos_q, sin_q, tm, seq):
    nr = MLA_HEADS * QK_ROPE
    nn = MLA_HEADS * QK_NOPE

    def epi(accs, e_refs, o_refs):
        qn_ref, w2_ref, cos_ref, sin_ref = e_refs
        cq = _rms(accs[0], qn_ref[...]).astype(BF16)
        q = jnp.dot(cq, w2_ref[...], preferred_element_type=F32)
        rope = q[:, nn:nn + nr] * cos_ref[...] + q[:, nn + nr:] * sin_ref[...]
        o_refs[0][:, :nn] = (q[:, :nn] * ATTN_SCALE).astype(BF16)
        o_refs[0][:, nn:] = (rope * ATTN_SCALE).astype(BF16)

    t = xn.shape[0]
    pos_tiles = seq // tm
    extras = [(q_norm.reshape(1, Q_RANK), pl.BlockSpec((1, Q_RANK), lambda j, i: (0, 0))),
              (wq2, pl.BlockSpec(wq2.shape, lambda j, i: (0, 0))),
              (cos_q, pl.BlockSpec((tm, nr), lambda j, i: (i % pos_tiles, 0))),
              (sin_q, pl.BlockSpec((tm, nr), lambda j, i: (i % pos_tiles, 0)))]
    outs = [(jax.ShapeDtypeStruct((t, nn + nr), BF16), pl.BlockSpec((tm, nn + nr), lambda j, i: (i, 0)))]
    return _fused_mm(xn, [(w_in_t, S5_WIDTH, "nk", Q_RANK)], epi, extras, outs, tm=tm, nj=1, name="q_path")[0]


def _kv_path(xn, w_in_t, w_kr_rot_t, kv_norm, w_ukv_bf, cos_k, sin_k, tm, seq):
    nkv = w_ukv_bf.shape[1]
    rope = cos_k is not None

    def epi(accs, e_refs, o_refs):
        acc = accs[0]
        ckv = _rms(acc[:, :KV_RANK], e_refs[0][...]).astype(BF16)
        o_refs[0][...] = jnp.dot(ckv, e_refs[1][...], preferred_element_type=F32).astype(BF16)
        kr = acc[:, KV_RANK:]
        if rope:
            kr = kr * e_refs[2][...] + accs[1] * e_refs[3][...]
        o_refs[1][...] = kr.astype(BF16)

    t = xn.shape[0]
    pos_tiles = seq // tm
    extras = [(kv_norm.reshape(1, KV_RANK), pl.BlockSpec((1, KV_RANK), lambda j, i: (0, 0))),
              (w_ukv_bf, pl.BlockSpec(w_ukv_bf.shape, lambda j, i: (0, 0)))]
    weights = [(w_in_t, S5_WIDTH + Q_RANK, "nk", KV_RANK + QK_ROPE)]
    if rope:
        weights.append((w_kr_rot_t, 0, "nk", QK_ROPE))
        extras += [(cos_k, pl.BlockSpec((tm, QK_ROPE), lambda j, i: (i % pos_tiles, 0))),
                   (sin_k, pl.BlockSpec((tm, QK_ROPE), lambda j, i: (i % pos_tiles, 0)))]
    outs = [(jax.ShapeDtypeStruct((t, nkv), BF16), pl.BlockSpec((tm, nkv), lambda j, i: (i, 0))),
            (jax.ShapeDtypeStruct((t, QK_ROPE), BF16), pl.BlockSpec((tm, QK_ROPE), lambda j, i: (i, 0)))]
    return _fused_mm(xn, weights, epi, extras, outs, tm=tm, nj=1, name="kv_path")


ATTN_SUB_ROWS = 512


def _attn_kernel(qn_ref, qr_ref, kvl_ref, kvc_ref, krl_ref, krc_ref, o_ref, k_scr, v_scr, *, seq, ctx):
    dk = QK_NOPE + QK_ROPE

    @pl.when(pl.program_id(2) == 0)
    def _():
        for h in range(2):
            base = h * (QK_NOPE + V_DIM)
            k_scr[h, :seq, :QK_NOPE] = kvl_ref[:, base:base + QK_NOPE]
            k_scr[h, seq:, :QK_NOPE] = kvc_ref[:, base:base + QK_NOPE]
            k_scr[h, :seq, QK_NOPE:dk] = krl_ref[...]
            k_scr[h, seq:, QK_NOPE:dk] = krc_ref[...]
            v_scr[h, :seq, :V_DIM] = kvl_ref[:, base + QK_NOPE:base + QK_NOPE + V_DIM]
            v_scr[h, seq:, :V_DIM] = kvc_ref[:, base + QK_NOPE:base + QK_NOPE + V_DIM]
            ones_col = lax.broadcasted_iota(jnp.int32, (seq + ctx, V_DIM), 1) == 0
            v_scr[h, :, V_DIM:] = jnp.where(ones_col, 1.0, 0.0).astype(BF16)

    for r0 in range(0, qn_ref.shape[0], ATTN_SUB_ROWS):
        rows = slice(r0, r0 + ATTN_SUB_ROWS)
        for h in range(2):
            q = jnp.concatenate([qn_ref[rows, h * QK_NOPE:(h + 1) * QK_NOPE],
                                 qr_ref[rows, h * QK_ROPE:(h + 1) * QK_ROPE]], axis=1)
            s = lax.dot_general(q, k_scr[h], (((1,), (1,)), ((), ())), preferred_element_type=F32)
            m = jnp.max(s, axis=-1, keepdims=True)
            p = jnp.exp((s - m).astype(BF16))
            ol = jnp.dot(p, v_scr[h], preferred_element_type=F32)
            o_ref[rows, h * V_DIM:(h + 1) * V_DIM] = (ol[:, :V_DIM] / ol[:, V_DIM:V_DIM + 1]).astype(o_ref.dtype)


def _attention(q, kv_lat, kv_ctx, kr_lat, kr_ctx, batch, seq, ctx, tq):
    nq = seq // tq
    nn_blocks = MLA_HEADS * QK_NOPE // (2 * QK_NOPE)
    dk = QK_NOPE + QK_ROPE
    hw = 2 * (QK_NOPE + V_DIM)
    return pl.pallas_call(
        functools.partial(_attn_kernel, seq=seq, ctx=ctx),
        grid=(batch, MLA_HEADS // 2, nq),
        in_specs=[pl.BlockSpec((tq, 2 * QK_NOPE), lambda b, hp, qi: (b * nq + qi, hp)),
                  pl.BlockSpec((tq, 2 * QK_ROPE), lambda b, hp, qi: (b * nq + qi, 2 * nn_blocks + hp)),
                  pl.BlockSpec((seq, hw), lambda b, hp, qi: (b, hp)),
                  pl.BlockSpec((ctx, hw), lambda b, hp, qi: (b, hp)),
                  pl.BlockSpec((seq, QK_ROPE), lambda b, hp, qi: (b, 0)),
                  pl.BlockSpec((ctx, QK_ROPE), lambda b, hp, qi: (b, 0))],
        out_specs=pl.BlockSpec((tq, 2 * V_DIM), lambda b, hp, qi: (b * nq + qi, hp)),
        out_shape=jax.ShapeDtypeStruct((batch * seq, MLA_HEADS * V_DIM), BF16),
        scratch_shapes=[pltpu.VMEM((2, seq + ctx, dk), BF16), pltpu.VMEM((2, seq + ctx, 2 * V_DIM), BF16)],
        compiler_params=_cparams(3),
        name="attention",
    )(q, q, kv_lat, kv_ctx, kr_lat, kr_ctx)


def _s5_prep_kernel(are_ref, aim_ref, ldt_ref, bre_ref, bim_ref, cre_ref, cim_ref,
                    toep_ref, wsr_ref, wsi_ref, wor_ref, woi_ref, atr_ref, ati_ref):
    tc, g = S5_CHUNK, S5_GROUP
    w = tc * g
    lane = lax.broadcasted_iota(jnp.int32, (g, 2 * S5_STATE), 1)
    in_group = (lane < S5_STATE, lane >= S5_STATE)
    lane_w = lax.broadcasted_iota(jnp.int32, (g, w), 1)
    nt = (((1,), (1,)), ((), ()))
    toep_rows = [[jnp.zeros((g, w), F32) for _ in range(tc)] for _ in range(2)]
    for d in range(2):
        lr, li = are_ref[d, 0], aim_ref[d, 0]
        dt = jnp.exp(ldt_ref[d, 0])
        mag = jnp.exp(lr * dt)
        ab_re, ab_im = mag * jnp.cos(li * dt), mag * jnp.sin(li * dt)
        den = lr * lr + li * li
        nr, ni = ab_re - 1.0, ab_im
        co_re = (nr * lr + ni * li) / den
        co_im = (ni * lr - nr * li) / den
        br, bi = bre_ref[d, 0], bim_ref[d, 0]
        bb_re = co_re * br - co_im * bi
        bb_im = co_re * bi + co_im * br
        pw = [(jnp.ones_like(ab_re), jnp.zeros_like(ab_re))]
        for _ in range(tc):
            pr, pi = pw[-1]
            pw.append((pr * ab_re - pi * ab_im, pr * ab_im + pi * ab_re))
        cr, ci = cre_ref[d, 0], cim_ref[d, 0]
        ca = [(cr * pr - ci * pi, cr * pi + ci * pr) for pr, pi in pw]
        taus = list(range(tc))[::-1] if d else list(range(tc))
        y_re = jnp.concatenate([ca[t][0] for t in taus], axis=0).astype(BF16)
        y_im = jnp.concatenate([ca[t][1] for t in taus], axis=0).astype(BF16)
        for j in range(2):
            x_re = jnp.where(in_group[j], bb_re, 0.0)
            x_im = jnp.where(in_group[j], bb_im, 0.0)
            kt = (lax.dot_general(x_re.astype(BF16), y_re, nt, preferred_element_type=F32)
                  - lax.dot_general(x_im.astype(BF16), y_im, nt, preferred_element_type=F32))
            for r in range(tc):
                sh = (r + 1) * g if d else r * g
                blk = pltpu.roll(kt, sh % w, 1) if sh % w else kt
                keep = (lane_w < sh) if d else (lane_w >= sh)
                toep_rows[j][r] = toep_rows[j][r] + jnp.where(keep, blk, 0.0)
        for r in range(tc):
            pr, pi = pw[r] if d else pw[tc - 1 - r]
            w_re = bb_re * pr - bb_im * pi
            w_im = bb_re * pi + bb_im * pr
            car, cai = ca[tc - r] if d else ca[r + 1]
            for j in range(2):
                rows = slice(j * w + r * g, j * w + (r + 1) * g)
                wsr_ref[d, 0, rows, :] = jnp.where(in_group[j], w_re, 0.0).astype(BF16)
                wsi_ref[d, 0, rows, :] = jnp.where(in_group[j], w_im, 0.0).astype(BF16)
                wor_ref[d, 0, rows, :] = jnp.where(in_group[j], car, 0.0).astype(BF16)
                woi_ref[d, 0, rows, :] = jnp.where(in_group[j], -cai, 0.0).astype(BF16)
        atr_ref[d, 0] = pw[tc][0]
        ati_ref[d, 0] = pw[tc][1]
    for j in range(2):
        toep_ref[0, j] = jnp.concatenate(toep_rows[j], axis=0).astype(BF16)


def _s5_prep(params):
    tc = S5_CHUNK
    wide = 2 * tc * S5_GROUP
    sl = 2 * S5_STATE

    def pair_lanes(v):
        return v.astype(F32).reshape(2, S5_PAIRS, 1, sl)

    def pair_rows(v):
        rows = v.shape[2]
        return v.astype(F32).reshape(2, S5_PAIRS, 2, rows, S5_STATE).transpose(0, 1, 3, 2, 4).reshape(
            2, S5_PAIRS, rows, sl)

    ldt = jnp.broadcast_to(params['s5_log_dt'].astype(F32)[:, :, None], (2, S5_GROUPS, S5_STATE))
    ins = [pair_lanes(params['s5_a_re']), pair_lanes(params['s5_a_im']), pair_lanes(ldt),
           pair_rows(params['s5_b_re'].transpose(0, 1, 3, 2)), pair_rows(params['s5_b_im'].transpose(0, 1, 3, 2)),
           pair_rows(params['s5_c_re']), pair_rows(params['s5_c_im'])]
    vec_spec = pl.BlockSpec((2, 1, 1, sl), lambda k: (0, k, 0, 0))
    mat_spec = pl.BlockSpec((2, 1, S5_GROUP, sl), lambda k: (0, k, 0, 0))
    w_spec = pl.BlockSpec((2, 1, wide, sl), lambda k: (0, k, 0, 0))
    w_sds = jax.ShapeDtypeStruct((2, S5_PAIRS, wide, sl), BF16)
    a_sds = jax.ShapeDtypeStruct((2, S5_PAIRS, 1, sl), F32)
    return pl.pallas_call(
        _s5_prep_kernel,
        grid=(S5_PAIRS,),
        in_specs=[vec_spec] * 3 + [mat_spec] * 4,
        out_specs=[pl.BlockSpec((1, 2, wide // 2, wide // 2), lambda k: (k, 0, 0, 0))] + [w_spec] * 4 + [vec_spec] * 2,
        out_shape=[jax.ShapeDtypeStruct((S5_PAIRS, 2, wide // 2, wide // 2), BF16)] + [w_sds] * 4 + [a_sds] * 2,
        compiler_params=_cparams(1),
        name="s5_prep",
    )(*ins)


S5_STEP_PAIRS = 4
def _s5_state_rows(batch, n_lat, n_ctx):
    def pitch(n):
        p = -(-n // SUBLANES)
        return SUBLANES * (p + 1 - p % 2)

    lat_pitch, ctx_pitch = pitch(n_lat), pitch(n_ctx)
    ctx_base = batch * lat_pitch
    return lat_pitch, ctx_pitch, ctx_base, ctx_base + batch * ctx_pitch


def _s5_kernel(ul_ref, uc_ref, dsk_ref, *refs, batch, seq, ctx):
    toep_ref, wsr, wsi, wor, woi, atr, ati = refs[:7]
    o_ref = refs[7]
    wscr, zscr, upscr, sre, sim, hre, him = refs[8:]
    tc = S5_CHUNK
    n_lat, n_ctx = seq // tc, ctx // tc
    rows_lat = n_lat * batch
    lat_pitch, ctx_pitch, ctx_base, _ = _s5_state_rows(batch, n_lat, n_ctx)
    w = tc * S5_GROUP
    lanes = S5_STEP_PAIRS * 2 * S5_GROUP

    def scatter_tiles(xt, r, col0):
        for kk in range(S5_STEP_PAIRS):
            for j in range(2):
                ch = (2 * kk + j) * S5_GROUP
                wscr[kk, j * w + r * S5_GROUP:j * w + (r + 1) * S5_GROUP, col0:col0 + lanes] = xt[ch:ch + S5_GROUP, :]

    for r in range(tc):
        for b in range(batch):
            x = ul_ref[pl.ds(b * seq + r, n_lat, stride=tc), :]
            scatter_tiles(x.T, r, b * n_lat)
        xc = jnp.concatenate([uc_ref[pl.ds(b * ctx + r, n_ctx, stride=tc), :] for b in range(batch)]
                             + [jnp.zeros((lanes - batch * n_ctx, lanes), F32)], axis=0)
        scatter_tiles(xc.T, r, rows_lat)

    chains = [(kk, d) for kk in range(S5_STEP_PAIRS) for d in range(2)]
    for kk in range(S5_STEP_PAIRS):
        upscr[kk] = wscr[kk].T.astype(BF16)
        for d in range(2):
            for dst, wst in ((sre, wsr), (sim, wsi)):
                s = jnp.dot(upscr[kk], wst[d, kk], preferred_element_type=F32)
                for b in range(batch):
                    dst[kk, d, b * lat_pitch:b * lat_pitch + n_lat, :] = s[b * n_lat:(b + 1) * n_lat]
                    dst[kk, d, ctx_base + b * ctx_pitch:ctx_base + b * ctx_pitch + n_ctx, :] = (
                        s[rows_lat + b * n_ctx:rows_lat + (b + 1) * n_ctx])

    ctx_rows = [pl.ds(ctx_base + c, batch, stride=ctx_pitch) for c in range(n_ctx)]
    lat_rows = [pl.ds(c, batch, stride=lat_pitch) for c in range(n_lat)]
    order = (ctx_rows + lat_rows, ctx_rows[::-1] + lat_rows[::-1])
    coef = {(kk, d): (atr[d, kk], ati[d, kk]) for kk, d in chains}
    state = {ch: (jnp.zeros((batch, 2 * S5_STATE), F32), jnp.zeros((batch, 2 * S5_STATE), F32)) for ch in chains}
    for t in range(n_ctx + n_lat):
        for kk, d in chains:
            rows = order[d][t]
            (ar, ai), (h_re, h_im) = coef[kk, d], state[kk, d]
            hre[kk, d, rows, :] = h_re
            him[kk, d, rows, :] = h_im
            state[kk, d] = (ar * h_re - ai * h_im + sre[kk, d, rows, :],
                            ar * h_im + ai * h_re + sim[kk, d, rows, :])

    for kk in range(S5_STEP_PAIRS):
        ul = upscr[kk, :rows_lat, :]
        y = ul.astype(F32) * dsk_ref[kk]
        y = y + jnp.concatenate(
            [jnp.dot(ul[:, :w], toep_ref[kk, 0], preferred_element_type=F32),
             jnp.dot(ul[:, w:], toep_ref[kk, 1], preferred_element_type=F32)], axis=1)
        nt = (((1,), (1,)), ((), ()))
        for d in range(2):
            h_r = jnp.concatenate([hre[kk, d, b * lat_pitch:b * lat_pitch + n_lat, :] for b in range(batch)], axis=0)
            h_i = jnp.concatenate([him[kk, d, b * lat_pitch:b * lat_pitch + n_lat, :] for b in range(batch)], axis=0)
            y = y + lax.dot_general(h_r.astype(BF16), wor[d, kk], nt, preferred_element_type=F32)
            y = y + lax.dot_general(h_i.astype(BF16), woi[d, kk], nt, preferred_element_type=F32)
        yt = y.T
        for b in range(batch):
            for s in range(tc):
                for j in range(2):
                    ch = (2 * kk + j) * S5_GROUP
                    zscr[b * tc + s, ch:ch + S5_GROUP, :] = yt[j * w + s * S5_GROUP:j * w + (s + 1) * S5_GROUP,
                                                               b * n_lat:(b + 1) * n_lat]

    for b in range(batch):
        for s in range(tc):
            o_ref[pl.ds(b * seq + s, n_lat, stride=tc), :] = jax.nn.gelu(zscr[b * tc + s].T)


def _s5_mixer(u_lat, u_ctx, params, batch, seq, ctx):
    tc = S5_CHUNK
    n_lat, n_ctx = seq // tc, ctx // tc
    sp = S5_STEP_PAIRS
    lanes = sp * 2 * S5_GROUP
    assert n_lat == lanes and batch * n_ctx <= lanes
    rows_lat = n_lat * batch
    rows_all = rows_lat + lanes
    state_rows = _s5_state_rows(batch, n_lat, n_ctx)[3]
    wide = 2 * tc * S5_GROUP
    dsk = jnp.broadcast_to(params['s5_d'].astype(F32).reshape(S5_PAIRS, 2, 1, S5_GROUP),
                           (S5_PAIRS, 2, tc, S5_GROUP)).reshape(S5_PAIRS, 1, wide)
    weights = _s5_prep(params)
    w_spec = pl.BlockSpec((2, sp, wide, 2 * S5_STATE), lambda k: (0, k, 0, 0))
    a_spec = pl.BlockSpec((2, sp, 1, 2 * S5_STATE), lambda k: (0, k, 0, 0))
    specs = [pl.BlockSpec((sp, 2, wide // 2, wide // 2), lambda k: (k, 0, 0, 0))] + [w_spec] * 4 + [a_spec] * 2
    return pl.pallas_call(
        functools.partial(_s5_kernel, batch=batch, seq=seq, ctx=ctx),
        grid=(S5_PAIRS // sp,),
        in_specs=[pl.BlockSpec((batch * seq, lanes), lambda k: (0, k)),
                  pl.BlockSpec((batch * ctx, lanes), lambda k: (0, k)),
                  pl.BlockSpec((sp, 1, wide), lambda k: (k, 0, 0))] + specs,
        out_specs=pl.BlockSpec((batch * seq, lanes), lambda k: (0, k)),
        out_shape=jax.ShapeDtypeStruct((batch * seq, S5_WIDTH), F32),
        scratch_shapes=[pltpu.VMEM((sp, wide, rows_all), F32),
                        pltpu.VMEM((batch * tc, lanes, n_lat), F32),
                        pltpu.VMEM((sp, rows_all, wide), BF16)]
                       + [pltpu.VMEM((sp, 2, state_rows, 2 * S5_STATE), F32) for _ in range(4)],
        compiler_params=_cparams(1),
        name="s5",
    )(u_lat, u_ctx, dsk, *weights)


def _merge_kernel(y5_ref, o_ref_in, gs_ref, gm_ref, wa_ref, wb_ref, wm_ref, out_ref, wa_s, wb_s, wm_s):
    @pl.when(pl.program_id(1) == 0)
    def _():
        wa_s[...] = wa_ref[...].astype(BF16)
        wb_s[...] = wb_ref[...].astype(BF16)
        wm_s[...] = wm_ref[...].astype(BF16)

    z = y5_ref[...].astype(BF16)
    a = jnp.dot(z, wa_s[...], preferred_element_type=F32)
    b = jnp.dot(z, wb_s[...], preferred_element_type=F32)
    mla = jnp.dot(o_ref_in[...], wm_s[...], preferred_element_type=F32)
    merged = gs_ref[...].astype(F32) * (a * jax.nn.sigmoid(b)) + gm_ref[...].astype(F32) * mla
    out_ref[...] = merged.astype(out_ref.dtype)


def _merge(y5, o_mla, gates, w_glu, w_mla_o, tm, tn):
    t = y5.shape[0]
    nj = D_MODEL // tn
    return pl.pallas_call(
        _merge_kernel,
        grid=(nj, t // tm),
        in_specs=[pl.BlockSpec((tm, S5_WIDTH), lambda j, i: (i, 0)),
                  pl.BlockSpec((tm, MLA_HEADS * V_DIM), lambda j, i: (i, 0)),
                  pl.BlockSpec((tm, tn), lambda j, i: (i, j)),
                  pl.BlockSpec((tm, tn), lambda j, i: (i, nj + j)),
                  pl.BlockSpec((S5_WIDTH, tn), lambda j, i: (0, j)),
                  pl.BlockSpec((S5_WIDTH, tn), lambda j, i: (0, nj + j)),
                  pl.BlockSpec((MLA_HEADS * V_DIM, tn), lambda j, i: (0, j))],
        out_specs=pl.BlockSpec((tm, tn), lambda j, i: (i, j)),
        out_shape=jax.ShapeDtypeStruct((t, D_MODEL), BF16),
        scratch_shapes=[pltpu.VMEM((S5_WIDTH, tn), BF16), pltpu.VMEM((S5_WIDTH, tn), BF16),
                        pltpu.VMEM((MLA_HEADS * V_DIM, tn), BF16)],
        compiler_params=_cparams(2),
        name="merge",
    )(y5, o_mla, gates, gates, w_glu, w_glu, w_mla_o)


def _resid_mm(a, w, x_res, gate, rows_per_batch, tm, tn, name, weight_buffers=2):
    def epi(accs, e_refs, o_refs):
        o_refs[0][...] = e_refs[0][...] + e_refs[1][0] * accs[0]

    t = a.shape[0]
    n = w.shape[1]
    tpb = rows_per_batch // tm
    extras = [(x_res, pl.BlockSpec((tm, tn), lambda j, i: (i, j))),
              (gate, pl.BlockSpec((1, 1, tn), lambda j, i: (i // tpb, 0, j)))]
    outs = [(jax.ShapeDtypeStruct((t, n), F32), pl.BlockSpec((tm, tn), lambda j, i: (i, j)))]
    return _fused_mm(a, [(w, 0, "kn", tn)], epi, extras, outs, tm=tm, nj=n // tn, name=name,
                     weight_buffers=weight_buffers)[0]


def _out_proj_norm_kernel(a_ref, w_ref, x_ref, g1_ref, n2_ref, sc_ref, sh_ref, x1_ref, xn_ref, w_s):
    @pl.when(pl.program_id(0) == 0)
    def _():
        w_s[...] = w_ref[...].astype(BF16)

    x1 = x_ref[...] + g1_ref[0] * jnp.dot(a_ref[...], w_s[...], preferred_element_type=F32)
    x1_ref[...] = x1
    xn_ref[...] = (_rms(x1, n2_ref[...]) * (1.0 + sc_ref[0]) + sh_ref[0]).astype(xn_ref.dtype)


def _out_proj_norm(merged, w_out, x_res, gate, gain2, scale2, shift2, rows_per_batch, tm):
    t, k = merged.shape
    tpb = rows_per_batch // tm
    row = pl.BlockSpec((tm, D_MODEL), lambda i: (i, 0))
    per_batch = pl.BlockSpec((1, 1, D_MODEL), lambda i: (i // tpb, 0, 0))
    return pl.pallas_call(
        _out_proj_norm_kernel,
        grid=(t // tm,),
        in_specs=[pl.BlockSpec((tm, k), lambda i: (i, 0)),
                  pl.BlockSpec((k, D_MODEL), lambda i: (0, 0), pipeline_mode=pl.Buffered(1)),
                  row, per_batch, pl.BlockSpec((1, D_MODEL), lambda i: (0, 0)), per_batch, per_batch],
        out_specs=[row, row],
        out_shape=[jax.ShapeDtypeStruct((t, D_MODEL), F32), jax.ShapeDtypeStruct((t, D_MODEL), BF16)],
        scratch_shapes=[pltpu.VMEM((k, D_MODEL), BF16)],
        compiler_params=_cparams(1),
        name="out_proj",
    )(merged, w_out, x_res, gate, gain2.reshape(1, D_MODEL), scale2, shift2)


def _ffn_in(xn, w_ffn_in, tm, tn):
    def epi(accs, e_refs, o_refs):
        o_refs[0][...] = (jax.nn.silu(accs[0]) * accs[1]).astype(BF16)

    t = xn.shape[0]
    nj = D_FF // tn
    outs = [(jax.ShapeDtypeStruct((t, D_FF), BF16), pl.BlockSpec((tm, tn), lambda j, i: (i, j)))]
    return _fused_mm(xn, [(w_ffn_in, 0, "kn", tn), (w_ffn_in, nj, "kn", tn)], epi, [], outs,
                     tm=tm, nj=nj, name="ffn_in")[0]


def _final_norm_kernel(x_ref, g_ref, o_ref):
    o_ref[...] = _rms(x_ref[...], g_ref[...])


def _final_norm(x2d, g, tm):
    t = x2d.shape[0]
    return pl.pallas_call(
        _final_norm_kernel,
        grid=(t // tm,),
        in_specs=[pl.BlockSpec((tm, D_MODEL), lambda i: (i, 0)),
                  pl.BlockSpec((1, D_MODEL), lambda i: (0, 0))],
        out_specs=pl.BlockSpec((tm, D_MODEL), lambda i: (i, 0)),
        out_shape=jax.ShapeDtypeStruct((t, D_MODEL), F32),
        compiler_params=_cparams(1),
        name="final_norm",
    )(x2d, g.reshape(1, D_MODEL))


def _rope_rot_cols(w):
    k = w.shape[0]
    ws = w.reshape(k, -1, 2, 2, QK_ROPE // 4)
    return jnp.stack([-ws[:, :, :, 1, :], ws[:, :, :, 0, :]], axis=3).reshape(k, -1)


def _rope_tables(n_tokens):
    rows = n_tokens // GRID_W
    row = jnp.repeat(jnp.arange(rows, dtype=F32), GRID_W)
    col = jnp.tile(jnp.arange(GRID_W, dtype=F32), rows)
    n_freq = QK_ROPE // 4
    inv = ROPE_BASE ** (-jnp.arange(n_freq, dtype=F32) / n_freq)
    ang = jnp.stack([row[:, None] * inv, col[:, None] * inv], axis=1)
    cos = jnp.broadcast_to(jnp.cos(ang)[:, :, None, :], (n_tokens, 2, 2, n_freq)).reshape(n_tokens, QK_ROPE)
    sin = jnp.broadcast_to(jnp.sin(ang)[:, :, None, :], (n_tokens, 2, 2, n_freq)).reshape(n_tokens, QK_ROPE)
    return cos, sin


def kernel(x, c, ctx, c_ctx, w_mod, b_mod, norm1, norm2, w_in, s5_a_re, s5_a_im, s5_log_dt, s5_b_re, s5_b_im,
           s5_c_re, s5_c_im, s5_d, w_glu, q_norm, kv_norm, w_uq, w_ukv, w_mla_o, w_out, w_ffn_in, w_ffn_out,
           norm_f):
    batch, seq, _ = x.shape
    n_ctx = ctx.shape[1]
    assert w_mod.shape[0] == 1, "single-layer block"
    p = dict(s5_a_re=s5_a_re[0], s5_a_im=s5_a_im[0], s5_log_dt=s5_log_dt[0], s5_b_re=s5_b_re[0],
             s5_b_im=s5_b_im[0], s5_c_re=s5_c_re[0], s5_c_im=s5_c_im[0], s5_d=s5_d[0])
    w_in_t = w_in.reshape(w_in.shape[1:]).T

    cv = jnp.concatenate([c, c_ctx[None], jnp.zeros((8 - batch - 1, D_MODEL), F32)], axis=0)
    m = _modulation(cv, w_mod[0], b_mod[0]).reshape(8, 6, 1, D_MODEL)
    m_lat = m[:batch]
    m_ctx = m[batch:batch + 1]

    x2d = x.reshape(batch * seq, D_MODEL)
    c2d = ctx.reshape(batch * n_ctx, D_MODEL)
    xn = _norm_mod(x2d, norm1[0], m_lat[:, 1], m_lat[:, 0], seq, 512)
    cn = _norm_mod(c2d, norm1[0], m_ctx[:, 1], m_ctx[:, 0], batch * n_ctx, 256)

    kv_lo = S5_WIDTH + Q_RANK
    w_kr_rot_t = _rope_rot_cols(w_in_t[kv_lo + KV_RANK:kv_lo + KV_RANK + QK_ROPE].T).T
    wq = w_uq[0].reshape(Q_RANK, MLA_HEADS, QK_NOPE + QK_ROPE)
    wq_rope = wq[:, :, QK_NOPE:].reshape(Q_RANK, MLA_HEADS * QK_ROPE)
    wq2 = jnp.concatenate([wq[:, :, :QK_NOPE].reshape(Q_RANK, MLA_HEADS * QK_NOPE), wq_rope,
                           _rope_rot_cols(wq_rope)], axis=1).astype(BF16)
    w_ukv_bf = w_ukv[0].astype(BF16)
    cos, sin = _rope_tables(seq)
    cos_q, sin_q = jnp.tile(cos, (1, MLA_HEADS)), jnp.tile(sin, (1, MLA_HEADS))

    u_lat = _proj_cast(xn, w_in_t, 0, S5_WIDTH, 1024, 1024, "u_lat", out_dtype=F32)
    u_ctx = _proj_cast(cn, w_in_t, 0, S5_WIDTH, 256, 512, "u_ctx", out_dtype=F32)
    q = _q_path(xn, w_in_t, q_norm[0], wq2, cos_q, sin_q, 1024, seq)
    kv_lat, kr_lat = _kv_path(xn, w_in_t, w_kr_rot_t, kv_norm[0], w_ukv_bf, cos, sin, 1024, seq)
    kv_ctx, kr_ctx = _kv_path(cn, w_in_t, None, kv_norm[0], w_ukv_bf, None, None, 256, n_ctx)
    gates = _proj_cast(xn, w_in_t, kv_lo + KV_RANK + QK_ROPE, 2 * D_MODEL, 1024, 1024, "gates",
                       act=jax.nn.sigmoid)

    y5 = _s5_mixer(u_lat, u_ctx, p, batch, seq, n_ctx)
    o_mla = _attention(q, kv_lat, kv_ctx, kr_lat, kr_ctx, batch, seq, n_ctx, 2048)

    merged = _merge(y5, o_mla, gates, w_glu[0], w_mla_o[0], 512, 1024)
    x1, xn2 = _out_proj_norm(merged, w_out[0], x2d, m_lat[:, 2], norm2[0], m_lat[:, 4], m_lat[:, 3], seq, 512)
    hid = _ffn_in(xn2, w_ffn_in[0], 2048, 512)
    x2 = _resid_mm(hid, w_ffn_out[0], x1, m_lat[:, 5], seq, 256, 1024, "ffn_out", weight_buffers=1)
    return _final_norm(x2, norm_f, 512).reshape(batch, seq, D_MODEL)
```

```python
import functools
import math

import jax
import jax.numpy as jnp
from jax import lax
from jax.experimental import pallas as pl
from jax.experimental.pallas import tpu as pltpu

F32 = jnp.float32
BF16 = jnp.bfloat16

D_MODEL = 2048
GRID_W = 64
EPS = 1e-6
S5_WIDTH = D_MODEL // 2
S5_GROUP = 16
S5_GROUPS = S5_WIDTH // S5_GROUP
S5_STATE = 64
S5_CHUNK = 16
S5_PAIRS = S5_GROUPS // 2
MLA_HEADS = 8
QK_NOPE = 128
QK_ROPE = 64
V_DIM = 128
Q_RANK = 512
KV_RANK = 256
ROPE_BASE = 10000.0
ATTN_SCALE = (QK_NOPE + QK_ROPE) ** -0.5
D_FF = -(-8 * D_MODEL // (3 * 256)) * 256

VMEM_LIMIT_BYTES = 56 * 1024 * 1024
SUBLANES = 8


def _cparams(n_axes):
    return pltpu.CompilerParams(dimension_semantics=("arbitrary",) * n_axes,
                                vmem_limit_bytes=VMEM_LIMIT_BYTES)


def _rms(x, g):
    return x * lax.rsqrt(jnp.mean(x * x, axis=-1, keepdims=True) + EPS) * g


def _mod_kernel(cv_ref, w_ref, b_ref, o_ref):
    s = jax.nn.silu(cv_ref[...]).astype(BF16)
    o_ref[...] = jnp.dot(s, w_ref[...].astype(BF16), preferred_element_type=F32) + b_ref[...]


def _modulation(cv, w_mod, b_mod):
    n = w_mod.shape[1]
    tn = 1536
    return pl.pallas_call(
        _mod_kernel,
        grid=(n // tn,),
        in_specs=[pl.BlockSpec((8, D_MODEL), lambda j: (0, 0)),
                  pl.BlockSpec((D_MODEL, tn), lambda j: (0, j)),
                  pl.BlockSpec((1, tn), lambda j: (0, j))],
        out_specs=pl.BlockSpec((8, tn), lambda j: (0, j)),
        out_shape=jax.ShapeDtypeStruct((8, n), F32),
        compiler_params=_cparams(1),
        name="mod",
    )(cv, w_mod, b_mod.reshape(1, n))


MOD_SHIFT1, MOD_SCALE1, MOD_GATE1, MOD_SHIFT2, MOD_SCALE2, MOD_GATE2 = range(6)


def _mod_spec(width, row_of, which, col_of=None):
    col_of = col_of or (lambda *ids: 0)
    return pl.BlockSpec((1, 1, 1, width), lambda *ids: (row_of(*ids), which, 0, col_of(*ids)))


def _norm_mod_kernel(x_ref, c_ref, g_ref, sc_ref, sh_ref, o_ref, *, lat_tiles):
    def emit(src_ref):
        y = _rms(src_ref[...], g_ref[...])
        o_ref[...] = (y * (1.0 + sc_ref[0, 0]) + sh_ref[0, 0]).astype(o_ref.dtype)

    pl.when(pl.program_id(0) < lat_tiles)(lambda: emit(x_ref))
    pl.when(pl.program_id(0) >= lat_tiles)(lambda: emit(c_ref))


def _norm_mod(x2d, c2d, gain, m, rows_per_batch, ctx_row, tm):
    lat_tiles, ctx_tiles = x2d.shape[0] // tm, c2d.shape[0] // tm
    tpb = rows_per_batch // tm

    def row(i):
        return jnp.where(i < lat_tiles, i // tpb, ctx_row)

    return pl.pallas_call(
        functools.partial(_norm_mod_kernel, lat_tiles=lat_tiles),
        grid=(lat_tiles + ctx_tiles,),
        in_specs=[pl.BlockSpec((tm, D_MODEL), lambda i: (jnp.minimum(i, lat_tiles - 1), 0)),
                  pl.BlockSpec((tm, D_MODEL), lambda i: (jnp.maximum(i - lat_tiles, 0), 0)),
                  pl.BlockSpec((1, D_MODEL), lambda i: (0, 0)),
                  _mod_spec(D_MODEL, row, MOD_SCALE1), _mod_spec(D_MODEL, row, MOD_SHIFT1)],
        out_specs=pl.BlockSpec((tm, D_MODEL), lambda i: (i, 0)),
        out_shape=jax.ShapeDtypeStruct((x2d.shape[0] + c2d.shape[0], D_MODEL), BF16),
        compiler_params=_cparams(1),
        name="norm_mod",
    )(x2d, c2d, gain.reshape(1, D_MODEL), m, m)


def _fused_mm(a, weights, epilogue, extras, outs, *, tm, nj, name, weight_buffers=2, rows=None):
    t, k = a.shape
    ni = (rows or t) // tm
    nw, ne, no = len(weights), len(extras), len(outs)
    need_cast = [w.dtype != BF16 for w, _, _, _ in weights]
    nt = (((1,), (1,)), ((), ()))

    def kernel(*refs):
        a_ref = refs[0]
        w_refs = refs[1:1 + nw]
        e_refs = refs[1 + nw:1 + nw + ne]
        o_refs = refs[1 + nw + ne:1 + nw + ne + no]
        s_refs = list(refs[1 + nw + ne + no:])
        staged = {idx: s_refs.pop(0) for idx in range(nw) if need_cast[idx]}
        if staged:
            @pl.when(pl.program_id(1) == 0)
            def _():
                for idx, s_ref in staged.items():
                    s_ref[...] = w_refs[idx][...].astype(BF16)

        av = a_ref[...]
        accs = []
        for idx in range(nw):
            w_ref = staged.get(idx, w_refs[idx])
            if weights[idx][2] == "kn":
                accs.append(jnp.dot(av, w_ref[...], preferred_element_type=F32))
            else:
                accs.append(lax.dot_general(av, w_ref[...], nt, preferred_element_type=F32))
        epilogue(accs, e_refs, o_refs)

    in_specs = [pl.BlockSpec((tm, k), lambda j, i: (i, 0))]
    scratch = []
    for (w, off, layout, width), cast in zip(weights, need_cast):
        if layout == "kn":
            shape = (k, width)
            mode = {} if weight_buffers == 2 else {"pipeline_mode": pl.Buffered(weight_buffers)}
            in_specs.append(pl.BlockSpec(shape, functools.partial(lambda j, i, off: (0, off + j), off=off), **mode))
        else:
            shape = (width, k)
            in_specs.append(pl.BlockSpec(
                (pl.Element(width), pl.Element(k)),
                functools.partial(lambda j, i, off, width: (pl.multiple_of(off + j * width, SUBLANES), 0),
                                  off=off, width=width)))
        if cast:
            scratch.append(pltpu.VMEM(shape, BF16))
    in_specs += [spec for _, spec in extras]
    return pl.pallas_call(
        kernel,
        grid=(nj, ni),
        in_specs=in_specs,
        out_specs=[spec for _, spec in outs],
        out_shape=[sds for sds, _ in outs],
        scratch_shapes=scratch,
        compiler_params=_cparams(2),
        name=name,
    )(a, *[w[0] for w in weights], *[e for e, _ in extras])


def _proj_cast(xn, w_in_t, row0, n, tm, tn, name, act=None, out_dtype=BF16, rows=None):
    def epi(accs, e_refs, o_refs):
        v = accs[0]
        if act is not None:
            v = act(v)
        o_refs[0][...] = v.astype(out_dtype)

    t = rows or xn.shape[0]
    return _fused_mm(xn, [(w_in_t, row0, "nk", tn)], epi, [],
                     [(jax.ShapeDtypeStruct((t, n), out_dtype), pl.BlockSpec((tm, tn), lambda j, i: (i, j)))],
                     tm=tm, nj=n // tn, name=name, rows=rows)[0]


def _q_path(xn, w_in_t, q_norm, wq2, cos2, sin2, tm, seq, rows):
    nr = MLA_HEADS * QK_ROPE
    nn = MLA_HEADS * QK_NOPE
    lanes = 2 * QK_ROPE

    def epi(accs, e_refs, o_refs):
        qn_ref, w2_ref, cos_ref, sin_ref = e_refs
        cq = _rms(accs[0], qn_ref[...]).astype(BF16)
        q = jnp.dot(cq, w2_ref[...], preferred_element_type=F32)
        o_refs[0][:, :nn] = (q[:, :nn] * ATTN_SCALE).astype(BF16)
        cos, sin = cos_ref[...] * ATTN_SCALE, sin_ref[...] * ATTN_SCALE
        for c0 in range(0, nr, lanes):
            rope = q[:, nn + c0:nn + c0 + lanes] * cos + q[:, nn + nr + c0:nn + nr + c0 + lanes] * sin
            o_refs[0][:, nn + c0:nn + c0 + lanes] = rope.astype(BF16)

    pos_tiles = seq // tm
    extras = [(q_norm.reshape(1, Q_RANK), pl.BlockSpec((1, Q_RANK), lambda j, i: (0, 0))),
              (wq2, pl.BlockSpec(wq2.shape, lambda j, i: (0, 0))),
              (cos2, pl.BlockSpec((tm, lanes), lambda j, i: (i % pos_tiles, 0))),
              (sin2, pl.BlockSpec((tm, lanes), lambda j, i: (i % pos_tiles, 0)))]
    outs = [(jax.ShapeDtypeStruct((rows, nn + nr), BF16), pl.BlockSpec((tm, nn + nr), lambda j, i: (i, 0)))]
    return _fused_mm(xn, [(w_in_t, S5_WIDTH, "nk", Q_RANK)], epi, extras, outs, tm=tm, nj=1, name="q_path",
                     rows=rows)[0]


def _kv_path(xn, w_in_t, w_kr_rot_t, kv_norm, w_ukv_bf, cos_k, sin_k, tm, seq, lat_rows):
    nkv = w_ukv_bf.shape[1]

    def epi(accs, e_refs, o_refs):
        acc = accs[0]
        ckv = _rms(acc[:, :KV_RANK], e_refs[0][...]).astype(BF16)
        o_refs[0][...] = jnp.dot(ckv, e_refs[1][...], preferred_element_type=F32).astype(BF16)
        o_refs[1][...] = (acc[:, KV_RANK:] * e_refs[2][...] + accs[1] * e_refs[3][...]).astype(BF16)

    t = xn.shape[0]
    pos_tiles, lat_tiles = seq // tm, lat_rows // tm

    def table_tile(j, i):
        return (jnp.where(i < lat_tiles, i % pos_tiles, pos_tiles), 0)

    extras = [(kv_norm.reshape(1, KV_RANK), pl.BlockSpec((1, KV_RANK), lambda j, i: (0, 0))),
              (w_ukv_bf, pl.BlockSpec(w_ukv_bf.shape, lambda j, i: (0, 0))),
              (cos_k, pl.BlockSpec((tm, QK_ROPE), table_tile)),
              (sin_k, pl.BlockSpec((tm, QK_ROPE), table_tile))]
    weights = [(w_in_t, S5_WIDTH + Q_RANK, "nk", KV_RANK + QK_ROPE), (w_kr_rot_t, 0, "nk", QK_ROPE)]
    outs = [(jax.ShapeDtypeStruct((t, nkv), BF16), pl.BlockSpec((tm, nkv), lambda j, i: (i, 0))),
            (jax.ShapeDtypeStruct((t, QK_ROPE), BF16), pl.BlockSpec((tm, QK_ROPE), lambda j, i: (i, 0)))]
    return _fused_mm(xn, weights, epi, extras, outs, tm=tm, nj=1, name="kv_path")


ATTN_SUB_ROWS = 512


def _attn_kernel(qn_ref, qr_ref, kvl_ref, kvc_ref, krl_ref, krc_ref, o_ref, k_scr, v_scr, *, seq, ctx):
    dk = QK_NOPE + QK_ROPE

    @pl.when(pl.program_id(2) == 0)
    def _():
        for h in range(2):
            base = h * (QK_NOPE + V_DIM)
            k_scr[h, :seq, :QK_NOPE] = kvl_ref[:, base:base + QK_NOPE]
            k_scr[h, seq:, :QK_NOPE] = kvc_ref[:, base:base + QK_NOPE]
            k_scr[h, :seq, QK_NOPE:dk] = krl_ref[...]
            k_scr[h, seq:, QK_NOPE:dk] = krc_ref[...]
            v_scr[h, :seq, :V_DIM] = kvl_ref[:, base + QK_NOPE:base + QK_NOPE + V_DIM]
            v_scr[h, seq:, :V_DIM] = kvc_ref[:, base + QK_NOPE:base + QK_NOPE + V_DIM]
            ones_col = lax.broadcasted_iota(jnp.int32, (seq + ctx, V_DIM), 1) == 0
            v_scr[h, :, V_DIM:] = jnp.where(ones_col, 1.0, 0.0).astype(BF16)

    for r0 in range(0, qn_ref.shape[0], ATTN_SUB_ROWS):
        rows = slice(r0, r0 + ATTN_SUB_ROWS)
        for h in range(2):
            q = jnp.concatenate([qn_ref[rows, h * QK_NOPE:(h + 1) * QK_NOPE],
                                 qr_ref[rows, h * QK_ROPE:(h + 1) * QK_ROPE]], axis=1)
            s = lax.dot_general(q, k_scr[h], (((1,), (1,)), ((), ())), preferred_element_type=F32)
            m = jnp.max(s, axis=-1, keepdims=True)
            p = jnp.exp((s - m).astype(BF16))
            ol = jnp.dot(p, v_scr[h], preferred_element_type=F32)
            o_ref[rows, h * V_DIM:(h + 1) * V_DIM] = (ol[:, :V_DIM] / ol[:, V_DIM:V_DIM + 1]).astype(o_ref.dtype)


def _attention(q, kv, kr, batch, seq, ctx, tq):
    nq = seq // tq
    ctx0 = batch * seq // ctx
    nn_blocks = MLA_HEADS * QK_NOPE // (2 * QK_NOPE)
    dk = QK_NOPE + QK_ROPE
    hw = 2 * (QK_NOPE + V_DIM)
    return pl.pallas_call(
        functools.partial(_attn_kernel, seq=seq, ctx=ctx),
        grid=(batch, MLA_HEADS // 2, nq),
        in_specs=[pl.BlockSpec((tq, 2 * QK_NOPE), lambda b, hp, qi: (b * nq + qi, hp)),
                  pl.BlockSpec((tq, 2 * QK_ROPE), lambda b, hp, qi: (b * nq + qi, 2 * nn_blocks + hp)),
                  pl.BlockSpec((seq, hw), lambda b, hp, qi: (b, hp)),
                  pl.BlockSpec((ctx, hw), lambda b, hp, qi: (ctx0 + b, hp)),
                  pl.BlockSpec((seq, QK_ROPE), lambda b, hp, qi: (b, 0)),
                  pl.BlockSpec((ctx, QK_ROPE), lambda b, hp, qi: (ctx0 + b, 0))],
        out_specs=pl.BlockSpec((tq, 2 * V_DIM), lambda b, hp, qi: (b * nq + qi, hp)),
        out_shape=jax.ShapeDtypeStruct((batch * seq, MLA_HEADS * V_DIM), BF16),
        scratch_shapes=[pltpu.VMEM((2, seq + ctx, dk), BF16), pltpu.VMEM((2, seq + ctx, 2 * V_DIM), BF16)],
        compiler_params=_cparams(3),
        name="attention",
    )(q, q, kv, kv, kr, kr)


def _s5_prep_kernel(are_ref, aim_ref, ldt_ref, bre_ref, bim_ref, cre_ref, cim_ref,
                    toep_ref, wsr_ref, wsi_ref, wor_ref, woi_ref, atr_ref, ati_ref):
    tc, g = S5_CHUNK, S5_GROUP
    w = tc * g
    lane = lax.broadcasted_iota(jnp.int32, (g, 2 * S5_STATE), 1)
    in_group = (lane < S5_STATE, lane >= S5_STATE)
    lane_w = lax.broadcasted_iota(jnp.int32, (g, w), 1)
    nt = (((1,), (1,)), ((), ()))
    toep_rows = [[jnp.zeros((g, w), F32) for _ in range(tc)] for _ in range(2)]
    for d in range(2):
        lr, li = are_ref[d, 0], aim_ref[d, 0]
        dt = jnp.exp(ldt_ref[d, 0])
        mag = jnp.exp(lr * dt)
        ab_re, ab_im = mag * jnp.cos(li * dt), mag * jnp.sin(li * dt)
        den = lr * lr + li * li
        nr, ni = ab_re - 1.0, ab_im
        co_re = (nr * lr + ni * li) / den
        co_im = (ni * lr - nr * li) / den
        br, bi = bre_ref[d, 0], bim_ref[d, 0]
        bb_re = co_re * br - co_im * bi
        bb_im = co_re * bi + co_im * br
        pw = [(jnp.ones_like(ab_re), jnp.zeros_like(ab_re))]
        for _ in range(tc):
            pr, pi = pw[-1]
            pw.append((pr * ab_re - pi * ab_im, pr * ab_im + pi * ab_re))
        cr, ci = cre_ref[d, 0], cim_ref[d, 0]
        ca = [(cr * pr - ci * pi, cr * pi + ci * pr) for pr, pi in pw]
        taus = list(range(tc))[::-1] if d else list(range(tc))
        y_re = jnp.concatenate([ca[t][0] for t in taus], axis=0).astype(BF16)
        y_im = jnp.concatenate([ca[t][1] for t in taus], axis=0).astype(BF16)
        for j in range(2):
            x_re = jnp.where(in_group[j], bb_re, 0.0)
            x_im = jnp.where(in_group[j], bb_im, 0.0)
            kt = (lax.dot_general(x_re.astype(BF16), y_re, nt, preferred_element_type=F32)
                  - lax.dot_general(x_im.astype(BF16), y_im, nt, preferred_element_type=F32))
            for r in range(tc):
                sh = (r + 1) * g if d else r * g
                blk = pltpu.roll(kt, sh % w, 1) if sh % w else kt
                keep = (lane_w < sh) if d else (lane_w >= sh)
                toep_rows[j][r] = toep_rows[j][r] + jnp.where(keep, blk, 0.0)
        for r in range(tc):
            pr, pi = pw[r] if d else pw[tc - 1 - r]
            w_re = bb_re * pr - bb_im * pi
            w_im = bb_re * pi + bb_im * pr
            car, cai = ca[tc - r] if d else ca[r + 1]
            for j in range(2):
                rows = slice(j * w + r * g, j * w + (r + 1) * g)
                wsr_ref[d, 0, rows, :] = jnp.where(in_group[j], w_re, 0.0).astype(BF16)
                wsi_ref[d, 0, rows, :] = jnp.where(in_group[j], w_im, 0.0).astype(BF16)
                wor_ref[d, 0, rows, :] = jnp.where(in_group[j], car, 0.0).astype(BF16)
                woi_ref[d, 0, rows, :] = jnp.where(in_group[j], -cai, 0.0).astype(BF16)
        atr_ref[d, 0] = pw[tc][0]
        ati_ref[d, 0] = pw[tc][1]
    for j in range(2):
        toep_ref[0, j] = jnp.concatenate(toep_rows[j], axis=0).astype(BF16)


def _s5_prep(params):
    tc = S5_CHUNK
    wide = 2 * tc * S5_GROUP
    sl = 2 * S5_STATE

    def pair_lanes(v):
        return v.astype(F32).reshape(2, S5_PAIRS, 1, sl)

    def pair_rows(v):
        rows = v.shape[2]
        return v.astype(F32).reshape(2, S5_PAIRS, 2, rows, S5_STATE).transpose(0, 1, 3, 2, 4).reshape(
            2, S5_PAIRS, rows, sl)

    ldt = jnp.broadcast_to(params['s5_log_dt'].astype(F32)[:, :, None], (2, S5_GROUPS, S5_STATE))
    ins = [pair_lanes(params['s5_a_re']), pair_lanes(params['s5_a_im']), pair_lanes(ldt),
           pair_rows(params['s5_b_re'].transpose(0, 1, 3, 2)), pair_rows(params['s5_b_im'].transpose(0, 1, 3, 2)),
           pair_rows(params['s5_c_re']), pair_rows(params['s5_c_im'])]
    vec_spec = pl.BlockSpec((2, 1, 1, sl), lambda k: (0, k, 0, 0))
    mat_spec = pl.BlockSpec((2, 1, S5_GROUP, sl), lambda k: (0, k, 0, 0))
    w_spec = pl.BlockSpec((2, 1, wide, sl), lambda k: (0, k, 0, 0))
    w_sds = jax.ShapeDtypeStruct((2, S5_PAIRS, wide, sl), BF16)
    a_sds = jax.ShapeDtypeStruct((2, S5_PAIRS, 1, sl), F32)
    return pl.pallas_call(
        _s5_prep_kernel,
        grid=(S5_PAIRS,),
        in_specs=[vec_spec] * 3 + [mat_spec] * 4,
        out_specs=[pl.BlockSpec((1, 2, wide // 2, wide // 2), lambda k: (k, 0, 0, 0))] + [w_spec] * 4 + [vec_spec] * 2,
        out_shape=[jax.ShapeDtypeStruct((S5_PAIRS, 2, wide // 2, wide // 2), BF16)] + [w_sds] * 4 + [a_sds] * 2,
        compiler_params=_cparams(1),
        name="s5_prep",
    )(*ins)


S5_STEP_PAIRS = 4
def _s5_state_rows(batch, n_lat, n_ctx):
    def pitch(n):
        p = -(-n // SUBLANES)
        return SUBLANES * (p + 1 - p % 2)

    lat_pitch, ctx_pitch = pitch(n_lat), pitch(n_ctx)
    ctx_base = batch * lat_pitch
    return lat_pitch, ctx_pitch, ctx_base, ctx_base + batch * ctx_pitch


def _s5_kernel(ul_ref, uc_ref, dsk_ref, *refs, batch, seq, ctx):
    toep_ref, wsr, wsi, wor, woi, atr, ati = refs[:7]
    o_ref = refs[7]
    wscr, zscr, upscr, sre, sim, hre, him = refs[8:]
    tc = S5_CHUNK
    n_lat, n_ctx = seq // tc, ctx // tc
    rows_lat = n_lat * batch
    lat_pitch, ctx_pitch, ctx_base, _ = _s5_state_rows(batch, n_lat, n_ctx)
    w = tc * S5_GROUP
    lanes = S5_STEP_PAIRS * 2 * S5_GROUP

    def scatter_tiles(xt, r, col0):
        for kk in range(S5_STEP_PAIRS):
            for j in range(2):
                ch = (2 * kk + j) * S5_GROUP
                wscr[kk, j * w + r * S5_GROUP:j * w + (r + 1) * S5_GROUP, col0:col0 + lanes] = xt[ch:ch + S5_GROUP, :]

    for r in range(tc):
        for b in range(batch):
            x = ul_ref[pl.ds(b * seq + r, n_lat, stride=tc), :]
            scatter_tiles(x.T, r, b * n_lat)
        xc = jnp.concatenate([uc_ref[pl.ds(b * ctx + r, n_ctx, stride=tc), :] for b in range(batch)]
                             + [jnp.zeros((lanes - batch * n_ctx, lanes), F32)], axis=0)
        scatter_tiles(xc.T, r, rows_lat)

    chains = [(kk, d) for kk in range(S5_STEP_PAIRS) for d in range(2)]
    for kk in range(S5_STEP_PAIRS):
        upscr[kk] = wscr[kk].T.astype(BF16)
        for d in range(2):
            for dst, wst in ((sre, wsr), (sim, wsi)):
                s = jnp.dot(upscr[kk], wst[d, kk], preferred_element_type=F32)
                for b in range(batch):
                    dst[kk, d, b * lat_pitch:b * lat_pitch + n_lat, :] = s[b * n_lat:(b + 1) * n_lat]
                    dst[kk, d, ctx_base + b * ctx_pitch:ctx_base + b * ctx_pitch + n_ctx, :] = (
                        s[rows_lat + b * n_ctx:rows_lat + (b + 1) * n_ctx])

    ctx_rows = [pl.ds(ctx_base + c, batch, stride=ctx_pitch) for c in range(n_ctx)]
    lat_rows = [pl.ds(c, batch, stride=lat_pitch) for c in range(n_lat)]
    order = (ctx_rows + lat_rows, ctx_rows[::-1] + lat_rows[::-1])
    coef = {(kk, d): (atr[d, kk], ati[d, kk]) for kk, d in chains}
    state = {ch: (jnp.zeros((batch, 2 * S5_STATE), F32), jnp.zeros((batch, 2 * S5_STATE), F32)) for ch in chains}
    for t in range(n_ctx + n_lat):
        for kk, d in chains:
            rows = order[d][t]
            (ar, ai), (h_re, h_im) = coef[kk, d], state[kk, d]
            hre[kk, d, rows, :] = h_re
            him[kk, d, rows, :] = h_im
            state[kk, d] = (ar * h_re - ai * h_im + sre[kk, d, rows, :],
                            ar * h_im + ai * h_re + sim[kk, d, rows, :])

    for kk in range(S5_STEP_PAIRS):
        ul = upscr[kk, :rows_lat, :]
        y = ul.astype(F32) * dsk_ref[kk]
        y = y + jnp.concatenate(
            [jnp.dot(ul[:, :w], toep_ref[kk, 0], preferred_element_type=F32),
             jnp.dot(ul[:, w:], toep_ref[kk, 1], preferred_element_type=F32)], axis=1)
        nt = (((1,), (1,)), ((), ()))
        for d in range(2):
            h_r = jnp.concatenate([hre[kk, d, b * lat_pitch:b * lat_pitch + n_lat, :] for b in range(batch)], axis=0)
            h_i = jnp.concatenate([him[kk, d, b * lat_pitch:b * lat_pitch + n_lat, :] for b in range(batch)], axis=0)
            y = y + lax.dot_general(h_r.astype(BF16), wor[d, kk], nt, preferred_element_type=F32)
            y = y + lax.dot_general(h_i.astype(BF16), woi[d, kk], nt, preferred_element_type=F32)
        yt = y.T
        for b in range(batch):
            for s in range(tc):
                for j in range(2):
                    ch = (2 * kk + j) * S5_GROUP
                    zscr[b * tc + s, ch:ch + S5_GROUP, :] = yt[j * w + s * S5_GROUP:j * w + (s + 1) * S5_GROUP,
                                                               b * n_lat:(b + 1) * n_lat]

    for b in range(batch):
        for s in range(tc):
            o_ref[pl.ds(b * seq + s, n_lat, stride=tc), :] = jax.nn.gelu(zscr[b * tc + s].T)


def _s5_mixer(u, params, batch, seq, ctx):
    tc = S5_CHUNK
    n_lat, n_ctx = seq // tc, ctx // tc
    sp = S5_STEP_PAIRS
    lanes = sp * 2 * S5_GROUP
    assert n_lat == lanes and batch * n_ctx <= lanes
    rows_lat = n_lat * batch
    rows_all = rows_lat + lanes
    state_rows = _s5_state_rows(batch, n_lat, n_ctx)[3]
    wide = 2 * tc * S5_GROUP
    dsk = jnp.broadcast_to(params['s5_d'].astype(F32).reshape(S5_PAIRS, 2, 1, S5_GROUP),
                           (S5_PAIRS, 2, tc, S5_GROUP)).reshape(S5_PAIRS, 1, wide)
    weights = _s5_prep(params)
    w_spec = pl.BlockSpec((2, sp, wide, 2 * S5_STATE), lambda k: (0, k, 0, 0))
    a_spec = pl.BlockSpec((2, sp, 1, 2 * S5_STATE), lambda k: (0, k, 0, 0))
    specs = [pl.BlockSpec((sp, 2, wide // 2, wide // 2), lambda k: (k, 0, 0, 0))] + [w_spec] * 4 + [a_spec] * 2
    return pl.pallas_call(
        functools.partial(_s5_kernel, batch=batch, seq=seq, ctx=ctx),
        grid=(S5_PAIRS // sp,),
        in_specs=[pl.BlockSpec((batch * seq, lanes), lambda k: (0, k)),
                  pl.BlockSpec((batch * ctx, lanes), lambda k: (seq // ctx, k)),
                  pl.BlockSpec((sp, 1, wide), lambda k: (k, 0, 0))] + specs,
        out_specs=pl.BlockSpec((batch * seq, lanes), lambda k: (0, k)),
        out_shape=jax.ShapeDtypeStruct((batch * seq, S5_WIDTH), F32),
        scratch_shapes=[pltpu.VMEM((sp, wide, rows_all), F32),
                        pltpu.VMEM((batch * tc, lanes, n_lat), F32),
                        pltpu.VMEM((sp, rows_all, wide), BF16)]
                       + [pltpu.VMEM((sp, 2, state_rows, 2 * S5_STATE), F32) for _ in range(4)],
        compiler_params=_cparams(1),
        name="s5",
    )(u, u, dsk, *weights)


def _merge_kernel(z5_ref, o_ref_in, gs_ref, gm_ref, wa_ref, wb_ref, wm_ref, out_ref, wa_s, wb_s, wm_s):
    @pl.when(pl.program_id(1) == 0)
    def _():
        wa_s[...] = wa_ref[...].astype(BF16)
        wb_s[...] = wb_ref[...].astype(BF16)
        wm_s[...] = wm_ref[...].astype(BF16)

    z = z5_ref[...].astype(BF16)
    a = jnp.dot(z, wa_s[...], preferred_element_type=F32)
    b = jnp.dot(z, wb_s[...], preferred_element_type=F32)
    mla = jnp.dot(o_ref_in[...], wm_s[...], preferred_element_type=F32)
    merged = gs_ref[...].astype(F32) * (a * jax.nn.sigmoid(b)) + gm_ref[...].astype(F32) * mla
    out_ref[...] = merged.astype(out_ref.dtype)


def _merge(y5, o_mla, gates, w_glu, w_mla_o, tm, tn):
    t = y5.shape[0]
    nj = D_MODEL // tn
    return pl.pallas_call(
        _merge_kernel,
        grid=(nj, t // tm),
        in_specs=[pl.BlockSpec((tm, S5_WIDTH), lambda j, i: (i, 0)),
                  pl.BlockSpec((tm, MLA_HEADS * V_DIM), lambda j, i: (i, 0)),
                  pl.BlockSpec((tm, tn), lambda j, i: (i, j)),
                  pl.BlockSpec((tm, tn), lambda j, i: (i, nj + j)),
                  pl.BlockSpec((S5_WIDTH, tn), lambda j, i: (0, j)),
                  pl.BlockSpec((S5_WIDTH, tn), lambda j, i: (0, nj + j)),
                  pl.BlockSpec((MLA_HEADS * V_DIM, tn), lambda j, i: (0, j))],
        out_specs=pl.BlockSpec((tm, tn), lambda j, i: (i, j)),
        out_shape=jax.ShapeDtypeStruct((t, D_MODEL), BF16),
        scratch_shapes=[pltpu.VMEM((S5_WIDTH, tn), BF16), pltpu.VMEM((S5_WIDTH, tn), BF16),
                        pltpu.VMEM((MLA_HEADS * V_DIM, tn), BF16)],
        compiler_params=_cparams(2),
        name="merge",
    )(y5, o_mla, gates, gates, w_glu, w_glu, w_mla_o)


def _resid_mm(a, w, x_res, m, which, rows_per_batch, tm, tn, name, weight_buffers=2):
    def epi(accs, e_refs, o_refs):
        o_refs[0][...] = e_refs[0][...] + e_refs[1][0, 0] * accs[0]

    t = a.shape[0]
    n = w.shape[1]
    tpb = rows_per_batch // tm
    extras = [(x_res, pl.BlockSpec((tm, tn), lambda j, i: (i, j))),
              (m, _mod_spec(tn, lambda j, i: i // tpb, which, lambda j, i: j))]
    outs = [(jax.ShapeDtypeStruct((t, n), F32), pl.BlockSpec((tm, tn), lambda j, i: (i, j)))]
    return _fused_mm(a, [(w, 0, "kn", tn)], epi, extras, outs, tm=tm, nj=n // tn, name=name,
                     weight_buffers=weight_buffers)[0]


def _out_proj_norm_kernel(a_ref, w_ref, x_ref, g1_ref, n2_ref, sc_ref, sh_ref, x1_ref, xn_ref, w_s):
    @pl.when(pl.program_id(0) == 0)
    def _():
        w_s[...] = w_ref[...].astype(BF16)

    x1 = x_ref[...] + g1_ref[0, 0] * jnp.dot(a_ref[...], w_s[...], preferred_element_type=F32)
    x1_ref[...] = x1
    xn_ref[...] = (_rms(x1, n2_ref[...]) * (1.0 + sc_ref[0, 0]) + sh_ref[0, 0]).astype(xn_ref.dtype)


def _out_proj_norm(merged, w_out, x_res, m, gain2, rows_per_batch, tm):
    t, k = merged.shape
    tpb = rows_per_batch // tm
    row = pl.BlockSpec((tm, D_MODEL), lambda i: (i, 0))
    gate1, scale2, shift2 = (_mod_spec(D_MODEL, lambda i: i // tpb, which)
                             for which in (MOD_GATE1, MOD_SCALE2, MOD_SHIFT2))
    return pl.pallas_call(
        _out_proj_norm_kernel,
        grid=(t // tm,),
        in_specs=[pl.BlockSpec((tm, k), lambda i: (i, 0)),
                  pl.BlockSpec((k, D_MODEL), lambda i: (0, 0), pipeline_mode=pl.Buffered(1)),
                  row, gate1, pl.BlockSpec((1, D_MODEL), lambda i: (0, 0)), scale2, shift2],
        out_specs=[row, row],
        out_shape=[jax.ShapeDtypeStruct((t, D_MODEL), F32), jax.ShapeDtypeStruct((t, D_MODEL), BF16)],
        scratch_shapes=[pltpu.VMEM((k, D_MODEL), BF16)],
        compiler_params=_cparams(1),
        name="out_proj",
    )(merged, w_out, x_res, m, gain2.reshape(1, D_MODEL), m, m)


def _ffn_in(xn, w_ffn_in, tm, tn):
    def epi(accs, e_refs, o_refs):
        o_refs[0][...] = (jax.nn.silu(accs[0]) * accs[1]).astype(BF16)

    t = xn.shape[0]
    nj = D_FF // tn
    outs = [(jax.ShapeDtypeStruct((t, D_FF), BF16), pl.BlockSpec((tm, tn), lambda j, i: (i, j)))]
    return _fused_mm(xn, [(w_ffn_in, 0, "kn", tn), (w_ffn_in, nj, "kn", tn)], epi, [], outs,
                     tm=tm, nj=nj, name="ffn_in")[0]


def _final_norm_kernel(x_ref, g_ref, o_ref):
    o_ref[...] = _rms(x_ref[...], g_ref[...])


def _final_norm(x2d, g, tm):
    t = x2d.shape[0]
    return pl.pallas_call(
        _final_norm_kernel,
        grid=(t // tm,),
        in_specs=[pl.BlockSpec((tm, D_MODEL), lambda i: (i, 0)),
                  pl.BlockSpec((1, D_MODEL), lambda i: (0, 0))],
        out_specs=pl.BlockSpec((tm, D_MODEL), lambda i: (i, 0)),
        out_shape=jax.ShapeDtypeStruct((t, D_MODEL), F32),
        compiler_params=_cparams(1),
        name="final_norm",
    )(x2d, g.reshape(1, D_MODEL))


def _rope_rot_cols(w):
    k = w.shape[0]
    ws = w.reshape(k, -1, 2, 2, QK_ROPE // 4)
    return jnp.stack([-ws[:, :, :, 1, :], ws[:, :, :, 0, :]], axis=3).reshape(k, -1)


def _rope_tables(n_tokens):
    rows = n_tokens // GRID_W
    row = jnp.repeat(jnp.arange(rows, dtype=F32), GRID_W)
    col = jnp.tile(jnp.arange(GRID_W, dtype=F32), rows)
    n_freq = QK_ROPE // 4
    inv = ROPE_BASE ** (-jnp.arange(n_freq, dtype=F32) / n_freq)
    ang = jnp.stack([row[:, None] * inv, col[:, None] * inv], axis=1)
    cos = jnp.broadcast_to(jnp.cos(ang)[:, :, None, :], (n_tokens, 2, 2, n_freq)).reshape(n_tokens, QK_ROPE)
    sin = jnp.broadcast_to(jnp.sin(ang)[:, :, None, :], (n_tokens, 2, 2, n_freq)).reshape(n_tokens, QK_ROPE)
    return cos, sin


def kernel(x, c, ctx, c_ctx, w_mod, b_mod, norm1, norm2, w_in, s5_a_re, s5_a_im, s5_log_dt, s5_b_re, s5_b_im,
           s5_c_re, s5_c_im, s5_d, w_glu, q_norm, kv_norm, w_uq, w_ukv, w_mla_o, w_out, w_ffn_in, w_ffn_out,
           norm_f):
    batch, seq, _ = x.shape
    n_ctx = ctx.shape[1]
    assert w_mod.shape[0] == 1, "single-layer block"
    p = dict(s5_a_re=s5_a_re[0], s5_a_im=s5_a_im[0], s5_log_dt=s5_log_dt[0], s5_b_re=s5_b_re[0],
             s5_b_im=s5_b_im[0], s5_c_re=s5_c_re[0], s5_c_im=s5_c_im[0], s5_d=s5_d[0])
    w_in_t = w_in.reshape(w_in.shape[1:]).T

    cv = jnp.concatenate([c, c_ctx[None], jnp.zeros((8 - batch - 1, D_MODEL), F32)], axis=0)
    m = _modulation(cv, w_mod[0], b_mod[0]).reshape(8, 6, 1, D_MODEL)

    lat_rows = batch * seq
    x2d = x.reshape(lat_rows, D_MODEL)
    c2d = ctx.reshape(batch * n_ctx, D_MODEL)
    xn = _norm_mod(x2d, c2d, norm1[0], m, seq, batch, 512)

    kv_lo = S5_WIDTH + Q_RANK
    w_kr_rot_t = _rope_rot_cols(w_in_t[kv_lo + KV_RANK:kv_lo + KV_RANK + QK_ROPE].T).T
    wq = w_uq[0].reshape(Q_RANK, MLA_HEADS, QK_NOPE + QK_ROPE)
    wq_rope = wq[:, :, QK_NOPE:].reshape(Q_RANK, MLA_HEADS * QK_ROPE)
    wq2 = jnp.concatenate([wq[:, :, :QK_NOPE].reshape(Q_RANK, MLA_HEADS * QK_NOPE), wq_rope,
                           _rope_rot_cols(wq_rope)], axis=1).astype(BF16)
    w_ukv_bf = w_ukv[0].astype(BF16)
    tm = 1024
    cos, sin = _rope_tables(seq)
    cos_k = jnp.concatenate([cos, jnp.ones((tm, QK_ROPE), F32)], axis=0)
    sin_k = jnp.concatenate([sin, jnp.zeros((tm, QK_ROPE), F32)], axis=0)

    u = _proj_cast(xn, w_in_t, 0, S5_WIDTH, tm, 1024, "u_proj", out_dtype=F32)
    q = _q_path(xn, w_in_t, q_norm[0], wq2, jnp.tile(cos, (1, 2)), jnp.tile(sin, (1, 2)), tm, seq, lat_rows)
    kv, kr = _kv_path(xn, w_in_t, w_kr_rot_t, kv_norm[0], w_ukv_bf, cos_k, sin_k, tm, seq, lat_rows)
    gates = _proj_cast(xn, w_in_t, kv_lo + KV_RANK + QK_ROPE, 2 * D_MODEL, tm, 1024, "gates",
                       act=jax.nn.sigmoid, rows=lat_rows)

    z5 = _s5_mixer(u, p, batch, seq, n_ctx)
    o_mla = _attention(q, kv, kr, batch, seq, n_ctx, 2048)

    merged = _merge(z5, o_mla, gates, w_glu[0], w_mla_o[0], 512, 1024)
    x1, xn2 = _out_proj_norm(merged, w_out[0], x2d, m, norm2[0], seq, 512)
    hid = _ffn_in(xn2, w_ffn_in[0], 2048, 512)
    x2 = _resid_mm(hid, w_ffn_out[0], x1, m, MOD_GATE2, seq, 256, 1024, "ffn_out", weight_buffers=1)
    return _final_norm(x2, norm_f, 512).reshape(batch, seq, D_MODEL)
```

```python
import functools
import math

import jax
import jax.numpy as jnp
from jax import lax
from jax.experimental import pallas as pl
from jax.experimental.pallas import tpu as pltpu

F32 = jnp.float32
BF16 = jnp.bfloat16

D_MODEL = 2048
GRID_W = 64
EPS = 1e-6
S5_WIDTH = D_MODEL // 2
S5_GROUP = 16
S5_GROUPS = S5_WIDTH // S5_GROUP
S5_STATE = 64
S5_CHUNK = 16
S5_PAIRS = S5_GROUPS // 2
MLA_HEADS = 8
QK_NOPE = 128
QK_ROPE = 64
V_DIM = 128
Q_RANK = 512
KV_RANK = 256
ROPE_BASE = 10000.0
ATTN_SCALE = (QK_NOPE + QK_ROPE) ** -0.5
D_FF = -(-8 * D_MODEL // (3 * 256)) * 256

VMEM_LIMIT_BYTES = 56 * 1024 * 1024
SUBLANES = 8
MM_SUB_ROWS = 512
MERGE_SUB_ROWS = 256


def _cparams(n_axes):
    return pltpu.CompilerParams(dimension_semantics=("arbitrary",) * n_axes,
                                vmem_limit_bytes=VMEM_LIMIT_BYTES)


def _rms(x, g):
    return x * lax.rsqrt(jnp.mean(x * x, axis=-1, keepdims=True) + EPS) * g


def _mod_kernel(cv_ref, w_ref, b_ref, o_ref):
    s = jax.nn.silu(cv_ref[...]).astype(BF16)
    o_ref[...] = jnp.dot(s, w_ref[...].astype(BF16), preferred_element_type=F32) + b_ref[...]


def _modulation(cv, w_mod, b_mod):
    n = w_mod.shape[1]
    tn = 1536
    return pl.pallas_call(
        _mod_kernel,
        grid=(n // tn,),
        in_specs=[pl.BlockSpec((8, D_MODEL), lambda j: (0, 0)),
                  pl.BlockSpec((D_MODEL, tn), lambda j: (0, j)),
                  pl.BlockSpec((1, tn), lambda j: (0, j))],
        out_specs=pl.BlockSpec((8, tn), lambda j: (0, j)),
        out_shape=jax.ShapeDtypeStruct((8, n), F32),
        compiler_params=_cparams(1),
        name="mod",
    )(cv, w_mod, b_mod.reshape(1, n))


MOD_SHIFT1, MOD_SCALE1, MOD_GATE1, MOD_SHIFT2, MOD_SCALE2, MOD_GATE2 = range(6)


def _mod_spec(width, row_of, which, col_of=None):
    col_of = col_of or (lambda *ids: 0)
    return pl.BlockSpec((1, 1, 1, width), lambda *ids: (row_of(*ids), which, 0, col_of(*ids)))


def _norm_mod_kernel(x_ref, c_ref, g_ref, sc_ref, sh_ref, o_ref, *, lat_tiles):
    def emit(src_ref):
        y = _rms(src_ref[...], g_ref[...])
        o_ref[...] = (y * (1.0 + sc_ref[0, 0]) + sh_ref[0, 0]).astype(o_ref.dtype)

    pl.when(pl.program_id(0) < lat_tiles)(lambda: emit(x_ref))
    pl.when(pl.program_id(0) >= lat_tiles)(lambda: emit(c_ref))


def _norm_mod(x2d, c2d, gain, m, rows_per_batch, ctx_row, tm):
    lat_tiles, ctx_tiles = x2d.shape[0] // tm, c2d.shape[0] // tm
    tpb = rows_per_batch // tm

    def row(i):
        return jnp.where(i < lat_tiles, i // tpb, ctx_row)

    return pl.pallas_call(
        functools.partial(_norm_mod_kernel, lat_tiles=lat_tiles),
        grid=(lat_tiles + ctx_tiles,),
        in_specs=[pl.BlockSpec((tm, D_MODEL), lambda i: (jnp.minimum(i, lat_tiles - 1), 0)),
                  pl.BlockSpec((tm, D_MODEL), lambda i: (jnp.maximum(i - lat_tiles, 0), 0)),
                  pl.BlockSpec((1, D_MODEL), lambda i: (0, 0)),
                  _mod_spec(D_MODEL, row, MOD_SCALE1), _mod_spec(D_MODEL, row, MOD_SHIFT1)],
        out_specs=pl.BlockSpec((tm, D_MODEL), lambda i: (i, 0)),
        out_shape=jax.ShapeDtypeStruct((x2d.shape[0] + c2d.shape[0], D_MODEL), BF16),
        compiler_params=_cparams(1),
        name="norm_mod",
    )(x2d, c2d, gain.reshape(1, D_MODEL), m, m)


def _fused_mm(a, weights, epilogue, extras, outs, *, tm, nj, name, weight_buffers=2, rows=None, sub_rows=None):
    t, k = a.shape
    ni = (rows or t) // tm
    nw, ne, no = len(weights), len(extras), len(outs)
    need_cast = [w.dtype != BF16 for w, _, _, _ in weights]
    nt = (((1,), (1,)), ((), ()))

    def kernel(*refs):
        a_ref = refs[0]
        w_refs = refs[1:1 + nw]
        e_refs = refs[1 + nw:1 + nw + ne]
        o_refs = refs[1 + nw + ne:1 + nw + ne + no]
        s_refs = list(refs[1 + nw + ne + no:])
        staged = {idx: s_refs.pop(0) for idx in range(nw) if need_cast[idx]}
        if staged:
            @pl.when(pl.program_id(1) == 0)
            def _():
                for idx, s_ref in staged.items():
                    s_ref[...] = w_refs[idx][...].astype(BF16)

        for r0 in range(0, tm, sub_rows or tm):
            rows = slice(r0, r0 + (sub_rows or tm))
            av = a_ref[rows, :]
            accs = []
            for idx in range(nw):
                w_ref = staged.get(idx, w_refs[idx])
                if weights[idx][2] == "kn":
                    accs.append(jnp.dot(av, w_ref[...], preferred_element_type=F32))
                else:
                    accs.append(lax.dot_general(av, w_ref[...], nt, preferred_element_type=F32))
            epilogue(accs, e_refs, o_refs, rows)

    in_specs = [pl.BlockSpec((tm, k), lambda j, i: (i, 0))]
    scratch = []
    for (w, off, layout, width), cast in zip(weights, need_cast):
        if layout == "kn":
            shape = (k, width)
            mode = {} if weight_buffers == 2 else {"pipeline_mode": pl.Buffered(weight_buffers)}
            in_specs.append(pl.BlockSpec(shape, functools.partial(lambda j, i, off: (0, off + j), off=off), **mode))
        else:
            shape = (width, k)
            in_specs.append(pl.BlockSpec(
                (pl.Element(width), pl.Element(k)),
                functools.partial(lambda j, i, off, width: (pl.multiple_of(off + j * width, SUBLANES), 0),
                                  off=off, width=width)))
        if cast:
            scratch.append(pltpu.VMEM(shape, BF16))
    in_specs += [spec for _, spec in extras]
    return pl.pallas_call(
        kernel,
        grid=(nj, ni),
        in_specs=in_specs,
        out_specs=[spec for _, spec in outs],
        out_shape=[sds for sds, _ in outs],
        scratch_shapes=scratch,
        compiler_params=_cparams(2),
        name=name,
    )(a, *[w[0] for w in weights], *[e for e, _ in extras])


def _proj_cast(xn, w_in_t, row0, n, tm, tn, name, act=None, out_dtype=BF16, rows=None):
    def epi(accs, e_refs, o_refs, rows):
        v = accs[0]
        if act is not None:
            v = act(v)
        o_refs[0][rows, :] = v.astype(out_dtype)

    t = rows or xn.shape[0]
    return _fused_mm(xn, [(w_in_t, row0, "nk", tn)], epi, [],
                     [(jax.ShapeDtypeStruct((t, n), out_dtype), pl.BlockSpec((tm, tn), lambda j, i: (i, j)))],
                     tm=tm, nj=n // tn, name=name, rows=rows, sub_rows=MM_SUB_ROWS)[0]


def _q_path(xn, w_in_t, q_norm, wq2, cos2, sin2, tm, seq, rows):
    nr = MLA_HEADS * QK_ROPE
    nn = MLA_HEADS * QK_NOPE
    lanes = 2 * QK_ROPE

    def epi(accs, e_refs, o_refs, rows):
        qn_ref, w2_ref, cos_ref, sin_ref = e_refs
        cq = _rms(accs[0], qn_ref[...]).astype(BF16)
        q = jnp.dot(cq, w2_ref[...], preferred_element_type=F32)
        o_refs[0][rows, :nn] = (q[:, :nn] * ATTN_SCALE).astype(BF16)
        cos, sin = cos_ref[rows, :] * ATTN_SCALE, sin_ref[rows, :] * ATTN_SCALE
        for c0 in range(0, nr, lanes):
            rope = q[:, nn + c0:nn + c0 + lanes] * cos + q[:, nn + nr + c0:nn + nr + c0 + lanes] * sin
            o_refs[0][rows, nn + c0:nn + c0 + lanes] = rope.astype(BF16)

    pos_tiles = seq // tm
    extras = [(q_norm.reshape(1, Q_RANK), pl.BlockSpec((1, Q_RANK), lambda j, i: (0, 0))),
              (wq2, pl.BlockSpec(wq2.shape, lambda j, i: (0, 0))),
              (cos2, pl.BlockSpec((tm, lanes), lambda j, i: (i % pos_tiles, 0))),
              (sin2, pl.BlockSpec((tm, lanes), lambda j, i: (i % pos_tiles, 0)))]
    outs = [(jax.ShapeDtypeStruct((rows, nn + nr), BF16), pl.BlockSpec((tm, nn + nr), lambda j, i: (i, 0)))]
    return _fused_mm(xn, [(w_in_t, S5_WIDTH, "nk", Q_RANK)], epi, extras, outs, tm=tm, nj=1, name="q_path",
                     rows=rows)[0]


def _kv_path(xn, w_in_t, w_kr_rot_t, kv_norm, w_ukv_bf, cos_k, sin_k, tm, seq, lat_rows):
    nkv = w_ukv_bf.shape[1]

    def epi(accs, e_refs, o_refs, rows):
        acc = accs[0]
        ckv = _rms(acc[:, :KV_RANK], e_refs[0][...]).astype(BF16)
        o_refs[0][rows, :] = jnp.dot(ckv, e_refs[1][...], preferred_element_type=F32).astype(BF16)
        o_refs[1][rows, :] = (acc[:, KV_RANK:] * e_refs[2][rows, :] + accs[1] * e_refs[3][rows, :]).astype(BF16)

    t = xn.shape[0]
    pos_tiles, lat_tiles = seq // tm, lat_rows // tm

    def table_tile(j, i):
        return (jnp.where(i < lat_tiles, i % pos_tiles, pos_tiles), 0)

    extras = [(kv_norm.reshape(1, KV_RANK), pl.BlockSpec((1, KV_RANK), lambda j, i: (0, 0))),
              (w_ukv_bf, pl.BlockSpec(w_ukv_bf.shape, lambda j, i: (0, 0))),
              (cos_k, pl.BlockSpec((tm, QK_ROPE), table_tile)),
              (sin_k, pl.BlockSpec((tm, QK_ROPE), table_tile))]
    weights = [(w_in_t, S5_WIDTH + Q_RANK, "nk", KV_RANK + QK_ROPE), (w_kr_rot_t, 0, "nk", QK_ROPE)]
    outs = [(jax.ShapeDtypeStruct((t, nkv), BF16), pl.BlockSpec((tm, nkv), lambda j, i: (i, 0))),
            (jax.ShapeDtypeStruct((t, QK_ROPE), BF16), pl.BlockSpec((tm, QK_ROPE), lambda j, i: (i, 0)))]
    return _fused_mm(xn, weights, epi, extras, outs, tm=tm, nj=1, name="kv_path")


ATTN_SUB_ROWS = 256


def _attn_kernel(qn_ref, qr_ref, kvl_ref, kvc_ref, krl_ref, krc_ref, o_ref, k_scr, v_scr, *, seq, ctx):
    dk = QK_NOPE + QK_ROPE

    @pl.when(pl.program_id(2) == 0)
    def _():
        for h in range(2):
            base = h * (QK_NOPE + V_DIM)
            k_scr[h, :seq, :QK_NOPE] = kvl_ref[:, base:base + QK_NOPE]
            k_scr[h, seq:, :QK_NOPE] = kvc_ref[:, base:base + QK_NOPE]
            k_scr[h, :seq, QK_NOPE:dk] = krl_ref[...]
            k_scr[h, seq:, QK_NOPE:dk] = krc_ref[...]
            v_scr[h, :seq, :V_DIM] = kvl_ref[:, base + QK_NOPE:base + QK_NOPE + V_DIM]
            v_scr[h, seq:, :V_DIM] = kvc_ref[:, base + QK_NOPE:base + QK_NOPE + V_DIM]
            ones_col = lax.broadcasted_iota(jnp.int32, (seq + ctx, V_DIM), 1) == 0
            v_scr[h, :, V_DIM:] = jnp.where(ones_col, 1.0, 0.0).astype(BF16)

    for r0 in range(0, qn_ref.shape[0], ATTN_SUB_ROWS):
        rows = slice(r0, r0 + ATTN_SUB_ROWS)
        for h in range(2):
            q = jnp.concatenate([qn_ref[rows, h * QK_NOPE:(h + 1) * QK_NOPE],
                                 qr_ref[rows, h * QK_ROPE:(h + 1) * QK_ROPE]], axis=1)
            s = lax.dot_general(q, k_scr[h], (((1,), (1,)), ((), ())), preferred_element_type=F32)
            m = jnp.max(s, axis=-1, keepdims=True)
            p = jnp.exp((s - m).astype(BF16))
            ol = jnp.dot(p, v_scr[h], preferred_element_type=F32)
            o_ref[rows, h * V_DIM:(h + 1) * V_DIM] = (ol[:, :V_DIM] / ol[:, V_DIM:V_DIM + 1]).astype(o_ref.dtype)


def _attention(q, kv, kr, batch, seq, ctx, tq):
    nq = seq // tq
    ctx0 = batch * seq // ctx
    nn_blocks = MLA_HEADS * QK_NOPE // (2 * QK_NOPE)
    dk = QK_NOPE + QK_ROPE
    hw = 2 * (QK_NOPE + V_DIM)
    return pl.pallas_call(
        functools.partial(_attn_kernel, seq=seq, ctx=ctx),
        grid=(batch, MLA_HEADS // 2, nq),
        in_specs=[pl.BlockSpec((tq, 2 * QK_NOPE), lambda b, hp, qi: (b * nq + qi, hp)),
                  pl.BlockSpec((tq, 2 * QK_ROPE), lambda b, hp, qi: (b * nq + qi, 2 * nn_blocks + hp)),
                  pl.BlockSpec((seq, hw), lambda b, hp, qi: (b, hp)),
                  pl.BlockSpec((ctx, hw), lambda b, hp, qi: (ctx0 + b, hp)),
                  pl.BlockSpec((seq, QK_ROPE), lambda b, hp, qi: (b, 0)),
                  pl.BlockSpec((ctx, QK_ROPE), lambda b, hp, qi: (ctx0 + b, 0))],
        out_specs=pl.BlockSpec((tq, 2 * V_DIM), lambda b, hp, qi: (b * nq + qi, hp)),
        out_shape=jax.ShapeDtypeStruct((batch * seq, MLA_HEADS * V_DIM), BF16),
        scratch_shapes=[pltpu.VMEM((2, seq + ctx, dk), BF16), pltpu.VMEM((2, seq + ctx, 2 * V_DIM), BF16)],
        compiler_params=_cparams(3),
        name="attention",
    )(q, q, kv, kv, kr, kr)


def _s5_prep_kernel(are_ref, aim_ref, ldt_ref, bre_ref, bim_ref, cre_ref, cim_ref,
                    toep_ref, wsr_ref, wsi_ref, wor_ref, woi_ref, atr_ref, ati_ref):
    tc, g = S5_CHUNK, S5_GROUP
    w = tc * g
    lane = lax.broadcasted_iota(jnp.int32, (g, 2 * S5_STATE), 1)
    in_group = (lane < S5_STATE, lane >= S5_STATE)
    lane_w = lax.broadcasted_iota(jnp.int32, (g, w), 1)
    nt = (((1,), (1,)), ((), ()))
    toep_rows = [[jnp.zeros((g, w), F32) for _ in range(tc)] for _ in range(2)]
    for d in range(2):
        lr, li = are_ref[d, 0], aim_ref[d, 0]
        dt = jnp.exp(ldt_ref[d, 0])
        mag = jnp.exp(lr * dt)
        ab_re, ab_im = mag * jnp.cos(li * dt), mag * jnp.sin(li * dt)
        den = lr * lr + li * li
        nr, ni = ab_re - 1.0, ab_im
        co_re = (nr * lr + ni * li) / den
        co_im = (ni * lr - nr * li) / den
        br, bi = bre_ref[d, 0], bim_ref[d, 0]
        bb_re = co_re * br - co_im * bi
        bb_im = co_re * bi + co_im * br
        pw = [(jnp.ones_like(ab_re), jnp.zeros_like(ab_re))]
        for _ in range(tc):
            pr, pi = pw[-1]
            pw.append((pr * ab_re - pi * ab_im, pr * ab_im + pi * ab_re))
        cr, ci = cre_ref[d, 0], cim_ref[d, 0]
        ca = [(cr * pr - ci * pi, cr * pi + ci * pr) for pr, pi in pw]
        taus = list(range(tc))[::-1] if d else list(range(tc))
        y_re = jnp.concatenate([ca[t][0] for t in taus], axis=0).astype(BF16)
        y_im = jnp.concatenate([ca[t][1] for t in taus], axis=0).astype(BF16)
        for j in range(2):
            x_re = jnp.where(in_group[j], bb_re, 0.0)
            x_im = jnp.where(in_group[j], bb_im, 0.0)
            kt = (lax.dot_general(x_re.astype(BF16), y_re, nt, preferred_element_type=F32)
                  - lax.dot_general(x_im.astype(BF16), y_im, nt, preferred_element_type=F32))
            for r in range(tc):
                sh = (r + 1) * g if d else r * g
                blk = pltpu.roll(kt, sh % w, 1) if sh % w else kt
                keep = (lane_w < sh) if d else (lane_w >= sh)
                toep_rows[j][r] = toep_rows[j][r] + jnp.where(keep, blk, 0.0)
        for r in range(tc):
            pr, pi = pw[r] if d else pw[tc - 1 - r]
            w_re = bb_re * pr - bb_im * pi
            w_im = bb_re * pi + bb_im * pr
            car, cai = ca[tc - r] if d else ca[r + 1]
            for j in range(2):
                rows = slice(j * w + r * g, j * w + (r + 1) * g)
                wsr_ref[d, 0, rows, :] = jnp.where(in_group[j], w_re, 0.0).astype(BF16)
                wsi_ref[d, 0, rows, :] = jnp.where(in_group[j], w_im, 0.0).astype(BF16)
                wor_ref[d, 0, rows, :] = jnp.where(in_group[j], car, 0.0).astype(BF16)
                woi_ref[d, 0, rows, :] = jnp.where(in_group[j], -cai, 0.0).astype(BF16)
        atr_ref[d, 0] = pw[tc][0]
        ati_ref[d, 0] = pw[tc][1]
    for j in range(2):
        toep_ref[0, j] = jnp.concatenate(toep_rows[j], axis=0).astype(BF16)


def _s5_prep(params):
    tc = S5_CHUNK
    wide = 2 * tc * S5_GROUP
    sl = 2 * S5_STATE

    def pair_lanes(v):
        return v.astype(F32).reshape(2, S5_PAIRS, 1, sl)

    def pair_rows(v):
        rows = v.shape[2]
        return v.astype(F32).reshape(2, S5_PAIRS, 2, rows, S5_STATE).transpose(0, 1, 3, 2, 4).reshape(
            2, S5_PAIRS, rows, sl)

    ldt = jnp.broadcast_to(params['s5_log_dt'].astype(F32)[:, :, None], (2, S5_GROUPS, S5_STATE))
    ins = [pair_lanes(params['s5_a_re']), pair_lanes(params['s5_a_im']), pair_lanes(ldt),
           pair_rows(params['s5_b_re'].transpose(0, 1, 3, 2)), pair_rows(params['s5_b_im'].transpose(0, 1, 3, 2)),
           pair_rows(params['s5_c_re']), pair_rows(params['s5_c_im'])]
    vec_spec = pl.BlockSpec((2, 1, 1, sl), lambda k: (0, k, 0, 0))
    mat_spec = pl.BlockSpec((2, 1, S5_GROUP, sl), lambda k: (0, k, 0, 0))
    w_spec = pl.BlockSpec((2, 1, wide, sl), lambda k: (0, k, 0, 0))
    w_sds = jax.ShapeDtypeStruct((2, S5_PAIRS, wide, sl), BF16)
    a_sds = jax.ShapeDtypeStruct((2, S5_PAIRS, 1, sl), F32)
    return pl.pallas_call(
        _s5_prep_kernel,
        grid=(S5_PAIRS,),
        in_specs=[vec_spec] * 3 + [mat_spec] * 4,
        out_specs=[pl.BlockSpec((1, 2, wide // 2, wide // 2), lambda k: (k, 0, 0, 0))] + [w_spec] * 4 + [vec_spec] * 2,
        out_shape=[jax.ShapeDtypeStruct((S5_PAIRS, 2, wide // 2, wide // 2), BF16)] + [w_sds] * 4 + [a_sds] * 2,
        compiler_params=_cparams(1),
        name="s5_prep",
    )(*ins)


S5_STEP_PAIRS = 4
def _s5_state_rows(batch, n_lat, n_ctx):
    def pitch(n):
        p = -(-n // SUBLANES)
        return SUBLANES * (p + 1 - p % 2)

    lat_pitch, ctx_pitch = pitch(n_lat), pitch(n_ctx)
    ctx_base = batch * lat_pitch
    return lat_pitch, ctx_pitch, ctx_base, ctx_base + batch * ctx_pitch


def _s5_kernel(ul_ref, uc_ref, dsk_ref, *refs, batch, seq, ctx):
    toep_ref, wsr, wsi, wor, woi, atr, ati = refs[:7]
    o_ref = refs[7]
    wscr, zscr, upscr, sre, sim, hre, him = refs[8:]
    tc = S5_CHUNK
    n_lat, n_ctx = seq // tc, ctx // tc
    rows_lat = n_lat * batch
    lat_pitch, ctx_pitch, ctx_base, _ = _s5_state_rows(batch, n_lat, n_ctx)
    w = tc * S5_GROUP
    lanes = S5_STEP_PAIRS * 2 * S5_GROUP

    def scatter_tiles(xt, r, col0):
        for kk in range(S5_STEP_PAIRS):
            for j in range(2):
                ch = (2 * kk + j) * S5_GROUP
                wscr[kk, j * w + r * S5_GROUP:j * w + (r + 1) * S5_GROUP, col0:col0 + lanes] = xt[ch:ch + S5_GROUP, :]

    for r in range(tc):
        for b in range(batch):
            x = ul_ref[pl.ds(b * seq + r, n_lat, stride=tc), :]
            scatter_tiles(x.T, r, b * n_lat)
        xc = jnp.concatenate([uc_ref[pl.ds(b * ctx + r, n_ctx, stride=tc), :] for b in range(batch)]
                             + [jnp.zeros((lanes - batch * n_ctx, lanes), F32)], axis=0)
        scatter_tiles(xc.T, r, rows_lat)

    chains = [(kk, d) for kk in range(S5_STEP_PAIRS) for d in range(2)]
    for kk in range(S5_STEP_PAIRS):
        upscr[kk] = wscr[kk].T.astype(BF16)
        for d in range(2):
            for dst, wst in ((sre, wsr), (sim, wsi)):
                s = jnp.dot(upscr[kk], wst[d, kk], preferred_element_type=F32)
                for b in range(batch):
                    dst[kk, d, b * lat_pitch:b * lat_pitch + n_lat, :] = s[b * n_lat:(b + 1) * n_lat]
                    dst[kk, d, ctx_base + b * ctx_pitch:ctx_base + b * ctx_pitch + n_ctx, :] = (
                        s[rows_lat + b * n_ctx:rows_lat + (b + 1) * n_ctx])

    ctx_rows = [pl.ds(ctx_base + c, batch, stride=ctx_pitch) for c in range(n_ctx)]
    lat_rows = [pl.ds(c, batch, stride=lat_pitch) for c in range(n_lat)]
    order = (ctx_rows + lat_rows, ctx_rows[::-1] + lat_rows[::-1])
    coef = {(kk, d): (atr[d, kk], ati[d, kk]) for kk, d in chains}
    state = {ch: (jnp.zeros((batch, 2 * S5_STATE), F32), jnp.zeros((batch, 2 * S5_STATE), F32)) for ch in chains}
    for t in range(n_ctx + n_lat):
        for kk, d in chains:
            rows = order[d][t]
            (ar, ai), (h_re, h_im) = coef[kk, d], state[kk, d]
            hre[kk, d, rows, :] = h_re
            him[kk, d, rows, :] = h_im
            state[kk, d] = (ar * h_re - ai * h_im + sre[kk, d, rows, :],
                            ar * h_im + ai * h_re + sim[kk, d, rows, :])

    for kk in range(S5_STEP_PAIRS):
        ul = upscr[kk, :rows_lat, :]
        y = ul.astype(F32) * dsk_ref[kk]
        y = y + jnp.concatenate(
            [jnp.dot(ul[:, :w], toep_ref[kk, 0], preferred_element_type=F32),
             jnp.dot(ul[:, w:], toep_ref[kk, 1], preferred_element_type=F32)], axis=1)
        nt = (((1,), (1,)), ((), ()))
        for d in range(2):
            h_r = jnp.concatenate([hre[kk, d, b * lat_pitch:b * lat_pitch + n_lat, :] for b in range(batch)], axis=0)
            h_i = jnp.concatenate([him[kk, d, b * lat_pitch:b * lat_pitch + n_lat, :] for b in range(batch)], axis=0)
            y = y + lax.dot_general(h_r.astype(BF16), wor[d, kk], nt, preferred_element_type=F32)
            y = y + lax.dot_general(h_i.astype(BF16), woi[d, kk], nt, preferred_element_type=F32)
        yt = y.T
        for b in range(batch):
            for s in range(tc):
                for j in range(2):
                    ch = (2 * kk + j) * S5_GROUP
                    zscr[b * tc + s, ch:ch + S5_GROUP, :] = yt[j * w + s * S5_GROUP:j * w + (s + 1) * S5_GROUP,
                                                               b * n_lat:(b + 1) * n_lat]

    for b in range(batch):
        for s in range(tc):
            o_ref[pl.ds(b * seq + s, n_lat, stride=tc), :] = jax.nn.gelu(zscr[b * tc + s].T)


def _s5_mixer(u, params, batch, seq, ctx):
    tc = S5_CHUNK
    n_lat, n_ctx = seq // tc, ctx // tc
    sp = S5_STEP_PAIRS
    lanes = sp * 2 * S5_GROUP
    assert n_lat == lanes and batch * n_ctx <= lanes
    rows_lat = n_lat * batch
    rows_all = rows_lat + lanes
    state_rows = _s5_state_rows(batch, n_lat, n_ctx)[3]
    wide = 2 * tc * S5_GROUP
    dsk = jnp.broadcast_to(params['s5_d'].astype(F32).reshape(S5_PAIRS, 2, 1, S5_GROUP),
                           (S5_PAIRS, 2, tc, S5_GROUP)).reshape(S5_PAIRS, 1, wide)
    weights = _s5_prep(params)
    w_spec = pl.BlockSpec((2, sp, wide, 2 * S5_STATE), lambda k: (0, k, 0, 0))
    a_spec = pl.BlockSpec((2, sp, 1, 2 * S5_STATE), lambda k: (0, k, 0, 0))
    specs = [pl.BlockSpec((sp, 2, wide // 2, wide // 2), lambda k: (k, 0, 0, 0))] + [w_spec] * 4 + [a_spec] * 2
    return pl.pallas_call(
        functools.partial(_s5_kernel, batch=batch, seq=seq, ctx=ctx),
        grid=(S5_PAIRS // sp,),
        in_specs=[pl.BlockSpec((batch * seq, lanes), lambda k: (0, k)),
                  pl.BlockSpec((batch * ctx, lanes), lambda k: (seq // ctx, k)),
                  pl.BlockSpec((sp, 1, wide), lambda k: (k, 0, 0))] + specs,
        out_specs=pl.BlockSpec((batch * seq, lanes), lambda k: (0, k)),
        out_shape=jax.ShapeDtypeStruct((batch * seq, S5_WIDTH), F32),
        scratch_shapes=[pltpu.VMEM((sp, wide, rows_all), F32),
                        pltpu.VMEM((batch * tc, lanes, n_lat), F32),
                        pltpu.VMEM((sp, rows_all, wide), BF16)]
                       + [pltpu.VMEM((sp, 2, state_rows, 2 * S5_STATE), F32) for _ in range(4)],
        compiler_params=_cparams(1),
        name="s5",
    )(u, u, dsk, *weights)


def _merge_kernel(z5_ref, o_ref_in, gs_ref, gm_ref, wa_ref, wb_ref, wm_ref, out_ref, wa_s, wb_s, wm_s):
    @pl.when(pl.program_id(1) == 0)
    def _():
        wa_s[...] = wa_ref[...].astype(BF16)
        wb_s[...] = wb_ref[...].astype(BF16)
        wm_s[...] = wm_ref[...].astype(BF16)

    sub = min(MERGE_SUB_ROWS, out_ref.shape[0])
    for r0 in range(0, out_ref.shape[0], sub):
        rows = slice(r0, r0 + sub)
        z = z5_ref[rows, :].astype(BF16)
        a = jnp.dot(z, wa_s[...], preferred_element_type=F32)
        b = jnp.dot(z, wb_s[...], preferred_element_type=F32)
        mla = jnp.dot(o_ref_in[rows, :], wm_s[...], preferred_element_type=F32)
        merged = gs_ref[rows, :].astype(F32) * (a * jax.nn.sigmoid(b)) + gm_ref[rows, :].astype(F32) * mla
        out_ref[rows, :] = merged.astype(out_ref.dtype)


def _merge(y5, o_mla, gates, w_glu, w_mla_o, tm, tn):
    t = y5.shape[0]
    nj = D_MODEL // tn
    return pl.pallas_call(
        _merge_kernel,
        grid=(nj, t // tm),
        in_specs=[pl.BlockSpec((tm, S5_WIDTH), lambda j, i: (i, 0)),
                  pl.BlockSpec((tm, MLA_HEADS * V_DIM), lambda j, i: (i, 0)),
                  pl.BlockSpec((tm, tn), lambda j, i: (i, j)),
                  pl.BlockSpec((tm, tn), lambda j, i: (i, nj + j)),
                  pl.BlockSpec((S5_WIDTH, tn), lambda j, i: (0, j)),
                  pl.BlockSpec((S5_WIDTH, tn), lambda j, i: (0, nj + j)),
                  pl.BlockSpec((MLA_HEADS * V_DIM, tn), lambda j, i: (0, j))],
        out_specs=pl.BlockSpec((tm, tn), lambda j, i: (i, j)),
        out_shape=jax.ShapeDtypeStruct((t, D_MODEL), BF16),
        scratch_shapes=[pltpu.VMEM((S5_WIDTH, tn), BF16), pltpu.VMEM((S5_WIDTH, tn), BF16),
                        pltpu.VMEM((MLA_HEADS * V_DIM, tn), BF16)],
        compiler_params=_cparams(2),
        name="merge",
    )(y5, o_mla, gates, gates, w_glu, w_glu, w_mla_o)


def _resid_mm(a, w, x_res, m, which, rows_per_batch, tm, tn, name, weight_buffers=2):
    def epi(accs, e_refs, o_refs, rows):
        o_refs[0][rows, :] = e_refs[0][rows, :] + e_refs[1][0, 0] * accs[0]

    t = a.shape[0]
    n = w.shape[1]
    tpb = rows_per_batch // tm
    extras = [(x_res, pl.BlockSpec((tm, tn), lambda j, i: (i, j))),
              (m, _mod_spec(tn, lambda j, i: i // tpb, which, lambda j, i: j))]
    outs = [(jax.ShapeDtypeStruct((t, n), F32), pl.BlockSpec((tm, tn), lambda j, i: (i, j)))]
    return _fused_mm(a, [(w, 0, "kn", tn)], epi, extras, outs, tm=tm, nj=n // tn, name=name,
                     weight_buffers=weight_buffers)[0]


def _out_proj_norm_kernel(a_ref, w_ref, x_ref, g1_ref, n2_ref, sc_ref, sh_ref, x1_ref, xn_ref, w_s):
    @pl.when(pl.program_id(0) == 0)
    def _():
        w_s[...] = w_ref[...].astype(BF16)

    sub = min(MERGE_SUB_ROWS, x1_ref.shape[0])
    for r0 in range(0, x1_ref.shape[0], sub):
        rows = slice(r0, r0 + sub)
        x1 = x_ref[rows, :] + g1_ref[0, 0] * jnp.dot(a_ref[rows, :], w_s[...], preferred_element_type=F32)
        x1_ref[rows, :] = x1
        xn_ref[rows, :] = (_rms(x1, n2_ref[...]) * (1.0 + sc_ref[0, 0]) + sh_ref[0, 0]).astype(xn_ref.dtype)


def _out_proj_norm(merged, w_out, x_res, m, gain2, rows_per_batch, tm):
    t, k = merged.shape
    tpb = rows_per_batch // tm
    row = pl.BlockSpec((tm, D_MODEL), lambda i: (i, 0))
    gate1, scale2, shift2 = (_mod_spec(D_MODEL, lambda i: i // tpb, which)
                             for which in (MOD_GATE1, MOD_SCALE2, MOD_SHIFT2))
    return pl.pallas_call(
        _out_proj_norm_kernel,
        grid=(t // tm,),
        in_specs=[pl.BlockSpec((tm, k), lambda i: (i, 0)),
                  pl.BlockSpec((k, D_MODEL), lambda i: (0, 0), pipeline_mode=pl.Buffered(1)),
                  row, gate1, pl.BlockSpec((1, D_MODEL), lambda i: (0, 0)), scale2, shift2],
        out_specs=[row, row],
        out_shape=[jax.ShapeDtypeStruct((t, D_MODEL), F32), jax.ShapeDtypeStruct((t, D_MODEL), BF16)],
        scratch_shapes=[pltpu.VMEM((k, D_MODEL), BF16)],
        compiler_params=_cparams(1),
        name="out_proj",
    )(merged, w_out, x_res, m, gain2.reshape(1, D_MODEL), m, m)


def _ffn_in(xn, w_ffn_in, tm, tn):
    def epi(accs, e_refs, o_refs, rows):
        o_refs[0][rows, :] = (jax.nn.silu(accs[0]) * accs[1]).astype(BF16)

    t = xn.shape[0]
    nj = D_FF // tn
    outs = [(jax.ShapeDtypeStruct((t, D_FF), BF16), pl.BlockSpec((tm, tn), lambda j, i: (i, j)))]
    return _fused_mm(xn, [(w_ffn_in, 0, "kn", tn), (w_ffn_in, nj, "kn", tn)], epi, [], outs,
                     tm=tm, nj=nj, name="ffn_in", sub_rows=MM_SUB_ROWS)[0]


def _final_norm_kernel(x_ref, g_ref, o_ref):
    o_ref[...] = _rms(x_ref[...], g_ref[...])


def _final_norm(x2d, g, tm):
    t = x2d.shape[0]
    return pl.pallas_call(
        _final_norm_kernel,
        grid=(t // tm,),
        in_specs=[pl.BlockSpec((tm, D_MODEL), lambda i: (i, 0)),
                  pl.BlockSpec((1, D_MODEL), lambda i: (0, 0))],
        out_specs=pl.BlockSpec((tm, D_MODEL), lambda i: (i, 0)),
        out_shape=jax.ShapeDtypeStruct((t, D_MODEL), F32),
        compiler_params=_cparams(1),
        name="final_norm",
    )(x2d, g.reshape(1, D_MODEL))


def _rope_rot_cols(w):
    k = w.shape[0]
    ws = w.reshape(k, -1, 2, 2, QK_ROPE // 4)
    return jnp.stack([-ws[:, :, :, 1, :], ws[:, :, :, 0, :]], axis=3).reshape(k, -1)


def _rope_tables(n_tokens):
    rows = n_tokens // GRID_W
    row = jnp.repeat(jnp.arange(rows, dtype=F32), GRID_W)
    col = jnp.tile(jnp.arange(GRID_W, dtype=F32), rows)
    n_freq = QK_ROPE // 4
    inv = ROPE_BASE ** (-jnp.arange(n_freq, dtype=F32) / n_freq)
    ang = jnp.stack([row[:, None] * inv, col[:, None] * inv], axis=1)
    cos = jnp.broadcast_to(jnp.cos(ang)[:, :, None, :], (n_tokens, 2, 2, n_freq)).reshape(n_tokens, QK_ROPE)
    sin = jnp.broadcast_to(jnp.sin(ang)[:, :, None, :], (n_tokens, 2, 2, n_freq)).reshape(n_tokens, QK_ROPE)
    return cos, sin


def kernel(x, c, ctx, c_ctx, w_mod, b_mod, norm1, norm2, w_in, s5_a_re, s5_a_im, s5_log_dt, s5_b_re, s5_b_im,
           s5_c_re, s5_c_im, s5_d, w_glu, q_norm, kv_norm, w_uq, w_ukv, w_mla_o, w_out, w_ffn_in, w_ffn_out,
           norm_f):
    batch, seq, _ = x.shape
    n_ctx = ctx.shape[1]
    assert w_mod.shape[0] == 1, "single-layer block"
    p = dict(s5_a_re=s5_a_re[0], s5_a_im=s5_a_im[0], s5_log_dt=s5_log_dt[0], s5_b_re=s5_b_re[0],
             s5_b_im=s5_b_im[0], s5_c_re=s5_c_re[0], s5_c_im=s5_c_im[0], s5_d=s5_d[0])
    w_in_t = w_in.reshape(w_in.shape[1:]).T

    cv = jnp.concatenate([c, c_ctx[None], jnp.zeros((8 - batch - 1, D_MODEL), F32)], axis=0)
    m = _modulation(cv, w_mod[0], b_mod[0]).reshape(8, 6, 1, D_MODEL)

    lat_rows = batch * seq
    x2d = x.reshape(lat_rows, D_MODEL)
    c2d = ctx.reshape(batch * n_ctx, D_MODEL)
    xn = _norm_mod(x2d, c2d, norm1[0], m, seq, batch, 1024)

    kv_lo = S5_WIDTH + Q_RANK
    w_kr_rot_t = _rope_rot_cols(w_in_t[kv_lo + KV_RANK:kv_lo + KV_RANK + QK_ROPE].T).T
    wq = w_uq[0].reshape(Q_RANK, MLA_HEADS, QK_NOPE + QK_ROPE)
    wq_rope = wq[:, :, QK_NOPE:].reshape(Q_RANK, MLA_HEADS * QK_ROPE)
    wq2 = jnp.concatenate([wq[:, :, :QK_NOPE].reshape(Q_RANK, MLA_HEADS * QK_NOPE), wq_rope,
                           _rope_rot_cols(wq_rope)], axis=1).astype(BF16)
    w_ukv_bf = w_ukv[0].astype(BF16)
    tm = 1024
    cos, sin = _rope_tables(seq)
    cos_k = jnp.concatenate([cos, jnp.ones((tm, QK_ROPE), F32)], axis=0)
    sin_k = jnp.concatenate([sin, jnp.zeros((tm, QK_ROPE), F32)], axis=0)

    u = _proj_cast(xn, w_in_t, 0, S5_WIDTH, tm, 1024, "u_proj", out_dtype=F32)
    q = _q_path(xn, w_in_t, q_norm[0], wq2, jnp.tile(cos, (1, 2)), jnp.tile(sin, (1, 2)), tm, seq, lat_rows)
    kv, kr = _kv_path(xn, w_in_t, w_kr_rot_t, kv_norm[0], w_ukv_bf, cos_k, sin_k, tm, seq, lat_rows)
    gates = _proj_cast(xn, w_in_t, kv_lo + KV_RANK + QK_ROPE, 2 * D_MODEL, tm, 1024, "gates",
                       act=jax.nn.sigmoid, rows=lat_rows)

    z5 = _s5_mixer(u, p, batch, seq, n_ctx)
    o_mla = _attention(q, kv, kr, batch, seq, n_ctx, 2048)

    merged = _merge(z5, o_mla, gates, w_glu[0], w_mla_o[0], 512, 1024)
    x1, xn2 = _out_proj_norm(merged, w_out[0], x2d, m, norm2[0], seq, 512)
    hid = _ffn_in(xn2, w_ffn_in[0], 2048, 512)
    x2 = _resid_mm(hid, w_ffn_out[0], x1, m, MOD_GATE2, seq, 256, 1024, "ffn_out", weight_buffers=1)
    return _final_norm(x2, norm_f, 1024).reshape(batch, seq, D_MODEL)
```

```python
import functools
import math

import jax
import jax.numpy as jnp
from jax import lax
from jax.experimental import pallas as pl
from jax.experimental.pallas import tpu as pltpu

F32 = jnp.float32
BF16 = jnp.bfloat16

D_MODEL = 2048
GRID_W = 64
EPS = 1e-6
S5_WIDTH = D_MODEL // 2
S5_GROUP = 16
S5_GROUPS = S5_WIDTH // S5_GROUP
S5_STATE = 64
S5_CHUNK = 16
S5_PAIRS = S5_GROUPS // 2
MLA_HEADS = 8
QK_NOPE = 128
QK_ROPE = 64
V_DIM = 128
Q_RANK = 512
KV_RANK = 256
ROPE_BASE = 10000.0
ATTN_SCALE = (QK_NOPE + QK_ROPE) ** -0.5
D_FF = -(-8 * D_MODEL // (3 * 256)) * 256

VMEM_LIMIT_BYTES = 56 * 1024 * 1024
SUBLANES = 8
MM_SUB_ROWS = 512
MERGE_SUB_ROWS = 256


def _cparams(n_axes):
    return pltpu.CompilerParams(dimension_semantics=("arbitrary",) * n_axes,
                                vmem_limit_bytes=VMEM_LIMIT_BYTES)


def _rms(x, g):
    return x * lax.rsqrt(jnp.mean(x * x, axis=-1, keepdims=True) + EPS) * g


def _mod_kernel(cv_ref, w_ref, b_ref, o_ref):
    s = jax.nn.silu(cv_ref[...]).astype(BF16)
    o_ref[...] = jnp.dot(s, w_ref[...].astype(BF16), preferred_element_type=F32) + b_ref[...]


def _modulation(cv, w_mod, b_mod):
    n = w_mod.shape[1]
    tn = 1536
    return pl.pallas_call(
        _mod_kernel,
        grid=(n // tn,),
        in_specs=[pl.BlockSpec((8, D_MODEL), lambda j: (0, 0)),
                  pl.BlockSpec((D_MODEL, tn), lambda j: (0, j)),
                  pl.BlockSpec((1, tn), lambda j: (0, j))],
        out_specs=pl.BlockSpec((8, tn), lambda j: (0, j)),
        out_shape=jax.ShapeDtypeStruct((8, n), F32),
        compiler_params=_cparams(1),
        name="mod",
    )(cv, w_mod, b_mod.reshape(1, n))


MOD_SHIFT1, MOD_SCALE1, MOD_GATE1, MOD_SHIFT2, MOD_SCALE2, MOD_GATE2 = range(6)


def _mod_spec(width, row_of, which, col_of=None):
    col_of = col_of or (lambda *ids: 0)
    return pl.BlockSpec((1, 1, 1, width), lambda *ids: (row_of(*ids), which, 0, col_of(*ids)))


def _norm_mod_kernel(x_ref, c_ref, g_ref, sc_ref, sh_ref, o_ref, *, lat_tiles):
    def emit(src_ref):
        y = _rms(src_ref[...], g_ref[...])
        o_ref[...] = (y * (1.0 + sc_ref[0, 0]) + sh_ref[0, 0]).astype(o_ref.dtype)

    pl.when(pl.program_id(0) < lat_tiles)(lambda: emit(x_ref))
    pl.when(pl.program_id(0) >= lat_tiles)(lambda: emit(c_ref))


def _norm_mod(x2d, c2d, gain, m, rows_per_batch, ctx_row, tm):
    lat_tiles, ctx_tiles = x2d.shape[0] // tm, c2d.shape[0] // tm
    tpb = rows_per_batch // tm

    def row(i):
        return jnp.where(i < lat_tiles, i // tpb, ctx_row)

    return pl.pallas_call(
        functools.partial(_norm_mod_kernel, lat_tiles=lat_tiles),
        grid=(lat_tiles + ctx_tiles,),
        in_specs=[pl.BlockSpec((tm, D_MODEL), lambda i: (jnp.minimum(i, lat_tiles - 1), 0)),
                  pl.BlockSpec((tm, D_MODEL), lambda i: (jnp.maximum(i - lat_tiles, 0), 0)),
                  pl.BlockSpec((1, D_MODEL), lambda i: (0, 0)),
                  _mod_spec(D_MODEL, row, MOD_SCALE1), _mod_spec(D_MODEL, row, MOD_SHIFT1)],
        out_specs=pl.BlockSpec((tm, D_MODEL), lambda i: (i, 0)),
        out_shape=jax.ShapeDtypeStruct((x2d.shape[0] + c2d.shape[0], D_MODEL), BF16),
        compiler_params=_cparams(1),
        name="norm_mod",
    )(x2d, c2d, gain.reshape(1, D_MODEL), m, m)


def _fused_mm(a, weights, epilogue, extras, outs, *, tm, nj, name, rows=None, sub_rows=None):
    t, k = a.shape
    ni = (rows or t) // tm
    nw, ne, no = len(weights), len(extras), len(outs)
    need_cast = [w.dtype != BF16 for w, _, _, _ in weights]
    nt = (((1,), (1,)), ((), ()))

    def kernel(*refs):
        a_ref = refs[0]
        w_refs = refs[1:1 + nw]
        e_refs = refs[1 + nw:1 + nw + ne]
        o_refs = refs[1 + nw + ne:1 + nw + ne + no]
        s_refs = list(refs[1 + nw + ne + no:])
        staged = {idx: s_refs.pop(0) for idx in range(nw) if need_cast[idx]}
        if staged:
            @pl.when(pl.program_id(1) == 0)
            def _():
                for idx, s_ref in staged.items():
                    s_ref[...] = w_refs[idx][...].astype(BF16)

        for r0 in range(0, tm, sub_rows or tm):
            rows = slice(r0, r0 + (sub_rows or tm))
            av = a_ref[rows, :]
            accs = []
            for idx in range(nw):
                w_ref = staged.get(idx, w_refs[idx])
                if weights[idx][2] == "kn":
                    accs.append(jnp.dot(av, w_ref[...], preferred_element_type=F32))
                else:
                    accs.append(lax.dot_general(av, w_ref[...], nt, preferred_element_type=F32))
            epilogue(accs, e_refs, o_refs, rows)

    in_specs = [pl.BlockSpec((tm, k), lambda j, i: (i, 0))]
    scratch = []
    for (w, off, layout, width), cast in zip(weights, need_cast):
        if layout == "kn":
            shape = (k, width)
            in_specs.append(pl.BlockSpec(shape, functools.partial(lambda j, i, off: (0, off + j), off=off)))
        else:
            shape = (width, k)
            in_specs.append(pl.BlockSpec(
                (pl.Element(width), pl.Element(k)),
                functools.partial(lambda j, i, off, width: (pl.multiple_of(off + j * width, SUBLANES), 0),
                                  off=off, width=width)))
        if cast:
            scratch.append(pltpu.VMEM(shape, BF16))
    in_specs += [spec for _, spec in extras]
    return pl.pallas_call(
        kernel,
        grid=(nj, ni),
        in_specs=in_specs,
        out_specs=[spec for _, spec in outs],
        out_shape=[sds for sds, _ in outs],
        scratch_shapes=scratch,
        compiler_params=_cparams(2),
        name=name,
    )(a, *[w[0] for w in weights], *[e for e, _ in extras])


def _proj_cast(xn, w_in_t, row0, n, tm, tn, name, act=None, out_dtype=BF16, rows=None):
    def epi(accs, e_refs, o_refs, rows):
        v = accs[0]
        if act is not None:
            v = act(v)
        o_refs[0][rows, :] = v.astype(out_dtype)

    t = rows or xn.shape[0]
    return _fused_mm(xn, [(w_in_t, row0, "nk", tn)], epi, [],
                     [(jax.ShapeDtypeStruct((t, n), out_dtype), pl.BlockSpec((tm, tn), lambda j, i: (i, j)))],
                     tm=tm, nj=n // tn, name=name, rows=rows)[0]


def _q_path(xn, w_in_t, q_norm, wq2, cos2, sin2, tm, seq, rows):
    nr = MLA_HEADS * QK_ROPE
    nn = MLA_HEADS * QK_NOPE
    lanes = 2 * QK_ROPE

    def epi(accs, e_refs, o_refs, rows):
        qn_ref, w2_ref, cos_ref, sin_ref = e_refs
        cq = _rms(accs[0], qn_ref[...]).astype(BF16)
        q = jnp.dot(cq, w2_ref[...], preferred_element_type=F32)
        o_refs[0][rows, :nn] = (q[:, :nn] * ATTN_SCALE).astype(BF16)
        cos, sin = cos_ref[rows, :] * ATTN_SCALE, sin_ref[rows, :] * ATTN_SCALE
        for c0 in range(0, nr, lanes):
            rope = q[:, nn + c0:nn + c0 + lanes] * cos + q[:, nn + nr + c0:nn + nr + c0 + lanes] * sin
            o_refs[0][rows, nn + c0:nn + c0 + lanes] = rope.astype(BF16)

    pos_tiles = seq // tm
    extras = [(q_norm.reshape(1, Q_RANK), pl.BlockSpec((1, Q_RANK), lambda j, i: (0, 0))),
              (wq2, pl.BlockSpec(wq2.shape, lambda j, i: (0, 0))),
              (cos2, pl.BlockSpec((tm, lanes), lambda j, i: (i % pos_tiles, 0))),
              (sin2, pl.BlockSpec((tm, lanes), lambda j, i: (i % pos_tiles, 0)))]
    outs = [(jax.ShapeDtypeStruct((rows, nn + nr), BF16), pl.BlockSpec((tm, nn + nr), lambda j, i: (i, 0)))]
    return _fused_mm(xn, [(w_in_t, S5_WIDTH, "nk", Q_RANK)], epi, extras, outs, tm=tm, nj=1, name="q_path",
                     rows=rows)[0]


def _kv_path(xn, w_in_t, w_kr_rot_t, kv_norm, w_ukv_bf, cos_k, sin_k, tm, seq, lat_rows):
    nkv = w_ukv_bf.shape[1]

    def epi(accs, e_refs, o_refs, rows):
        acc = accs[0]
        ckv = _rms(acc[:, :KV_RANK], e_refs[0][...]).astype(BF16)
        o_refs[0][rows, :] = jnp.dot(ckv, e_refs[1][...], preferred_element_type=F32).astype(BF16)
        o_refs[1][rows, :] = (acc[:, KV_RANK:] * e_refs[2][rows, :] + accs[1] * e_refs[3][rows, :]).astype(BF16)

    t = xn.shape[0]
    pos_tiles, lat_tiles = seq // tm, lat_rows // tm

    def table_tile(j, i):
        return (jnp.where(i < lat_tiles, i % pos_tiles, pos_tiles), 0)

    extras = [(kv_norm.reshape(1, KV_RANK), pl.BlockSpec((1, KV_RANK), lambda j, i: (0, 0))),
              (w_ukv_bf, pl.BlockSpec(w_ukv_bf.shape, lambda j, i: (0, 0))),
              (cos_k, pl.BlockSpec((tm, QK_ROPE), table_tile)),
              (sin_k, pl.BlockSpec((tm, QK_ROPE), table_tile))]
    weights = [(w_in_t, S5_WIDTH + Q_RANK, "nk", KV_RANK + QK_ROPE), (w_kr_rot_t, 0, "nk", QK_ROPE)]
    outs = [(jax.ShapeDtypeStruct((t, nkv), BF16), pl.BlockSpec((tm, nkv), lambda j, i: (i, 0))),
            (jax.ShapeDtypeStruct((t, QK_ROPE), BF16), pl.BlockSpec((tm, QK_ROPE), lambda j, i: (i, 0)))]
    return _fused_mm(xn, weights, epi, extras, outs, tm=tm, nj=1, name="kv_path")


ATTN_SUB_ROWS = 256


def _attn_kernel(qn_ref, qr_ref, kvl_ref, kvc_ref, krl_ref, krc_ref, o_ref, k_scr, v_scr, *, seq, ctx):
    dk = QK_NOPE + QK_ROPE

    @pl.when(pl.program_id(2) == 0)
    def _():
        for h in range(2):
            base = h * (QK_NOPE + V_DIM)
            k_scr[h, :seq, :QK_NOPE] = kvl_ref[:, base:base + QK_NOPE]
            k_scr[h, seq:, :QK_NOPE] = kvc_ref[:, base:base + QK_NOPE]
            k_scr[h, :seq, QK_NOPE:dk] = krl_ref[...]
            k_scr[h, seq:, QK_NOPE:dk] = krc_ref[...]
            v_scr[h, :seq, :V_DIM] = kvl_ref[:, base + QK_NOPE:base + QK_NOPE + V_DIM]
            v_scr[h, seq:, :V_DIM] = kvc_ref[:, base + QK_NOPE:base + QK_NOPE + V_DIM]
            ones_col = lax.broadcasted_iota(jnp.int32, (seq + ctx, V_DIM), 1) == 0
            v_scr[h, :, V_DIM:] = jnp.where(ones_col, 1.0, 0.0).astype(BF16)

    for r0 in range(0, qn_ref.shape[0], ATTN_SUB_ROWS):
        rows = slice(r0, r0 + ATTN_SUB_ROWS)
        for h in range(2):
            q = jnp.concatenate([qn_ref[rows, h * QK_NOPE:(h + 1) * QK_NOPE],
                                 qr_ref[rows, h * QK_ROPE:(h + 1) * QK_ROPE]], axis=1)
            s = lax.dot_general(q, k_scr[h], (((1,), (1,)), ((), ())), preferred_element_type=F32)
            m = jnp.max(s, axis=-1, keepdims=True)
            p = jnp.exp((s - m).astype(BF16))
            ol = jnp.dot(p, v_scr[h], preferred_element_type=F32)
            o_ref[rows, h * V_DIM:(h + 1) * V_DIM] = (ol[:, :V_DIM] / ol[:, V_DIM:V_DIM + 1]).astype(o_ref.dtype)


def _attention(q, kv, kr, batch, seq, ctx, tq):
    nq = seq // tq
    ctx0 = batch * seq // ctx
    nn_blocks = MLA_HEADS * QK_NOPE // (2 * QK_NOPE)
    dk = QK_NOPE + QK_ROPE
    hw = 2 * (QK_NOPE + V_DIM)
    return pl.pallas_call(
        functools.partial(_attn_kernel, seq=seq, ctx=ctx),
        grid=(batch, MLA_HEADS // 2, nq),
        in_specs=[pl.BlockSpec((tq, 2 * QK_NOPE), lambda b, hp, qi: (b * nq + qi, hp)),
                  pl.BlockSpec((tq, 2 * QK_ROPE), lambda b, hp, qi: (b * nq + qi, 2 * nn_blocks + hp)),
                  pl.BlockSpec((seq, hw), lambda b, hp, qi: (b, hp)),
                  pl.BlockSpec((ctx, hw), lambda b, hp, qi: (ctx0 + b, hp)),
                  pl.BlockSpec((seq, QK_ROPE), lambda b, hp, qi: (b, 0)),
                  pl.BlockSpec((ctx, QK_ROPE), lambda b, hp, qi: (ctx0 + b, 0))],
        out_specs=pl.BlockSpec((tq, 2 * V_DIM), lambda b, hp, qi: (b * nq + qi, hp)),
        out_shape=jax.ShapeDtypeStruct((batch * seq, MLA_HEADS * V_DIM), BF16),
        scratch_shapes=[pltpu.VMEM((2, seq + ctx, dk), BF16), pltpu.VMEM((2, seq + ctx, 2 * V_DIM), BF16)],
        compiler_params=_cparams(3),
        name="attention",
    )(q, q, kv, kv, kr, kr)


def _s5_prep_kernel(are_ref, aim_ref, ldt_ref, bre_ref, bim_ref, cre_ref, cim_ref,
                    toep_ref, wsr_ref, wsi_ref, wor_ref, woi_ref, atr_ref, ati_ref):
    tc, g = S5_CHUNK, S5_GROUP
    w = tc * g
    lane = lax.broadcasted_iota(jnp.int32, (g, 2 * S5_STATE), 1)
    in_group = (lane < S5_STATE, lane >= S5_STATE)
    lane_w = lax.broadcasted_iota(jnp.int32, (g, w), 1)
    nt = (((1,), (1,)), ((), ()))
    toep_rows = [[jnp.zeros((g, w), F32) for _ in range(tc)] for _ in range(2)]
    for d in range(2):
        lr, li = are_ref[d, 0], aim_ref[d, 0]
        dt = jnp.exp(ldt_ref[d, 0])
        mag = jnp.exp(lr * dt)
        ab_re, ab_im = mag * jnp.cos(li * dt), mag * jnp.sin(li * dt)
        den = lr * lr + li * li
        nr, ni = ab_re - 1.0, ab_im
        co_re = (nr * lr + ni * li) / den
        co_im = (ni * lr - nr * li) / den
        br, bi = bre_ref[d, 0], bim_ref[d, 0]
        bb_re = co_re * br - co_im * bi
        bb_im = co_re * bi + co_im * br
        pw = [(jnp.ones_like(ab_re), jnp.zeros_like(ab_re))]
        for _ in range(tc):
            pr, pi = pw[-1]
            pw.append((pr * ab_re - pi * ab_im, pr * ab_im + pi * ab_re))
        cr, ci = cre_ref[d, 0], cim_ref[d, 0]
        ca = [(cr * pr - ci * pi, cr * pi + ci * pr) for pr, pi in pw]
        taus = list(range(tc))[::-1] if d else list(range(tc))
        y_re = jnp.concatenate([ca[t][0] for t in taus], axis=0).astype(BF16)
        y_im = jnp.concatenate([ca[t][1] for t in taus], axis=0).astype(BF16)
        for j in range(2):
            x_re = jnp.where(in_group[j], bb_re, 0.0)
            x_im = jnp.where(in_group[j], bb_im, 0.0)
            kt = (lax.dot_general(x_re.astype(BF16), y_re, nt, preferred_element_type=F32)
                  - lax.dot_general(x_im.astype(BF16), y_im, nt, preferred_element_type=F32))
            for r in range(tc):
                sh = (r + 1) * g if d else r * g
                blk = pltpu.roll(kt, sh % w, 1) if sh % w else kt
                keep = (lane_w < sh) if d else (lane_w >= sh)
                toep_rows[j][r] = toep_rows[j][r] + jnp.where(keep, blk, 0.0)
        for r in range(tc):
            pr, pi = pw[r] if d else pw[tc - 1 - r]
            w_re = bb_re * pr - bb_im * pi
            w_im = bb_re * pi + bb_im * pr
            car, cai = ca[tc - r] if d else ca[r + 1]
            for j in range(2):
                rows = slice(j * w + r * g, j * w + (r + 1) * g)
                wsr_ref[d, 0, rows, :] = jnp.where(in_group[j], w_re, 0.0).astype(BF16)
                wsi_ref[d, 0, rows, :] = jnp.where(in_group[j], w_im, 0.0).astype(BF16)
                wor_ref[d, 0, rows, :] = jnp.where(in_group[j], car, 0.0).astype(BF16)
                woi_ref[d, 0, rows, :] = jnp.where(in_group[j], -cai, 0.0).astype(BF16)
        atr_ref[d, 0] = pw[tc][0]
        ati_ref[d, 0] = pw[tc][1]
    for j in range(2):
        toep_ref[0, j] = jnp.concatenate(toep_rows[j], axis=0).astype(BF16)


def _s5_prep(params):
    tc = S5_CHUNK
    wide = 2 * tc * S5_GROUP
    sl = 2 * S5_STATE

    def pair_lanes(v):
        return v.astype(F32).reshape(2, S5_PAIRS, 1, sl)

    def pair_rows(v):
        rows = v.shape[2]
        return v.astype(F32).reshape(2, S5_PAIRS, 2, rows, S5_STATE).transpose(0, 1, 3, 2, 4).reshape(
            2, S5_PAIRS, rows, sl)

    ldt = jnp.broadcast_to(params['s5_log_dt'].astype(F32)[:, :, None], (2, S5_GROUPS, S5_STATE))
    ins = [pair_lanes(params['s5_a_re']), pair_lanes(params['s5_a_im']), pair_lanes(ldt),
           pair_rows(params['s5_b_re'].transpose(0, 1, 3, 2)), pair_rows(params['s5_b_im'].transpose(0, 1, 3, 2)),
           pair_rows(params['s5_c_re']), pair_rows(params['s5_c_im'])]
    vec_spec = pl.BlockSpec((2, 1, 1, sl), lambda k: (0, k, 0, 0))
    mat_spec = pl.BlockSpec((2, 1, S5_GROUP, sl), lambda k: (0, k, 0, 0))
    w_spec = pl.BlockSpec((2, 1, wide, sl), lambda k: (0, k, 0, 0))
    w_sds = jax.ShapeDtypeStruct((2, S5_PAIRS, wide, sl), BF16)
    a_sds = jax.ShapeDtypeStruct((2, S5_PAIRS, 1, sl), F32)
    return pl.pallas_call(
        _s5_prep_kernel,
        grid=(S5_PAIRS,),
        in_specs=[vec_spec] * 3 + [mat_spec] * 4,
        out_specs=[pl.BlockSpec((1, 2, wide // 2, wide // 2), lambda k: (k, 0, 0, 0))] + [w_spec] * 4 + [vec_spec] * 2,
        out_shape=[jax.ShapeDtypeStruct((S5_PAIRS, 2, wide // 2, wide // 2), BF16)] + [w_sds] * 4 + [a_sds] * 2,
        compiler_params=_cparams(1),
        name="s5_prep",
    )(*ins)


S5_STEP_PAIRS = 4
def _s5_state_rows(batch, n_lat, n_ctx):
    def pitch(n):
        p = -(-n // SUBLANES)
        return SUBLANES * (p + 1 - p % 2)

    lat_pitch, ctx_pitch = pitch(n_lat), pitch(n_ctx)
    ctx_base = batch * lat_pitch
    return lat_pitch, ctx_pitch, ctx_base, ctx_base + batch * ctx_pitch


def _s5_kernel(ul_ref, uc_ref, dsk_ref, *refs, batch, seq, ctx):
    toep_ref, wsr, wsi, wor, woi, atr, ati = refs[:7]
    o_ref = refs[7]
    wscr, zscr, upscr, sre, sim, hre, him = refs[8:]
    tc = S5_CHUNK
    n_lat, n_ctx = seq // tc, ctx // tc
    rows_lat = n_lat * batch
    lat_pitch, ctx_pitch, ctx_base, _ = _s5_state_rows(batch, n_lat, n_ctx)
    w = tc * S5_GROUP
    lanes = S5_STEP_PAIRS * 2 * S5_GROUP

    def scatter_tiles(xt, r, col0):
        for kk in range(S5_STEP_PAIRS):
            for j in range(2):
                ch = (2 * kk + j) * S5_GROUP
                wscr[kk, j * w + r * S5_GROUP:j * w + (r + 1) * S5_GROUP, col0:col0 + lanes] = xt[ch:ch + S5_GROUP, :]

    for r in range(tc):
        for b in range(batch):
            x = ul_ref[pl.ds(b * seq + r, n_lat, stride=tc), :]
            scatter_tiles(x.T, r, b * n_lat)
        xc = jnp.concatenate([uc_ref[pl.ds(b * ctx + r, n_ctx, stride=tc), :] for b in range(batch)]
                             + [jnp.zeros((lanes - batch * n_ctx, lanes), F32)], axis=0)
        scatter_tiles(xc.T, r, rows_lat)

    chains = [(kk, d) for kk in range(S5_STEP_PAIRS) for d in range(2)]
    for kk in range(S5_STEP_PAIRS):
        upscr[kk] = wscr[kk].T.astype(BF16)
        for d in range(2):
            for dst, wst in ((sre, wsr), (sim, wsi)):
                s = jnp.dot(upscr[kk], wst[d, kk], preferred_element_type=F32)
                for b in range(batch):
                    dst[kk, d, b * lat_pitch:b * lat_pitch + n_lat, :] = s[b * n_lat:(b + 1) * n_lat]
                    dst[kk, d, ctx_base + b * ctx_pitch:ctx_base + b * ctx_pitch + n_ctx, :] = (
                        s[rows_lat + b * n_ctx:rows_lat + (b + 1) * n_ctx])

    ctx_rows = [pl.ds(ctx_base + c, batch, stride=ctx_pitch) for c in range(n_ctx)]
    lat_rows = [pl.ds(c, batch, stride=lat_pitch) for c in range(n_lat)]
    order = (ctx_rows + lat_rows, ctx_rows[::-1] + lat_rows[::-1])
    coef = {(kk, d): (atr[d, kk], ati[d, kk]) for kk, d in chains}
    state = {ch: (jnp.zeros((batch, 2 * S5_STATE), F32), jnp.zeros((batch, 2 * S5_STATE), F32)) for ch in chains}
    for t in range(n_ctx + n_lat):
        for kk, d in chains:
            rows = order[d][t]
            (ar, ai), (h_re, h_im) = coef[kk, d], state[kk, d]
            hre[kk, d, rows, :] = h_re
            him[kk, d, rows, :] = h_im
            state[kk, d] = (ar * h_re - ai * h_im + sre[kk, d, rows, :],
                            ar * h_im + ai * h_re + sim[kk, d, rows, :])

    for kk in range(S5_STEP_PAIRS):
        ul = upscr[kk, :rows_lat, :]
        y = ul.astype(F32) * dsk_ref[kk]
        y = y + jnp.concatenate(
            [jnp.dot(ul[:, :w], toep_ref[kk, 0], preferred_element_type=F32),
             jnp.dot(ul[:, w:], toep_ref[kk, 1], preferred_element_type=F32)], axis=1)
        nt = (((1,), (1,)), ((), ()))
        for d in range(2):
            h_r = jnp.concatenate([hre[kk, d, b * lat_pitch:b * lat_pitch + n_lat, :] for b in range(batch)], axis=0)
            h_i = jnp.concatenate([him[kk, d, b * lat_pitch:b * lat_pitch + n_lat, :] for b in range(batch)], axis=0)
            y = y + lax.dot_general(h_r.astype(BF16), wor[d, kk], nt, preferred_element_type=F32)
            y = y + lax.dot_general(h_i.astype(BF16), woi[d, kk], nt, preferred_element_type=F32)
        yt = y.T
        for b in range(batch):
            for s in range(tc):
                for j in range(2):
                    ch = (2 * kk + j) * S5_GROUP
                    zscr[b * tc + s, ch:ch + S5_GROUP, :] = yt[j * w + s * S5_GROUP:j * w + (s + 1) * S5_GROUP,
                                                               b * n_lat:(b + 1) * n_lat]

    for b in range(batch):
        for s in range(tc):
            o_ref[pl.ds(b * seq + s, n_lat, stride=tc), :] = jax.nn.gelu(zscr[b * tc + s].T)


def _s5_mixer(u, params, batch, seq, ctx):
    tc = S5_CHUNK
    n_lat, n_ctx = seq // tc, ctx // tc
    sp = S5_STEP_PAIRS
    lanes = sp * 2 * S5_GROUP
    assert n_lat == lanes and batch * n_ctx <= lanes
    rows_lat = n_lat * batch
    rows_all = rows_lat + lanes
    state_rows = _s5_state_rows(batch, n_lat, n_ctx)[3]
    wide = 2 * tc * S5_GROUP
    dsk = jnp.broadcast_to(params['s5_d'].astype(F32).reshape(S5_PAIRS, 2, 1, S5_GROUP),
                           (S5_PAIRS, 2, tc, S5_GROUP)).reshape(S5_PAIRS, 1, wide)
    weights = _s5_prep(params)
    w_spec = pl.BlockSpec((2, sp, wide, 2 * S5_STATE), lambda k: (0, k, 0, 0))
    a_spec = pl.BlockSpec((2, sp, 1, 2 * S5_STATE), lambda k: (0, k, 0, 0))
    specs = [pl.BlockSpec((sp, 2, wide // 2, wide // 2), lambda k: (k, 0, 0, 0))] + [w_spec] * 4 + [a_spec] * 2
    return pl.pallas_call(
        functools.partial(_s5_kernel, batch=batch, seq=seq, ctx=ctx),
        grid=(S5_PAIRS // sp,),
        in_specs=[pl.BlockSpec((batch * seq, lanes), lambda k: (0, k)),
                  pl.BlockSpec((batch * ctx, lanes), lambda k: (seq // ctx, k)),
                  pl.BlockSpec((sp, 1, wide), lambda k: (k, 0, 0))] + specs,
        out_specs=pl.BlockSpec((batch * seq, lanes), lambda k: (0, k)),
        out_shape=jax.ShapeDtypeStruct((batch * seq, S5_WIDTH), F32),
        scratch_shapes=[pltpu.VMEM((sp, wide, rows_all), F32),
                        pltpu.VMEM((batch * tc, lanes, n_lat), F32),
                        pltpu.VMEM((sp, rows_all, wide), BF16)]
                       + [pltpu.VMEM((sp, 2, state_rows, 2 * S5_STATE), F32) for _ in range(4)],
        compiler_params=_cparams(1),
        name="s5",
    )(u, u, dsk, *weights)


def _merge_kernel(z5_ref, o_ref_in, gs_ref, gm_ref, wa_ref, wb_ref, wm_ref, out_ref, wa_s, wb_s, wm_s):
    @pl.when(pl.program_id(1) == 0)
    def _():
        wa_s[...] = wa_ref[...].astype(BF16)
        wb_s[...] = wb_ref[...].astype(BF16)
        wm_s[...] = wm_ref[...].astype(BF16)

    sub = min(MERGE_SUB_ROWS, out_ref.shape[0])
    for r0 in range(0, out_ref.shape[0], sub):
        rows = slice(r0, r0 + sub)
        z = z5_ref[rows, :].astype(BF16)
        a = jnp.dot(z, wa_s[...], preferred_element_type=F32)
        b = jnp.dot(z, wb_s[...], preferred_element_type=F32)
        mla = jnp.dot(o_ref_in[rows, :], wm_s[...], preferred_element_type=F32)
        merged = gs_ref[rows, :].astype(F32) * (a * jax.nn.sigmoid(b)) + gm_ref[rows, :].astype(F32) * mla
        out_ref[rows, :] = merged.astype(out_ref.dtype)


def _merge(y5, o_mla, gates, w_glu, w_mla_o, tm, tn):
    t = y5.shape[0]
    nj = D_MODEL // tn
    return pl.pallas_call(
        _merge_kernel,
        grid=(nj, t // tm),
        in_specs=[pl.BlockSpec((tm, S5_WIDTH), lambda j, i: (i, 0)),
                  pl.BlockSpec((tm, MLA_HEADS * V_DIM), lambda j, i: (i, 0)),
                  pl.BlockSpec((tm, tn), lambda j, i: (i, j)),
                  pl.BlockSpec((tm, tn), lambda j, i: (i, nj + j)),
                  pl.BlockSpec((S5_WIDTH, tn), lambda j, i: (0, j)),
                  pl.BlockSpec((S5_WIDTH, tn), lambda j, i: (0, nj + j)),
                  pl.BlockSpec((MLA_HEADS * V_DIM, tn), lambda j, i: (0, j))],
        out_specs=pl.BlockSpec((tm, tn), lambda j, i: (i, j)),
        out_shape=jax.ShapeDtypeStruct((t, D_MODEL), BF16),
        scratch_shapes=[pltpu.VMEM((S5_WIDTH, tn), BF16), pltpu.VMEM((S5_WIDTH, tn), BF16),
                        pltpu.VMEM((MLA_HEADS * V_DIM, tn), BF16)],
        compiler_params=_cparams(2),
        name="merge",
    )(y5, o_mla, gates, gates, w_glu, w_glu, w_mla_o)


def _out_proj_norm_kernel(a_ref, w_ref, x_ref, g1_ref, n2_ref, sc_ref, sh_ref, x1_ref, xn_ref, w_s):
    @pl.when(pl.program_id(0) == 0)
    def _():
        w_s[...] = w_ref[...].astype(BF16)

    sub = min(MERGE_SUB_ROWS, x1_ref.shape[0])
    for r0 in range(0, x1_ref.shape[0], sub):
        rows = slice(r0, r0 + sub)
        x1 = x_ref[rows, :] + g1_ref[0, 0] * jnp.dot(a_ref[rows, :], w_s[...], preferred_element_type=F32)
        x1_ref[rows, :] = x1
        xn_ref[rows, :] = (_rms(x1, n2_ref[...]) * (1.0 + sc_ref[0, 0]) + sh_ref[0, 0]).astype(xn_ref.dtype)


def _out_proj_norm(merged, w_out, x_res, m, gain2, rows_per_batch, tm):
    t, k = merged.shape
    tpb = rows_per_batch // tm
    row = pl.BlockSpec((tm, D_MODEL), lambda i: (i, 0))
    gate1, scale2, shift2 = (_mod_spec(D_MODEL, lambda i: i // tpb, which)
                             for which in (MOD_GATE1, MOD_SCALE2, MOD_SHIFT2))
    return pl.pallas_call(
        _out_proj_norm_kernel,
        grid=(t // tm,),
        in_specs=[pl.BlockSpec((tm, k), lambda i: (i, 0)),
                  pl.BlockSpec((k, D_MODEL), lambda i: (0, 0), pipeline_mode=pl.Buffered(1)),
                  row, gate1, pl.BlockSpec((1, D_MODEL), lambda i: (0, 0)), scale2, shift2],
        out_specs=[row, row],
        out_shape=[jax.ShapeDtypeStruct((t, D_MODEL), F32), jax.ShapeDtypeStruct((t, D_MODEL), BF16)],
        scratch_shapes=[pltpu.VMEM((k, D_MODEL), BF16)],
        compiler_params=_cparams(1),
        name="out_proj",
    )(merged, w_out, x_res, m, gain2.reshape(1, D_MODEL), m, m)


def _ffn_in(xn, w_ffn_in, tm, tn):
    def epi(accs, e_refs, o_refs, rows):
        o_refs[0][rows, :] = (jax.nn.silu(accs[0]) * accs[1]).astype(BF16)

    t = xn.shape[0]
    nj = D_FF // tn
    outs = [(jax.ShapeDtypeStruct((t, D_FF), BF16), pl.BlockSpec((tm, tn), lambda j, i: (i, j)))]
    return _fused_mm(xn, [(w_ffn_in, 0, "kn", tn), (w_ffn_in, nj, "kn", tn)], epi, [], outs,
                     tm=tm, nj=nj, name="ffn_in", sub_rows=MM_SUB_ROWS)[0]


def _ffn_out_kernel(h_ref, w_ref, x_ref, g2_ref, nf_ref, o_ref, w_s, *, last):
    @pl.when(pl.program_id(0) == 0)
    def _():
        w_s[...] = w_ref[...].astype(BF16)

    y = x_ref[...] + g2_ref[0, 0] * jnp.dot(h_ref[...], w_s[...], preferred_element_type=F32)
    o_ref[...] = _rms(y, nf_ref[...]) if last else y


def _ffn_out(hid, w_ffn_out, x_res, m, norm_f, rows_per_batch, tm):
    t, k = hid.shape
    kh = k // 2
    tpb = rows_per_batch // tm
    row = pl.BlockSpec((tm, D_MODEL), lambda i: (i, 0))
    y = x_res
    for half in range(2):
        y = pl.pallas_call(
            functools.partial(_ffn_out_kernel, last=half == 1),
            grid=(t // tm,),
            in_specs=[pl.BlockSpec((tm, kh), functools.partial(lambda i, half: (i, half), half=half)),
                      pl.BlockSpec((kh, D_MODEL), functools.partial(lambda i, half: (half, 0), half=half),
                                   pipeline_mode=pl.Buffered(1)),
                      row, _mod_spec(D_MODEL, lambda i: i // tpb, MOD_GATE2),
                      pl.BlockSpec((1, D_MODEL), lambda i: (0, 0))],
            out_specs=row,
            out_shape=jax.ShapeDtypeStruct((t, D_MODEL), F32),
            scratch_shapes=[pltpu.VMEM((kh, D_MODEL), BF16)],
            compiler_params=_cparams(1),
            name="ffn_out",
        )(hid, w_ffn_out, y, m, norm_f.reshape(1, D_MODEL))
    return y


def _rope_rot_cols(w):
    k = w.shape[0]
    ws = w.reshape(k, -1, 2, 2, QK_ROPE // 4)
    return jnp.stack([-ws[:, :, :, 1, :], ws[:, :, :, 0, :]], axis=3).reshape(k, -1)


def _rope_tables(n_tokens):
    rows = n_tokens // GRID_W
    row = jnp.repeat(jnp.arange(rows, dtype=F32), GRID_W)
    col = jnp.tile(jnp.arange(GRID_W, dtype=F32), rows)
    n_freq = QK_ROPE // 4
    inv = ROPE_BASE ** (-jnp.arange(n_freq, dtype=F32) / n_freq)
    ang = jnp.stack([row[:, None] * inv, col[:, None] * inv], axis=1)
    cos = jnp.broadcast_to(jnp.cos(ang)[:, :, None, :], (n_tokens, 2, 2, n_freq)).reshape(n_tokens, QK_ROPE)
    sin = jnp.broadcast_to(jnp.sin(ang)[:, :, None, :], (n_tokens, 2, 2, n_freq)).reshape(n_tokens, QK_ROPE)
    return cos, sin


def kernel(x, c, ctx, c_ctx, w_mod, b_mod, norm1, norm2, w_in, s5_a_re, s5_a_im, s5_log_dt, s5_b_re, s5_b_im,
           s5_c_re, s5_c_im, s5_d, w_glu, q_norm, kv_norm, w_uq, w_ukv, w_mla_o, w_out, w_ffn_in, w_ffn_out,
           norm_f):
    batch, seq, _ = x.shape
    n_ctx = ctx.shape[1]
    assert w_mod.shape[0] == 1, "single-layer block"
    p = dict(s5_a_re=s5_a_re[0], s5_a_im=s5_a_im[0], s5_log_dt=s5_log_dt[0], s5_b_re=s5_b_re[0],
             s5_b_im=s5_b_im[0], s5_c_re=s5_c_re[0], s5_c_im=s5_c_im[0], s5_d=s5_d[0])
    w_in_t = w_in.reshape(w_in.shape[1:]).T

    cv = jnp.concatenate([c, c_ctx[None], jnp.zeros((8 - batch - 1, D_MODEL), F32)], axis=0)
    m = _modulation(cv, w_mod[0], b_mod[0]).reshape(8, 6, 1, D_MODEL)

    lat_rows = batch * seq
    x2d = x.reshape(lat_rows, D_MODEL)
    c2d = ctx.reshape(batch * n_ctx, D_MODEL)
    xn = _norm_mod(x2d, c2d, norm1[0], m, seq, batch, 1024)

    kv_lo = S5_WIDTH + Q_RANK
    w_kr_rot_t = _rope_rot_cols(w_in_t[kv_lo + KV_RANK:kv_lo + KV_RANK + QK_ROPE].T).T
    wq = w_uq[0].reshape(Q_RANK, MLA_HEADS, QK_NOPE + QK_ROPE)
    wq_rope = wq[:, :, QK_NOPE:].reshape(Q_RANK, MLA_HEADS * QK_ROPE)
    wq2 = jnp.concatenate([wq[:, :, :QK_NOPE].reshape(Q_RANK, MLA_HEADS * QK_NOPE), wq_rope,
                           _rope_rot_cols(wq_rope)], axis=1).astype(BF16)
    w_ukv_bf = w_ukv[0].astype(BF16)
    tm = 1024
    cos, sin = _rope_tables(seq)
    cos_k = jnp.concatenate([cos, jnp.ones((tm, QK_ROPE), F32)], axis=0)
    sin_k = jnp.concatenate([sin, jnp.zeros((tm, QK_ROPE), F32)], axis=0)

    u = _proj_cast(xn, w_in_t, 0, S5_WIDTH, tm, 1024, "u_proj", out_dtype=F32)
    q = _q_path(xn, w_in_t, q_norm[0], wq2, jnp.tile(cos, (1, 2)), jnp.tile(sin, (1, 2)), tm, seq, lat_rows)
    kv, kr = _kv_path(xn, w_in_t, w_kr_rot_t, kv_norm[0], w_ukv_bf, cos_k, sin_k, tm, seq, lat_rows)
    gates = _proj_cast(xn, w_in_t, kv_lo + KV_RANK + QK_ROPE, 2 * D_MODEL, tm, 1024, "gates",
                       act=jax.nn.sigmoid, rows=lat_rows)

    z5 = _s5_mixer(u, p, batch, seq, n_ctx)
    o_mla = _attention(q, kv, kr, batch, seq, n_ctx, 2048)

    merged = _merge(z5, o_mla, gates, w_glu[0], w_mla_o[0], 512, 1024)
    x1, xn2 = _out_proj_norm(merged, w_out[0], x2d, m, norm2[0], seq, 512)
    hid = _ffn_in(xn2, w_ffn_in[0], 2048, 512)
    return _ffn_out(hid, w_ffn_out[0], x1, m, norm_f, seq, 256).reshape(batch, seq, D_MODEL)
```

```python
import functools
import math

import jax
import jax.numpy as jnp
from jax import lax
from jax.experimental import pallas as pl
from jax.experimental.pallas import tpu as pltpu

F32 = jnp.float32
BF16 = jnp.bfloat16

D_MODEL = 2048
GRID_W = 64
EPS = 1e-6
S5_WIDTH = D_MODEL // 2
S5_GROUP = 16
S5_GROUPS = S5_WIDTH // S5_GROUP
S5_STATE = 64
S5_CHUNK = 16
S5_PAIRS = S5_GROUPS // 2
MLA_HEADS = 8
QK_NOPE = 128
QK_ROPE = 64
V_DIM = 128
Q_RANK = 512
KV_RANK = 256
ROPE_BASE = 10000.0
ATTN_SCALE = (QK_NOPE + QK_ROPE) ** -0.5
D_FF = -(-8 * D_MODEL // (3 * 256)) * 256

VMEM_LIMIT_BYTES = 56 * 1024 * 1024
SUBLANES = 8
MM_SUB_ROWS = 512
MERGE_SUB_ROWS = 256


def _cparams(n_axes):
    return pltpu.CompilerParams(dimension_semantics=("arbitrary",) * n_axes,
                                vmem_limit_bytes=VMEM_LIMIT_BYTES)


def _rms(x, g):
    return x * lax.rsqrt(jnp.mean(x * x, axis=-1, keepdims=True) + EPS) * g


def _mod_kernel(cv_ref, w_ref, b_ref, o_ref):
    s = jax.nn.silu(cv_ref[...]).astype(BF16)
    o_ref[...] = jnp.dot(s, w_ref[...].astype(BF16), preferred_element_type=F32) + b_ref[...]


def _modulation(cv, w_mod, b_mod):
    n = w_mod.shape[1]
    tn = 1536
    return pl.pallas_call(
        _mod_kernel,
        grid=(n // tn,),
        in_specs=[pl.BlockSpec((8, D_MODEL), lambda j: (0, 0)),
                  pl.BlockSpec((D_MODEL, tn), lambda j: (0, j)),
                  pl.BlockSpec((1, tn), lambda j: (0, j))],
        out_specs=pl.BlockSpec((8, tn), lambda j: (0, j)),
        out_shape=jax.ShapeDtypeStruct((8, n), F32),
        compiler_params=_cparams(1),
        name="mod",
    )(cv, w_mod, b_mod.reshape(1, n))


MOD_SHIFT1, MOD_SCALE1, MOD_GATE1, MOD_SHIFT2, MOD_SCALE2, MOD_GATE2 = range(6)


def _mod_spec(width, row_of, which, col_of=None):
    col_of = col_of or (lambda *ids: 0)
    return pl.BlockSpec((1, 1, 1, width), lambda *ids: (row_of(*ids), which, 0, col_of(*ids)))


def _norm_mod_kernel(x_ref, c_ref, g_ref, sc_ref, sh_ref, o_ref, *, lat_tiles):
    def emit(src_ref):
        y = _rms(src_ref[...], g_ref[...])
        o_ref[...] = (y * (1.0 + sc_ref[0, 0]) + sh_ref[0, 0]).astype(o_ref.dtype)

    pl.when(pl.program_id(0) < lat_tiles)(lambda: emit(x_ref))
    pl.when(pl.program_id(0) >= lat_tiles)(lambda: emit(c_ref))


def _norm_mod(x2d, c2d, gain, m, rows_per_batch, ctx_row, tm):
    lat_tiles, ctx_tiles = x2d.shape[0] // tm, c2d.shape[0] // tm
    tpb = rows_per_batch // tm

    def row(i):
        return jnp.where(i < lat_tiles, i // tpb, ctx_row)

    return pl.pallas_call(
        functools.partial(_norm_mod_kernel, lat_tiles=lat_tiles),
        grid=(lat_tiles + ctx_tiles,),
        in_specs=[pl.BlockSpec((tm, D_MODEL), lambda i: (jnp.minimum(i, lat_tiles - 1), 0)),
                  pl.BlockSpec((tm, D_MODEL), lambda i: (jnp.maximum(i - lat_tiles, 0), 0)),
                  pl.BlockSpec((1, D_MODEL), lambda i: (0, 0)),
                  _mod_spec(D_MODEL, row, MOD_SCALE1), _mod_spec(D_MODEL, row, MOD_SHIFT1)],
        out_specs=pl.BlockSpec((tm, D_MODEL), lambda i: (i, 0)),
        out_shape=jax.ShapeDtypeStruct((x2d.shape[0] + c2d.shape[0], D_MODEL), BF16),
        compiler_params=_cparams(1),
        name="norm_mod",
    )(x2d, c2d, gain.reshape(1, D_MODEL), m, m)


def _fused_mm(a, weights, epilogue, extras, outs, *, tm, nj, name, rows=None, sub_rows=None):
    t, k = a.shape
    ni = (rows or t) // tm
    nw, ne, no = len(weights), len(extras), len(outs)
    need_cast = [w.dtype != BF16 for w, _, _, _ in weights]
    nt = (((1,), (1,)), ((), ()))

    def kernel(*refs):
        a_ref = refs[0]
        w_refs = refs[1:1 + nw]
        e_refs = refs[1 + nw:1 + nw + ne]
        o_refs = refs[1 + nw + ne:1 + nw + ne + no]
        s_refs = list(refs[1 + nw + ne + no:])
        staged = {idx: s_refs.pop(0) for idx in range(nw) if need_cast[idx]}
        if staged:
            @pl.when(pl.program_id(1) == 0)
            def _():
                for idx, s_ref in staged.items():
                    s_ref[...] = w_refs[idx][...].astype(BF16)

        for r0 in range(0, tm, sub_rows or tm):
            rows = slice(r0, r0 + (sub_rows or tm))
            av = a_ref[rows, :]
            accs = []
            for idx in range(nw):
                w_ref = staged.get(idx, w_refs[idx])
                if weights[idx][2] == "kn":
                    accs.append(jnp.dot(av, w_ref[...], preferred_element_type=F32))
                else:
                    accs.append(lax.dot_general(av, w_ref[...], nt, preferred_element_type=F32))
            epilogue(accs, e_refs, o_refs, rows)

    in_specs = [pl.BlockSpec((tm, k), lambda j, i: (i, 0))]
    scratch = []
    for (w, off, layout, width), cast in zip(weights, need_cast):
        if layout == "kn":
            shape = (k, width)
            in_specs.append(pl.BlockSpec(shape, functools.partial(lambda j, i, off: (0, off + j), off=off)))
        else:
            shape = (width, k)
            in_specs.append(pl.BlockSpec(
                (pl.Element(width), pl.Element(k)),
                functools.partial(lambda j, i, off, width: (pl.multiple_of(off + j * width, SUBLANES), 0),
                                  off=off, width=width)))
        if cast:
            scratch.append(pltpu.VMEM(shape, BF16))
    in_specs += [spec for _, spec in extras]
    return pl.pallas_call(
        kernel,
        grid=(nj, ni),
        in_specs=in_specs,
        out_specs=[spec for _, spec in outs],
        out_shape=[sds for sds, _ in outs],
        scratch_shapes=scratch,
        compiler_params=_cparams(2),
        name=name,
    )(a, *[w[0] for w in weights], *[e for e, _ in extras])


def _proj_cast(xn, w_in_t, row0, n, tm, tn, name, act=None, out_dtype=BF16, rows=None):
    def epi(accs, e_refs, o_refs, rows):
        v = accs[0]
        if act is not None:
            v = act(v)
        o_refs[0][rows, :] = v.astype(out_dtype)

    t = rows or xn.shape[0]
    return _fused_mm(xn, [(w_in_t, row0, "nk", tn)], epi, [],
                     [(jax.ShapeDtypeStruct((t, n), out_dtype), pl.BlockSpec((tm, tn), lambda j, i: (i, j)))],
                     tm=tm, nj=n // tn, name=name, rows=rows)[0]


def _q_path(xn, w_in_t, q_norm, wq2, cos2, sin2, tm, seq, rows):
    nr = MLA_HEADS * QK_ROPE
    nn = MLA_HEADS * QK_NOPE
    lanes = 2 * QK_ROPE

    def epi(accs, e_refs, o_refs, rows):
        qn_ref, w2_ref, cos_ref, sin_ref = e_refs
        cq = _rms(accs[0], qn_ref[...]).astype(BF16)
        q = jnp.dot(cq, w2_ref[...], preferred_element_type=F32)
        o_refs[0][rows, :nn] = (q[:, :nn] * ATTN_SCALE).astype(BF16)
        cos, sin = cos_ref[rows, :] * ATTN_SCALE, sin_ref[rows, :] * ATTN_SCALE
        for c0 in range(0, nr, lanes):
            rope = q[:, nn + c0:nn + c0 + lanes] * cos + q[:, nn + nr + c0:nn + nr + c0 + lanes] * sin
            o_refs[0][rows, nn + c0:nn + c0 + lanes] = rope.astype(BF16)

    pos_tiles = seq // tm
    extras = [(q_norm.reshape(1, Q_RANK), pl.BlockSpec((1, Q_RANK), lambda j, i: (0, 0))),
              (wq2, pl.BlockSpec(wq2.shape, lambda j, i: (0, 0))),
              (cos2, pl.BlockSpec((tm, lanes), lambda j, i: (i % pos_tiles, 0))),
              (sin2, pl.BlockSpec((tm, lanes), lambda j, i: (i % pos_tiles, 0)))]
    outs = [(jax.ShapeDtypeStruct((rows, nn + nr), BF16), pl.BlockSpec((tm, nn + nr), lambda j, i: (i, 0)))]
    return _fused_mm(xn, [(w_in_t, S5_WIDTH, "nk", Q_RANK)], epi, extras, outs, tm=tm, nj=1, name="q_path",
                     rows=rows)[0]


def _kv_path(xn, w_kv_t, kv_norm, w_ukv_bf, cos_sin_k, tm, seq, lat_rows):
    nkv = w_ukv_bf.shape[1]

    def epi(accs, e_refs, o_refs, rows):
        acc = accs[0]
        ckv = _rms(acc[:, :KV_RANK], e_refs[0][...]).astype(BF16)
        o_refs[0][rows, :] = jnp.dot(ckv, e_refs[1][...], preferred_element_type=F32).astype(BF16)
        prod = acc[:, KV_RANK:] * e_refs[2][rows, :]
        o_refs[1][rows, :] = (prod + pltpu.roll(prod, QK_ROPE, 1))[:, :QK_ROPE].astype(BF16)

    t = xn.shape[0]
    pos_tiles, lat_tiles = seq // tm, lat_rows // tm
    extras = [(kv_norm.reshape(1, KV_RANK), pl.BlockSpec((1, KV_RANK), lambda j, i: (0, 0))),
              (w_ukv_bf, pl.BlockSpec(w_ukv_bf.shape, lambda j, i: (0, 0))),
              (cos_sin_k, pl.BlockSpec((tm, 2 * QK_ROPE),
                                       lambda j, i: (jnp.where(i < lat_tiles, i % pos_tiles, pos_tiles), 0)))]
    outs = [(jax.ShapeDtypeStruct((t, nkv), BF16), pl.BlockSpec((tm, nkv), lambda j, i: (i, 0))),
            (jax.ShapeDtypeStruct((t, QK_ROPE), BF16), pl.BlockSpec((tm, QK_ROPE), lambda j, i: (i, 0)))]
    return _fused_mm(xn, [(w_kv_t, 0, "nk", w_kv_t.shape[0])], epi, extras, outs, tm=tm, nj=1, name="kv_path")


ATTN_SUB_ROWS = 256


def _attn_kernel(qn_ref, qr_ref, kvl_ref, kvc_ref, krl_ref, krc_ref, o_ref, k_scr, v_scr, *, seq, ctx):
    dk = QK_NOPE + QK_ROPE

    @pl.when(pl.program_id(2) == 0)
    def _():
        for h in range(2):
            base = h * (QK_NOPE + V_DIM)
            k_scr[h, :seq, :QK_NOPE] = kvl_ref[:, base:base + QK_NOPE]
            k_scr[h, seq:, :QK_NOPE] = kvc_ref[:, base:base + QK_NOPE]
            k_scr[h, :seq, QK_NOPE:dk] = krl_ref[...]
            k_scr[h, seq:, QK_NOPE:dk] = krc_ref[...]
            v_scr[h, :seq, :V_DIM] = kvl_ref[:, base + QK_NOPE:base + QK_NOPE + V_DIM]
            v_scr[h, seq:, :V_DIM] = kvc_ref[:, base + QK_NOPE:base + QK_NOPE + V_DIM]
            ones_col = lax.broadcasted_iota(jnp.int32, (seq + ctx, V_DIM), 1) == 0
            v_scr[h, :, V_DIM:] = jnp.where(ones_col, 1.0, 0.0).astype(BF16)

    for r0 in range(0, qn_ref.shape[0], ATTN_SUB_ROWS):
        rows = slice(r0, r0 + ATTN_SUB_ROWS)
        for h in range(2):
            q = jnp.concatenate([qn_ref[rows, h * QK_NOPE:(h + 1) * QK_NOPE],
                                 qr_ref[rows, h * QK_ROPE:(h + 1) * QK_ROPE]], axis=1)
            s = lax.dot_general(q, k_scr[h], (((1,), (1,)), ((), ())), preferred_element_type=F32)
            m = jnp.max(s, axis=-1, keepdims=True)
            p = jnp.exp((s - m).astype(BF16))
            ol = jnp.dot(p, v_scr[h], preferred_element_type=F32)
            o_ref[rows, h * V_DIM:(h + 1) * V_DIM] = (ol[:, :V_DIM] / ol[:, V_DIM:V_DIM + 1]).astype(o_ref.dtype)


def _attention(q, kv, kr, batch, seq, ctx, tq):
    nq = seq // tq
    ctx0 = batch * seq // ctx
    nn_blocks = MLA_HEADS * QK_NOPE // (2 * QK_NOPE)
    dk = QK_NOPE + QK_ROPE
    hw = 2 * (QK_NOPE + V_DIM)
    return pl.pallas_call(
        functools.partial(_attn_kernel, seq=seq, ctx=ctx),
        grid=(batch, MLA_HEADS // 2, nq),
        in_specs=[pl.BlockSpec((tq, 2 * QK_NOPE), lambda b, hp, qi: (b * nq + qi, hp)),
                  pl.BlockSpec((tq, 2 * QK_ROPE), lambda b, hp, qi: (b * nq + qi, 2 * nn_blocks + hp)),
                  pl.BlockSpec((seq, hw), lambda b, hp, qi: (b, hp)),
                  pl.BlockSpec((ctx, hw), lambda b, hp, qi: (ctx0 + b, hp)),
                  pl.BlockSpec((seq, QK_ROPE), lambda b, hp, qi: (b, 0)),
                  pl.BlockSpec((ctx, QK_ROPE), lambda b, hp, qi: (ctx0 + b, 0))],
        out_specs=pl.BlockSpec((tq, 2 * V_DIM), lambda b, hp, qi: (b * nq + qi, hp)),
        out_shape=jax.ShapeDtypeStruct((batch * seq, MLA_HEADS * V_DIM), BF16),
        scratch_shapes=[pltpu.VMEM((2, seq + ctx, dk), BF16), pltpu.VMEM((2, seq + ctx, 2 * V_DIM), BF16)],
        compiler_params=_cparams(3),
        name="attention",
    )(q, q, kv, kv, kr, kr)


def _s5_prep_pair(pp, are_ref, aim_ref, ldt_ref, bre_ref, bim_ref, cre_ref, cim_ref,
                  toep_ref, wsr_ref, wsi_ref, wor_ref, woi_ref, atr_ref, ati_ref):
    tc, g = S5_CHUNK, S5_GROUP
    w = tc * g
    lane = lax.broadcasted_iota(jnp.int32, (g, 2 * S5_STATE), 1)
    in_group = (lane < S5_STATE, lane >= S5_STATE)
    lane_w = lax.broadcasted_iota(jnp.int32, (g, w), 1)
    nt = (((1,), (1,)), ((), ()))
    toep_rows = [[jnp.zeros((g, w), F32) for _ in range(tc)] for _ in range(2)]
    for d in range(2):
        lr, li = are_ref[d, pp], aim_ref[d, pp]
        dt = jnp.exp(ldt_ref[d, pp])
        mag = jnp.exp(lr * dt)
        ab_re, ab_im = mag * jnp.cos(li * dt), mag * jnp.sin(li * dt)
        den = lr * lr + li * li
        nr, ni = ab_re - 1.0, ab_im
        co_re = (nr * lr + ni * li) / den
        co_im = (ni * lr - nr * li) / den
        br, bi = bre_ref[d, pp], bim_ref[d, pp]
        bb_re = co_re * br - co_im * bi
        bb_im = co_re * bi + co_im * br
        pw = [(jnp.ones_like(ab_re), jnp.zeros_like(ab_re))]
        for _ in range(tc):
            pr, pi = pw[-1]
            pw.append((pr * ab_re - pi * ab_im, pr * ab_im + pi * ab_re))
        cr, ci = cre_ref[d, pp], cim_ref[d, pp]
        ca = [(cr * pr - ci * pi, cr * pi + ci * pr) for pr, pi in pw]
        taus = list(range(tc))[::-1] if d else list(range(tc))
        y_re = jnp.concatenate([ca[t][0] for t in taus], axis=0).astype(BF16)
        y_im = jnp.concatenate([ca[t][1] for t in taus], axis=0).astype(BF16)
        for j in range(2):
            x_re = jnp.where(in_group[j], bb_re, 0.0)
            x_im = jnp.where(in_group[j], bb_im, 0.0)
            kt = (lax.dot_general(x_re.astype(BF16), y_re, nt, preferred_element_type=F32)
                  - lax.dot_general(x_im.astype(BF16), y_im, nt, preferred_element_type=F32))
            for r in range(tc):
                sh = (r + 1) * g if d else r * g
                blk = pltpu.roll(kt, sh % w, 1) if sh % w else kt
                keep = (lane_w < sh) if d else (lane_w >= sh)
                toep_rows[j][r] = toep_rows[j][r] + jnp.where(keep, blk, 0.0)
        for r in range(tc):
            pr, pi = pw[r] if d else pw[tc - 1 - r]
            w_re = bb_re * pr - bb_im * pi
            w_im = bb_re * pi + bb_im * pr
            car, cai = ca[tc - r] if d else ca[r + 1]
            for j in range(2):
                rows = slice(j * w + r * g, j * w + (r + 1) * g)
                wsr_ref[d, pp, rows, :] = jnp.where(in_group[j], w_re, 0.0).astype(BF16)
                wsi_ref[d, pp, rows, :] = jnp.where(in_group[j], w_im, 0.0).astype(BF16)
                wor_ref[d, pp, rows, :] = jnp.where(in_group[j], car, 0.0).astype(BF16)
                woi_ref[d, pp, rows, :] = jnp.where(in_group[j], -cai, 0.0).astype(BF16)
        atr_ref[d, pp] = pw[tc][0]
        ati_ref[d, pp] = pw[tc][1]
    for j in range(2):
        toep_ref[pp, j] = jnp.concatenate(toep_rows[j], axis=0).astype(BF16)


S5_PREP_PAIRS = 4


def _s5_prep_kernel(*refs):
    for pp in range(S5_PREP_PAIRS):
        _s5_prep_pair(pp, *refs)


def _s5_prep(params):
    tc = S5_CHUNK
    wide = 2 * tc * S5_GROUP
    sl = 2 * S5_STATE

    def pair_lanes(v):
        return v.astype(F32).reshape(2, S5_PAIRS, 1, sl)

    def pair_rows(v):
        rows = v.shape[2]
        return v.astype(F32).reshape(2, S5_PAIRS, 2, rows, S5_STATE).transpose(0, 1, 3, 2, 4).reshape(
            2, S5_PAIRS, rows, sl)

    ldt = jnp.broadcast_to(params['s5_log_dt'].astype(F32)[:, :, None], (2, S5_GROUPS, S5_STATE))
    ins = [pair_lanes(params['s5_a_re']), pair_lanes(params['s5_a_im']), pair_lanes(ldt),
           pair_rows(params['s5_b_re'].transpose(0, 1, 3, 2)), pair_rows(params['s5_b_im'].transpose(0, 1, 3, 2)),
           pair_rows(params['s5_c_re']), pair_rows(params['s5_c_im'])]
    pp = S5_PREP_PAIRS
    vec_spec = pl.BlockSpec((2, pp, 1, sl), lambda k: (0, k, 0, 0))
    mat_spec = pl.BlockSpec((2, pp, S5_GROUP, sl), lambda k: (0, k, 0, 0))
    w_spec = pl.BlockSpec((2, pp, wide, sl), lambda k: (0, k, 0, 0))
    w_sds = jax.ShapeDtypeStruct((2, S5_PAIRS, wide, sl), BF16)
    a_sds = jax.ShapeDtypeStruct((2, S5_PAIRS, 1, sl), F32)
    return pl.pallas_call(
        _s5_prep_kernel,
        grid=(S5_PAIRS // pp,),
        in_specs=[vec_spec] * 3 + [mat_spec] * 4,
        out_specs=[pl.BlockSpec((pp, 2, wide // 2, wide // 2), lambda k: (k, 0, 0, 0))] + [w_spec] * 4 + [vec_spec] * 2,
        out_shape=[jax.ShapeDtypeStruct((S5_PAIRS, 2, wide // 2, wide // 2), BF16)] + [w_sds] * 4 + [a_sds] * 2,
        compiler_params=_cparams(1),
        name="s5_prep",
    )(*ins)


S5_STEP_PAIRS = 4
def _s5_state_rows(batch, n_lat, n_ctx):
    def pitch(n):
        p = -(-n // SUBLANES)
        return SUBLANES * (p + 1 - p % 2)

    lat_pitch, ctx_pitch = pitch(n_lat), pitch(n_ctx)
    ctx_base = batch * lat_pitch
    return lat_pitch, ctx_pitch, ctx_base, ctx_base + batch * ctx_pitch


def _s5_kernel(ul_ref, uc_ref, dsk_ref, *refs, batch, seq, ctx):
    toep_ref, wsr, wsi, wor, woi, atr, ati = refs[:7]
    o_ref = refs[7]
    wscr, zscr, upscr, sre, sim, hre, him = refs[8:]
    tc = S5_CHUNK
    n_lat, n_ctx = seq // tc, ctx // tc
    rows_lat = n_lat * batch
    lat_pitch, ctx_pitch, ctx_base, _ = _s5_state_rows(batch, n_lat, n_ctx)
    w = tc * S5_GROUP
    lanes = S5_STEP_PAIRS * 2 * S5_GROUP

    def scatter_tiles(xt, r, col0):
        for kk in range(S5_STEP_PAIRS):
            for j in range(2):
                ch = (2 * kk + j) * S5_GROUP
                wscr[kk, j * w + r * S5_GROUP:j * w + (r + 1) * S5_GROUP, col0:col0 + lanes] = xt[ch:ch + S5_GROUP, :]

    for r in range(tc):
        for b in range(batch):
            x = ul_ref[pl.ds(b * seq + r, n_lat, stride=tc), :]
            scatter_tiles(x.T, r, b * n_lat)
        xc = jnp.concatenate([uc_ref[pl.ds(b * ctx + r, n_ctx, stride=tc), :] for b in range(batch)]
                             + [jnp.zeros((lanes - batch * n_ctx, lanes), F32)], axis=0)
        scatter_tiles(xc.T, r, rows_lat)

    chains = [(kk, d) for kk in range(S5_STEP_PAIRS) for d in range(2)]
    for kk in range(S5_STEP_PAIRS):
        upscr[kk] = wscr[kk].T.astype(BF16)
        for d in range(2):
            for dst, wst in ((sre, wsr), (sim, wsi)):
                s = jnp.dot(upscr[kk], wst[d, kk], preferred_element_type=F32)
                for b in range(batch):
                    dst[kk, d, b * lat_pitch:b * lat_pitch + n_lat, :] = s[b * n_lat:(b + 1) * n_lat]
                    dst[kk, d, ctx_base + b * ctx_pitch:ctx_base + b * ctx_pitch + n_ctx, :] = (
                        s[rows_lat + b * n_ctx:rows_lat + (b + 1) * n_ctx])

    ctx_rows = [pl.ds(ctx_base + c, batch, stride=ctx_pitch) for c in range(n_ctx)]
    lat_rows = [pl.ds(c, batch, stride=lat_pitch) for c in range(n_lat)]
    order = (ctx_rows + lat_rows, ctx_rows[::-1] + lat_rows[::-1])
    coef = {(kk, d): (atr[d, kk], ati[d, kk]) for kk, d in chains}
    state = {ch: (jnp.zeros((batch, 2 * S5_STATE), F32), jnp.zeros((batch, 2 * S5_STATE), F32)) for ch in chains}
    for t in range(n_ctx + n_lat):
        for kk, d in chains:
            rows = order[d][t]
            (ar, ai), (h_re, h_im) = coef[kk, d], state[kk, d]
            hre[kk, d, rows, :] = h_re
            him[kk, d, rows, :] = h_im
            state[kk, d] = (ar * h_re - ai * h_im + sre[kk, d, rows, :],
                            ar * h_im + ai * h_re + sim[kk, d, rows, :])

    for kk in range(S5_STEP_PAIRS):
        ul = upscr[kk, :rows_lat, :]
        y = ul.astype(F32) * dsk_ref[kk]
        y = y + jnp.concatenate(
            [jnp.dot(ul[:, :w], toep_ref[kk, 0], preferred_element_type=F32),
             jnp.dot(ul[:, w:], toep_ref[kk, 1], preferred_element_type=F32)], axis=1)
        nt = (((1,), (1,)), ((), ()))
        for d in range(2):
            h_r = jnp.concatenate([hre[kk, d, b * lat_pitch:b * lat_pitch + n_lat, :] for b in range(batch)], axis=0)
            h_i = jnp.concatenate([him[kk, d, b * lat_pitch:b * lat_pitch + n_lat, :] for b in range(batch)], axis=0)
            y = y + lax.dot_general(h_r.astype(BF16), wor[d, kk], nt, preferred_element_type=F32)
            y = y + lax.dot_general(h_i.astype(BF16), woi[d, kk], nt, preferred_element_type=F32)
        yt = y.T
        for b in range(batch):
            for s in range(tc):
                for j in range(2):
                    ch = (2 * kk + j) * S5_GROUP
                    zscr[b * tc + s, ch:ch + S5_GROUP, :] = yt[j * w + s * S5_GROUP:j * w + (s + 1) * S5_GROUP,
                                                               b * n_lat:(b + 1) * n_lat]

    for b in range(batch):
        for s in range(tc):
            o_ref[pl.ds(b * seq + s, n_lat, stride=tc), :] = jax.nn.gelu(zscr[b * tc + s].T)


def _s5_mixer(u, params, batch, seq, ctx):
    tc = S5_CHUNK
    n_lat, n_ctx = seq // tc, ctx // tc
    sp = S5_STEP_PAIRS
    lanes = sp * 2 * S5_GROUP
    assert n_lat == lanes and batch * n_ctx <= lanes
    rows_lat = n_lat * batch
    rows_all = rows_lat + lanes
    state_rows = _s5_state_rows(batch, n_lat, n_ctx)[3]
    wide = 2 * tc * S5_GROUP
    dsk = jnp.broadcast_to(params['s5_d'].astype(F32).reshape(S5_PAIRS, 2, 1, S5_GROUP),
                           (S5_PAIRS, 2, tc, S5_GROUP)).reshape(S5_PAIRS, 1, wide)
    weights = _s5_prep(params)
    w_spec = pl.BlockSpec((2, sp, wide, 2 * S5_STATE), lambda k: (0, k, 0, 0))
    a_spec = pl.BlockSpec((2, sp, 1, 2 * S5_STATE), lambda k: (0, k, 0, 0))
    specs = [pl.BlockSpec((sp, 2, wide // 2, wide // 2), lambda k: (k, 0, 0, 0))] + [w_spec] * 4 + [a_spec] * 2
    return pl.pallas_call(
        functools.partial(_s5_kernel, batch=batch, seq=seq, ctx=ctx),
        grid=(S5_PAIRS // sp,),
        in_specs=[pl.BlockSpec((batch * seq, lanes), lambda k: (0, k)),
                  pl.BlockSpec((batch * ctx, lanes), lambda k: (seq // ctx, k)),
                  pl.BlockSpec((sp, 1, wide), lambda k: (k, 0, 0))] + specs,
        out_specs=pl.BlockSpec((batch * seq, lanes), lambda k: (0, k)),
        out_shape=jax.ShapeDtypeStruct((batch * seq, S5_WIDTH), F32),
        scratch_shapes=[pltpu.VMEM((sp, wide, rows_all), F32),
                        pltpu.VMEM((batch * tc, lanes, n_lat), F32),
                        pltpu.VMEM((sp, rows_all, wide), BF16)]
                       + [pltpu.VMEM((sp, 2, state_rows, 2 * S5_STATE), F32) for _ in range(4)],
        compiler_params=_cparams(1),
        name="s5",
    )(u, u, dsk, *weights)


def _merge_kernel(z5_ref, o_ref_in, gs_ref, gm_ref, wa_ref, wb_ref, wm_ref, out_ref, wa_s, wb_s, wm_s):
    @pl.when(pl.program_id(1) == 0)
    def _():
        wa_s[...] = wa_ref[...].astype(BF16)
        wb_s[...] = wb_ref[...].astype(BF16)
        wm_s[...] = wm_ref[...].astype(BF16)

    sub = min(MERGE_SUB_ROWS, out_ref.shape[0])
    for r0 in range(0, out_ref.shape[0], sub):
        rows = slice(r0, r0 + sub)
        z = z5_ref[rows, :].astype(BF16)
        a = jnp.dot(z, wa_s[...], preferred_element_type=F32)
        b = jnp.dot(z, wb_s[...], preferred_element_type=F32)
        mla = jnp.dot(o_ref_in[rows, :], wm_s[...], preferred_element_type=F32)
        merged = gs_ref[rows, :].astype(F32) * (a * jax.nn.sigmoid(b)) + gm_ref[rows, :].astype(F32) * mla
        out_ref[rows, :] = merged.astype(out_ref.dtype)


def _merge(y5, o_mla, gates, w_glu, w_mla_o, tm, tn):
    t = y5.shape[0]
    nj = D_MODEL // tn
    return pl.pallas_call(
        _merge_kernel,
        grid=(nj, t // tm),
        in_specs=[pl.BlockSpec((tm, S5_WIDTH), lambda j, i: (i, 0)),
                  pl.BlockSpec((tm, MLA_HEADS * V_DIM), lambda j, i: (i, 0)),
                  pl.BlockSpec((tm, tn), lambda j, i: (i, j)),
                  pl.BlockSpec((tm, tn), lambda j, i: (i, nj + j)),
                  pl.BlockSpec((S5_WIDTH, tn), lambda j, i: (0, j)),
                  pl.BlockSpec((S5_WIDTH, tn), lambda j, i: (0, nj + j)),
                  pl.BlockSpec((MLA_HEADS * V_DIM, tn), lambda j, i: (0, j))],
        out_specs=pl.BlockSpec((tm, tn), lambda j, i: (i, j)),
        out_shape=jax.ShapeDtypeStruct((t, D_MODEL), BF16),
        scratch_shapes=[pltpu.VMEM((S5_WIDTH, tn), BF16), pltpu.VMEM((S5_WIDTH, tn), BF16),
                        pltpu.VMEM((MLA_HEADS * V_DIM, tn), BF16)],
        compiler_params=_cparams(2),
        name="merge",
    )(y5, o_mla, gates, gates, w_glu, w_glu, w_mla_o)


def _out_proj_norm_kernel(a_ref, w_ref, x_ref, g1_ref, n2_ref, sc_ref, sh_ref, x1_ref, xn_ref, w_s):
    @pl.when(pl.program_id(0) == 0)
    def _():
        w_s[...] = w_ref[...].astype(BF16)

    sub = min(MERGE_SUB_ROWS, x1_ref.shape[0])
    for r0 in range(0, x1_ref.shape[0], sub):
        rows = slice(r0, r0 + sub)
        x1 = x_ref[rows, :] + g1_ref[0, 0] * jnp.dot(a_ref[rows, :], w_s[...], preferred_element_type=F32)
        x1_ref[rows, :] = x1
        xn_ref[rows, :] = (_rms(x1, n2_ref[...]) * (1.0 + sc_ref[0, 0]) + sh_ref[0, 0]).astype(xn_ref.dtype)


def _out_proj_norm(merged, w_out, x_res, m, gain2, rows_per_batch, tm):
    t, k = merged.shape
    tpb = rows_per_batch // tm
    row = pl.BlockSpec((tm, D_MODEL), lambda i: (i, 0))
    gate1, scale2, shift2 = (_mod_spec(D_MODEL, lambda i: i // tpb, which)
                             for which in (MOD_GATE1, MOD_SCALE2, MOD_SHIFT2))
    return pl.pallas_call(
        _out_proj_norm_kernel,
        grid=(t // tm,),
        in_specs=[pl.BlockSpec((tm, k), lambda i: (i, 0)),
                  pl.BlockSpec((k, D_MODEL), lambda i: (0, 0), pipeline_mode=pl.Buffered(1)),
                  row, gate1, pl.BlockSpec((1, D_MODEL), lambda i: (0, 0)), scale2, shift2],
        out_specs=[row, row],
        out_shape=[jax.ShapeDtypeStruct((t, D_MODEL), F32), jax.ShapeDtypeStruct((t, D_MODEL), BF16)],
        scratch_shapes=[pltpu.VMEM((k, D_MODEL), BF16)],
        compiler_params=_cparams(1),
        name="out_proj",
    )(merged, w_out, x_res, m, gain2.reshape(1, D_MODEL), m, m)


def _ffn_in(xn, w_ffn_in, tm, tn):
    def epi(accs, e_refs, o_refs, rows):
        o_refs[0][rows, :] = (jax.nn.silu(accs[0]) * accs[1]).astype(BF16)

    t = xn.shape[0]
    nj = D_FF // tn
    outs = [(jax.ShapeDtypeStruct((t, D_FF), BF16), pl.BlockSpec((tm, tn), lambda j, i: (i, j)))]
    return _fused_mm(xn, [(w_ffn_in, 0, "kn", tn), (w_ffn_in, nj, "kn", tn)], epi, [], outs,
                     tm=tm, nj=nj, name="ffn_in", sub_rows=MM_SUB_ROWS)[0]


def _ffn_out_kernel(h_ref, w_ref, x_ref, g2_ref, nf_ref, o_ref, w_s, *, last):
    @pl.when(pl.program_id(0) == 0)
    def _():
        w_s[...] = w_ref[...].astype(BF16)

    y = x_ref[...] + g2_ref[0, 0] * jnp.dot(h_ref[...], w_s[...], preferred_element_type=F32)
    o_ref[...] = _rms(y, nf_ref[...]) if last else y


def _ffn_out(hid, w_ffn_out, x_res, m, norm_f, rows_per_batch, tm):
    t, k = hid.shape
    kh = k // 2
    tpb = rows_per_batch // tm
    row = pl.BlockSpec((tm, D_MODEL), lambda i: (i, 0))
    y = x_res
    for half in range(2):
        y = pl.pallas_call(
            functools.partial(_ffn_out_kernel, last=half == 1),
            grid=(t // tm,),
            in_specs=[pl.BlockSpec((tm, kh), functools.partial(lambda i, half: (i, half), half=half)),
                      pl.BlockSpec((kh, D_MODEL), functools.partial(lambda i, half: (half, 0), half=half),
                                   pipeline_mode=pl.Buffered(1)),
                      row, _mod_spec(D_MODEL, lambda i: i // tpb, MOD_GATE2),
                      pl.BlockSpec((1, D_MODEL), lambda i: (0, 0))],
            out_specs=row,
            out_shape=jax.ShapeDtypeStruct((t, D_MODEL), F32),
            scratch_shapes=[pltpu.VMEM((kh, D_MODEL), BF16)],
            compiler_params=_cparams(1),
            name="ffn_out",
        )(hid, w_ffn_out, y, m, norm_f.reshape(1, D_MODEL))
    return y


def _rope_rot_cols(w):
    k = w.shape[0]
    ws = w.reshape(k, -1, 2, 2, QK_ROPE // 4)
    return jnp.stack([-ws[:, :, :, 1, :], ws[:, :, :, 0, :]], axis=3).reshape(k, -1)


def _rope_tables(n_tokens):
    rows = n_tokens // GRID_W
    row = jnp.repeat(jnp.arange(rows, dtype=F32), GRID_W)
    col = jnp.tile(jnp.arange(GRID_W, dtype=F32), rows)
    n_freq = QK_ROPE // 4
    inv = ROPE_BASE ** (-jnp.arange(n_freq, dtype=F32) / n_freq)
    ang = jnp.stack([row[:, None] * inv, col[:, None] * inv], axis=1)
    cos = jnp.broadcast_to(jnp.cos(ang)[:, :, None, :], (n_tokens, 2, 2, n_freq)).reshape(n_tokens, QK_ROPE)
    sin = jnp.broadcast_to(jnp.sin(ang)[:, :, None, :], (n_tokens, 2, 2, n_freq)).reshape(n_tokens, QK_ROPE)
    return cos, sin


def kernel(x, c, ctx, c_ctx, w_mod, b_mod, norm1, norm2, w_in, s5_a_re, s5_a_im, s5_log_dt, s5_b_re, s5_b_im,
           s5_c_re, s5_c_im, s5_d, w_glu, q_norm, kv_norm, w_uq, w_ukv, w_mla_o, w_out, w_ffn_in, w_ffn_out,
           norm_f):
    batch, seq, _ = x.shape
    n_ctx = ctx.shape[1]
    assert w_mod.shape[0] == 1, "single-layer block"
    p = dict(s5_a_re=s5_a_re[0], s5_a_im=s5_a_im[0], s5_log_dt=s5_log_dt[0], s5_b_re=s5_b_re[0],
             s5_b_im=s5_b_im[0], s5_c_re=s5_c_re[0], s5_c_im=s5_c_im[0], s5_d=s5_d[0])
    w_in_t = w_in.reshape(w_in.shape[1:]).T

    cv = jnp.concatenate([c, c_ctx[None], jnp.zeros((8 - batch - 1, D_MODEL), F32)], axis=0)
    m = _modulation(cv, w_mod[0], b_mod[0]).reshape(8, 6, 1, D_MODEL)

    lat_rows = batch * seq
    x2d = x.reshape(lat_rows, D_MODEL)
    c2d = ctx.reshape(batch * n_ctx, D_MODEL)
    xn = _norm_mod(x2d, c2d, norm1[0], m, seq, batch, 1024)

    kv_lo = S5_WIDTH + Q_RANK
    w_kv_rows = w_in_t[kv_lo:kv_lo + KV_RANK + QK_ROPE]
    w_kv_t = jnp.concatenate([w_kv_rows, _rope_rot_cols(w_kv_rows[KV_RANK:].T).T], axis=0)
    wq = w_uq[0].reshape(Q_RANK, MLA_HEADS, QK_NOPE + QK_ROPE)
    wq_rope = wq[:, :, QK_NOPE:].reshape(Q_RANK, MLA_HEADS * QK_ROPE)
    wq2 = jnp.concatenate([wq[:, :, :QK_NOPE].reshape(Q_RANK, MLA_HEADS * QK_NOPE), wq_rope,
                           _rope_rot_cols(wq_rope)], axis=1).astype(BF16)
    w_ukv_bf = w_ukv[0].astype(BF16)
    tm = 1024
    cos, sin = _rope_tables(seq)
    cos_sin_k = jnp.concatenate([jnp.concatenate([cos, sin], axis=1),
                                 jnp.concatenate([jnp.ones((tm, QK_ROPE), F32), jnp.zeros((tm, QK_ROPE), F32)], axis=1)],
                                axis=0)

    u = _proj_cast(xn, w_in_t, 0, S5_WIDTH, tm, 1024, "u_proj", out_dtype=F32)
    q = _q_path(xn, w_in_t, q_norm[0], wq2, jnp.tile(cos, (1, 2)), jnp.tile(sin, (1, 2)), tm, seq, lat_rows)
    kv, kr = _kv_path(xn, w_kv_t, kv_norm[0], w_ukv_bf, cos_sin_k, tm, seq, lat_rows)
    gates = _proj_cast(xn, w_in_t, kv_lo + KV_RANK + QK_ROPE, 2 * D_MODEL, tm, 1024, "gates",
                       act=jax.nn.sigmoid, rows=lat_rows)

    z5 = _s5_mixer(u, p, batch, seq, n_ctx)
    o_mla = _attention(q, kv, kr, batch, seq, n_ctx, 2048)

    merged = _merge(z5, o_mla, gates, w_glu[0], w_mla_o[0], 512, 1024)
    x1, xn2 = _out_proj_norm(merged, w_out[0], x2d, m, norm2[0], seq, 512)
    hid = _ffn_in(xn2, w_ffn_in[0], 2048, 512)
    return _ffn_out(hid, w_ffn_out[0], x1, m, norm_f, seq, 256).reshape(batch, seq, D_MODEL)
```

```python
import functools
import math

import jax
import jax.numpy as jnp
from jax import lax
from jax.experimental import pallas as pl
from jax.experimental.pallas import tpu as pltpu

F32 = jnp.float32
BF16 = jnp.bfloat16

D_MODEL = 2048
GRID_W = 64
EPS = 1e-6
S5_WIDTH = D_MODEL // 2
S5_GROUP = 16
S5_GROUPS = S5_WIDTH // S5_GROUP
S5_STATE = 64
S5_CHUNK = 16
S5_PAIRS = S5_GROUPS // 2
MLA_HEADS = 8
QK_NOPE = 128
QK_ROPE = 64
V_DIM = 128
Q_RANK = 512
KV_RANK = 256
ROPE_BASE = 10000.0
ATTN_SCALE = (QK_NOPE + QK_ROPE) ** -0.5
D_FF = -(-8 * D_MODEL // (3 * 256)) * 256

VMEM_LIMIT_BYTES = 56 * 1024 * 1024
SUBLANES = 8
MM_SUB_ROWS = 512
MERGE_SUB_ROWS = 256


def _cparams(n_axes):
    return pltpu.CompilerParams(dimension_semantics=("arbitrary",) * n_axes,
                                vmem_limit_bytes=VMEM_LIMIT_BYTES)


def _rms(x, g):
    return x * lax.rsqrt(jnp.mean(x * x, axis=-1, keepdims=True) + EPS) * g


def _mod_kernel(cv_ref, w_ref, b_ref, o_ref):
    s = jax.nn.silu(cv_ref[...]).astype(BF16)
    o_ref[...] = jnp.dot(s, w_ref[...].astype(BF16), preferred_element_type=F32) + b_ref[...]


def _modulation(cv, w_mod, b_mod):
    n = w_mod.shape[1]
    tn = 1536
    return pl.pallas_call(
        _mod_kernel,
        grid=(n // tn,),
        in_specs=[pl.BlockSpec((8, D_MODEL), lambda j: (0, 0)),
                  pl.BlockSpec((D_MODEL, tn), lambda j: (0, j)),
                  pl.BlockSpec((1, tn), lambda j: (0, j))],
        out_specs=pl.BlockSpec((8, tn), lambda j: (0, j)),
        out_shape=jax.ShapeDtypeStruct((8, n), F32),
        compiler_params=_cparams(1),
        name="mod",
    )(cv, w_mod, b_mod.reshape(1, n))


MOD_SHIFT1, MOD_SCALE1, MOD_GATE1, MOD_SHIFT2, MOD_SCALE2, MOD_GATE2 = range(6)


def _mod_spec(width, row_of, which, col_of=None):
    col_of = col_of or (lambda *ids: 0)
    return pl.BlockSpec((1, 1, 1, width), lambda *ids: (row_of(*ids), which, 0, col_of(*ids)))


def _norm_mod_kernel(x_ref, c_ref, g_ref, sc_ref, sh_ref, o_ref, *, lat_tiles):
    def emit(src_ref):
        y = _rms(src_ref[...], g_ref[...])
        o_ref[...] = (y * (1.0 + sc_ref[0, 0]) + sh_ref[0, 0]).astype(o_ref.dtype)

    pl.when(pl.program_id(0) < lat_tiles)(lambda: emit(x_ref))
    pl.when(pl.program_id(0) >= lat_tiles)(lambda: emit(c_ref))


def _norm_mod(x2d, c2d, gain, m, rows_per_batch, ctx_row, tm):
    lat_tiles, ctx_tiles = x2d.shape[0] // tm, c2d.shape[0] // tm
    tpb = rows_per_batch // tm

    def row(i):
        return jnp.where(i < lat_tiles, i // tpb, ctx_row)

    return pl.pallas_call(
        functools.partial(_norm_mod_kernel, lat_tiles=lat_tiles),
        grid=(lat_tiles + ctx_tiles,),
        in_specs=[pl.BlockSpec((tm, D_MODEL), lambda i: (jnp.minimum(i, lat_tiles - 1), 0)),
                  pl.BlockSpec((tm, D_MODEL), lambda i: (jnp.maximum(i - lat_tiles, 0), 0)),
                  pl.BlockSpec((1, D_MODEL), lambda i: (0, 0)),
                  _mod_spec(D_MODEL, row, MOD_SCALE1), _mod_spec(D_MODEL, row, MOD_SHIFT1)],
        out_specs=pl.BlockSpec((tm, D_MODEL), lambda i: (i, 0)),
        out_shape=jax.ShapeDtypeStruct((x2d.shape[0] + c2d.shape[0], D_MODEL), BF16),
        compiler_params=_cparams(1),
        name="norm_mod",
    )(x2d, c2d, gain.reshape(1, D_MODEL), m, m)


def _fused_mm(a, weights, epilogue, extras, outs, *, tm, nj, name, rows=None, sub_rows=None, resident=False):
    t, k = a.shape
    ni = (rows or t) // tm
    nw, ne, no = len(weights), len(extras), len(outs)
    need_cast = [w.dtype != BF16 for w, _, _, _ in weights]
    assert all(need_cast) or not resident
    nt = (((1,), (1,)), ((), ()))
    j_axis, i_axis = (1, 0) if resident else (0, 1)

    def kernel(*refs):
        a_ref = refs[0]
        w_refs = refs[1:1 + nw]
        e_refs = refs[1 + nw:1 + nw + ne]
        o_refs = refs[1 + nw + ne:1 + nw + ne + no]
        s_refs = list(refs[1 + nw + ne + no:])
        staged = {idx: s_refs.pop(0) for idx in range(nw) if need_cast[idx]}
        if resident:
            j = pl.program_id(j_axis)
            staged = {idx: s_ref.at[j] for idx, s_ref in staged.items()}
        if staged:
            @pl.when(pl.program_id(i_axis) == 0)
            def _():
                for idx, s_ref in staged.items():
                    s_ref[...] = w_refs[idx][...].astype(BF16)

        for r0 in range(0, tm, sub_rows or tm):
            rows = slice(r0, r0 + (sub_rows or tm))
            av = a_ref[rows, :]
            accs = []
            for idx in range(nw):
                w_ref = staged.get(idx, w_refs[idx])
                if weights[idx][2] == "kn":
                    accs.append(jnp.dot(av, w_ref[...], preferred_element_type=F32))
                else:
                    accs.append(lax.dot_general(av, w_ref[...], nt, preferred_element_type=F32))
            epilogue(accs, e_refs, o_refs, rows)

    def ordered(index_map):
        return (lambda i, j: index_map(j, i)) if resident else index_map

    def w_tile(j, i):
        return jnp.where(i == 0, j, nj - 1) if resident else j

    in_specs = [pl.BlockSpec((tm, k), ordered(lambda j, i: (i, 0)))]
    scratch = []
    for (w, off, layout, width), cast in zip(weights, need_cast):
        if layout == "kn":
            shape = (k, width)
            in_specs.append(pl.BlockSpec(shape, ordered(
                functools.partial(lambda j, i, off: (0, off + w_tile(j, i)), off=off))))
        else:
            shape = (width, k)
            in_specs.append(pl.BlockSpec((pl.Element(width), pl.Element(k)), ordered(functools.partial(
                lambda j, i, off, width: (pl.multiple_of(off + w_tile(j, i) * width, SUBLANES), 0),
                off=off, width=width))))
        if cast:
            scratch.append(pltpu.VMEM(((nj,) if resident else ()) + shape, BF16))
    in_specs += [pl.BlockSpec(spec.block_shape, ordered(spec.index_map)) for _, spec in extras]
    return pl.pallas_call(
        kernel,
        grid=(ni, nj) if resident else (nj, ni),
        in_specs=in_specs,
        out_specs=[pl.BlockSpec(spec.block_shape, ordered(spec.index_map)) for _, spec in outs],
        out_shape=[sds for sds, _ in outs],
        scratch_shapes=scratch,
        compiler_params=_cparams(2),
        name=name,
    )(a, *[w[0] for w in weights], *[e for e, _ in extras])


def _proj_cast(xn, w_in_t, row0, n, tm, tn, name, act=None, out_dtype=BF16, rows=None, resident=False):
    def epi(accs, e_refs, o_refs, rows):
        v = accs[0]
        if act is not None:
            v = act(v)
        o_refs[0][rows, :] = v.astype(out_dtype)

    t = rows or xn.shape[0]
    return _fused_mm(xn, [(w_in_t, row0, "nk", tn)], epi, [],
                     [(jax.ShapeDtypeStruct((t, n), out_dtype), pl.BlockSpec((tm, tn), lambda j, i: (i, j)))],
                     tm=tm, nj=n // tn, name=name, rows=rows, resident=resident)[0]


def _q_path(xn, w_in_t, q_norm, wq2, cos2, sin2, tm, seq, rows):
    nr = MLA_HEADS * QK_ROPE
    nn = MLA_HEADS * QK_NOPE
    lanes = 2 * QK_ROPE

    def epi(accs, e_refs, o_refs, rows):
        qn_ref, w2_ref, cos_ref, sin_ref = e_refs
        cq = _rms(accs[0], qn_ref[...]).astype(BF16)
        q = jnp.dot(cq, w2_ref[...], preferred_element_type=F32)
        o_refs[0][rows, :nn] = (q[:, :nn] * ATTN_SCALE).astype(BF16)
        cos, sin = cos_ref[rows, :] * ATTN_SCALE, sin_ref[rows, :] * ATTN_SCALE
        for c0 in range(0, nr, lanes):
            rope = q[:, nn + c0:nn + c0 + lanes] * cos + q[:, nn + nr + c0:nn + nr + c0 + lanes] * sin
            o_refs[0][rows, nn + c0:nn + c0 + lanes] = rope.astype(BF16)

    pos_tiles = seq // tm
    extras = [(q_norm.reshape(1, Q_RANK), pl.BlockSpec((1, Q_RANK), lambda j, i: (0, 0))),
              (wq2, pl.BlockSpec(wq2.shape, lambda j, i: (0, 0))),
              (cos2, pl.BlockSpec((tm, lanes), lambda j, i: (i % pos_tiles, 0))),
              (sin2, pl.BlockSpec((tm, lanes), lambda j, i: (i % pos_tiles, 0)))]
    outs = [(jax.ShapeDtypeStruct((rows, nn + nr), BF16), pl.BlockSpec((tm, nn + nr), lambda j, i: (i, 0)))]
    return _fused_mm(xn, [(w_in_t, S5_WIDTH, "nk", Q_RANK)], epi, extras, outs, tm=tm, nj=1, name="q_path",
                     rows=rows)[0]


def _kv_path(xn, w_kv_t, kv_norm, w_ukv_bf, cos_sin_k, tm, seq, lat_rows):
    nkv = w_ukv_bf.shape[1]

    def epi(accs, e_refs, o_refs, rows):
        acc = accs[0]
        ckv = _rms(acc[:, :KV_RANK], e_refs[0][...]).astype(BF16)
        o_refs[0][rows, :] = jnp.dot(ckv, e_refs[1][...], preferred_element_type=F32).astype(BF16)
        prod = acc[:, KV_RANK:] * e_refs[2][rows, :]
        o_refs[1][rows, :] = (prod + pltpu.roll(prod, QK_ROPE, 1))[:, :QK_ROPE].astype(BF16)

    t = xn.shape[0]
    pos_tiles, lat_tiles = seq // tm, lat_rows // tm
    extras = [(kv_norm.reshape(1, KV_RANK), pl.BlockSpec((1, KV_RANK), lambda j, i: (0, 0))),
              (w_ukv_bf, pl.BlockSpec(w_ukv_bf.shape, lambda j, i: (0, 0))),
              (cos_sin_k, pl.BlockSpec((tm, 2 * QK_ROPE),
                                       lambda j, i: (jnp.where(i < lat_tiles, i % pos_tiles, pos_tiles), 0)))]
    outs = [(jax.ShapeDtypeStruct((t, nkv), BF16), pl.BlockSpec((tm, nkv), lambda j, i: (i, 0))),
            (jax.ShapeDtypeStruct((t, QK_ROPE), BF16), pl.BlockSpec((tm, QK_ROPE), lambda j, i: (i, 0)))]
    return _fused_mm(xn, [(w_kv_t, 0, "nk", w_kv_t.shape[0])], epi, extras, outs, tm=tm, nj=1, name="kv_path")


ATTN_SUB_ROWS = 256


def _attn_kernel(qn_ref, qr_ref, kvl_ref, kvc_ref, krl_ref, krc_ref, o_ref, k_scr, v_scr, *, seq, ctx):
    dk = QK_NOPE + QK_ROPE

    @pl.when(pl.program_id(2) == 0)
    def _():
        for h in range(2):
            base = h * (QK_NOPE + V_DIM)
            k_scr[h, :seq, :QK_NOPE] = kvl_ref[:, base:base + QK_NOPE]
            k_scr[h, seq:, :QK_NOPE] = kvc_ref[:, base:base + QK_NOPE]
            k_scr[h, :seq, QK_NOPE:dk] = krl_ref[...]
            k_scr[h, seq:, QK_NOPE:dk] = krc_ref[...]
            v_scr[h, :seq, :V_DIM] = kvl_ref[:, base + QK_NOPE:base + QK_NOPE + V_DIM]
            v_scr[h, seq:, :V_DIM] = kvc_ref[:, base + QK_NOPE:base + QK_NOPE + V_DIM]
            ones_col = lax.broadcasted_iota(jnp.int32, (seq + ctx, V_DIM), 1) == 0
            v_scr[h, :, V_DIM:] = jnp.where(ones_col, 1.0, 0.0).astype(BF16)

    for r0 in range(0, qn_ref.shape[0], ATTN_SUB_ROWS):
        rows = slice(r0, r0 + ATTN_SUB_ROWS)
        for h in range(2):
            q = jnp.concatenate([qn_ref[rows, h * QK_NOPE:(h + 1) * QK_NOPE],
                                 qr_ref[rows, h * QK_ROPE:(h + 1) * QK_ROPE]], axis=1)
            s = lax.dot_general(q, k_scr[h], (((1,), (1,)), ((), ())), preferred_element_type=F32)
            m = jnp.max(s, axis=-1, keepdims=True)
            p = jnp.exp((s - m).astype(BF16))
            ol = jnp.dot(p, v_scr[h], preferred_element_type=F32)
            o_ref[rows, h * V_DIM:(h + 1) * V_DIM] = (ol[:, :V_DIM] / ol[:, V_DIM:V_DIM + 1]).astype(o_ref.dtype)


def _attention(q, kv, kr, batch, seq, ctx, tq):
    nq = seq // tq
    ctx0 = batch * seq // ctx
    nn_blocks = MLA_HEADS * QK_NOPE // (2 * QK_NOPE)
    dk = QK_NOPE + QK_ROPE
    hw = 2 * (QK_NOPE + V_DIM)
    return pl.pallas_call(
        functools.partial(_attn_kernel, seq=seq, ctx=ctx),
        grid=(batch, MLA_HEADS // 2, nq),
        in_specs=[pl.BlockSpec((tq, 2 * QK_NOPE), lambda b, hp, qi: (b * nq + qi, hp)),
                  pl.BlockSpec((tq, 2 * QK_ROPE), lambda b, hp, qi: (b * nq + qi, 2 * nn_blocks + hp)),
                  pl.BlockSpec((seq, hw), lambda b, hp, qi: (b, hp)),
                  pl.BlockSpec((ctx, hw), lambda b, hp, qi: (ctx0 + b, hp)),
                  pl.BlockSpec((seq, QK_ROPE), lambda b, hp, qi: (b, 0)),
                  pl.BlockSpec((ctx, QK_ROPE), lambda b, hp, qi: (ctx0 + b, 0))],
        out_specs=pl.BlockSpec((tq, 2 * V_DIM), lambda b, hp, qi: (b * nq + qi, hp)),
        out_shape=jax.ShapeDtypeStruct((batch * seq, MLA_HEADS * V_DIM), BF16),
        scratch_shapes=[pltpu.VMEM((2, seq + ctx, dk), BF16), pltpu.VMEM((2, seq + ctx, 2 * V_DIM), BF16)],
        compiler_params=_cparams(3),
        name="attention",
    )(q, q, kv, kv, kr, kr)


def _s5_prep_pair(pp, are_ref, aim_ref, ldt_ref, bre_ref, bim_ref, cre_ref, cim_ref,
                  toep_ref, wsr_ref, wsi_ref, wor_ref, woi_ref, atr_ref, ati_ref):
    tc, g = S5_CHUNK, S5_GROUP
    w = tc * g
    lane = lax.broadcasted_iota(jnp.int32, (g, 2 * S5_STATE), 1)
    in_group = (lane < S5_STATE, lane >= S5_STATE)
    lane_w = lax.broadcasted_iota(jnp.int32, (g, w), 1)
    nt = (((1,), (1,)), ((), ()))
    toep_rows = [[jnp.zeros((g, w), F32) for _ in range(tc)] for _ in range(2)]
    for d in range(2):
        lr, li = are_ref[d, pp], aim_ref[d, pp]
        dt = jnp.exp(ldt_ref[d, pp])
        mag = jnp.exp(lr * dt)
        ab_re, ab_im = mag * jnp.cos(li * dt), mag * jnp.sin(li * dt)
        den = lr * lr + li * li
        nr, ni = ab_re - 1.0, ab_im
        co_re = (nr * lr + ni * li) / den
        co_im = (ni * lr - nr * li) / den
        br, bi = bre_ref[d, pp], bim_ref[d, pp]
        bb_re = co_re * br - co_im * bi
        bb_im = co_re * bi + co_im * br
        pw = [(jnp.ones_like(ab_re), jnp.zeros_like(ab_re))]
        for _ in range(tc):
            pr, pi = pw[-1]
            pw.append((pr * ab_re - pi * ab_im, pr * ab_im + pi * ab_re))
        cr, ci = cre_ref[d, pp], cim_ref[d, pp]
        ca = [(cr * pr - ci * pi, cr * pi + ci * pr) for pr, pi in pw]
        taus = list(range(tc))[::-1] if d else list(range(tc))
        y_re = jnp.concatenate([ca[t][0] for t in taus], axis=0).astype(BF16)
        y_im = jnp.concatenate([ca[t][1] for t in taus], axis=0).astype(BF16)
        for j in range(2):
            x_re = jnp.where(in_group[j], bb_re, 0.0)
            x_im = jnp.where(in_group[j], bb_im, 0.0)
            kt = (lax.dot_general(x_re.astype(BF16), y_re, nt, preferred_element_type=F32)
                  - lax.dot_general(x_im.astype(BF16), y_im, nt, preferred_element_type=F32))
            for r in range(tc):
                sh = (r + 1) * g if d else r * g
                blk = pltpu.roll(kt, sh % w, 1) if sh % w else kt
                keep = (lane_w < sh) if d else (lane_w >= sh)
                toep_rows[j][r] = toep_rows[j][r] + jnp.where(keep, blk, 0.0)
        for r in range(tc):
            pr, pi = pw[r] if d else pw[tc - 1 - r]
            w_re = bb_re * pr - bb_im * pi
            w_im = bb_re * pi + bb_im * pr
            car, cai = ca[tc - r] if d else ca[r + 1]
            for j in range(2):
                rows = slice(j * w + r * g, j * w + (r + 1) * g)
                wsr_ref[d, pp, rows, :] = jnp.where(in_group[j], w_re, 0.0).astype(BF16)
                wsi_ref[d, pp, rows, :] = jnp.where(in_group[j], w_im, 0.0).astype(BF16)
                wor_ref[d, pp, rows, :] = jnp.where(in_group[j], car, 0.0).astype(BF16)
                woi_ref[d, pp, rows, :] = jnp.where(in_group[j], -cai, 0.0).astype(BF16)
        atr_ref[d, pp] = pw[tc][0]
        ati_ref[d, pp] = pw[tc][1]
    for j in range(2):
        toep_ref[pp, j] = jnp.concatenate(toep_rows[j], axis=0).astype(BF16)


S5_PREP_PAIRS = 4


def _s5_prep_kernel(*refs):
    for pp in range(S5_PREP_PAIRS):
        _s5_prep_pair(pp, *refs)


def _s5_prep(params):
    tc = S5_CHUNK
    wide = 2 * tc * S5_GROUP
    sl = 2 * S5_STATE

    def pair_lanes(v):
        return v.astype(F32).reshape(2, S5_PAIRS, 1, sl)

    def pair_rows(v):
        rows = v.shape[2]
        return v.astype(F32).reshape(2, S5_PAIRS, 2, rows, S5_STATE).transpose(0, 1, 3, 2, 4).reshape(
            2, S5_PAIRS, rows, sl)

    ldt = jnp.broadcast_to(params['s5_log_dt'].astype(F32)[:, :, None], (2, S5_GROUPS, S5_STATE))
    ins = [pair_lanes(params['s5_a_re']), pair_lanes(params['s5_a_im']), pair_lanes(ldt),
           pair_rows(params['s5_b_re'].transpose(0, 1, 3, 2)), pair_rows(params['s5_b_im'].transpose(0, 1, 3, 2)),
           pair_rows(params['s5_c_re']), pair_rows(params['s5_c_im'])]
    pp = S5_PREP_PAIRS
    vec_spec = pl.BlockSpec((2, pp, 1, sl), lambda k: (0, k, 0, 0))
    mat_spec = pl.BlockSpec((2, pp, S5_GROUP, sl), lambda k: (0, k, 0, 0))
    w_spec = pl.BlockSpec((2, pp, wide, sl), lambda k: (0, k, 0, 0))
    w_sds = jax.ShapeDtypeStruct((2, S5_PAIRS, wide, sl), BF16)
    a_sds = jax.ShapeDtypeStruct((2, S5_PAIRS, 1, sl), F32)
    return pl.pallas_call(
        _s5_prep_kernel,
        grid=(S5_PAIRS // pp,),
        in_specs=[vec_spec] * 3 + [mat_spec] * 4,
        out_specs=[pl.BlockSpec((pp, 2, wide // 2, wide // 2), lambda k: (k, 0, 0, 0))] + [w_spec] * 4 + [vec_spec] * 2,
        out_shape=[jax.ShapeDtypeStruct((S5_PAIRS, 2, wide // 2, wide // 2), BF16)] + [w_sds] * 4 + [a_sds] * 2,
        compiler_params=_cparams(1),
        name="s5_prep",
    )(*ins)


S5_STEP_PAIRS = 4
def _s5_state_rows(batch, n_lat, n_ctx):
    def pitch(n):
        p = -(-n // SUBLANES)
        return SUBLANES * (p + 1 - p % 2)

    lat_pitch, ctx_pitch = pitch(n_lat), pitch(n_ctx)
    ctx_base = batch * lat_pitch
    return lat_pitch, ctx_pitch, ctx_base, ctx_base + batch * ctx_pitch


def _s5_kernel(ul_ref, uc_ref, dsk_ref, *refs, batch, seq, ctx):
    toep_ref, wsr, wsi, wor, woi, atr, ati = refs[:7]
    o_ref = refs[7]
    wscr, zscr, upscr, sre, sim, hre, him = refs[8:]
    tc = S5_CHUNK
    n_lat, n_ctx = seq // tc, ctx // tc
    rows_lat = n_lat * batch
    lat_pitch, ctx_pitch, ctx_base, _ = _s5_state_rows(batch, n_lat, n_ctx)
    w = tc * S5_GROUP
    lanes = S5_STEP_PAIRS * 2 * S5_GROUP

    def scatter_tiles(xt, r, col0):
        for kk in range(S5_STEP_PAIRS):
            for j in range(2):
                ch = (2 * kk + j) * S5_GROUP
                wscr[kk, j * w + r * S5_GROUP:j * w + (r + 1) * S5_GROUP, col0:col0 + lanes] = xt[ch:ch + S5_GROUP, :]

    for r in range(tc):
        for b in range(batch):
            x = ul_ref[pl.ds(b * seq + r, n_lat, stride=tc), :]
            scatter_tiles(x.T, r, b * n_lat)
        xc = jnp.concatenate([uc_ref[pl.ds(b * ctx + r, n_ctx, stride=tc), :] for b in range(batch)]
                             + [jnp.zeros((lanes - batch * n_ctx, lanes), F32)], axis=0)
        scatter_tiles(xc.T, r, rows_lat)

    chains = [(kk, d) for kk in range(S5_STEP_PAIRS) for d in range(2)]
    for kk in range(S5_STEP_PAIRS):
        upscr[kk] = wscr[kk].T.astype(BF16)
        for d in range(2):
            for dst, wst in ((sre, wsr), (sim, wsi)):
                s = jnp.dot(upscr[kk], wst[d, kk], preferred_element_type=F32)
                for b in range(batch):
                    dst[kk, d, b * lat_pitch:b * lat_pitch + n_lat, :] = s[b * n_lat:(b + 1) * n_lat]
                    dst[kk, d, ctx_base + b * ctx_pitch:ctx_base + b * ctx_pitch + n_ctx, :] = (
                        s[rows_lat + b * n_ctx:rows_lat + (b + 1) * n_ctx])

    ctx_rows = [pl.ds(ctx_base + c, batch, stride=ctx_pitch) for c in range(n_ctx)]
    lat_rows = [pl.ds(c, batch, stride=lat_pitch) for c in range(n_lat)]
    order = (ctx_rows + lat_rows, ctx_rows[::-1] + lat_rows[::-1])
    coef = {(kk, d): (atr[d, kk], ati[d, kk]) for kk, d in chains}
    state = {ch: (jnp.zeros((batch, 2 * S5_STATE), F32), jnp.zeros((batch, 2 * S5_STATE), F32)) for ch in chains}
    for t in range(n_ctx + n_lat):
        for kk, d in chains:
            rows = order[d][t]
            (ar, ai), (h_re, h_im) = coef[kk, d], state[kk, d]
            hre[kk, d, rows, :] = h_re
            him[kk, d, rows, :] = h_im
            state[kk, d] = (ar * h_re - ai * h_im + sre[kk, d, rows, :],
                            ar * h_im + ai * h_re + sim[kk, d, rows, :])

    for kk in range(S5_STEP_PAIRS):
        ul = upscr[kk, :rows_lat, :]
        y = ul.astype(F32) * dsk_ref[kk]
        y = y + jnp.concatenate(
            [jnp.dot(ul[:, :w], toep_ref[kk, 0], preferred_element_type=F32),
             jnp.dot(ul[:, w:], toep_ref[kk, 1], preferred_element_type=F32)], axis=1)
        nt = (((1,), (1,)), ((), ()))
        for d in range(2):
            h_r = jnp.concatenate([hre[kk, d, b * lat_pitch:b * lat_pitch + n_lat, :] for b in range(batch)], axis=0)
            h_i = jnp.concatenate([him[kk, d, b * lat_pitch:b * lat_pitch + n_lat, :] for b in range(batch)], axis=0)
            y = y + lax.dot_general(h_r.astype(BF16), wor[d, kk], nt, preferred_element_type=F32)
            y = y + lax.dot_general(h_i.astype(BF16), woi[d, kk], nt, preferred_element_type=F32)
        yt = y.T
        for b in range(batch):
            for s in range(tc):
                for j in range(2):
                    ch = (2 * kk + j) * S5_GROUP
                    zscr[b * tc + s, ch:ch + S5_GROUP, :] = yt[j * w + s * S5_GROUP:j * w + (s + 1) * S5_GROUP,
                                                               b * n_lat:(b + 1) * n_lat]

    for b in range(batch):
        for s in range(tc):
            o_ref[pl.ds(b * seq + s, n_lat, stride=tc), :] = jax.nn.gelu(zscr[b * tc + s].T)


def _s5_mixer(u, params, batch, seq, ctx):
    tc = S5_CHUNK
    n_lat, n_ctx = seq // tc, ctx // tc
    sp = S5_STEP_PAIRS
    lanes = sp * 2 * S5_GROUP
    assert n_lat == lanes and batch * n_ctx <= lanes
    rows_lat = n_lat * batch
    rows_all = rows_lat + lanes
    state_rows = _s5_state_rows(batch, n_lat, n_ctx)[3]
    wide = 2 * tc * S5_GROUP
    dsk = jnp.broadcast_to(params['s5_d'].astype(F32).reshape(S5_PAIRS, 2, 1, S5_GROUP),
                           (S5_PAIRS, 2, tc, S5_GROUP)).reshape(S5_PAIRS, 1, wide)
    weights = _s5_prep(params)
    w_spec = pl.BlockSpec((2, sp, wide, 2 * S5_STATE), lambda k: (0, k, 0, 0))
    a_spec = pl.BlockSpec((2, sp, 1, 2 * S5_STATE), lambda k: (0, k, 0, 0))
    specs = [pl.BlockSpec((sp, 2, wide // 2, wide // 2), lambda k: (k, 0, 0, 0))] + [w_spec] * 4 + [a_spec] * 2
    return pl.pallas_call(
        functools.partial(_s5_kernel, batch=batch, seq=seq, ctx=ctx),
        grid=(S5_PAIRS // sp,),
        in_specs=[pl.BlockSpec((batch * seq, lanes), lambda k: (0, k)),
                  pl.BlockSpec((batch * ctx, lanes), lambda k: (seq // ctx, k)),
                  pl.BlockSpec((sp, 1, wide), lambda k: (k, 0, 0))] + specs,
        out_specs=pl.BlockSpec((batch * seq, lanes), lambda k: (0, k)),
        out_shape=jax.ShapeDtypeStruct((batch * seq, S5_WIDTH), F32),
        scratch_shapes=[pltpu.VMEM((sp, wide, rows_all), F32),
                        pltpu.VMEM((batch * tc, lanes, n_lat), F32),
                        pltpu.VMEM((sp, rows_all, wide), BF16)]
                       + [pltpu.VMEM((sp, 2, state_rows, 2 * S5_STATE), F32) for _ in range(4)],
        compiler_params=_cparams(1),
        name="s5",
    )(u, u, dsk, *weights)


def _merge_kernel(z5_ref, o_ref_in, gs_ref, gm_ref, wa_ref, wb_ref, wm_ref, out_ref, wa_s, wb_s, wm_s):
    @pl.when(pl.program_id(1) == 0)
    def _():
        wa_s[...] = wa_ref[...].astype(BF16)
        wb_s[...] = wb_ref[...].astype(BF16)
        wm_s[...] = wm_ref[...].astype(BF16)

    sub = min(MERGE_SUB_ROWS, out_ref.shape[0])
    for r0 in range(0, out_ref.shape[0], sub):
        rows = slice(r0, r0 + sub)
        z = z5_ref[rows, :].astype(BF16)
        a = jnp.dot(z, wa_s[...], preferred_element_type=F32)
        b = jnp.dot(z, wb_s[...], preferred_element_type=F32)
        mla = jnp.dot(o_ref_in[rows, :], wm_s[...], preferred_element_type=F32)
        merged = gs_ref[rows, :].astype(F32) * (a * jax.nn.sigmoid(b)) + gm_ref[rows, :].astype(F32) * mla
        out_ref[rows, :] = merged.astype(out_ref.dtype)


def _merge(y5, o_mla, gates, w_glu, w_mla_o, tm, tn):
    t = y5.shape[0]
    nj = D_MODEL // tn
    return pl.pallas_call(
        _merge_kernel,
        grid=(nj, t // tm),
        in_specs=[pl.BlockSpec((tm, S5_WIDTH), lambda j, i: (i, 0)),
                  pl.BlockSpec((tm, MLA_HEADS * V_DIM), lambda j, i: (i, 0)),
                  pl.BlockSpec((tm, tn), lambda j, i: (i, j)),
                  pl.BlockSpec((tm, tn), lambda j, i: (i, nj + j)),
                  pl.BlockSpec((S5_WIDTH, tn), lambda j, i: (0, j)),
                  pl.BlockSpec((S5_WIDTH, tn), lambda j, i: (0, nj + j)),
                  pl.BlockSpec((MLA_HEADS * V_DIM, tn), lambda j, i: (0, j))],
        out_specs=pl.BlockSpec((tm, tn), lambda j, i: (i, j)),
        out_shape=jax.ShapeDtypeStruct((t, D_MODEL), BF16),
        scratch_shapes=[pltpu.VMEM((S5_WIDTH, tn), BF16), pltpu.VMEM((S5_WIDTH, tn), BF16),
                        pltpu.VMEM((MLA_HEADS * V_DIM, tn), BF16)],
        compiler_params=_cparams(2),
        name="merge",
    )(y5, o_mla, gates, gates, w_glu, w_glu, w_mla_o)


def _out_proj_norm_kernel(a_ref, w_ref, x_ref, g1_ref, n2_ref, sc_ref, sh_ref, x1_ref, xn_ref, w_s):
    @pl.when(pl.program_id(0) == 0)
    def _():
        w_s[...] = w_ref[...].astype(BF16)

    sub = min(MERGE_SUB_ROWS, x1_ref.shape[0])
    for r0 in range(0, x1_ref.shape[0], sub):
        rows = slice(r0, r0 + sub)
        x1 = x_ref[rows, :] + g1_ref[0, 0] * jnp.dot(a_ref[rows, :], w_s[...], preferred_element_type=F32)
        x1_ref[rows, :] = x1
        xn_ref[rows, :] = (_rms(x1, n2_ref[...]) * (1.0 + sc_ref[0, 0]) + sh_ref[0, 0]).astype(xn_ref.dtype)


def _out_proj_norm(merged, w_out, x_res, m, gain2, rows_per_batch, tm):
    t, k = merged.shape
    tpb = rows_per_batch // tm
    row = pl.BlockSpec((tm, D_MODEL), lambda i: (i, 0))
    gate1, scale2, shift2 = (_mod_spec(D_MODEL, lambda i: i // tpb, which)
                             for which in (MOD_GATE1, MOD_SCALE2, MOD_SHIFT2))
    return pl.pallas_call(
        _out_proj_norm_kernel,
        grid=(t // tm,),
        in_specs=[pl.BlockSpec((tm, k), lambda i: (i, 0)),
                  pl.BlockSpec((k, D_MODEL), lambda i: (0, 0), pipeline_mode=pl.Buffered(1)),
                  row, gate1, pl.BlockSpec((1, D_MODEL), lambda i: (0, 0)), scale2, shift2],
        out_specs=[row, row],
        out_shape=[jax.ShapeDtypeStruct((t, D_MODEL), F32), jax.ShapeDtypeStruct((t, D_MODEL), BF16)],
        scratch_shapes=[pltpu.VMEM((k, D_MODEL), BF16)],
        compiler_params=_cparams(1),
        name="out_proj",
    )(merged, w_out, x_res, m, gain2.reshape(1, D_MODEL), m, m)


def _ffn_in(xn, w_ffn_in, tm, tn):
    def epi(accs, e_refs, o_refs, rows):
        o_refs[0][rows, :] = (jax.nn.silu(accs[0]) * accs[1]).astype(BF16)

    t = xn.shape[0]
    nj = D_FF // tn
    outs = [(jax.ShapeDtypeStruct((t, D_FF), BF16), pl.BlockSpec((tm, tn), lambda j, i: (i, j)))]
    return _fused_mm(xn, [(w_ffn_in, 0, "kn", tn), (w_ffn_in, nj, "kn", tn)], epi, [], outs,
                     tm=tm, nj=nj, name="ffn_in", sub_rows=MM_SUB_ROWS)[0]


def _ffn_out_kernel(h_ref, w_ref, x_ref, g2_ref, nf_ref, o_ref, w_s, *, last):
    @pl.when(pl.program_id(0) == 0)
    def _():
        w_s[...] = w_ref[...].astype(BF16)

    y = x_ref[...] + g2_ref[0, 0] * jnp.dot(h_ref[...], w_s[...], preferred_element_type=F32)
    o_ref[...] = _rms(y, nf_ref[...]) if last else y


def _ffn_out(hid, w_ffn_out, x_res, m, norm_f, rows_per_batch, tm):
    t, k = hid.shape
    kh = k // 2
    tpb = rows_per_batch // tm
    row = pl.BlockSpec((tm, D_MODEL), lambda i: (i, 0))
    y = x_res
    for half in range(2):
        y = pl.pallas_call(
            functools.partial(_ffn_out_kernel, last=half == 1),
            grid=(t // tm,),
            in_specs=[pl.BlockSpec((tm, kh), functools.partial(lambda i, half: (i, half), half=half)),
                      pl.BlockSpec((kh, D_MODEL), functools.partial(lambda i, half: (half, 0), half=half),
                                   pipeline_mode=pl.Buffered(1)),
                      row, _mod_spec(D_MODEL, lambda i: i // tpb, MOD_GATE2),
                      pl.BlockSpec((1, D_MODEL), lambda i: (0, 0))],
            out_specs=row,
            out_shape=jax.ShapeDtypeStruct((t, D_MODEL), F32),
            scratch_shapes=[pltpu.VMEM((kh, D_MODEL), BF16)],
            compiler_params=_cparams(1),
            name="ffn_out",
        )(hid, w_ffn_out, y, m, norm_f.reshape(1, D_MODEL))
    return y


def _rope_rot_cols(w):
    k = w.shape[0]
    ws = w.reshape(k, -1, 2, 2, QK_ROPE // 4)
    return jnp.stack([-ws[:, :, :, 1, :], ws[:, :, :, 0, :]], axis=3).reshape(k, -1)


def _rope_tables(n_tokens):
    rows = n_tokens // GRID_W
    row = jnp.repeat(jnp.arange(rows, dtype=F32), GRID_W)
    col = jnp.tile(jnp.arange(GRID_W, dtype=F32), rows)
    n_freq = QK_ROPE // 4
    inv = ROPE_BASE ** (-jnp.arange(n_freq, dtype=F32) / n_freq)
    ang = jnp.stack([row[:, None] * inv, col[:, None] * inv], axis=1)
    cos = jnp.broadcast_to(jnp.cos(ang)[:, :, None, :], (n_tokens, 2, 2, n_freq)).reshape(n_tokens, QK_ROPE)
    sin = jnp.broadcast_to(jnp.sin(ang)[:, :, None, :], (n_tokens, 2, 2, n_freq)).reshape(n_tokens, QK_ROPE)
    return cos, sin


def kernel(x, c, ctx, c_ctx, w_mod, b_mod, norm1, norm2, w_in, s5_a_re, s5_a_im, s5_log_dt, s5_b_re, s5_b_im,
           s5_c_re, s5_c_im, s5_d, w_glu, q_norm, kv_norm, w_uq, w_ukv, w_mla_o, w_out, w_ffn_in, w_ffn_out,
           norm_f):
    batch, seq, _ = x.shape
    n_ctx = ctx.shape[1]
    assert w_mod.shape[0] == 1, "single-layer block"
    p = dict(s5_a_re=s5_a_re[0], s5_a_im=s5_a_im[0], s5_log_dt=s5_log_dt[0], s5_b_re=s5_b_re[0],
             s5_b_im=s5_b_im[0], s5_c_re=s5_c_re[0], s5_c_im=s5_c_im[0], s5_d=s5_d[0])
    w_in_t = w_in.reshape(w_in.shape[1:]).T

    cv = jnp.concatenate([c, c_ctx[None], jnp.zeros((8 - batch - 1, D_MODEL), F32)], axis=0)
    m = _modulation(cv, w_mod[0], b_mod[0]).reshape(8, 6, 1, D_MODEL)

    lat_rows = batch * seq
    x2d = x.reshape(lat_rows, D_MODEL)
    c2d = ctx.reshape(batch * n_ctx, D_MODEL)
    xn = _norm_mod(x2d, c2d, norm1[0], m, seq, batch, 1024)

    kv_lo = S5_WIDTH + Q_RANK
    w_kv_rows = w_in_t[kv_lo:kv_lo + KV_RANK + QK_ROPE]
    w_kv_t = jnp.concatenate([w_kv_rows, _rope_rot_cols(w_kv_rows[KV_RANK:].T).T], axis=0)
    wq = w_uq[0].reshape(Q_RANK, MLA_HEADS, QK_NOPE + QK_ROPE)
    wq_rope = wq[:, :, QK_NOPE:].reshape(Q_RANK, MLA_HEADS * QK_ROPE)
    wq2 = jnp.concatenate([wq[:, :, :QK_NOPE].reshape(Q_RANK, MLA_HEADS * QK_NOPE), wq_rope,
                           _rope_rot_cols(wq_rope)], axis=1).astype(BF16)
    w_ukv_bf = w_ukv[0].astype(BF16)
    tm = 1024
    cos, sin = _rope_tables(seq)
    cos_sin_k = jnp.concatenate([jnp.concatenate([cos, sin], axis=1),
                                 jnp.concatenate([jnp.ones((tm, QK_ROPE), F32), jnp.zeros((tm, QK_ROPE), F32)], axis=1)],
                                axis=0)

    u = _proj_cast(xn, w_in_t, 0, S5_WIDTH, tm, 1024, "u_proj", out_dtype=F32)
    q = _q_path(xn, w_in_t, q_norm[0], wq2, jnp.tile(cos, (1, 2)), jnp.tile(sin, (1, 2)), tm, seq, lat_rows)
    kv, kr = _kv_path(xn, w_kv_t, kv_norm[0], w_ukv_bf, cos_sin_k, tm, seq, lat_rows)
    gates = _proj_cast(xn, w_in_t, kv_lo + KV_RANK + QK_ROPE, 2 * D_MODEL, tm, 1024, "gates",
                       act=jax.nn.sigmoid, rows=lat_rows, resident=True)

    z5 = _s5_mixer(u, p, batch, seq, n_ctx)
    o_mla = _attention(q, kv, kr, batch, seq, n_ctx, 2048)

    merged = _merge(z5, o_mla, gates, w_glu[0], w_mla_o[0], 512, 1024)
    x1, xn2 = _out_proj_norm(merged, w_out[0], x2d, m, norm2[0], seq, 512)
    hid = _ffn_in(xn2, w_ffn_in[0], 2048, 512)
    return _ffn_out(hid, w_ffn_out[0], x1, m, norm_f, seq, 256).reshape(batch, seq, D_MODEL)
```

```python
import functools
import math

import jax
import jax.numpy as jnp
from jax import lax
from jax.experimental import pallas as pl
from jax.experimental.pallas import tpu as pltpu

F32 = jnp.float32
BF16 = jnp.bfloat16

D_MODEL = 2048
GRID_W = 64
EPS = 1e-6
S5_WIDTH = D_MODEL // 2
S5_GROUP = 16
S5_GROUPS = S5_WIDTH // S5_GROUP
S5_STATE = 64
S5_CHUNK = 16
S5_PAIRS = S5_GROUPS // 2
MLA_HEADS = 8
QK_NOPE = 128
QK_ROPE = 64
V_DIM = 128
Q_RANK = 512
KV_RANK = 256
ROPE_BASE = 10000.0
ATTN_SCALE = (QK_NOPE + QK_ROPE) ** -0.5
D_FF = -(-8 * D_MODEL // (3 * 256)) * 256

VMEM_LIMIT_BYTES = 56 * 1024 * 1024
SUBLANES = 8
MM_SUB_ROWS = 512
MERGE_SUB_ROWS = 256


def _cparams(n_axes):
    return pltpu.CompilerParams(dimension_semantics=("arbitrary",) * n_axes,
                                vmem_limit_bytes=VMEM_LIMIT_BYTES)


def _rms(x, g):
    return x * lax.rsqrt(jnp.mean(x * x, axis=-1, keepdims=True) + EPS) * g


def _mod_kernel(cv_ref, w_ref, b_ref, o_ref):
    s = jax.nn.silu(cv_ref[...]).astype(BF16)
    o_ref[...] = jnp.dot(s, w_ref[...].astype(BF16), preferred_element_type=F32) + b_ref[...]


def _modulation(cv, w_mod, b_mod):
    n = w_mod.shape[1]
    tn = 1536
    return pl.pallas_call(
        _mod_kernel,
        grid=(n // tn,),
        in_specs=[pl.BlockSpec((8, D_MODEL), lambda j: (0, 0)),
                  pl.BlockSpec((D_MODEL, tn), lambda j: (0, j)),
                  pl.BlockSpec((1, tn), lambda j: (0, j))],
        out_specs=pl.BlockSpec((8, tn), lambda j: (0, j)),
        out_shape=jax.ShapeDtypeStruct((8, n), F32),
        compiler_params=_cparams(1),
        name="mod",
    )(cv, w_mod, b_mod.reshape(1, n))


MOD_SHIFT1, MOD_SCALE1, MOD_GATE1, MOD_SHIFT2, MOD_SCALE2, MOD_GATE2 = range(6)


def _mod_spec(width, row_of, which, col_of=None):
    col_of = col_of or (lambda *ids: 0)
    return pl.BlockSpec((1, 1, 1, width), lambda *ids: (row_of(*ids), which, 0, col_of(*ids)))


def _norm_mod_kernel(x_ref, c_ref, g_ref, sc_ref, sh_ref, o_ref, *, lat_tiles):
    def emit(src_ref):
        y = _rms(src_ref[...], g_ref[...])
        o_ref[...] = (y * (1.0 + sc_ref[0, 0]) + sh_ref[0, 0]).astype(o_ref.dtype)

    pl.when(pl.program_id(0) < lat_tiles)(lambda: emit(x_ref))
    pl.when(pl.program_id(0) >= lat_tiles)(lambda: emit(c_ref))


def _norm_mod(x2d, c2d, gain, m, rows_per_batch, ctx_row, tm):
    lat_tiles, ctx_tiles = x2d.shape[0] // tm, c2d.shape[0] // tm
    tpb = rows_per_batch // tm

    def row(i):
        return jnp.where(i < lat_tiles, i // tpb, ctx_row)

    return pl.pallas_call(
        functools.partial(_norm_mod_kernel, lat_tiles=lat_tiles),
        grid=(lat_tiles + ctx_tiles,),
        in_specs=[pl.BlockSpec((tm, D_MODEL), lambda i: (jnp.minimum(i, lat_tiles - 1), 0)),
                  pl.BlockSpec((tm, D_MODEL), lambda i: (jnp.maximum(i - lat_tiles, 0), 0)),
                  pl.BlockSpec((1, D_MODEL), lambda i: (0, 0)),
                  _mod_spec(D_MODEL, row, MOD_SCALE1), _mod_spec(D_MODEL, row, MOD_SHIFT1)],
        out_specs=pl.BlockSpec((tm, D_MODEL), lambda i: (i, 0)),
        out_shape=jax.ShapeDtypeStruct((x2d.shape[0] + c2d.shape[0], D_MODEL), BF16),
        compiler_params=_cparams(1),
        name="norm_mod",
    )(x2d, c2d, gain.reshape(1, D_MODEL), m, m)


def _fused_mm(a, weights, epilogue, extras, outs, *, tm, nj, name, rows=None, sub_rows=None):
    t, k = a.shape
    ni = (rows or t) // tm
    nw, ne, no = len(weights), len(extras), len(outs)
    need_cast = [w.dtype != BF16 for w, _, _, _ in weights]
    nt = (((1,), (1,)), ((), ()))

    def kernel(*refs):
        a_ref = refs[0]
        w_refs = refs[1:1 + nw]
        e_refs = refs[1 + nw:1 + nw + ne]
        o_refs = refs[1 + nw + ne:1 + nw + ne + no]
        s_refs = list(refs[1 + nw + ne + no:])
        staged = {idx: s_refs.pop(0) for idx in range(nw) if need_cast[idx]}
        if staged:
            @pl.when(pl.program_id(1) == 0)
            def _():
                for idx, s_ref in staged.items():
                    s_ref[...] = w_refs[idx][...].astype(BF16)

        for r0 in range(0, tm, sub_rows or tm):
            rows = slice(r0, r0 + (sub_rows or tm))
            av = a_ref[rows, :]
            accs = []
            for idx in range(nw):
                w_ref = staged.get(idx, w_refs[idx])
                if weights[idx][2] == "kn":
                    accs.append(jnp.dot(av, w_ref[...], preferred_element_type=F32))
                else:
                    accs.append(lax.dot_general(av, w_ref[...], nt, preferred_element_type=F32))
            epilogue(accs, e_refs, o_refs, rows)

    in_specs = [pl.BlockSpec((tm, k), lambda j, i: (i, 0))]
    scratch = []
    for (w, off, layout, width), cast in zip(weights, need_cast):
        if layout == "kn":
            shape = (k, width)
            in_specs.append(pl.BlockSpec(shape, functools.partial(lambda j, i, off: (0, off + j), off=off)))
        else:
            shape = (width, k)
            in_specs.append(pl.BlockSpec(
                (pl.Element(width), pl.Element(k)),
                functools.partial(lambda j, i, off, width: (pl.multiple_of(off + j * width, SUBLANES), 0),
                                  off=off, width=width)))
        if cast:
            scratch.append(pltpu.VMEM(shape, BF16))
    in_specs += [spec for _, spec in extras]
    return pl.pallas_call(
        kernel,
        grid=(nj, ni),
        in_specs=in_specs,
        out_specs=[spec for _, spec in outs],
        out_shape=[sds for sds, _ in outs],
        scratch_shapes=scratch,
        compiler_params=_cparams(2),
        name=name,
    )(a, *[w[0] for w in weights], *[e for e, _ in extras])


def _proj_cast(xn, w_in_t, row0, n, tm, tn, name, act=None, out_dtype=BF16, rows=None):
    def epi(accs, e_refs, o_refs, rows):
        v = accs[0]
        if act is not None:
            v = act(v)
        o_refs[0][rows, :] = v.astype(out_dtype)

    t = rows or xn.shape[0]
    return _fused_mm(xn, [(w_in_t, row0, "nk", tn)], epi, [],
                     [(jax.ShapeDtypeStruct((t, n), out_dtype), pl.BlockSpec((tm, tn), lambda j, i: (i, j)))],
                     tm=tm, nj=n // tn, name=name, rows=rows)[0]


def _q_path(xn, w_in_t, q_norm, wq2, cos2, sin2, tm, seq, rows):
    nr = MLA_HEADS * QK_ROPE
    nn = MLA_HEADS * QK_NOPE
    lanes = 2 * QK_ROPE

    def epi(accs, e_refs, o_refs, rows):
        qn_ref, w2_ref, cos_ref, sin_ref = e_refs
        cq = _rms(accs[0], qn_ref[...]).astype(BF16)
        q = jnp.dot(cq, w2_ref[...], preferred_element_type=F32)
        o_refs[0][rows, :nn] = (q[:, :nn] * ATTN_SCALE).astype(BF16)
        cos, sin = cos_ref[rows, :] * ATTN_SCALE, sin_ref[rows, :] * ATTN_SCALE
        for c0 in range(0, nr, lanes):
            rope = q[:, nn + c0:nn + c0 + lanes] * cos + q[:, nn + nr + c0:nn + nr + c0 + lanes] * sin
            o_refs[0][rows, nn + c0:nn + c0 + lanes] = rope.astype(BF16)

    pos_tiles = seq // tm
    extras = [(q_norm.reshape(1, Q_RANK), pl.BlockSpec((1, Q_RANK), lambda j, i: (0, 0))),
              (wq2, pl.BlockSpec(wq2.shape, lambda j, i: (0, 0))),
              (cos2, pl.BlockSpec((tm, lanes), lambda j, i: (i % pos_tiles, 0))),
              (sin2, pl.BlockSpec((tm, lanes), lambda j, i: (i % pos_tiles, 0)))]
    outs = [(jax.ShapeDtypeStruct((rows, nn + nr), BF16), pl.BlockSpec((tm, nn + nr), lambda j, i: (i, 0)))]
    return _fused_mm(xn, [(w_in_t, S5_WIDTH, "nk", Q_RANK)], epi, extras, outs, tm=tm, nj=1, name="q_path",
                     rows=rows)[0]


def _kv_path(xn, w_kv_t, kv_norm, w_ukv_bf, cos_sin_k, tm, seq, lat_rows):
    nkv = w_ukv_bf.shape[1]

    def epi(accs, e_refs, o_refs, rows):
        acc = accs[0]
        ckv = _rms(acc[:, :KV_RANK], e_refs[0][...]).astype(BF16)
        o_refs[0][rows, :] = jnp.dot(ckv, e_refs[1][...], preferred_element_type=F32).astype(BF16)
        prod = acc[:, KV_RANK:] * e_refs[2][rows, :]
        o_refs[1][rows, :] = (prod + pltpu.roll(prod, QK_ROPE, 1))[:, :QK_ROPE].astype(BF16)

    t = xn.shape[0]
    pos_tiles, lat_tiles = seq // tm, lat_rows // tm
    extras = [(kv_norm.reshape(1, KV_RANK), pl.BlockSpec((1, KV_RANK), lambda j, i: (0, 0))),
              (w_ukv_bf, pl.BlockSpec(w_ukv_bf.shape, lambda j, i: (0, 0))),
              (cos_sin_k, pl.BlockSpec((tm, 2 * QK_ROPE),
                                       lambda j, i: (jnp.where(i < lat_tiles, i % pos_tiles, pos_tiles), 0)))]
    outs = [(jax.ShapeDtypeStruct((t, nkv), BF16), pl.BlockSpec((tm, nkv), lambda j, i: (i, 0))),
            (jax.ShapeDtypeStruct((t, QK_ROPE), BF16), pl.BlockSpec((tm, QK_ROPE), lambda j, i: (i, 0)))]
    return _fused_mm(xn, [(w_kv_t, 0, "nk", w_kv_t.shape[0])], epi, extras, outs, tm=tm, nj=1, name="kv_path")


ATTN_SUB_ROWS = 256


def _attn_kernel(qn_ref, qr_ref, kvl_ref, kvc_ref, krl_ref, krc_ref, o_ref, k_scr, v_scr, *, seq, ctx):
    dk = QK_NOPE + QK_ROPE

    @pl.when(pl.program_id(2) == 0)
    def _():
        for h in range(2):
            base = h * (QK_NOPE + V_DIM)
            k_scr[h, :seq, :QK_NOPE] = kvl_ref[:, base:base + QK_NOPE]
            k_scr[h, seq:, :QK_NOPE] = kvc_ref[:, base:base + QK_NOPE]
            k_scr[h, :seq, QK_NOPE:dk] = krl_ref[...]
            k_scr[h, seq:, QK_NOPE:dk] = krc_ref[...]
            v_scr[h, :seq, :V_DIM] = kvl_ref[:, base + QK_NOPE:base + QK_NOPE + V_DIM]
            v_scr[h, seq:, :V_DIM] = kvc_ref[:, base + QK_NOPE:base + QK_NOPE + V_DIM]
            ones_col = lax.broadcasted_iota(jnp.int32, (seq + ctx, V_DIM), 1) == 0
            v_scr[h, :, V_DIM:] = jnp.where(ones_col, 1.0, 0.0).astype(BF16)

    for r0 in range(0, qn_ref.shape[0], ATTN_SUB_ROWS):
        rows = slice(r0, r0 + ATTN_SUB_ROWS)
        for h in range(2):
            q = jnp.concatenate([qn_ref[rows, h * QK_NOPE:(h + 1) * QK_NOPE],
                                 qr_ref[rows, h * QK_ROPE:(h + 1) * QK_ROPE]], axis=1)
            s = lax.dot_general(q, k_scr[h], (((1,), (1,)), ((), ())), preferred_element_type=F32)
            m = jnp.max(s, axis=-1, keepdims=True)
            p = jnp.exp((s - m).astype(BF16))
            ol = jnp.dot(p, v_scr[h], preferred_element_type=F32)
            o_ref[rows, h * V_DIM:(h + 1) * V_DIM] = (ol[:, :V_DIM] / ol[:, V_DIM:V_DIM + 1]).astype(o_ref.dtype)


def _attention(q, kv, kr, batch, seq, ctx, tq):
    nq = seq // tq
    ctx0 = batch * seq // ctx
    nn_blocks = MLA_HEADS * QK_NOPE // (2 * QK_NOPE)
    dk = QK_NOPE + QK_ROPE
    hw = 2 * (QK_NOPE + V_DIM)
    return pl.pallas_call(
        functools.partial(_attn_kernel, seq=seq, ctx=ctx),
        grid=(batch, MLA_HEADS // 2, nq),
        in_specs=[pl.BlockSpec((tq, 2 * QK_NOPE), lambda b, hp, qi: (b * nq + qi, hp)),
                  pl.BlockSpec((tq, 2 * QK_ROPE), lambda b, hp, qi: (b * nq + qi, 2 * nn_blocks + hp)),
                  pl.BlockSpec((seq, hw), lambda b, hp, qi: (b, hp)),
                  pl.BlockSpec((ctx, hw), lambda b, hp, qi: (ctx0 + b, hp)),
                  pl.BlockSpec((seq, QK_ROPE), lambda b, hp, qi: (b, 0)),
                  pl.BlockSpec((ctx, QK_ROPE), lambda b, hp, qi: (ctx0 + b, 0))],
        out_specs=pl.BlockSpec((tq, 2 * V_DIM), lambda b, hp, qi: (b * nq + qi, hp)),
        out_shape=jax.ShapeDtypeStruct((batch * seq, MLA_HEADS * V_DIM), BF16),
        scratch_shapes=[pltpu.VMEM((2, seq + ctx, dk), BF16), pltpu.VMEM((2, seq + ctx, 2 * V_DIM), BF16)],
        compiler_params=_cparams(3),
        name="attention",
    )(q, q, kv, kv, kr, kr)


def _s5_prep_pair(pp, are_ref, aim_ref, ldt_ref, bre_ref, bim_ref, cre_ref, cim_ref,
                  toep_ref, wsr_ref, wsi_ref, wor_ref, woi_ref, atr_ref, ati_ref):
    tc, g = S5_CHUNK, S5_GROUP
    w = tc * g
    lane = lax.broadcasted_iota(jnp.int32, (g, 2 * S5_STATE), 1)
    in_group = (lane < S5_STATE, lane >= S5_STATE)
    lane_w = lax.broadcasted_iota(jnp.int32, (g, w), 1)
    nt = (((1,), (1,)), ((), ()))
    toep_rows = [[jnp.zeros((g, w), F32) for _ in range(tc)] for _ in range(2)]
    for d in range(2):
        lr, li = are_ref[d, pp], aim_ref[d, pp]
        dt = jnp.exp(ldt_ref[d, pp])
        mag = jnp.exp(lr * dt)
        ab_re, ab_im = mag * jnp.cos(li * dt), mag * jnp.sin(li * dt)
        den = lr * lr + li * li
        nr, ni = ab_re - 1.0, ab_im
        co_re = (nr * lr + ni * li) / den
        co_im = (ni * lr - nr * li) / den
        br, bi = bre_ref[d, pp], bim_ref[d, pp]
        bb_re = co_re * br - co_im * bi
        bb_im = co_re * bi + co_im * br
        pw = [(jnp.ones_like(ab_re), jnp.zeros_like(ab_re))]
        for _ in range(tc):
            pr, pi = pw[-1]
            pw.append((pr * ab_re - pi * ab_im, pr * ab_im + pi * ab_re))
        cr, ci = cre_ref[d, pp], cim_ref[d, pp]
        ca = [(cr * pr - ci * pi, cr * pi + ci * pr) for pr, pi in pw]
        taus = list(range(tc))[::-1] if d else list(range(tc))
        y_re = jnp.concatenate([ca[t][0] for t in taus], axis=0).astype(BF16)
        y_im = jnp.concatenate([ca[t][1] for t in taus], axis=0).astype(BF16)
        for j in range(2):
            x_re = jnp.where(in_group[j], bb_re, 0.0)
            x_im = jnp.where(in_group[j], bb_im, 0.0)
            kt = (lax.dot_general(x_re.astype(BF16), y_re, nt, preferred_element_type=F32)
                  - lax.dot_general(x_im.astype(BF16), y_im, nt, preferred_element_type=F32))
            for r in range(tc):
                sh = (r + 1) * g if d else r * g
                blk = pltpu.roll(kt, sh % w, 1) if sh % w else kt
                keep = (lane_w < sh) if d else (lane_w >= sh)
                toep_rows[j][r] = toep_rows[j][r] + jnp.where(keep, blk, 0.0)
        for r in range(tc):
            pr, pi = pw[r] if d else pw[tc - 1 - r]
            w_re = bb_re * pr - bb_im * pi
            w_im = bb_re * pi + bb_im * pr
            car, cai = ca[tc - r] if d else ca[r + 1]
            for j in range(2):
                rows = slice(j * w + r * g, j * w + (r + 1) * g)
                wsr_ref[d, pp, rows, :] = jnp.where(in_group[j], w_re, 0.0).astype(BF16)
                wsi_ref[d, pp, rows, :] = jnp.where(in_group[j], w_im, 0.0).astype(BF16)
                wor_ref[d, pp, rows, :] = jnp.where(in_group[j], car, 0.0).astype(BF16)
                woi_ref[d, pp, rows, :] = jnp.where(in_group[j], -cai, 0.0).astype(BF16)
        atr_ref[d, pp] = pw[tc][0]
        ati_ref[d, pp] = pw[tc][1]
    for j in range(2):
        toep_ref[pp, j] = jnp.concatenate(toep_rows[j], axis=0).astype(BF16)


S5_PREP_PAIRS = 4


def _s5_prep_kernel(*refs):
    for pp in range(S5_PREP_PAIRS):
        _s5_prep_pair(pp, *refs)


def _s5_prep(params):
    tc = S5_CHUNK
    wide = 2 * tc * S5_GROUP
    sl = 2 * S5_STATE

    def pair_lanes(v):
        return v.astype(F32).reshape(2, S5_PAIRS, 1, sl)

    def pair_rows(v):
        rows = v.shape[2]
        return v.astype(F32).reshape(2, S5_PAIRS, 2, rows, S5_STATE).transpose(0, 1, 3, 2, 4).reshape(
            2, S5_PAIRS, rows, sl)

    ldt = jnp.broadcast_to(params['s5_log_dt'].astype(F32)[:, :, None], (2, S5_GROUPS, S5_STATE))
    ins = [pair_lanes(params['s5_a_re']), pair_lanes(params['s5_a_im']), pair_lanes(ldt),
           pair_rows(params['s5_b_re'].transpose(0, 1, 3, 2)), pair_rows(params['s5_b_im'].transpose(0, 1, 3, 2)),
           pair_rows(params['s5_c_re']), pair_rows(params['s5_c_im'])]
    pp = S5_PREP_PAIRS
    vec_spec = pl.BlockSpec((2, pp, 1, sl), lambda k: (0, k, 0, 0))
    mat_spec = pl.BlockSpec((2, pp, S5_GROUP, sl), lambda k: (0, k, 0, 0))
    w_spec = pl.BlockSpec((2, pp, wide, sl), lambda k: (0, k, 0, 0))
    w_sds = jax.ShapeDtypeStruct((2, S5_PAIRS, wide, sl), BF16)
    a_sds = jax.ShapeDtypeStruct((2, S5_PAIRS, 1, sl), F32)
    return pl.pallas_call(
        _s5_prep_kernel,
        grid=(S5_PAIRS // pp,),
        in_specs=[vec_spec] * 3 + [mat_spec] * 4,
        out_specs=[pl.BlockSpec((pp, 2, wide // 2, wide // 2), lambda k: (k, 0, 0, 0))] + [w_spec] * 4 + [vec_spec] * 2,
        out_shape=[jax.ShapeDtypeStruct((S5_PAIRS, 2, wide // 2, wide // 2), BF16)] + [w_sds] * 4 + [a_sds] * 2,
        compiler_params=_cparams(1),
        name="s5_prep",
    )(*ins)


S5_STEP_PAIRS = 4
def _s5_state_rows(batch, n_lat, n_ctx):
    def pitch(n):
        p = -(-n // SUBLANES)
        return SUBLANES * (p + 1 - p % 2)

    lat_pitch, ctx_pitch = pitch(n_lat), pitch(n_ctx)
    ctx_base = batch * lat_pitch
    return lat_pitch, ctx_pitch, ctx_base, ctx_base + batch * ctx_pitch


def _s5_kernel(ul_ref, uc_ref, dsk_ref, *refs, batch, seq, ctx):
    toep_ref, wsr, wsi, wor, woi, atr, ati = refs[:7]
    o_ref = refs[7]
    wscr, zscr, upscr, sre, sim, hre, him = refs[8:]
    tc = S5_CHUNK
    n_lat, n_ctx = seq // tc, ctx // tc
    rows_lat = n_lat * batch
    lat_pitch, ctx_pitch, ctx_base, _ = _s5_state_rows(batch, n_lat, n_ctx)
    w = tc * S5_GROUP
    lanes = S5_STEP_PAIRS * 2 * S5_GROUP

    def scatter_tiles(xt, r, col0):
        for kk in range(S5_STEP_PAIRS):
            for j in range(2):
                ch = (2 * kk + j) * S5_GROUP
                wscr[kk, j * w + r * S5_GROUP:j * w + (r + 1) * S5_GROUP, col0:col0 + lanes] = xt[ch:ch + S5_GROUP, :]

    for r in range(tc):
        for b in range(batch):
            x = ul_ref[pl.ds(b * seq + r, n_lat, stride=tc), :]
            scatter_tiles(x.T, r, b * n_lat)
        xc = jnp.concatenate([uc_ref[pl.ds(b * ctx + r, n_ctx, stride=tc), :] for b in range(batch)]
                             + [jnp.zeros((lanes - batch * n_ctx, lanes), F32)], axis=0)
        scatter_tiles(xc.T, r, rows_lat)

    chains = [(kk, d) for kk in range(S5_STEP_PAIRS) for d in range(2)]
    for kk in range(S5_STEP_PAIRS):
        upscr[kk] = wscr[kk].T.astype(BF16)
        for d in range(2):
            for dst, wst in ((sre, wsr), (sim, wsi)):
                s = jnp.dot(upscr[kk], wst[d, kk], preferred_element_type=F32)
                for b in range(batch):
                    dst[kk, d, b * lat_pitch:b * lat_pitch + n_lat, :] = s[b * n_lat:(b + 1) * n_lat]
                    dst[kk, d, ctx_base + b * ctx_pitch:ctx_base + b * ctx_pitch + n_ctx, :] = (
                        s[rows_lat + b * n_ctx:rows_lat + (b + 1) * n_ctx])

    ctx_rows = [pl.ds(ctx_base + c, batch, stride=ctx_pitch) for c in range(n_ctx)]
    lat_rows = [pl.ds(c, batch, stride=lat_pitch) for c in range(n_lat)]
    order = (ctx_rows + lat_rows, ctx_rows[::-1] + lat_rows[::-1])
    coef = {(kk, d): (atr[d, kk], ati[d, kk]) for kk, d in chains}
    state = {ch: (jnp.zeros((batch, 2 * S5_STATE), F32), jnp.zeros((batch, 2 * S5_STATE), F32)) for ch in chains}
    for t in range(n_ctx + n_lat):
        for kk, d in chains:
            rows = order[d][t]
            (ar, ai), (h_re, h_im) = coef[kk, d], state[kk, d]
            hre[kk, d, rows, :] = h_re
            him[kk, d, rows, :] = h_im
            state[kk, d] = (ar * h_re - ai * h_im + sre[kk, d, rows, :],
                            ar * h_im + ai * h_re + sim[kk, d, rows, :])

    for kk in range(S5_STEP_PAIRS):
        ul = upscr[kk, :rows_lat, :]
        y = ul.astype(F32) * dsk_ref[kk]
        y = y + jnp.concatenate(
            [jnp.dot(ul[:, :w], toep_ref[kk, 0], preferred_element_type=F32),
             jnp.dot(ul[:, w:], toep_ref[kk, 1], preferred_element_type=F32)], axis=1)
        nt = (((1,), (1,)), ((), ()))
        for d in range(2):
            h_r = jnp.concatenate([hre[kk, d, b * lat_pitch:b * lat_pitch + n_lat, :] for b in range(batch)], axis=0)
            h_i = jnp.concatenate([him[kk, d, b * lat_pitch:b * lat_pitch + n_lat, :] for b in range(batch)], axis=0)
            y = y + lax.dot_general(h_r.astype(BF16), wor[d, kk], nt, preferred_element_type=F32)
            y = y + lax.dot_general(h_i.astype(BF16), woi[d, kk], nt, preferred_element_type=F32)
        yt = y.T
        for b in range(batch):
            for s in range(tc):
                for j in range(2):
                    ch = (2 * kk + j) * S5_GROUP
                    zscr[b * tc + s, ch:ch + S5_GROUP, :] = yt[j * w + s * S5_GROUP:j * w + (s + 1) * S5_GROUP,
                                                               b * n_lat:(b + 1) * n_lat]

    for b in range(batch):
        for s in range(tc):
            o_ref[pl.ds(b * seq + s, n_lat, stride=tc), :] = jax.nn.gelu(zscr[b * tc + s].T)


def _s5_mixer(u, params, batch, seq, ctx):
    tc = S5_CHUNK
    n_lat, n_ctx = seq // tc, ctx // tc
    sp = S5_STEP_PAIRS
    lanes = sp * 2 * S5_GROUP
    assert n_lat == lanes and batch * n_ctx <= lanes
    rows_lat = n_lat * batch
    rows_all = rows_lat + lanes
    state_rows = _s5_state_rows(batch, n_lat, n_ctx)[3]
    wide = 2 * tc * S5_GROUP
    dsk = jnp.broadcast_to(params['s5_d'].astype(F32).reshape(S5_PAIRS, 2, 1, S5_GROUP),
                           (S5_PAIRS, 2, tc, S5_GROUP)).reshape(S5_PAIRS, 1, wide)
    weights = _s5_prep(params)
    w_spec = pl.BlockSpec((2, sp, wide, 2 * S5_STATE), lambda k: (0, k, 0, 0))
    a_spec = pl.BlockSpec((2, sp, 1, 2 * S5_STATE), lambda k: (0, k, 0, 0))
    specs = [pl.BlockSpec((sp, 2, wide // 2, wide // 2), lambda k: (k, 0, 0, 0))] + [w_spec] * 4 + [a_spec] * 2
    return pl.pallas_call(
        functools.partial(_s5_kernel, batch=batch, seq=seq, ctx=ctx),
        grid=(S5_PAIRS // sp,),
        in_specs=[pl.BlockSpec((batch * seq, lanes), lambda k: (0, k)),
                  pl.BlockSpec((batch * ctx, lanes), lambda k: (seq // ctx, k)),
                  pl.BlockSpec((sp, 1, wide), lambda k: (k, 0, 0))] + specs,
        out_specs=pl.BlockSpec((batch * seq, lanes), lambda k: (0, k)),
        out_shape=jax.ShapeDtypeStruct((batch * seq, S5_WIDTH), F32),
        scratch_shapes=[pltpu.VMEM((sp, wide, rows_all), F32),
                        pltpu.VMEM((batch * tc, lanes, n_lat), F32),
                        pltpu.VMEM((sp, rows_all, wide), BF16)]
                       + [pltpu.VMEM((sp, 2, state_rows, 2 * S5_STATE), F32) for _ in range(4)],
        compiler_params=_cparams(1),
        name="s5",
    )(u, u, dsk, *weights)


def _merge_kernel(z5_ref, o_ref_in, gs_ref, gm_ref, wa_ref, wb_ref, wm_ref, out_ref, wa_s, wb_s, wm_s):
    @pl.when(pl.program_id(1) == 0)
    def _():
        wa_s[...] = wa_ref[...].astype(BF16)
        wb_s[...] = wb_ref[...].astype(BF16)
        wm_s[...] = wm_ref[...].astype(BF16)

    sub = min(MERGE_SUB_ROWS, out_ref.shape[0])
    for r0 in range(0, out_ref.shape[0], sub):
        rows = slice(r0, r0 + sub)
        z = z5_ref[rows, :].astype(BF16)
        a = jnp.dot(z, wa_s[...], preferred_element_type=F32)
        b = jnp.dot(z, wb_s[...], preferred_element_type=F32)
        mla = jnp.dot(o_ref_in[rows, :], wm_s[...], preferred_element_type=F32)
        merged = gs_ref[rows, :].astype(F32) * (a * jax.nn.sigmoid(b)) + gm_ref[rows, :].astype(F32) * mla
        out_ref[rows, :] = merged.astype(out_ref.dtype)


def _merge(y5, o_mla, gates, w_glu, w_mla_o, tm, tn):
    t = y5.shape[0]
    nj = D_MODEL // tn
    return pl.pallas_call(
        _merge_kernel,
        grid=(nj, t // tm),
        in_specs=[pl.BlockSpec((tm, S5_WIDTH), lambda j, i: (i, 0)),
                  pl.BlockSpec((tm, MLA_HEADS * V_DIM), lambda j, i: (i, 0)),
                  pl.BlockSpec((tm, tn), lambda j, i: (i, j)),
                  pl.BlockSpec((tm, tn), lambda j, i: (i, nj + j)),
                  pl.BlockSpec((S5_WIDTH, tn), lambda j, i: (0, j)),
                  pl.BlockSpec((S5_WIDTH, tn), lambda j, i: (0, nj + j)),
                  pl.BlockSpec((MLA_HEADS * V_DIM, tn), lambda j, i: (0, j))],
        out_specs=pl.BlockSpec((tm, tn), lambda j, i: (i, j)),
        out_shape=jax.ShapeDtypeStruct((t, D_MODEL), BF16),
        scratch_shapes=[pltpu.VMEM((S5_WIDTH, tn), BF16), pltpu.VMEM((S5_WIDTH, tn), BF16),
                        pltpu.VMEM((MLA_HEADS * V_DIM, tn), BF16)],
        compiler_params=_cparams(2),
        name="merge",
    )(y5, o_mla, gates, gates, w_glu, w_glu, w_mla_o)


def _out_proj_norm_kernel(a_ref, w_ref, x_ref, g1_ref, n2_ref, sc_ref, sh_ref, x1_ref, xn_ref, w_s):
    @pl.when(pl.program_id(0) == 0)
    def _():
        w_s[...] = w_ref[...].astype(BF16)

    sub = min(MERGE_SUB_ROWS, x1_ref.shape[0])
    for r0 in range(0, x1_ref.shape[0], sub):
        rows = slice(r0, r0 + sub)
        x1 = x_ref[rows, :] + g1_ref[0, 0] * jnp.dot(a_ref[rows, :], w_s[...], preferred_element_type=F32)
        x1_ref[rows, :] = x1
        xn_ref[rows, :] = (_rms(x1, n2_ref[...]) * (1.0 + sc_ref[0, 0]) + sh_ref[0, 0]).astype(xn_ref.dtype)


def _out_proj_norm(merged, w_out, x_res, m, gain2, rows_per_batch, tm):
    t, k = merged.shape
    tpb = rows_per_batch // tm
    row = pl.BlockSpec((tm, D_MODEL), lambda i: (i, 0))
    gate1, scale2, shift2 = (_mod_spec(D_MODEL, lambda i: i // tpb, which)
                             for which in (MOD_GATE1, MOD_SCALE2, MOD_SHIFT2))
    return pl.pallas_call(
        _out_proj_norm_kernel,
        grid=(t // tm,),
        in_specs=[pl.BlockSpec((tm, k), lambda i: (i, 0)),
                  pl.BlockSpec((k, D_MODEL), lambda i: (0, 0), pipeline_mode=pl.Buffered(1)),
                  row, gate1, pl.BlockSpec((1, D_MODEL), lambda i: (0, 0)), scale2, shift2],
        out_specs=[row, row],
        out_shape=[jax.ShapeDtypeStruct((t, D_MODEL), F32), jax.ShapeDtypeStruct((t, D_MODEL), BF16)],
        scratch_shapes=[pltpu.VMEM((k, D_MODEL), BF16)],
        compiler_params=_cparams(1),
        name="out_proj",
    )(merged, w_out, x_res, m, gain2.reshape(1, D_MODEL), m, m)


def _ffn_in(xn, w_ffn_in, tm, tn):
    def epi(accs, e_refs, o_refs, rows):
        o_refs[0][rows, :] = (jax.nn.silu(accs[0]) * accs[1]).astype(BF16)

    t = xn.shape[0]
    nj = D_FF // tn
    outs = [(jax.ShapeDtypeStruct((t, D_FF), BF16), pl.BlockSpec((tm, tn), lambda j, i: (i, j)))]
    return _fused_mm(xn, [(w_ffn_in, 0, "kn", tn), (w_ffn_in, nj, "kn", tn)], epi, [], outs,
                     tm=tm, nj=nj, name="ffn_in", sub_rows=MM_SUB_ROWS)[0]


def _ffn_out_kernel(h_ref, w_ref, x_ref, g2_ref, nf_ref, o_ref, w_s, *, last):
    @pl.when(pl.program_id(0) == 0)
    def _():
        w_s[...] = w_ref[...].astype(BF16)

    y = x_ref[...] + g2_ref[0, 0] * jnp.dot(h_ref[...], w_s[...], preferred_element_type=F32)
    o_ref[...] = _rms(y, nf_ref[...]) if last else y


def _ffn_out(hid, w_ffn_out, x_res, m, norm_f, rows_per_batch, tm):
    t, k = hid.shape
    kh = k // 2
    tpb = rows_per_batch // tm
    row = pl.BlockSpec((tm, D_MODEL), lambda i: (i, 0))
    y = x_res
    for half in range(2):
        y = pl.pallas_call(
            functools.partial(_ffn_out_kernel, last=half == 1),
            grid=(t // tm,),
            in_specs=[pl.BlockSpec((tm, kh), functools.partial(lambda i, half: (i, half), half=half)),
                      pl.BlockSpec((kh, D_MODEL), functools.partial(lambda i, half: (half, 0), half=half),
                                   pipeline_mode=pl.Buffered(1)),
                      row, _mod_spec(D_MODEL, lambda i: i // tpb, MOD_GATE2),
                      pl.BlockSpec((1, D_MODEL), lambda i: (0, 0))],
            out_specs=row,
            out_shape=jax.ShapeDtypeStruct((t, D_MODEL), F32),
            scratch_shapes=[pltpu.VMEM((kh, D_MODEL), BF16)],
            compiler_params=_cparams(1),
            name="ffn_out",
        )(hid, w_ffn_out, y, m, norm_f.reshape(1, D_MODEL))
    return y


def _rope_rot_cols(w):
    k = w.shape[0]
    ws = w.reshape(k, -1, 2, 2, QK_ROPE // 4)
    return jnp.stack([-ws[:, :, :, 1, :], ws[:, :, :, 0, :]], axis=3).reshape(k, -1)


def _rope_tables(n_tokens):
    rows = n_tokens // GRID_W
    row = jnp.repeat(jnp.arange(rows, dtype=F32), GRID_W)
    col = jnp.tile(jnp.arange(GRID_W, dtype=F32), rows)
    n_freq = QK_ROPE // 4
    inv = ROPE_BASE ** (-jnp.arange(n_freq, dtype=F32) / n_freq)
    ang = jnp.stack([row[:, None] * inv, col[:, None] * inv], axis=1)
    cos = jnp.broadcast_to(jnp.cos(ang)[:, :, None, :], (n_tokens, 2, 2, n_freq)).reshape(n_tokens, QK_ROPE)
    sin = jnp.broadcast_to(jnp.sin(ang)[:, :, None, :], (n_tokens, 2, 2, n_freq)).reshape(n_tokens, QK_ROPE)
    return cos, sin


def kernel(x, c, ctx, c_ctx, w_mod, b_mod, norm1, norm2, w_in, s5_a_re, s5_a_im, s5_log_dt, s5_b_re, s5_b_im,
           s5_c_re, s5_c_im, s5_d, w_glu, q_norm, kv_norm, w_uq, w_ukv, w_mla_o, w_out, w_ffn_in, w_ffn_out,
           norm_f):
    batch, seq, _ = x.shape
    n_ctx = ctx.shape[1]
    assert w_mod.shape[0] == 1, "single-layer block"
    p = dict(s5_a_re=s5_a_re[0], s5_a_im=s5_a_im[0], s5_log_dt=s5_log_dt[0], s5_b_re=s5_b_re[0],
             s5_b_im=s5_b_im[0], s5_c_re=s5_c_re[0], s5_c_im=s5_c_im[0], s5_d=s5_d[0])
    w_in_t = w_in.reshape(w_in.shape[1:]).T

    cv = jnp.concatenate([c, c_ctx[None], jnp.zeros((8 - batch - 1, D_MODEL), F32)], axis=0)
    m = _modulation(cv, w_mod[0], b_mod[0]).reshape(8, 6, 1, D_MODEL)

    lat_rows = batch * seq
    x2d = x.reshape(lat_rows, D_MODEL)
    c2d = ctx.reshape(batch * n_ctx, D_MODEL)
    xn = _norm_mod(x2d, c2d, norm1[0], m, seq, batch, 1024)

    kv_lo = S5_WIDTH + Q_RANK
    w_kv_rows = w_in_t[kv_lo:kv_lo + KV_RANK + QK_ROPE]
    w_kv_t = jnp.concatenate([w_kv_rows, _rope_rot_cols(w_kv_rows[KV_RANK:].T).T], axis=0)
    wq = w_uq[0].reshape(Q_RANK, MLA_HEADS, QK_NOPE + QK_ROPE)
    wq_rope = wq[:, :, QK_NOPE:].reshape(Q_RANK, MLA_HEADS * QK_ROPE)
    wq2 = jnp.concatenate([wq[:, :, :QK_NOPE].reshape(Q_RANK, MLA_HEADS * QK_NOPE), wq_rope,
                           _rope_rot_cols(wq_rope)], axis=1).astype(BF16)
    w_ukv_bf = w_ukv[0].astype(BF16)
    tm = 1024
    cos, sin = _rope_tables(seq)
    cos_sin_k = jnp.concatenate([jnp.concatenate([cos, sin], axis=1),
                                 jnp.concatenate([jnp.ones((tm, QK_ROPE), F32), jnp.zeros((tm, QK_ROPE), F32)], axis=1)],
                                axis=0)

    u = _proj_cast(xn, w_in_t, 0, S5_WIDTH, tm, 1024, "u_proj", out_dtype=F32)
    q = _q_path(xn, w_in_t, q_norm[0], wq2, jnp.tile(cos, (1, 2)), jnp.tile(sin, (1, 2)), tm, seq, lat_rows)
    kv, kr = _kv_path(xn, w_kv_t, kv_norm[0], w_ukv_bf, cos_sin_k, tm, seq, lat_rows)
    gates = _proj_cast(xn, w_in_t, kv_lo + KV_RANK + QK_ROPE, 2 * D_MODEL, 2 * tm, 1024, "gates",
                       act=jax.nn.sigmoid, rows=lat_rows)

    z5 = _s5_mixer(u, p, batch, seq, n_ctx)
    o_mla = _attention(q, kv, kr, batch, seq, n_ctx, 2048)

    merged = _merge(z5, o_mla, gates, w_glu[0], w_mla_o[0], 512, 1024)
    x1, xn2 = _out_proj_norm(merged, w_out[0], x2d, m, norm2[0], seq, 512)
    hid = _ffn_in(xn2, w_ffn_in[0], 2048, 512)
    return _ffn_out(hid, w_ffn_out[0], x1, m, norm_f, seq, 256).reshape(batch, seq, D_MODEL)
```

```python
import functools
import math

import jax
import jax.numpy as jnp
from jax import lax
from jax.experimental import pallas as pl
from jax.experimental.pallas import tpu as pltpu

F32 = jnp.float32
BF16 = jnp.bfloat16

D_MODEL = 2048
GRID_W = 64
EPS = 1e-6
S5_WIDTH = D_MODEL // 2
S5_GROUP = 16
S5_GROUPS = S5_WIDTH // S5_GROUP
S5_STATE = 64
S5_CHUNK = 16
S5_PAIRS = S5_GROUPS // 2
MLA_HEADS = 8
QK_NOPE = 128
QK_ROPE = 64
V_DIM = 128
Q_RANK = 512
KV_RANK = 256
ROPE_BASE = 10000.0
ATTN_SCALE = (QK_NOPE + QK_ROPE) ** -0.5
D_FF = -(-8 * D_MODEL // (3 * 256)) * 256

VMEM_LIMIT_BYTES = 56 * 1024 * 1024
SUBLANES = 8
MM_SUB_ROWS = 512
MERGE_SUB_ROWS = 256


def _cparams(n_axes):
    return pltpu.CompilerParams(dimension_semantics=("arbitrary",) * n_axes,
                                vmem_limit_bytes=VMEM_LIMIT_BYTES)


def _rms(x, g):
    return x * lax.rsqrt(jnp.mean(x * x, axis=-1, keepdims=True) + EPS) * g


def _mod_kernel(cv_ref, w_ref, b_ref, o_ref):
    s = jax.nn.silu(cv_ref[...]).astype(BF16)
    o_ref[...] = jnp.dot(s, w_ref[...].astype(BF16), preferred_element_type=F32) + b_ref[...]


def _modulation(cv, w_mod, b_mod):
    n = w_mod.shape[1]
    tn = 1536
    return pl.pallas_call(
        _mod_kernel,
        grid=(n // tn,),
        in_specs=[pl.BlockSpec((8, D_MODEL), lambda j: (0, 0)),
                  pl.BlockSpec((D_MODEL, tn), lambda j: (0, j)),
                  pl.BlockSpec((1, tn), lambda j: (0, j))],
        out_specs=pl.BlockSpec((8, tn), lambda j: (0, j)),
        out_shape=jax.ShapeDtypeStruct((8, n), F32),
        compiler_params=_cparams(1),
        name="mod",
    )(cv, w_mod, b_mod.reshape(1, n))


MOD_SHIFT1, MOD_SCALE1, MOD_GATE1, MOD_SHIFT2, MOD_SCALE2, MOD_GATE2 = range(6)


def _mod_spec(width, row_of, which, col_of=None):
    col_of = col_of or (lambda *ids: 0)
    return pl.BlockSpec((1, 1, 1, width), lambda *ids: (row_of(*ids), which, 0, col_of(*ids)))


def _norm_u_kernel(x_ref, c_ref, g_ref, sc_ref, sh_ref, w_ref, xn_ref, u_ref, w_s, *, lat_tiles):
    @pl.when(pl.program_id(0) == 0)
    def _():
        w_s[...] = w_ref[...].astype(BF16)

    def emit(src_ref):
        y = _rms(src_ref[...], g_ref[...])
        xn_ref[...] = (y * (1.0 + sc_ref[0, 0]) + sh_ref[0, 0]).astype(xn_ref.dtype)

    pl.when(pl.program_id(0) < lat_tiles)(lambda: emit(x_ref))
    pl.when(pl.program_id(0) >= lat_tiles)(lambda: emit(c_ref))
    u_ref[...] = lax.dot_general(xn_ref[...], w_s[...], (((1,), (1,)), ((), ())), preferred_element_type=F32)


def _norm_u_proj(x2d, c2d, gain, m, w_in_t, rows_per_batch, ctx_row, tm):
    lat_tiles, ctx_tiles = x2d.shape[0] // tm, c2d.shape[0] // tm
    tpb = rows_per_batch // tm
    t_all = x2d.shape[0] + c2d.shape[0]

    def row(i):
        return jnp.where(i < lat_tiles, i // tpb, ctx_row)

    once = pl.Buffered(1)
    return pl.pallas_call(
        functools.partial(_norm_u_kernel, lat_tiles=lat_tiles),
        grid=(lat_tiles + ctx_tiles,),
        in_specs=[pl.BlockSpec((tm, D_MODEL), lambda i: (jnp.minimum(i, lat_tiles - 1), 0)),
                  pl.BlockSpec((tm, D_MODEL), lambda i: (jnp.maximum(i - lat_tiles, 0), 0)),
                  pl.BlockSpec((1, D_MODEL), lambda i: (0, 0)),
                  _mod_spec(D_MODEL, row, MOD_SCALE1), _mod_spec(D_MODEL, row, MOD_SHIFT1),
                  pl.BlockSpec((S5_WIDTH, D_MODEL), lambda i: (0, 0), pipeline_mode=once)],
        out_specs=[pl.BlockSpec((tm, D_MODEL), lambda i: (i, 0)), pl.BlockSpec((tm, S5_WIDTH), lambda i: (i, 0))],
        out_shape=[jax.ShapeDtypeStruct((t_all, D_MODEL), BF16), jax.ShapeDtypeStruct((t_all, S5_WIDTH), F32)],
        scratch_shapes=[pltpu.VMEM((S5_WIDTH, D_MODEL), BF16)],
        compiler_params=_cparams(1),
        name="norm_u_proj",
    )(x2d, c2d, gain.reshape(1, D_MODEL), m, m, w_in_t)


def _fused_mm(a, weights, epilogue, extras, outs, *, tm, nj, name, rows=None, sub_rows=None):
    t, k = a.shape
    ni = (rows or t) // tm
    nw, ne, no = len(weights), len(extras), len(outs)
    need_cast = [w.dtype != BF16 for w, _, _, _ in weights]
    nt = (((1,), (1,)), ((), ()))

    def kernel(*refs):
        a_ref = refs[0]
        w_refs = refs[1:1 + nw]
        e_refs = refs[1 + nw:1 + nw + ne]
        o_refs = refs[1 + nw + ne:1 + nw + ne + no]
        s_refs = list(refs[1 + nw + ne + no:])
        staged = {idx: s_refs.pop(0) for idx in range(nw) if need_cast[idx]}
        if staged:
            @pl.when(pl.program_id(1) == 0)
            def _():
                for idx, s_ref in staged.items():
                    s_ref[...] = w_refs[idx][...].astype(BF16)

        for r0 in range(0, tm, sub_rows or tm):
            rows = slice(r0, r0 + (sub_rows or tm))
            av = a_ref[rows, :]
            accs = []
            for idx in range(nw):
                w_ref = staged.get(idx, w_refs[idx])
                if weights[idx][2] == "kn":
                    accs.append(jnp.dot(av, w_ref[...], preferred_element_type=F32))
                else:
                    accs.append(lax.dot_general(av, w_ref[...], nt, preferred_element_type=F32))
            epilogue(accs, e_refs, o_refs, rows)

    in_specs = [pl.BlockSpec((tm, k), lambda j, i: (i, 0))]
    scratch = []
    for (w, off, layout, width), cast in zip(weights, need_cast):
        if layout == "kn":
            shape = (k, width)
            in_specs.append(pl.BlockSpec(shape, functools.partial(lambda j, i, off: (0, off + j), off=off)))
        else:
            shape = (width, k)
            in_specs.append(pl.BlockSpec(
                (pl.Element(width), pl.Element(k)),
                functools.partial(lambda j, i, off, width: (pl.multiple_of(off + j * width, SUBLANES), 0),
                                  off=off, width=width)))
        if cast:
            scratch.append(pltpu.VMEM(shape, BF16))
    in_specs += [spec for _, spec in extras]
    return pl.pallas_call(
        kernel,
        grid=(nj, ni),
        in_specs=in_specs,
        out_specs=[spec for _, spec in outs],
        out_shape=[sds for sds, _ in outs],
        scratch_shapes=scratch,
        compiler_params=_cparams(2),
        name=name,
    )(a, *[w[0] for w in weights], *[e for e, _ in extras])


def _gates(xn, w_in_t, row0, tm, tn, rows):
    def epi(accs, e_refs, o_refs, rows):
        o_refs[0][rows, :] = jax.nn.sigmoid(accs[0]).astype(BF16)

    n = 2 * D_MODEL
    return _fused_mm(xn, [(w_in_t, row0, "nk", tn)], epi, [],
                     [(jax.ShapeDtypeStruct((rows, n), BF16), pl.BlockSpec((tm, tn), lambda j, i: (i, j)))],
                     tm=tm, nj=n // tn, name="gates", rows=rows)[0]


def _q_path(xn, w_in_t, q_norm, wq2, cos2, sin2, tm, seq, rows):
    nr = MLA_HEADS * QK_ROPE
    nn = MLA_HEADS * QK_NOPE
    lanes = 2 * QK_ROPE

    def epi(accs, e_refs, o_refs, rows):
        qn_ref, w2_ref, cos_ref, sin_ref = e_refs
        cq = _rms(accs[0], qn_ref[...]).astype(BF16)
        q = jnp.dot(cq, w2_ref[...], preferred_element_type=F32)
        o_refs[0][rows, :nn] = (q[:, :nn] * ATTN_SCALE).astype(BF16)
        cos, sin = cos_ref[rows, :] * ATTN_SCALE, sin_ref[rows, :] * ATTN_SCALE
        for c0 in range(0, nr, lanes):
            rope = q[:, nn + c0:nn + c0 + lanes] * cos + q[:, nn + nr + c0:nn + nr + c0 + lanes] * sin
            o_refs[0][rows, nn + c0:nn + c0 + lanes] = rope.astype(BF16)

    pos_tiles = seq // tm
    extras = [(q_norm.reshape(1, Q_RANK), pl.BlockSpec((1, Q_RANK), lambda j, i: (0, 0))),
              (wq2, pl.BlockSpec(wq2.shape, lambda j, i: (0, 0))),
              (cos2, pl.BlockSpec((tm, lanes), lambda j, i: (i % pos_tiles, 0))),
              (sin2, pl.BlockSpec((tm, lanes), lambda j, i: (i % pos_tiles, 0)))]
    outs = [(jax.ShapeDtypeStruct((rows, nn + nr), BF16), pl.BlockSpec((tm, nn + nr), lambda j, i: (i, 0)))]
    return _fused_mm(xn, [(w_in_t, S5_WIDTH, "nk", Q_RANK)], epi, extras, outs, tm=tm, nj=1, name="q_path",
                     rows=rows)[0]


def _kv_path(xn, w_kv_t, kv_norm, w_ukv_bf, cos_sin_k, tm, seq, lat_rows):
    nkv = w_ukv_bf.shape[1]

    def epi(accs, e_refs, o_refs, rows):
        acc = accs[0]
        ckv = _rms(acc[:, :KV_RANK], e_refs[0][...]).astype(BF16)
        o_refs[0][rows, :] = jnp.dot(ckv, e_refs[1][...], preferred_element_type=F32).astype(BF16)
        prod = acc[:, KV_RANK:] * e_refs[2][rows, :]
        o_refs[1][rows, :] = (prod + pltpu.roll(prod, QK_ROPE, 1))[:, :QK_ROPE].astype(BF16)

    t = xn.shape[0]
    pos_tiles, lat_tiles = seq // tm, lat_rows // tm
    extras = [(kv_norm.reshape(1, KV_RANK), pl.BlockSpec((1, KV_RANK), lambda j, i: (0, 0))),
              (w_ukv_bf, pl.BlockSpec(w_ukv_bf.shape, lambda j, i: (0, 0))),
              (cos_sin_k, pl.BlockSpec((tm, 2 * QK_ROPE),
                                       lambda j, i: (jnp.where(i < lat_tiles, i % pos_tiles, pos_tiles), 0)))]
    outs = [(jax.ShapeDtypeStruct((t, nkv), BF16), pl.BlockSpec((tm, nkv), lambda j, i: (i, 0))),
            (jax.ShapeDtypeStruct((t, QK_ROPE), BF16), pl.BlockSpec((tm, QK_ROPE), lambda j, i: (i, 0)))]
    return _fused_mm(xn, [(w_kv_t, 0, "nk", w_kv_t.shape[0])], epi, extras, outs, tm=tm, nj=1, name="kv_path")


ATTN_SUB_ROWS = 256


def _attn_kernel(qn_ref, qr_ref, kvl_ref, kvc_ref, krl_ref, krc_ref, o_ref, k_scr, v_scr, *, seq, ctx):
    dk = QK_NOPE + QK_ROPE

    @pl.when(pl.program_id(2) == 0)
    def _():
        for h in range(2):
            base = h * (QK_NOPE + V_DIM)
            k_scr[h, :seq, :QK_NOPE] = kvl_ref[:, base:base + QK_NOPE]
            k_scr[h, seq:, :QK_NOPE] = kvc_ref[:, base:base + QK_NOPE]
            k_scr[h, :seq, QK_NOPE:dk] = krl_ref[...]
            k_scr[h, seq:, QK_NOPE:dk] = krc_ref[...]
            v_scr[h, :seq, :V_DIM] = kvl_ref[:, base + QK_NOPE:base + QK_NOPE + V_DIM]
            v_scr[h, seq:, :V_DIM] = kvc_ref[:, base + QK_NOPE:base + QK_NOPE + V_DIM]
            ones_col = lax.broadcasted_iota(jnp.int32, (seq + ctx, V_DIM), 1) == 0
            v_scr[h, :, V_DIM:] = jnp.where(ones_col, 1.0, 0.0).astype(BF16)

    for r0 in range(0, qn_ref.shape[0], ATTN_SUB_ROWS):
        rows = slice(r0, r0 + ATTN_SUB_ROWS)
        for h in range(2):
            q = jnp.concatenate([qn_ref[rows, h * QK_NOPE:(h + 1) * QK_NOPE],
                                 qr_ref[rows, h * QK_ROPE:(h + 1) * QK_ROPE]], axis=1)
            s = lax.dot_general(q, k_scr[h], (((1,), (1,)), ((), ())), preferred_element_type=F32)
            m = jnp.max(s, axis=-1, keepdims=True)
            p = jnp.exp((s - m).astype(BF16))
            ol = jnp.dot(p, v_scr[h], preferred_element_type=F32)
            o_ref[rows, h * V_DIM:(h + 1) * V_DIM] = (ol[:, :V_DIM] / ol[:, V_DIM:V_DIM + 1]).astype(o_ref.dtype)


def _attention(q, kv, kr, batch, seq, ctx, tq):
    nq = seq // tq
    ctx0 = batch * seq // ctx
    nn_blocks = MLA_HEADS * QK_NOPE // (2 * QK_NOPE)
    dk = QK_NOPE + QK_ROPE
    hw = 2 * (QK_NOPE + V_DIM)
    return pl.pallas_call(
        functools.partial(_attn_kernel, seq=seq, ctx=ctx),
        grid=(batch, MLA_HEADS // 2, nq),
        in_specs=[pl.BlockSpec((tq, 2 * QK_NOPE), lambda b, hp, qi: (b * nq + qi, hp)),
                  pl.BlockSpec((tq, 2 * QK_ROPE), lambda b, hp, qi: (b * nq + qi, 2 * nn_blocks + hp)),
                  pl.BlockSpec((seq, hw), lambda b, hp, qi: (b, hp)),
                  pl.BlockSpec((ctx, hw), lambda b, hp, qi: (ctx0 + b, hp)),
                  pl.BlockSpec((seq, QK_ROPE), lambda b, hp, qi: (b, 0)),
                  pl.BlockSpec((ctx, QK_ROPE), lambda b, hp, qi: (ctx0 + b, 0))],
        out_specs=pl.BlockSpec((tq, 2 * V_DIM), lambda b, hp, qi: (b * nq + qi, hp)),
        out_shape=jax.ShapeDtypeStruct((batch * seq, MLA_HEADS * V_DIM), BF16),
        scratch_shapes=[pltpu.VMEM((2, seq + ctx, dk), BF16), pltpu.VMEM((2, seq + ctx, 2 * V_DIM), BF16)],
        compiler_params=_cparams(3),
        name="attention",
    )(q, q, kv, kv, kr, kr)


def _s5_prep_pair(pp, are_ref, aim_ref, ldt_ref, bre_ref, bim_ref, cre_ref, cim_ref,
                  toep_ref, wsr_ref, wsi_ref, wor_ref, woi_ref, atr_ref, ati_ref):
    tc, g = S5_CHUNK, S5_GROUP
    w = tc * g
    lane = lax.broadcasted_iota(jnp.int32, (g, 2 * S5_STATE), 1)
    in_group = (lane < S5_STATE, lane >= S5_STATE)
    lane_w = lax.broadcasted_iota(jnp.int32, (g, w), 1)
    nt = (((1,), (1,)), ((), ()))
    toep_rows = [[jnp.zeros((g, w), F32) for _ in range(tc)] for _ in range(2)]
    for d in range(2):
        lr, li = are_ref[d, pp], aim_ref[d, pp]
        dt = jnp.exp(ldt_ref[d, pp])
        mag = jnp.exp(lr * dt)
        ab_re, ab_im = mag * jnp.cos(li * dt), mag * jnp.sin(li * dt)
        den = lr * lr + li * li
        nr, ni = ab_re - 1.0, ab_im
        co_re = (nr * lr + ni * li) / den
        co_im = (ni * lr - nr * li) / den
        br, bi = bre_ref[d, pp], bim_ref[d, pp]
        bb_re = co_re * br - co_im * bi
        bb_im = co_re * bi + co_im * br
        pw = [(jnp.ones_like(ab_re), jnp.zeros_like(ab_re))]
        for _ in range(tc):
            pr, pi = pw[-1]
            pw.append((pr * ab_re - pi * ab_im, pr * ab_im + pi * ab_re))
        cr, ci = cre_ref[d, pp], cim_ref[d, pp]
        ca = [(cr * pr - ci * pi, cr * pi + ci * pr) for pr, pi in pw]
        taus = list(range(tc))[::-1] if d else list(range(tc))
        y_re = jnp.concatenate([ca[t][0] for t in taus], axis=0).astype(BF16)
        y_im = jnp.concatenate([ca[t][1] for t in taus], axis=0).astype(BF16)
        for j in range(2):
            x_re = jnp.where(in_group[j], bb_re, 0.0)
            x_im = jnp.where(in_group[j], bb_im, 0.0)
            kt = (lax.dot_general(x_re.astype(BF16), y_re, nt, preferred_element_type=F32)
                  - lax.dot_general(x_im.astype(BF16), y_im, nt, preferred_element_type=F32))
            for r in range(tc):
                sh = (r + 1) * g if d else r * g
                blk = pltpu.roll(kt, sh % w, 1) if sh % w else kt
                keep = (lane_w < sh) if d else (lane_w >= sh)
                toep_rows[j][r] = toep_rows[j][r] + jnp.where(keep, blk, 0.0)
        for r in range(tc):
            pr, pi = pw[r] if d else pw[tc - 1 - r]
            w_re = bb_re * pr - bb_im * pi
            w_im = bb_re * pi + bb_im * pr
            car, cai = ca[tc - r] if d else ca[r + 1]
            for j in range(2):
                rows = slice(j * w + r * g, j * w + (r + 1) * g)
                wsr_ref[d, pp, rows, :] = jnp.where(in_group[j], w_re, 0.0).astype(BF16)
                wsi_ref[d, pp, rows, :] = jnp.where(in_group[j], w_im, 0.0).astype(BF16)
                wor_ref[d, pp, rows, :] = jnp.where(in_group[j], car, 0.0).astype(BF16)
                woi_ref[d, pp, rows, :] = jnp.where(in_group[j], -cai, 0.0).astype(BF16)
        atr_ref[d, pp] = pw[tc][0]
        ati_ref[d, pp] = pw[tc][1]
    for j in range(2):
        toep_ref[pp, j] = jnp.concatenate(toep_rows[j], axis=0).astype(BF16)


S5_PREP_PAIRS = 4


def _s5_prep_kernel(*refs):
    for pp in range(S5_PREP_PAIRS):
        _s5_prep_pair(pp, *refs)


def _s5_prep(params):
    tc = S5_CHUNK
    wide = 2 * tc * S5_GROUP
    sl = 2 * S5_STATE

    def pair_lanes(v):
        return v.astype(F32).reshape(2, S5_PAIRS, 1, sl)

    def pair_rows(v):
        rows = v.shape[2]
        return v.astype(F32).reshape(2, S5_PAIRS, 2, rows, S5_STATE).transpose(0, 1, 3, 2, 4).reshape(
            2, S5_PAIRS, rows, sl)

    ldt = jnp.broadcast_to(params['s5_log_dt'].astype(F32)[:, :, None], (2, S5_GROUPS, S5_STATE))
    ins = [pair_lanes(params['s5_a_re']), pair_lanes(params['s5_a_im']), pair_lanes(ldt),
           pair_rows(params['s5_b_re'].transpose(0, 1, 3, 2)), pair_rows(params['s5_b_im'].transpose(0, 1, 3, 2)),
           pair_rows(params['s5_c_re']), pair_rows(params['s5_c_im'])]
    pp = S5_PREP_PAIRS
    vec_spec = pl.BlockSpec((2, pp, 1, sl), lambda k: (0, k, 0, 0))
    mat_spec = pl.BlockSpec((2, pp, S5_GROUP, sl), lambda k: (0, k, 0, 0))
    w_spec = pl.BlockSpec((2, pp, wide, sl), lambda k: (0, k, 0, 0))
    w_sds = jax.ShapeDtypeStruct((2, S5_PAIRS, wide, sl), BF16)
    a_sds = jax.ShapeDtypeStruct((2, S5_PAIRS, 1, sl), F32)
    return pl.pallas_call(
        _s5_prep_kernel,
        grid=(S5_PAIRS // pp,),
        in_specs=[vec_spec] * 3 + [mat_spec] * 4,
        out_specs=[pl.BlockSpec((pp, 2, wide // 2, wide // 2), lambda k: (k, 0, 0, 0))] + [w_spec] * 4 + [vec_spec] * 2,
        out_shape=[jax.ShapeDtypeStruct((S5_PAIRS, 2, wide // 2, wide // 2), BF16)] + [w_sds] * 4 + [a_sds] * 2,
        compiler_params=_cparams(1),
        name="s5_prep",
    )(*ins)


S5_STEP_PAIRS = 4
def _s5_state_rows(batch, n_lat, n_ctx):
    def pitch(n):
        p = -(-n // SUBLANES)
        return SUBLANES * (p + 1 - p % 2)

    lat_pitch, ctx_pitch = pitch(n_lat), pitch(n_ctx)
    ctx_base = batch * lat_pitch
    return lat_pitch, ctx_pitch, ctx_base, ctx_base + batch * ctx_pitch


def _s5_kernel(ul_ref, uc_ref, dsk_ref, *refs, batch, seq, ctx):
    toep_ref, wsr, wsi, wor, woi, atr, ati = refs[:7]
    o_ref = refs[7]
    wscr, zscr, upscr, sre, sim, hre, him = refs[8:]
    tc = S5_CHUNK
    n_lat, n_ctx = seq // tc, ctx // tc
    rows_lat = n_lat * batch
    lat_pitch, ctx_pitch, ctx_base, _ = _s5_state_rows(batch, n_lat, n_ctx)
    w = tc * S5_GROUP
    lanes = S5_STEP_PAIRS * 2 * S5_GROUP

    def scatter_tiles(xt, r, col0):
        for kk in range(S5_STEP_PAIRS):
            for j in range(2):
                ch = (2 * kk + j) * S5_GROUP
                wscr[kk, j * w + r * S5_GROUP:j * w + (r + 1) * S5_GROUP, col0:col0 + lanes] = xt[ch:ch + S5_GROUP, :]

    for r in range(tc):
        for b in range(batch):
            x = ul_ref[pl.ds(b * seq + r, n_lat, stride=tc), :]
            scatter_tiles(x.T, r, b * n_lat)
        xc = jnp.concatenate([uc_ref[pl.ds(b * ctx + r, n_ctx, stride=tc), :] for b in range(batch)]
                             + [jnp.zeros((lanes - batch * n_ctx, lanes), F32)], axis=0)
        scatter_tiles(xc.T, r, rows_lat)

    chains = [(kk, d) for kk in range(S5_STEP_PAIRS) for d in range(2)]
    for kk in range(S5_STEP_PAIRS):
        upscr[kk] = wscr[kk].T.astype(BF16)
        for d in range(2):
            for dst, wst in ((sre, wsr), (sim, wsi)):
                s = jnp.dot(upscr[kk], wst[d, kk], preferred_element_type=F32)
                for b in range(batch):
                    dst[kk, d, b * lat_pitch:b * lat_pitch + n_lat, :] = s[b * n_lat:(b + 1) * n_lat]
                    dst[kk, d, ctx_base + b * ctx_pitch:ctx_base + b * ctx_pitch + n_ctx, :] = (
                        s[rows_lat + b * n_ctx:rows_lat + (b + 1) * n_ctx])

    ctx_rows = [pl.ds(ctx_base + c, batch, stride=ctx_pitch) for c in range(n_ctx)]
    lat_rows = [pl.ds(c, batch, stride=lat_pitch) for c in range(n_lat)]
    order = (ctx_rows + lat_rows, ctx_rows[::-1] + lat_rows[::-1])
    coef = {(kk, d): (atr[d, kk], ati[d, kk]) for kk, d in chains}
    state = {ch: (jnp.zeros((batch, 2 * S5_STATE), F32), jnp.zeros((batch, 2 * S5_STATE), F32)) for ch in chains}
    for t in range(n_ctx + n_lat):
        for kk, d in chains:
            rows = order[d][t]
            (ar, ai), (h_re, h_im) = coef[kk, d], state[kk, d]
            hre[kk, d, rows, :] = h_re
            him[kk, d, rows, :] = h_im
            state[kk, d] = (ar * h_re - ai * h_im + sre[kk, d, rows, :],
                            ar * h_im + ai * h_re + sim[kk, d, rows, :])

    for kk in range(S5_STEP_PAIRS):
        ul = upscr[kk, :rows_lat, :]
        y = ul.astype(F32) * dsk_ref[kk]
        y = y + jnp.concatenate(
            [jnp.dot(ul[:, :w], toep_ref[kk, 0], preferred_element_type=F32),
             jnp.dot(ul[:, w:], toep_ref[kk, 1], preferred_element_type=F32)], axis=1)
        nt = (((1,), (1,)), ((), ()))
        for d in range(2):
            h_r = jnp.concatenate([hre[kk, d, b * lat_pitch:b * lat_pitch + n_lat, :] for b in range(batch)], axis=0)
            h_i = jnp.concatenate([him[kk, d, b * lat_pitch:b * lat_pitch + n_lat, :] for b in range(batch)], axis=0)
            y = y + lax.dot_general(h_r.astype(BF16), wor[d, kk], nt, preferred_element_type=F32)
            y = y + lax.dot_general(h_i.astype(BF16), woi[d, kk], nt, preferred_element_type=F32)
        yt = y.T
        for b in range(batch):
            for s in range(tc):
                for j in range(2):
                    ch = (2 * kk + j) * S5_GROUP
                    zscr[b * tc + s, ch:ch + S5_GROUP, :] = yt[j * w + s * S5_GROUP:j * w + (s + 1) * S5_GROUP,
                                                               b * n_lat:(b + 1) * n_lat]

    for b in range(batch):
        for s in range(tc):
            o_ref[pl.ds(b * seq + s, n_lat, stride=tc), :] = jax.nn.gelu(zscr[b * tc + s].T)


def _s5_mixer(u, params, batch, seq, ctx):
    tc = S5_CHUNK
    n_lat, n_ctx = seq // tc, ctx // tc
    sp = S5_STEP_PAIRS
    lanes = sp * 2 * S5_GROUP
    assert n_lat == lanes and batch * n_ctx <= lanes
    rows_lat = n_lat * batch
    rows_all = rows_lat + lanes
    state_rows = _s5_state_rows(batch, n_lat, n_ctx)[3]
    wide = 2 * tc * S5_GROUP
    dsk = jnp.broadcast_to(params['s5_d'].astype(F32).reshape(S5_PAIRS, 2, 1, S5_GROUP),
                           (S5_PAIRS, 2, tc, S5_GROUP)).reshape(S5_PAIRS, 1, wide)
    weights = _s5_prep(params)
    w_spec = pl.BlockSpec((2, sp, wide, 2 * S5_STATE), lambda k: (0, k, 0, 0))
    a_spec = pl.BlockSpec((2, sp, 1, 2 * S5_STATE), lambda k: (0, k, 0, 0))
    specs = [pl.BlockSpec((sp, 2, wide // 2, wide // 2), lambda k: (k, 0, 0, 0))] + [w_spec] * 4 + [a_spec] * 2
    return pl.pallas_call(
        functools.partial(_s5_kernel, batch=batch, seq=seq, ctx=ctx),
        grid=(S5_PAIRS // sp,),
        in_specs=[pl.BlockSpec((batch * seq, lanes), lambda k: (0, k)),
                  pl.BlockSpec((batch * ctx, lanes), lambda k: (seq // ctx, k)),
                  pl.BlockSpec((sp, 1, wide), lambda k: (k, 0, 0))] + specs,
        out_specs=pl.BlockSpec((batch * seq, lanes), lambda k: (0, k)),
        out_shape=jax.ShapeDtypeStruct((batch * seq, S5_WIDTH), F32),
        scratch_shapes=[pltpu.VMEM((sp, wide, rows_all), F32),
                        pltpu.VMEM((batch * tc, lanes, n_lat), F32),
                        pltpu.VMEM((sp, rows_all, wide), BF16)]
                       + [pltpu.VMEM((sp, 2, state_rows, 2 * S5_STATE), F32) for _ in range(4)],
        compiler_params=_cparams(1),
        name="s5",
    )(u, u, dsk, *weights)


def _merge_kernel(z5_ref, o_ref_in, gs_ref, gm_ref, wa_ref, wb_ref, wm_ref, out_ref, wa_s, wb_s, wm_s):
    @pl.when(pl.program_id(1) == 0)
    def _():
        wa_s[...] = wa_ref[...].astype(BF16)
        wb_s[...] = wb_ref[...].astype(BF16)
        wm_s[...] = wm_ref[...].astype(BF16)

    sub = min(MERGE_SUB_ROWS, out_ref.shape[0])
    for r0 in range(0, out_ref.shape[0], sub):
        rows = slice(r0, r0 + sub)
        z = z5_ref[rows, :].astype(BF16)
        a = jnp.dot(z, wa_s[...], preferred_element_type=F32)
        b = jnp.dot(z, wb_s[...], preferred_element_type=F32)
        mla = jnp.dot(o_ref_in[rows, :], wm_s[...], preferred_element_type=F32)
        merged = gs_ref[rows, :].astype(F32) * (a * jax.nn.sigmoid(b)) + gm_ref[rows, :].astype(F32) * mla
        out_ref[rows, :] = merged.astype(out_ref.dtype)


def _merge(y5, o_mla, gates, w_glu, w_mla_o, tm, tn):
    t = y5.shape[0]
    nj = D_MODEL // tn
    return pl.pallas_call(
        _merge_kernel,
        grid=(nj, t // tm),
        in_specs=[pl.BlockSpec((tm, S5_WIDTH), lambda j, i: (i, 0)),
                  pl.BlockSpec((tm, MLA_HEADS * V_DIM), lambda j, i: (i, 0)),
                  pl.BlockSpec((tm, tn), lambda j, i: (i, j)),
                  pl.BlockSpec((tm, tn), lambda j, i: (i, nj + j)),
                  pl.BlockSpec((S5_WIDTH, tn), lambda j, i: (0, j)),
                  pl.BlockSpec((S5_WIDTH, tn), lambda j, i: (0, nj + j)),
                  pl.BlockSpec((MLA_HEADS * V_DIM, tn), lambda j, i: (0, j))],
        out_specs=pl.BlockSpec((tm, tn), lambda j, i: (i, j)),
        out_shape=jax.ShapeDtypeStruct((t, D_MODEL), BF16),
        scratch_shapes=[pltpu.VMEM((S5_WIDTH, tn), BF16), pltpu.VMEM((S5_WIDTH, tn), BF16),
                        pltpu.VMEM((MLA_HEADS * V_DIM, tn), BF16)],
        compiler_params=_cparams(2),
        name="merge",
    )(y5, o_mla, gates, gates, w_glu, w_glu, w_mla_o)


def _out_proj_norm_kernel(a_ref, w_ref, x_ref, g1_ref, n2_ref, sc_ref, sh_ref, x1_ref, xn_ref, w_s):
    @pl.when(pl.program_id(0) == 0)
    def _():
        w_s[...] = w_ref[...].astype(BF16)

    sub = min(MERGE_SUB_ROWS, x1_ref.shape[0])
    for r0 in range(0, x1_ref.shape[0], sub):
        rows = slice(r0, r0 + sub)
        x1 = x_ref[rows, :] + g1_ref[0, 0] * jnp.dot(a_ref[rows, :], w_s[...], preferred_element_type=F32)
        x1_ref[rows, :] = x1
        xn_ref[rows, :] = (_rms(x1, n2_ref[...]) * (1.0 + sc_ref[0, 0]) + sh_ref[0, 0]).astype(xn_ref.dtype)


def _out_proj_norm(merged, w_out, x_res, m, gain2, rows_per_batch, tm):
    t, k = merged.shape
    tpb = rows_per_batch // tm
    row = pl.BlockSpec((tm, D_MODEL), lambda i: (i, 0))
    gate1, scale2, shift2 = (_mod_spec(D_MODEL, lambda i: i // tpb, which)
                             for which in (MOD_GATE1, MOD_SCALE2, MOD_SHIFT2))
    return pl.pallas_call(
        _out_proj_norm_kernel,
        grid=(t // tm,),
        in_specs=[pl.BlockSpec((tm, k), lambda i: (i, 0)),
                  pl.BlockSpec((k, D_MODEL), lambda i: (0, 0), pipeline_mode=pl.Buffered(1)),
                  row, gate1, pl.BlockSpec((1, D_MODEL), lambda i: (0, 0)), scale2, shift2],
        out_specs=[row, row],
        out_shape=[jax.ShapeDtypeStruct((t, D_MODEL), F32), jax.ShapeDtypeStruct((t, D_MODEL), BF16)],
        scratch_shapes=[pltpu.VMEM((k, D_MODEL), BF16)],
        compiler_params=_cparams(1),
        name="out_proj",
    )(merged, w_out, x_res, m, gain2.reshape(1, D_MODEL), m, m)


def _ffn_in(xn, w_ffn_in, tm, tn):
    def epi(accs, e_refs, o_refs, rows):
        o_refs[0][rows, :] = (jax.nn.silu(accs[0]) * accs[1]).astype(BF16)

    t = xn.shape[0]
    nj = D_FF // tn
    outs = [(jax.ShapeDtypeStruct((t, D_FF), BF16), pl.BlockSpec((tm, tn), lambda j, i: (i, j)))]
    return _fused_mm(xn, [(w_ffn_in, 0, "kn", tn), (w_ffn_in, nj, "kn", tn)], epi, [], outs,
                     tm=tm, nj=nj, name="ffn_in", sub_rows=MM_SUB_ROWS)[0]


def _ffn_out_kernel(h_ref, w_ref, x_ref, g2_ref, nf_ref, o_ref, w_s, *, last):
    @pl.when(pl.program_id(0) == 0)
    def _():
        w_s[...] = w_ref[...].astype(BF16)

    y = x_ref[...] + g2_ref[0, 0] * jnp.dot(h_ref[...], w_s[...], preferred_element_type=F32)
    o_ref[...] = _rms(y, nf_ref[...]) if last else y


def _ffn_out(hid, w_ffn_out, x_res, m, norm_f, rows_per_batch, tm):
    t, k = hid.shape
    kh = k // 2
    tpb = rows_per_batch // tm
    row = pl.BlockSpec((tm, D_MODEL), lambda i: (i, 0))
    y = x_res
    for half in range(2):
        y = pl.pallas_call(
            functools.partial(_ffn_out_kernel, last=half == 1),
            grid=(t // tm,),
            in_specs=[pl.BlockSpec((tm, kh), functools.partial(lambda i, half: (i, half), half=half)),
                      pl.BlockSpec((kh, D_MODEL), functools.partial(lambda i, half: (half, 0), half=half),
                                   pipeline_mode=pl.Buffered(1)),
                      row, _mod_spec(D_MODEL, lambda i: i // tpb, MOD_GATE2),
                      pl.BlockSpec((1, D_MODEL), lambda i: (0, 0))],
            out_specs=row,
            out_shape=jax.ShapeDtypeStruct((t, D_MODEL), F32),
            scratch_shapes=[pltpu.VMEM((kh, D_MODEL), BF16)],
            compiler_params=_cparams(1),
            name="ffn_out",
        )(hid, w_ffn_out, y, m, norm_f.reshape(1, D_MODEL))
    return y


def _rope_rot_cols(w):
    k = w.shape[0]
    ws = w.reshape(k, -1, 2, 2, QK_ROPE // 4)
    return jnp.stack([-ws[:, :, :, 1, :], ws[:, :, :, 0, :]], axis=3).reshape(k, -1)


def _rope_tables(n_tokens):
    rows = n_tokens // GRID_W
    row = jnp.repeat(jnp.arange(rows, dtype=F32), GRID_W)
    col = jnp.tile(jnp.arange(GRID_W, dtype=F32), rows)
    n_freq = QK_ROPE // 4
    inv = ROPE_BASE ** (-jnp.arange(n_freq, dtype=F32) / n_freq)
    ang = jnp.stack([row[:, None] * inv, col[:, None] * inv], axis=1)
    cos = jnp.broadcast_to(jnp.cos(ang)[:, :, None, :], (n_tokens, 2, 2, n_freq)).reshape(n_tokens, QK_ROPE)
    sin = jnp.broadcast_to(jnp.sin(ang)[:, :, None, :], (n_tokens, 2, 2, n_freq)).reshape(n_tokens, QK_ROPE)
    return cos, sin


def kernel(x, c, ctx, c_ctx, w_mod, b_mod, norm1, norm2, w_in, s5_a_re, s5_a_im, s5_log_dt, s5_b_re, s5_b_im,
           s5_c_re, s5_c_im, s5_d, w_glu, q_norm, kv_norm, w_uq, w_ukv, w_mla_o, w_out, w_ffn_in, w_ffn_out,
           norm_f):
    batch, seq, _ = x.shape
    n_ctx = ctx.shape[1]
    assert w_mod.shape[0] == 1, "single-layer block"
    p = dict(s5_a_re=s5_a_re[0], s5_a_im=s5_a_im[0], s5_log_dt=s5_log_dt[0], s5_b_re=s5_b_re[0],
             s5_b_im=s5_b_im[0], s5_c_re=s5_c_re[0], s5_c_im=s5_c_im[0], s5_d=s5_d[0])
    w_in_t = w_in.reshape(w_in.shape[1:]).T

    cv = jnp.concatenate([c, c_ctx[None], jnp.zeros((8 - batch - 1, D_MODEL), F32)], axis=0)
    m = _modulation(cv, w_mod[0], b_mod[0]).reshape(8, 6, 1, D_MODEL)

    lat_rows = batch * seq
    x2d = x.reshape(lat_rows, D_MODEL)
    c2d = ctx.reshape(batch * n_ctx, D_MODEL)
    xn, u = _norm_u_proj(x2d, c2d, norm1[0], m, w_in_t, seq, batch, 512)

    kv_lo = S5_WIDTH + Q_RANK
    w_kv_rows = w_in_t[kv_lo:kv_lo + KV_RANK + QK_ROPE]
    w_kv_t = jnp.concatenate([w_kv_rows, _rope_rot_cols(w_kv_rows[KV_RANK:].T).T], axis=0)
    wq = w_uq[0].reshape(Q_RANK, MLA_HEADS, QK_NOPE + QK_ROPE)
    wq_rope = wq[:, :, QK_NOPE:].reshape(Q_RANK, MLA_HEADS * QK_ROPE)
    wq2 = jnp.concatenate([wq[:, :, :QK_NOPE].reshape(Q_RANK, MLA_HEADS * QK_NOPE), wq_rope,
                           _rope_rot_cols(wq_rope)], axis=1).astype(BF16)
    w_ukv_bf = w_ukv[0].astype(BF16)
    tm = 1024
    cos, sin = _rope_tables(seq)
    cos_sin_k = jnp.concatenate([jnp.concatenate([cos, sin], axis=1),
                                 jnp.concatenate([jnp.ones((tm, QK_ROPE), F32), jnp.zeros((tm, QK_ROPE), F32)], axis=1)],
                                axis=0)

    q = _q_path(xn, w_in_t, q_norm[0], wq2, jnp.tile(cos, (1, 2)), jnp.tile(sin, (1, 2)), tm, seq, lat_rows)
    kv, kr = _kv_path(xn, w_kv_t, kv_norm[0], w_ukv_bf, cos_sin_k, tm, seq, lat_rows)
    gates = _gates(xn, w_in_t, kv_lo + KV_RANK + QK_ROPE, tm, 1024, lat_rows)

    z5 = _s5_mixer(u, p, batch, seq, n_ctx)
    o_mla = _attention(q, kv, kr, batch, seq, n_ctx, 2048)

    merged = _merge(z5, o_mla, gates, w_glu[0], w_mla_o[0], 512, 1024)
    x1, xn2 = _out_proj_norm(merged, w_out[0], x2d, m, norm2[0], seq, 512)
    hid = _ffn_in(xn2, w_ffn_in[0], 2048, 512)
    return _ffn_out(hid, w_ffn_out[0], x1, m, norm_f, seq, 256).reshape(batch, seq, D_MODEL)
```

```python
import functools
import math

import jax
import jax.numpy as jnp
from jax import lax
from jax.experimental import pallas as pl
from jax.experimental.pallas import tpu as pltpu

F32 = jnp.float32
BF16 = jnp.bfloat16

D_MODEL = 2048
GRID_W = 64
EPS = 1e-6
S5_WIDTH = D_MODEL // 2
S5_GROUP = 16
S5_GROUPS = S5_WIDTH // S5_GROUP
S5_STATE = 64
S5_CHUNK = 16
S5_PAIRS = S5_GROUPS // 2
MLA_HEADS = 8
QK_NOPE = 128
QK_ROPE = 64
V_DIM = 128
Q_RANK = 512
KV_RANK = 256
ROPE_BASE = 10000.0
ATTN_SCALE = (QK_NOPE + QK_ROPE) ** -0.5
D_FF = -(-8 * D_MODEL // (3 * 256)) * 256

VMEM_LIMIT_BYTES = 56 * 1024 * 1024
SUBLANES = 8
MM_SUB_ROWS = 512
MERGE_SUB_ROWS = 256


def _cparams(n_axes):
    return pltpu.CompilerParams(dimension_semantics=("arbitrary",) * n_axes,
                                vmem_limit_bytes=VMEM_LIMIT_BYTES)


def _rms(x, g):
    return x * lax.rsqrt(jnp.mean(x * x, axis=-1, keepdims=True) + EPS) * g


def _mod_kernel(cv_ref, w_ref, b_ref, o_ref):
    s = jax.nn.silu(cv_ref[...]).astype(BF16)
    o_ref[...] = jnp.dot(s, w_ref[...].astype(BF16), preferred_element_type=F32) + b_ref[...]


def _modulation(cv, w_mod, b_mod):
    n = w_mod.shape[1]
    tn = 1536
    return pl.pallas_call(
        _mod_kernel,
        grid=(n // tn,),
        in_specs=[pl.BlockSpec((8, D_MODEL), lambda j: (0, 0)),
                  pl.BlockSpec((D_MODEL, tn), lambda j: (0, j)),
                  pl.BlockSpec((1, tn), lambda j: (0, j))],
        out_specs=pl.BlockSpec((8, tn), lambda j: (0, j)),
        out_shape=jax.ShapeDtypeStruct((8, n), F32),
        compiler_params=_cparams(1),
        name="mod",
    )(cv, w_mod, b_mod.reshape(1, n))


MOD_SHIFT1, MOD_SCALE1, MOD_GATE1, MOD_SHIFT2, MOD_SCALE2, MOD_GATE2 = range(6)


def _mod_spec(width, row_of, which, col_of=None):
    col_of = col_of or (lambda *ids: 0)
    return pl.BlockSpec((1, 1, 1, width), lambda *ids: (row_of(*ids), which, 0, col_of(*ids)))


def _norm_u_kernel(x_ref, c_ref, g_ref, sc_ref, sh_ref, w_ref, xn_ref, u_ref, w_s, *, lat_tiles):
    @pl.when(pl.program_id(0) == 0)
    def _():
        w_s[...] = w_ref[...].astype(BF16)

    is_lat = pl.program_id(0) < lat_tiles
    sub = min(MERGE_SUB_ROWS, xn_ref.shape[0])
    for r0 in range(0, xn_ref.shape[0], sub):
        rows = slice(r0, r0 + sub)
        src = jnp.where(is_lat, x_ref[rows, :], c_ref[rows, :])
        xn = (_rms(src, g_ref[...]) * (1.0 + sc_ref[0, 0]) + sh_ref[0, 0]).astype(xn_ref.dtype)
        xn_ref[rows, :] = xn
        u_ref[rows, :] = lax.dot_general(xn, w_s[...], (((1,), (1,)), ((), ())), preferred_element_type=F32)


def _norm_u_proj(x2d, c2d, gain, m, w_in_t, rows_per_batch, ctx_row, tm):
    lat_tiles, ctx_tiles = x2d.shape[0] // tm, c2d.shape[0] // tm
    tpb = rows_per_batch // tm
    t_all = x2d.shape[0] + c2d.shape[0]

    def row(i):
        return jnp.where(i < lat_tiles, i // tpb, ctx_row)

    once = pl.Buffered(1)
    return pl.pallas_call(
        functools.partial(_norm_u_kernel, lat_tiles=lat_tiles),
        grid=(lat_tiles + ctx_tiles,),
        in_specs=[pl.BlockSpec((tm, D_MODEL), lambda i: (jnp.minimum(i, lat_tiles - 1), 0)),
                  pl.BlockSpec((tm, D_MODEL), lambda i: (jnp.maximum(i - lat_tiles, 0), 0)),
                  pl.BlockSpec((1, D_MODEL), lambda i: (0, 0)),
                  _mod_spec(D_MODEL, row, MOD_SCALE1), _mod_spec(D_MODEL, row, MOD_SHIFT1),
                  pl.BlockSpec((S5_WIDTH, D_MODEL), lambda i: (0, 0), pipeline_mode=once)],
        out_specs=[pl.BlockSpec((tm, D_MODEL), lambda i: (i, 0)), pl.BlockSpec((tm, S5_WIDTH), lambda i: (i, 0))],
        out_shape=[jax.ShapeDtypeStruct((t_all, D_MODEL), BF16), jax.ShapeDtypeStruct((t_all, S5_WIDTH), F32)],
        scratch_shapes=[pltpu.VMEM((S5_WIDTH, D_MODEL), BF16)],
        compiler_params=_cparams(1),
        name="norm_u_proj",
    )(x2d, c2d, gain.reshape(1, D_MODEL), m, m, w_in_t)


def _fused_mm(a, weights, epilogue, extras, outs, *, tm, nj, name, rows=None, sub_rows=None):
    t, k = a.shape
    ni = (rows or t) // tm
    nw, ne, no = len(weights), len(extras), len(outs)
    need_cast = [w.dtype != BF16 for w, _, _, _ in weights]
    nt = (((1,), (1,)), ((), ()))

    def kernel(*refs):
        a_ref = refs[0]
        w_refs = refs[1:1 + nw]
        e_refs = refs[1 + nw:1 + nw + ne]
        o_refs = refs[1 + nw + ne:1 + nw + ne + no]
        s_refs = list(refs[1 + nw + ne + no:])
        staged = {idx: s_refs.pop(0) for idx in range(nw) if need_cast[idx]}
        if staged:
            @pl.when(pl.program_id(1) == 0)
            def _():
                for idx, s_ref in staged.items():
                    s_ref[...] = w_refs[idx][...].astype(BF16)

        for r0 in range(0, tm, sub_rows or tm):
            rows = slice(r0, r0 + (sub_rows or tm))
            av = a_ref[rows, :]
            accs = []
            for idx in range(nw):
                w_ref = staged.get(idx, w_refs[idx])
                if weights[idx][2] == "kn":
                    accs.append(jnp.dot(av, w_ref[...], preferred_element_type=F32))
                else:
                    accs.append(lax.dot_general(av, w_ref[...], nt, preferred_element_type=F32))
            epilogue(accs, e_refs, o_refs, rows)

    in_specs = [pl.BlockSpec((tm, k), lambda j, i: (i, 0))]
    scratch = []
    for (w, off, layout, width), cast in zip(weights, need_cast):
        if layout == "kn":
            shape = (k, width)
            in_specs.append(pl.BlockSpec(shape, functools.partial(lambda j, i, off: (0, off + j), off=off)))
        else:
            shape = (width, k)
            in_specs.append(pl.BlockSpec(
                (pl.Element(width), pl.Element(k)),
                functools.partial(lambda j, i, off, width: (pl.multiple_of(off + j * width, SUBLANES), 0),
                                  off=off, width=width)))
        if cast:
            scratch.append(pltpu.VMEM(shape, BF16))
    in_specs += [spec for _, spec in extras]
    return pl.pallas_call(
        kernel,
        grid=(nj, ni),
        in_specs=in_specs,
        out_specs=[spec for _, spec in outs],
        out_shape=[sds for sds, _ in outs],
        scratch_shapes=scratch,
        compiler_params=_cparams(2),
        name=name,
    )(a, *[w[0] for w in weights], *[e for e, _ in extras])


def _gates(xn, w_in_t, row0, tm, tn, rows):
    def epi(accs, e_refs, o_refs, rows):
        o_refs[0][rows, :] = jax.nn.sigmoid(accs[0]).astype(BF16)

    n = 2 * D_MODEL
    return _fused_mm(xn, [(w_in_t, row0, "nk", tn)], epi, [],
                     [(jax.ShapeDtypeStruct((rows, n), BF16), pl.BlockSpec((tm, tn), lambda j, i: (i, j)))],
                     tm=tm, nj=n // tn, name="gates", rows=rows)[0]


def _q_path(xn, w_in_t, q_norm, wq2, cos2, sin2, tm, seq, rows):
    nr = MLA_HEADS * QK_ROPE
    nn = MLA_HEADS * QK_NOPE
    lanes = 2 * QK_ROPE

    def epi(accs, e_refs, o_refs, rows):
        qn_ref, w2_ref, cos_ref, sin_ref = e_refs
        cq = _rms(accs[0], qn_ref[...]).astype(BF16)
        q = jnp.dot(cq, w2_ref[...], preferred_element_type=F32)
        o_refs[0][rows, :nn] = (q[:, :nn] * ATTN_SCALE).astype(BF16)
        cos, sin = cos_ref[rows, :] * ATTN_SCALE, sin_ref[rows, :] * ATTN_SCALE
        for c0 in range(0, nr, lanes):
            rope = q[:, nn + c0:nn + c0 + lanes] * cos + q[:, nn + nr + c0:nn + nr + c0 + lanes] * sin
            o_refs[0][rows, nn + c0:nn + c0 + lanes] = rope.astype(BF16)

    pos_tiles = seq // tm
    extras = [(q_norm.reshape(1, Q_RANK), pl.BlockSpec((1, Q_RANK), lambda j, i: (0, 0))),
              (wq2, pl.BlockSpec(wq2.shape, lambda j, i: (0, 0))),
              (cos2, pl.BlockSpec((tm, lanes), lambda j, i: (i % pos_tiles, 0))),
              (sin2, pl.BlockSpec((tm, lanes), lambda j, i: (i % pos_tiles, 0)))]
    outs = [(jax.ShapeDtypeStruct((rows, nn + nr), BF16), pl.BlockSpec((tm, nn + nr), lambda j, i: (i, 0)))]
    return _fused_mm(xn, [(w_in_t, S5_WIDTH, "nk", Q_RANK)], epi, extras, outs, tm=tm, nj=1, name="q_path",
                     rows=rows)[0]


def _kv_path(xn, w_kv_t, kv_norm, w_ukv_bf, cos_sin_k, tm, seq, lat_rows):
    nkv = w_ukv_bf.shape[1]

    def epi(accs, e_refs, o_refs, rows):
        acc = accs[0]
        ckv = _rms(acc[:, :KV_RANK], e_refs[0][...]).astype(BF16)
        o_refs[0][rows, :] = jnp.dot(ckv, e_refs[1][...], preferred_element_type=F32).astype(BF16)
        prod = acc[:, KV_RANK:] * e_refs[2][rows, :]
        o_refs[1][rows, :] = (prod + pltpu.roll(prod, QK_ROPE, 1))[:, :QK_ROPE].astype(BF16)

    t = xn.shape[0]
    pos_tiles, lat_tiles = seq // tm, lat_rows // tm
    extras = [(kv_norm.reshape(1, KV_RANK), pl.BlockSpec((1, KV_RANK), lambda j, i: (0, 0))),
              (w_ukv_bf, pl.BlockSpec(w_ukv_bf.shape, lambda j, i: (0, 0))),
              (cos_sin_k, pl.BlockSpec((tm, 2 * QK_ROPE),
                                       lambda j, i: (jnp.where(i < lat_tiles, i % pos_tiles, pos_tiles), 0)))]
    outs = [(jax.ShapeDtypeStruct((t, nkv), BF16), pl.BlockSpec((tm, nkv), lambda j, i: (i, 0))),
            (jax.ShapeDtypeStruct((t, QK_ROPE), BF16), pl.BlockSpec((tm, QK_ROPE), lambda j, i: (i, 0)))]
    return _fused_mm(xn, [(w_kv_t, 0, "nk", w_kv_t.shape[0])], epi, extras, outs, tm=tm, nj=1, name="kv_path")


ATTN_SUB_ROWS = 256


def _attn_kernel(qn_ref, qr_ref, kvl_ref, kvc_ref, krl_ref, krc_ref, o_ref, k_scr, v_scr, *, seq, ctx):
    dk = QK_NOPE + QK_ROPE

    @pl.when(pl.program_id(2) == 0)
    def _():
        for h in range(2):
            base = h * (QK_NOPE + V_DIM)
            k_scr[h, :seq, :QK_NOPE] = kvl_ref[:, base:base + QK_NOPE]
            k_scr[h, seq:, :QK_NOPE] = kvc_ref[:, base:base + QK_NOPE]
            k_scr[h, :seq, QK_NOPE:dk] = krl_ref[...]
            k_scr[h, seq:, QK_NOPE:dk] = krc_ref[...]
            v_scr[h, :seq, :V_DIM] = kvl_ref[:, base + QK_NOPE:base + QK_NOPE + V_DIM]
            v_scr[h, seq:, :V_DIM] = kvc_ref[:, base + QK_NOPE:base + QK_NOPE + V_DIM]
            ones_col = lax.broadcasted_iota(jnp.int32, (seq + ctx, V_DIM), 1) == 0
            v_scr[h, :, V_DIM:] = jnp.where(ones_col, 1.0, 0.0).astype(BF16)

    for r0 in range(0, qn_ref.shape[0], ATTN_SUB_ROWS):
        rows = slice(r0, r0 + ATTN_SUB_ROWS)
        for h in range(2):
            q = jnp.concatenate([qn_ref[rows, h * QK_NOPE:(h + 1) * QK_NOPE],
                                 qr_ref[rows, h * QK_ROPE:(h + 1) * QK_ROPE]], axis=1)
            s = lax.dot_general(q, k_scr[h], (((1,), (1,)), ((), ())), preferred_element_type=F32)
            m = jnp.max(s, axis=-1, keepdims=True)
            p = jnp.exp((s - m).astype(BF16))
            ol = jnp.dot(p, v_scr[h], preferred_element_type=F32)
            o_ref[rows, h * V_DIM:(h + 1) * V_DIM] = (ol[:, :V_DIM] / ol[:, V_DIM:V_DIM + 1]).astype(o_ref.dtype)


def _attention(q, kv, kr, batch, seq, ctx, tq):
    nq = seq // tq
    ctx0 = batch * seq // ctx
    nn_blocks = MLA_HEADS * QK_NOPE // (2 * QK_NOPE)
    dk = QK_NOPE + QK_ROPE
    hw = 2 * (QK_NOPE + V_DIM)
    return pl.pallas_call(
        functools.partial(_attn_kernel, seq=seq, ctx=ctx),
        grid=(batch, MLA_HEADS // 2, nq),
        in_specs=[pl.BlockSpec((tq, 2 * QK_NOPE), lambda b, hp, qi: (b * nq + qi, hp)),
                  pl.BlockSpec((tq, 2 * QK_ROPE), lambda b, hp, qi: (b * nq + qi, 2 * nn_blocks + hp)),
                  pl.BlockSpec((seq, hw), lambda b, hp, qi: (b, hp)),
                  pl.BlockSpec((ctx, hw), lambda b, hp, qi: (ctx0 + b, hp)),
                  pl.BlockSpec((seq, QK_ROPE), lambda b, hp, qi: (b, 0)),
                  pl.BlockSpec((ctx, QK_ROPE), lambda b, hp, qi: (ctx0 + b, 0))],
        out_specs=pl.BlockSpec((tq, 2 * V_DIM), lambda b, hp, qi: (b * nq + qi, hp)),
        out_shape=jax.ShapeDtypeStruct((batch * seq, MLA_HEADS * V_DIM), BF16),
        scratch_shapes=[pltpu.VMEM((2, seq + ctx, dk), BF16), pltpu.VMEM((2, seq + ctx, 2 * V_DIM), BF16)],
        compiler_params=_cparams(3),
        name="attention",
    )(q, q, kv, kv, kr, kr)


def _s5_prep_pair(pp, are_ref, aim_ref, ldt_ref, bre_ref, bim_ref, cre_ref, cim_ref,
                  toep_ref, wsr_ref, wsi_ref, wor_ref, woi_ref, atr_ref, ati_ref):
    tc, g = S5_CHUNK, S5_GROUP
    w = tc * g
    lane = lax.broadcasted_iota(jnp.int32, (g, 2 * S5_STATE), 1)
    in_group = (lane < S5_STATE, lane >= S5_STATE)
    lane_w = lax.broadcasted_iota(jnp.int32, (g, w), 1)
    nt = (((1,), (1,)), ((), ()))
    toep_rows = [[jnp.zeros((g, w), F32) for _ in range(tc)] for _ in range(2)]
    for d in range(2):
        lr, li = are_ref[d, pp], aim_ref[d, pp]
        dt = jnp.exp(ldt_ref[d, pp])
        mag = jnp.exp(lr * dt)
        ab_re, ab_im = mag * jnp.cos(li * dt), mag * jnp.sin(li * dt)
        den = lr * lr + li * li
        nr, ni = ab_re - 1.0, ab_im
        co_re = (nr * lr + ni * li) / den
        co_im = (ni * lr - nr * li) / den
        br, bi = bre_ref[d, pp], bim_ref[d, pp]
        bb_re = co_re * br - co_im * bi
        bb_im = co_re * bi + co_im * br
        pw = [(jnp.ones_like(ab_re), jnp.zeros_like(ab_re))]
        for _ in range(tc):
            pr, pi = pw[-1]
            pw.append((pr * ab_re - pi * ab_im, pr * ab_im + pi * ab_re))
        cr, ci = cre_ref[d, pp], cim_ref[d, pp]
        ca = [(cr * pr - ci * pi, cr * pi + ci * pr) for pr, pi in pw]
        taus = list(range(tc))[::-1] if d else list(range(tc))
        y_re = jnp.concatenate([ca[t][0] for t in taus], axis=0).astype(BF16)
        y_im = jnp.concatenate([ca[t][1] for t in taus], axis=0).astype(BF16)
        for j in range(2):
            x_re = jnp.where(in_group[j], bb_re, 0.0)
            x_im = jnp.where(in_group[j], bb_im, 0.0)
            kt = (lax.dot_general(x_re.astype(BF16), y_re, nt, preferred_element_type=F32)
                  - lax.dot_general(x_im.astype(BF16), y_im, nt, preferred_element_type=F32))
            for r in range(tc):
                sh = (r + 1) * g if d else r * g
                blk = pltpu.roll(kt, sh % w, 1) if sh % w else kt
                keep = (lane_w < sh) if d else (lane_w >= sh)
                toep_rows[j][r] = toep_rows[j][r] + jnp.where(keep, blk, 0.0)
        for r in range(tc):
            pr, pi = pw[r] if d else pw[tc - 1 - r]
            w_re = bb_re * pr - bb_im * pi
            w_im = bb_re * pi + bb_im * pr
            car, cai = ca[tc - r] if d else ca[r + 1]
            for j in range(2):
                rows = slice(j * w + r * g, j * w + (r + 1) * g)
                wsr_ref[d, pp, rows, :] = jnp.where(in_group[j], w_re, 0.0).astype(BF16)
                wsi_ref[d, pp, rows, :] = jnp.where(in_group[j], w_im, 0.0).astype(BF16)
                wor_ref[d, pp, rows, :] = jnp.where(in_group[j], car, 0.0).astype(BF16)
                woi_ref[d, pp, rows, :] = jnp.where(in_group[j], -cai, 0.0).astype(BF16)
        atr_ref[d, pp] = pw[tc][0]
        ati_ref[d, pp] = pw[tc][1]
    for j in range(2):
        toep_ref[pp, j] = jnp.concatenate(toep_rows[j], axis=0).astype(BF16)


S5_PREP_PAIRS = 4


def _s5_prep_kernel(*refs):
    for pp in range(S5_PREP_PAIRS):
        _s5_prep_pair(pp, *refs)


def _s5_prep(params):
    tc = S5_CHUNK
    wide = 2 * tc * S5_GROUP
    sl = 2 * S5_STATE

    def pair_lanes(v):
        return v.astype(F32).reshape(2, S5_PAIRS, 1, sl)

    def pair_rows(v):
        rows = v.shape[2]
        return v.astype(F32).reshape(2, S5_PAIRS, 2, rows, S5_STATE).transpose(0, 1, 3, 2, 4).reshape(
            2, S5_PAIRS, rows, sl)

    ldt = jnp.broadcast_to(params['s5_log_dt'].astype(F32)[:, :, None], (2, S5_GROUPS, S5_STATE))
    ins = [pair_lanes(params['s5_a_re']), pair_lanes(params['s5_a_im']), pair_lanes(ldt),
           pair_rows(params['s5_b_re'].transpose(0, 1, 3, 2)), pair_rows(params['s5_b_im'].transpose(0, 1, 3, 2)),
           pair_rows(params['s5_c_re']), pair_rows(params['s5_c_im'])]
    pp = S5_PREP_PAIRS
    vec_spec = pl.BlockSpec((2, pp, 1, sl), lambda k: (0, k, 0, 0))
    mat_spec = pl.BlockSpec((2, pp, S5_GROUP, sl), lambda k: (0, k, 0, 0))
    w_spec = pl.BlockSpec((2, pp, wide, sl), lambda k: (0, k, 0, 0))
    w_sds = jax.ShapeDtypeStruct((2, S5_PAIRS, wide, sl), BF16)
    a_sds = jax.ShapeDtypeStruct((2, S5_PAIRS, 1, sl), F32)
    return pl.pallas_call(
        _s5_prep_kernel,
        grid=(S5_PAIRS // pp,),
        in_specs=[vec_spec] * 3 + [mat_spec] * 4,
        out_specs=[pl.BlockSpec((pp, 2, wide // 2, wide // 2), lambda k: (k, 0, 0, 0))] + [w_spec] * 4 + [vec_spec] * 2,
        out_shape=[jax.ShapeDtypeStruct((S5_PAIRS, 2, wide // 2, wide // 2), BF16)] + [w_sds] * 4 + [a_sds] * 2,
        compiler_params=_cparams(1),
        name="s5_prep",
    )(*ins)


S5_STEP_PAIRS = 4
def _s5_state_rows(batch, n_lat, n_ctx):
    def pitch(n):
        p = -(-n // SUBLANES)
        return SUBLANES * (p + 1 - p % 2)

    lat_pitch, ctx_pitch = pitch(n_lat), pitch(n_ctx)
    ctx_base = batch * lat_pitch
    return lat_pitch, ctx_pitch, ctx_base, ctx_base + batch * ctx_pitch


def _s5_kernel(ul_ref, uc_ref, dsk_ref, *refs, batch, seq, ctx):
    toep_ref, wsr, wsi, wor, woi, atr, ati = refs[:7]
    o_ref = refs[7]
    wscr, zscr, upscr, sre, sim, hre, him = refs[8:]
    tc = S5_CHUNK
    n_lat, n_ctx = seq // tc, ctx // tc
    rows_lat = n_lat * batch
    lat_pitch, ctx_pitch, ctx_base, _ = _s5_state_rows(batch, n_lat, n_ctx)
    w = tc * S5_GROUP
    lanes = S5_STEP_PAIRS * 2 * S5_GROUP

    def scatter_tiles(xt, r, col0):
        for kk in range(S5_STEP_PAIRS):
            for j in range(2):
                ch = (2 * kk + j) * S5_GROUP
                wscr[kk, j * w + r * S5_GROUP:j * w + (r + 1) * S5_GROUP, col0:col0 + lanes] = xt[ch:ch + S5_GROUP, :]

    for r in range(tc):
        for b in range(batch):
            x = ul_ref[pl.ds(b * seq + r, n_lat, stride=tc), :]
            scatter_tiles(x.T, r, b * n_lat)
        xc = jnp.concatenate([uc_ref[pl.ds(b * ctx + r, n_ctx, stride=tc), :] for b in range(batch)]
                             + [jnp.zeros((lanes - batch * n_ctx, lanes), F32)], axis=0)
        scatter_tiles(xc.T, r, rows_lat)

    chains = [(kk, d) for kk in range(S5_STEP_PAIRS) for d in range(2)]
    for kk in range(S5_STEP_PAIRS):
        upscr[kk] = wscr[kk].T.astype(BF16)
        for d in range(2):
            for dst, wst in ((sre, wsr), (sim, wsi)):
                s = jnp.dot(upscr[kk], wst[d, kk], preferred_element_type=F32)
                for b in range(batch):
                    dst[kk, d, b * lat_pitch:b * lat_pitch + n_lat, :] = s[b * n_lat:(b + 1) * n_lat]
                    dst[kk, d, ctx_base + b * ctx_pitch:ctx_base + b * ctx_pitch + n_ctx, :] = (
                        s[rows_lat + b * n_ctx:rows_lat + (b + 1) * n_ctx])

    ctx_rows = [pl.ds(ctx_base + c, batch, stride=ctx_pitch) for c in range(n_ctx)]
    lat_rows = [pl.ds(c, batch, stride=lat_pitch) for c in range(n_lat)]
    order = (ctx_rows + lat_rows, ctx_rows[::-1] + lat_rows[::-1])
    coef = {(kk, d): (atr[d, kk], ati[d, kk]) for kk, d in chains}
    state = {ch: (jnp.zeros((batch, 2 * S5_STATE), F32), jnp.zeros((batch, 2 * S5_STATE), F32)) for ch in chains}
    for t in range(n_ctx + n_lat):
        for kk, d in chains:
            rows = order[d][t]
            (ar, ai), (h_re, h_im) = coef[kk, d], state[kk, d]
            hre[kk, d, rows, :] = h_re
            him[kk, d, rows, :] = h_im
            state[kk, d] = (ar * h_re - ai * h_im + sre[kk, d, rows, :],
                            ar * h_im + ai * h_re + sim[kk, d, rows, :])

    for kk in range(S5_STEP_PAIRS):
        ul = upscr[kk, :rows_lat, :]
        y = ul.astype(F32) * dsk_ref[kk]
        y = y + jnp.concatenate(
            [jnp.dot(ul[:, :w], toep_ref[kk, 0], preferred_element_type=F32),
             jnp.dot(ul[:, w:], toep_ref[kk, 1], preferred_element_type=F32)], axis=1)
        nt = (((1,), (1,)), ((), ()))
        for d in range(2):
            h_r = jnp.concatenate([hre[kk, d, b * lat_pitch:b * lat_pitch + n_lat, :] for b in range(batch)], axis=0)
            h_i = jnp.concatenate([him[kk, d, b * lat_pitch:b * lat_pitch + n_lat, :] for b in range(batch)], axis=0)
            y = y + lax.dot_general(h_r.astype(BF16), wor[d, kk], nt, preferred_element_type=F32)
            y = y + lax.dot_general(h_i.astype(BF16), woi[d, kk], nt, preferred_element_type=F32)
        yt = y.T
        for b in range(batch):
            for s in range(tc):
                for j in range(2):
                    ch = (2 * kk + j) * S5_GROUP
                    zscr[b * tc + s, ch:ch + S5_GROUP, :] = yt[j * w + s * S5_GROUP:j * w + (s + 1) * S5_GROUP,
                                                               b * n_lat:(b + 1) * n_lat]

    for b in range(batch):
        for s in range(tc):
            o_ref[pl.ds(b * seq + s, n_lat, stride=tc), :] = jax.nn.gelu(zscr[b * tc + s].T)


def _s5_mixer(u, params, batch, seq, ctx):
    tc = S5_CHUNK
    n_lat, n_ctx = seq // tc, ctx // tc
    sp = S5_STEP_PAIRS
    lanes = sp * 2 * S5_GROUP
    assert n_lat == lanes and batch * n_ctx <= lanes
    rows_lat = n_lat * batch
    rows_all = rows_lat + lanes
    state_rows = _s5_state_rows(batch, n_lat, n_ctx)[3]
    wide = 2 * tc * S5_GROUP
    dsk = jnp.broadcast_to(params['s5_d'].astype(F32).reshape(S5_PAIRS, 2, 1, S5_GROUP),
                           (S5_PAIRS, 2, tc, S5_GROUP)).reshape(S5_PAIRS, 1, wide)
    weights = _s5_prep(params)
    w_spec = pl.BlockSpec((2, sp, wide, 2 * S5_STATE), lambda k: (0, k, 0, 0))
    a_spec = pl.BlockSpec((2, sp, 1, 2 * S5_STATE), lambda k: (0, k, 0, 0))
    specs = [pl.BlockSpec((sp, 2, wide // 2, wide // 2), lambda k: (k, 0, 0, 0))] + [w_spec] * 4 + [a_spec] * 2
    return pl.pallas_call(
        functools.partial(_s5_kernel, batch=batch, seq=seq, ctx=ctx),
        grid=(S5_PAIRS // sp,),
        in_specs=[pl.BlockSpec((batch * seq, lanes), lambda k: (0, k)),
                  pl.BlockSpec((batch * ctx, lanes), lambda k: (seq // ctx, k)),
                  pl.BlockSpec((sp, 1, wide), lambda k: (k, 0, 0))] + specs,
        out_specs=pl.BlockSpec((batch * seq, lanes), lambda k: (0, k)),
        out_shape=jax.ShapeDtypeStruct((batch * seq, S5_WIDTH), F32),
        scratch_shapes=[pltpu.VMEM((sp, wide, rows_all), F32),
                        pltpu.VMEM((batch * tc, lanes, n_lat), F32),
                        pltpu.VMEM((sp, rows_all, wide), BF16)]
                       + [pltpu.VMEM((sp, 2, state_rows, 2 * S5_STATE), F32) for _ in range(4)],
        compiler_params=_cparams(1),
        name="s5",
    )(u, u, dsk, *weights)


def _merge_kernel(z5_ref, o_ref_in, gs_ref, gm_ref, wa_ref, wb_ref, wm_ref, out_ref, wa_s, wb_s, wm_s):
    @pl.when(pl.program_id(1) == 0)
    def _():
        wa_s[...] = wa_ref[...].astype(BF16)
        wb_s[...] = wb_ref[...].astype(BF16)
        wm_s[...] = wm_ref[...].astype(BF16)

    sub = min(MERGE_SUB_ROWS, out_ref.shape[0])
    for r0 in range(0, out_ref.shape[0], sub):
        rows = slice(r0, r0 + sub)
        z = z5_ref[rows, :].astype(BF16)
        a = jnp.dot(z, wa_s[...], preferred_element_type=F32)
        b = jnp.dot(z, wb_s[...], preferred_element_type=F32)
        mla = jnp.dot(o_ref_in[rows, :], wm_s[...], preferred_element_type=F32)
        merged = gs_ref[rows, :].astype(F32) * (a * jax.nn.sigmoid(b)) + gm_ref[rows, :].astype(F32) * mla
        out_ref[rows, :] = merged.astype(out_ref.dtype)


def _merge(y5, o_mla, gates, w_glu, w_mla_o, tm, tn):
    t = y5.shape[0]
    nj = D_MODEL // tn
    return pl.pallas_call(
        _merge_kernel,
        grid=(nj, t // tm),
        in_specs=[pl.BlockSpec((tm, S5_WIDTH), lambda j, i: (i, 0)),
                  pl.BlockSpec((tm, MLA_HEADS * V_DIM), lambda j, i: (i, 0)),
                  pl.BlockSpec((tm, tn), lambda j, i: (i, j)),
                  pl.BlockSpec((tm, tn), lambda j, i: (i, nj + j)),
                  pl.BlockSpec((S5_WIDTH, tn), lambda j, i: (0, j)),
                  pl.BlockSpec((S5_WIDTH, tn), lambda j, i: (0, nj + j)),
                  pl.BlockSpec((MLA_HEADS * V_DIM, tn), lambda j, i: (0, j))],
        out_specs=pl.BlockSpec((tm, tn), lambda j, i: (i, j)),
        out_shape=jax.ShapeDtypeStruct((t, D_MODEL), BF16),
        scratch_shapes=[pltpu.VMEM((S5_WIDTH, tn), BF16), pltpu.VMEM((S5_WIDTH, tn), BF16),
                        pltpu.VMEM((MLA_HEADS * V_DIM, tn), BF16)],
        compiler_params=_cparams(2),
        name="merge",
    )(y5, o_mla, gates, gates, w_glu, w_glu, w_mla_o)


def _out_proj_norm_kernel(a_ref, w_ref, x_ref, g1_ref, n2_ref, sc_ref, sh_ref, x1_ref, xn_ref, w_s):
    @pl.when(pl.program_id(0) == 0)
    def _():
        w_s[...] = w_ref[...].astype(BF16)

    sub = min(MERGE_SUB_ROWS, x1_ref.shape[0])
    for r0 in range(0, x1_ref.shape[0], sub):
        rows = slice(r0, r0 + sub)
        x1 = x_ref[rows, :] + g1_ref[0, 0] * jnp.dot(a_ref[rows, :], w_s[...], preferred_element_type=F32)
        x1_ref[rows, :] = x1
        xn_ref[rows, :] = (_rms(x1, n2_ref[...]) * (1.0 + sc_ref[0, 0]) + sh_ref[0, 0]).astype(xn_ref.dtype)


def _out_proj_norm(merged, w_out, x_res, m, gain2, rows_per_batch, tm):
    t, k = merged.shape
    tpb = rows_per_batch // tm
    row = pl.BlockSpec((tm, D_MODEL), lambda i: (i, 0))
    gate1, scale2, shift2 = (_mod_spec(D_MODEL, lambda i: i // tpb, which)
                             for which in (MOD_GATE1, MOD_SCALE2, MOD_SHIFT2))
    return pl.pallas_call(
        _out_proj_norm_kernel,
        grid=(t // tm,),
        in_specs=[pl.BlockSpec((tm, k), lambda i: (i, 0)),
                  pl.BlockSpec((k, D_MODEL), lambda i: (0, 0), pipeline_mode=pl.Buffered(1)),
                  row, gate1, pl.BlockSpec((1, D_MODEL), lambda i: (0, 0)), scale2, shift2],
        out_specs=[row, row],
        out_shape=[jax.ShapeDtypeStruct((t, D_MODEL), F32), jax.ShapeDtypeStruct((t, D_MODEL), BF16)],
        scratch_shapes=[pltpu.VMEM((k, D_MODEL), BF16)],
        compiler_params=_cparams(1),
        name="out_proj",
    )(merged, w_out, x_res, m, gain2.reshape(1, D_MODEL), m, m)


def _ffn_in(xn, w_ffn_in, tm, tn):
    def epi(accs, e_refs, o_refs, rows):
        o_refs[0][rows, :] = (jax.nn.silu(accs[0]) * accs[1]).astype(BF16)

    t = xn.shape[0]
    nj = D_FF // tn
    outs = [(jax.ShapeDtypeStruct((t, D_FF), BF16), pl.BlockSpec((tm, tn), lambda j, i: (i, j)))]
    return _fused_mm(xn, [(w_ffn_in, 0, "kn", tn), (w_ffn_in, nj, "kn", tn)], epi, [], outs,
                     tm=tm, nj=nj, name="ffn_in", sub_rows=MM_SUB_ROWS)[0]


def _ffn_out_kernel(h_ref, w_ref, x_ref, g2_ref, nf_ref, o_ref, w_s, *, last):
    @pl.when(pl.program_id(0) == 0)
    def _():
        w_s[...] = w_ref[...].astype(BF16)

    y = x_ref[...] + g2_ref[0, 0] * jnp.dot(h_ref[...], w_s[...], preferred_element_type=F32)
    o_ref[...] = _rms(y, nf_ref[...]) if last else y


def _ffn_out(hid, w_ffn_out, x_res, m, norm_f, rows_per_batch, tm):
    t, k = hid.shape
    kh = k // 2
    tpb = rows_per_batch // tm
    row = pl.BlockSpec((tm, D_MODEL), lambda i: (i, 0))
    y = x_res
    for half in range(2):
        y = pl.pallas_call(
            functools.partial(_ffn_out_kernel, last=half == 1),
            grid=(t // tm,),
            in_specs=[pl.BlockSpec((tm, kh), functools.partial(lambda i, half: (i, half), half=half)),
                      pl.BlockSpec((kh, D_MODEL), functools.partial(lambda i, half: (half, 0), half=half),
                                   pipeline_mode=pl.Buffered(1)),
                      row, _mod_spec(D_MODEL, lambda i: i // tpb, MOD_GATE2),
                      pl.BlockSpec((1, D_MODEL), lambda i: (0, 0))],
            out_specs=row,
            out_shape=jax.ShapeDtypeStruct((t, D_MODEL), F32),
            scratch_shapes=[pltpu.VMEM((kh, D_MODEL), BF16)],
            compiler_params=_cparams(1),
            name="ffn_out",
        )(hid, w_ffn_out, y, m, norm_f.reshape(1, D_MODEL))
    return y


def _rope_rot_cols(w):
    k = w.shape[0]
    ws = w.reshape(k, -1, 2, 2, QK_ROPE // 4)
    return jnp.stack([-ws[:, :, :, 1, :], ws[:, :, :, 0, :]], axis=3).reshape(k, -1)


def _rope_tables(n_tokens):
    rows = n_tokens // GRID_W
    row = jnp.repeat(jnp.arange(rows, dtype=F32), GRID_W)
    col = jnp.tile(jnp.arange(GRID_W, dtype=F32), rows)
    n_freq = QK_ROPE // 4
    inv = ROPE_BASE ** (-jnp.arange(n_freq, dtype=F32) / n_freq)
    ang = jnp.stack([row[:, None] * inv, col[:, None] * inv], axis=1)
    cos = jnp.broadcast_to(jnp.cos(ang)[:, :, None, :], (n_tokens, 2, 2, n_freq)).reshape(n_tokens, QK_ROPE)
    sin = jnp.broadcast_to(jnp.sin(ang)[:, :, None, :], (n_tokens, 2, 2, n_freq)).reshape(n_tokens, QK_ROPE)
    return cos, sin


def kernel(x, c, ctx, c_ctx, w_mod, b_mod, norm1, norm2, w_in, s5_a_re, s5_a_im, s5_log_dt, s5_b_re, s5_b_im,
           s5_c_re, s5_c_im, s5_d, w_glu, q_norm, kv_norm, w_uq, w_ukv, w_mla_o, w_out, w_ffn_in, w_ffn_out,
           norm_f):
    batch, seq, _ = x.shape
    n_ctx = ctx.shape[1]
    assert w_mod.shape[0] == 1, "single-layer block"
    p = dict(s5_a_re=s5_a_re[0], s5_a_im=s5_a_im[0], s5_log_dt=s5_log_dt[0], s5_b_re=s5_b_re[0],
             s5_b_im=s5_b_im[0], s5_c_re=s5_c_re[0], s5_c_im=s5_c_im[0], s5_d=s5_d[0])
    w_in_t = w_in.reshape(w_in.shape[1:]).T

    cv = jnp.concatenate([c, c_ctx[None], jnp.zeros((8 - batch - 1, D_MODEL), F32)], axis=0)
    m = _modulation(cv, w_mod[0], b_mod[0]).reshape(8, 6, 1, D_MODEL)

    lat_rows = batch * seq
    x2d = x.reshape(lat_rows, D_MODEL)
    c2d = ctx.reshape(batch * n_ctx, D_MODEL)
    xn, u = _norm_u_proj(x2d, c2d, norm1[0], m, w_in_t, seq, batch, 512)

    kv_lo = S5_WIDTH + Q_RANK
    w_kv_rows = w_in_t[kv_lo:kv_lo + KV_RANK + QK_ROPE]
    w_kv_t = jnp.concatenate([w_kv_rows, _rope_rot_cols(w_kv_rows[KV_RANK:].T).T], axis=0)
    wq = w_uq[0].reshape(Q_RANK, MLA_HEADS, QK_NOPE + QK_ROPE)
    wq_rope = wq[:, :, QK_NOPE:].reshape(Q_RANK, MLA_HEADS * QK_ROPE)
    wq2 = jnp.concatenate([wq[:, :, :QK_NOPE].reshape(Q_RANK, MLA_HEADS * QK_NOPE), wq_rope,
                           _rope_rot_cols(wq_rope)], axis=1).astype(BF16)
    w_ukv_bf = w_ukv[0].astype(BF16)
    tm = 1024
    cos, sin = _rope_tables(seq)
    cos_sin_k = jnp.concatenate([jnp.concatenate([cos, sin], axis=1),
                                 jnp.concatenate([jnp.ones((tm, QK_ROPE), F32), jnp.zeros((tm, QK_ROPE), F32)], axis=1)],
                                axis=0)

    q = _q_path(xn, w_in_t, q_norm[0], wq2, jnp.tile(cos, (1, 2)), jnp.tile(sin, (1, 2)), tm, seq, lat_rows)
    kv, kr = _kv_path(xn, w_kv_t, kv_norm[0], w_ukv_bf, cos_sin_k, tm, seq, lat_rows)
    gates = _gates(xn, w_in_t, kv_lo + KV_RANK + QK_ROPE, tm, 1024, lat_rows)

    z5 = _s5_mixer(u, p, batch, seq, n_ctx)
    o_mla = _attention(q, kv, kr, batch, seq, n_ctx, 2048)

    merged = _merge(z5, o_mla, gates, w_glu[0], w_mla_o[0], 512, 1024)
    x1, xn2 = _out_proj_norm(merged, w_out[0], x2d, m, norm2[0], seq, 512)
    hid = _ffn_in(xn2, w_ffn_in[0], 2048, 512)
    return _ffn_out(hid, w_ffn_out[0], x1, m, norm_f, seq, 256).reshape(batch, seq, D_MODEL)
```

```python
import functools

import jax
import jax.numpy as jnp
import numpy as np
from jax import lax
from jax.experimental import pallas as pl
from jax.experimental.pallas import tpu as pltpu

F32 = jnp.float32
BF16 = jnp.bfloat16

D_MODEL = 2048
GRID_W = 64
EPS = 1e-6
S5_WIDTH = D_MODEL // 2
S5_GROUP = 16
S5_GROUPS = S5_WIDTH // S5_GROUP
S5_STATE = 64
S5_CHUNK = 16
S5_PAIRS = S5_GROUPS // 2
MLA_HEADS = 8
QK_NOPE = 128
QK_ROPE = 64
V_DIM = 128
Q_RANK = 512
KV_RANK = 256
ROPE_BASE = 10000.0
ATTN_SCALE = (QK_NOPE + QK_ROPE) ** -0.5
D_FF = -(-8 * D_MODEL // (3 * 256)) * 256

VMEM_LIMIT_BYTES = 56 * 1024 * 1024
SUBLANES = 8
MM_SUB_ROWS = 512
MERGE_SUB_ROWS = 256


def _cparams(n_axes):
    return pltpu.CompilerParams(dimension_semantics=("arbitrary",) * n_axes,
                                vmem_limit_bytes=VMEM_LIMIT_BYTES)


def _rms(x, g):
    return x * lax.rsqrt(jnp.mean(x * x, axis=-1, keepdims=True) + EPS) * g


def _mod_kernel(cv_ref, w_ref, b_ref, o_ref):
    s = jax.nn.silu(cv_ref[...]).astype(BF16)
    o_ref[...] = jnp.dot(s, w_ref[...].astype(BF16), preferred_element_type=F32) + b_ref[...]


def _modulation(cv, w_mod, b_mod):
    n = w_mod.shape[1]
    tn = 1536
    return pl.pallas_call(
        _mod_kernel,
        grid=(n // tn,),
        in_specs=[pl.BlockSpec((8, D_MODEL), lambda j: (0, 0)),
                  pl.BlockSpec((D_MODEL, tn), lambda j: (0, j)),
                  pl.BlockSpec((1, tn), lambda j: (0, j))],
        out_specs=pl.BlockSpec((8, tn), lambda j: (0, j)),
        out_shape=jax.ShapeDtypeStruct((8, n), F32),
        compiler_params=_cparams(1),
        name="mod",
    )(cv, w_mod, b_mod.reshape(1, n))


MOD_SHIFT1, MOD_SCALE1, MOD_GATE1, MOD_SHIFT2, MOD_SCALE2, MOD_GATE2 = range(6)


def _mod_spec(width, row_of, which, col_of=None):
    col_of = col_of or (lambda *ids: 0)
    return pl.BlockSpec((1, 1, 1, width), lambda *ids: (row_of(*ids), which, 0, col_of(*ids)))


def _norm_u_kernel(x_ref, c_ref, g_ref, sc_ref, sh_ref, w_ref, xn_ref, u_ref, w_s, *, lat_tiles):
    @pl.when(pl.program_id(0) == 0)
    def _():
        w_s[...] = w_ref[...].astype(BF16)

    is_lat = pl.program_id(0) < lat_tiles
    sub = min(MERGE_SUB_ROWS, xn_ref.shape[0])
    for r0 in range(0, xn_ref.shape[0], sub):
        rows = slice(r0, r0 + sub)
        src = jnp.where(is_lat, x_ref[rows, :], c_ref[rows, :])
        xn = (_rms(src, g_ref[...]) * (1.0 + sc_ref[0, 0]) + sh_ref[0, 0]).astype(xn_ref.dtype)
        xn_ref[rows, :] = xn
        u_ref[rows, :] = lax.dot_general(xn, w_s[...], (((1,), (1,)), ((), ())), preferred_element_type=F32)


def _norm_u_proj(x2d, c2d, gain, m, w_in_t, rows_per_batch, ctx_row, tm):
    lat_tiles, ctx_tiles = x2d.shape[0] // tm, c2d.shape[0] // tm
    tpb = rows_per_batch // tm
    t_all = x2d.shape[0] + c2d.shape[0]

    def row(i):
        return jnp.where(i < lat_tiles, i // tpb, ctx_row)

    once = pl.Buffered(1)
    return pl.pallas_call(
        functools.partial(_norm_u_kernel, lat_tiles=lat_tiles),
        grid=(lat_tiles + ctx_tiles,),
        in_specs=[pl.BlockSpec((tm, D_MODEL), lambda i: (jnp.minimum(i, lat_tiles - 1), 0)),
                  pl.BlockSpec((tm, D_MODEL), lambda i: (jnp.maximum(i - lat_tiles, 0), 0)),
                  pl.BlockSpec((1, D_MODEL), lambda i: (0, 0)),
                  _mod_spec(D_MODEL, row, MOD_SCALE1), _mod_spec(D_MODEL, row, MOD_SHIFT1),
                  pl.BlockSpec((S5_WIDTH, D_MODEL), lambda i: (0, 0), pipeline_mode=once)],
        out_specs=[pl.BlockSpec((tm, D_MODEL), lambda i: (i, 0)), pl.BlockSpec((tm, S5_WIDTH), lambda i: (i, 0))],
        out_shape=[jax.ShapeDtypeStruct((t_all, D_MODEL), BF16), jax.ShapeDtypeStruct((t_all, S5_WIDTH), F32)],
        scratch_shapes=[pltpu.VMEM((S5_WIDTH, D_MODEL), BF16)],
        compiler_params=_cparams(1),
        name="norm_u_proj",
    )(x2d, c2d, gain.reshape(1, D_MODEL), m, m, w_in_t)


def _fused_mm(a, weights, epilogue, extras, outs, *, tm, nj, name, rows=None, sub_rows=None):
    t, k = a.shape
    ni = (rows or t) // tm
    nw, ne, no = len(weights), len(extras), len(outs)
    need_cast = [w.dtype != BF16 for w, _, _, _ in weights]
    nt = (((1,), (1,)), ((), ()))

    def kernel(*refs):
        a_ref = refs[0]
        w_refs = refs[1:1 + nw]
        e_refs = refs[1 + nw:1 + nw + ne]
        o_refs = refs[1 + nw + ne:1 + nw + ne + no]
        s_refs = list(refs[1 + nw + ne + no:])
        staged = {idx: s_refs.pop(0) for idx in range(nw) if need_cast[idx]}
        if staged:
            @pl.when(pl.program_id(1) == 0)
            def _():
                for idx, s_ref in staged.items():
                    s_ref[...] = w_refs[idx][...].astype(BF16)

        for r0 in range(0, tm, sub_rows or tm):
            rows = slice(r0, r0 + (sub_rows or tm))
            av = a_ref[rows, :]
            accs = []
            for idx in range(nw):
                w_ref = staged.get(idx, w_refs[idx])
                if weights[idx][2] == "kn":
                    accs.append(jnp.dot(av, w_ref[...], preferred_element_type=F32))
                else:
                    accs.append(lax.dot_general(av, w_ref[...], nt, preferred_element_type=F32))
            epilogue(accs, e_refs, o_refs, rows)

    in_specs = [pl.BlockSpec((tm, k), lambda j, i: (i, 0))]
    scratch = []
    for (w, off, layout, width), cast in zip(weights, need_cast):
        if layout == "kn":
            shape = (k, width)
            in_specs.append(pl.BlockSpec(shape, functools.partial(lambda j, i, off: (0, off + j), off=off)))
        else:
            shape = (width, k)
            in_specs.append(pl.BlockSpec(
                (pl.Element(width), pl.Element(k)),
                functools.partial(lambda j, i, off, width: (pl.multiple_of(off + j * width, SUBLANES), 0),
                                  off=off, width=width)))
        if cast:
            scratch.append(pltpu.VMEM(shape, BF16))
    in_specs += [spec for _, spec in extras]
    return pl.pallas_call(
        kernel,
        grid=(nj, ni),
        in_specs=in_specs,
        out_specs=[spec for _, spec in outs],
        out_shape=[sds for sds, _ in outs],
        scratch_shapes=scratch,
        compiler_params=_cparams(2),
        name=name,
    )(a, *[w[0] for w in weights], *[e for e, _ in extras])


def _gates(xn, w_in_t, row0, tm, tn, rows):
    def epi(accs, e_refs, o_refs, rows):
        o_refs[0][rows, :] = jax.nn.sigmoid(accs[0]).astype(BF16)

    n = 2 * D_MODEL
    return _fused_mm(xn, [(w_in_t, row0, "nk", tn)], epi, [],
                     [(jax.ShapeDtypeStruct((rows, n), BF16), pl.BlockSpec((tm, tn), lambda j, i: (i, j)))],
                     tm=tm, nj=n // tn, name="gates", rows=rows)[0]


def _q_path(xn, w_in_t, q_norm, wq2, cos2, sin2, tm, seq, rows):
    nr = MLA_HEADS * QK_ROPE
    nn = MLA_HEADS * QK_NOPE
    lanes = 2 * QK_ROPE

    def epi(accs, e_refs, o_refs, rows):
        qn_ref, w2_ref, cos_ref, sin_ref = e_refs
        cq = _rms(accs[0], qn_ref[...]).astype(BF16)
        q = jnp.dot(cq, w2_ref[...], preferred_element_type=F32)
        o_refs[0][rows, :nn] = (q[:, :nn] * ATTN_SCALE).astype(BF16)
        cos, sin = cos_ref[rows, :] * ATTN_SCALE, sin_ref[rows, :] * ATTN_SCALE
        for c0 in range(0, nr, lanes):
            rope = q[:, nn + c0:nn + c0 + lanes] * cos + q[:, nn + nr + c0:nn + nr + c0 + lanes] * sin
            o_refs[0][rows, nn + c0:nn + c0 + lanes] = rope.astype(BF16)

    pos_tiles = seq // tm
    extras = [(q_norm.reshape(1, Q_RANK), pl.BlockSpec((1, Q_RANK), lambda j, i: (0, 0))),
              (wq2, pl.BlockSpec(wq2.shape, lambda j, i: (0, 0))),
              (cos2, pl.BlockSpec((tm, lanes), lambda j, i: (i % pos_tiles, 0))),
              (sin2, pl.BlockSpec((tm, lanes), lambda j, i: (i % pos_tiles, 0)))]
    outs = [(jax.ShapeDtypeStruct((rows, nn + nr), BF16), pl.BlockSpec((tm, nn + nr), lambda j, i: (i, 0)))]
    return _fused_mm(xn, [(w_in_t, S5_WIDTH, "nk", Q_RANK)], epi, extras, outs, tm=tm, nj=1, name="q_path",
                     rows=rows)[0]


def _kv_path(xn, w_kv_t, kv_norm, w_ukv_bf, cos_sin_k, tm, seq, lat_rows):
    nkv = w_ukv_bf.shape[1]

    def epi(accs, e_refs, o_refs, rows):
        acc = accs[0]
        ckv = _rms(acc[:, :KV_RANK], e_refs[0][...]).astype(BF16)
        o_refs[0][rows, :] = jnp.dot(ckv, e_refs[1][...], preferred_element_type=F32).astype(BF16)
        prod = acc[:, KV_RANK:] * e_refs[2][rows, :]
        o_refs[1][rows, :] = (prod + pltpu.roll(prod, QK_ROPE, 1))[:, :QK_ROPE].astype(BF16)

    t = xn.shape[0]
    pos_tiles, lat_tiles = seq // tm, lat_rows // tm
    extras = [(kv_norm.reshape(1, KV_RANK), pl.BlockSpec((1, KV_RANK), lambda j, i: (0, 0))),
              (w_ukv_bf, pl.BlockSpec(w_ukv_bf.shape, lambda j, i: (0, 0))),
              (cos_sin_k, pl.BlockSpec((tm, 2 * QK_ROPE),
                                       lambda j, i: (jnp.where(i < lat_tiles, i % pos_tiles, pos_tiles), 0)))]
    outs = [(jax.ShapeDtypeStruct((t, nkv), BF16), pl.BlockSpec((tm, nkv), lambda j, i: (i, 0))),
            (jax.ShapeDtypeStruct((t, QK_ROPE), BF16), pl.BlockSpec((tm, QK_ROPE), lambda j, i: (i, 0)))]
    return _fused_mm(xn, [(w_kv_t, 0, "nk", w_kv_t.shape[0])], epi, extras, outs, tm=tm, nj=1, name="kv_path")


ATTN_SUB_ROWS = 256


def _attn_kernel(qn_ref, qr_ref, kvl_ref, kvc_ref, krl_ref, krc_ref, o_ref, k_scr, v_scr, *, seq, ctx):
    dk = QK_NOPE + QK_ROPE

    @pl.when(pl.program_id(2) == 0)
    def _():
        for h in range(2):
            base = h * (QK_NOPE + V_DIM)
            k_scr[h, :seq, :QK_NOPE] = kvl_ref[:, base:base + QK_NOPE]
            k_scr[h, seq:, :QK_NOPE] = kvc_ref[:, base:base + QK_NOPE]
            k_scr[h, :seq, QK_NOPE:dk] = krl_ref[...]
            k_scr[h, seq:, QK_NOPE:dk] = krc_ref[...]
            v_scr[h, :seq, :V_DIM] = kvl_ref[:, base + QK_NOPE:base + QK_NOPE + V_DIM]
            v_scr[h, seq:, :V_DIM] = kvc_ref[:, base + QK_NOPE:base + QK_NOPE + V_DIM]
            ones_col = lax.broadcasted_iota(jnp.int32, (seq + ctx, V_DIM), 1) == 0
            v_scr[h, :, V_DIM:] = jnp.where(ones_col, 1.0, 0.0).astype(BF16)

    for r0 in range(0, qn_ref.shape[0], ATTN_SUB_ROWS):
        rows = slice(r0, r0 + ATTN_SUB_ROWS)
        for h in range(2):
            q = jnp.concatenate([qn_ref[rows, h * QK_NOPE:(h + 1) * QK_NOPE],
                                 qr_ref[rows, h * QK_ROPE:(h + 1) * QK_ROPE]], axis=1)
            s = lax.dot_general(q, k_scr[h], (((1,), (1,)), ((), ())), preferred_element_type=F32)
            m = jnp.max(s, axis=-1, keepdims=True)
            p = jnp.exp((s - m).astype(BF16))
            ol = jnp.dot(p, v_scr[h], preferred_element_type=F32)
            o_ref[rows, h * V_DIM:(h + 1) * V_DIM] = (ol[:, :V_DIM] / ol[:, V_DIM:V_DIM + 1]).astype(o_ref.dtype)


def _attention(q, kv, kr, batch, seq, ctx, tq):
    nq = seq // tq
    ctx0 = batch * seq // ctx
    nn_blocks = MLA_HEADS * QK_NOPE // (2 * QK_NOPE)
    dk = QK_NOPE + QK_ROPE
    hw = 2 * (QK_NOPE + V_DIM)
    return pl.pallas_call(
        functools.partial(_attn_kernel, seq=seq, ctx=ctx),
        grid=(batch, MLA_HEADS // 2, nq),
        in_specs=[pl.BlockSpec((tq, 2 * QK_NOPE), lambda b, hp, qi: (b * nq + qi, hp)),
                  pl.BlockSpec((tq, 2 * QK_ROPE), lambda b, hp, qi: (b * nq + qi, 2 * nn_blocks + hp)),
                  pl.BlockSpec((seq, hw), lambda b, hp, qi: (b, hp)),
                  pl.BlockSpec((ctx, hw), lambda b, hp, qi: (ctx0 + b, hp)),
                  pl.BlockSpec((seq, QK_ROPE), lambda b, hp, qi: (b, 0)),
                  pl.BlockSpec((ctx, QK_ROPE), lambda b, hp, qi: (ctx0 + b, 0))],
        out_specs=pl.BlockSpec((tq, 2 * V_DIM), lambda b, hp, qi: (b * nq + qi, hp)),
        out_shape=jax.ShapeDtypeStruct((batch * seq, MLA_HEADS * V_DIM), BF16),
        scratch_shapes=[pltpu.VMEM((2, seq + ctx, dk), BF16), pltpu.VMEM((2, seq + ctx, 2 * V_DIM), BF16)],
        compiler_params=_cparams(3),
        name="attention",
    )(q, q, kv, kv, kr, kr)


def _s5_prep_pair(pp, are_ref, aim_ref, ldt_ref, bre_ref, bim_ref, cre_ref, cim_ref,
                  toep_ref, wsr_ref, wsi_ref, wor_ref, woi_ref, atr_ref, ati_ref):
    tc, g = S5_CHUNK, S5_GROUP
    w = tc * g
    lane = lax.broadcasted_iota(jnp.int32, (g, 2 * S5_STATE), 1)
    in_group = (lane < S5_STATE, lane >= S5_STATE)
    lane_w = lax.broadcasted_iota(jnp.int32, (g, w), 1)
    nt = (((1,), (1,)), ((), ()))
    toep_rows = [[jnp.zeros((g, w), F32) for _ in range(tc)] for _ in range(2)]
    for d in range(2):
        lr, li = are_ref[d, pp], aim_ref[d, pp]
        dt = jnp.exp(ldt_ref[d, pp])
        mag = jnp.exp(lr * dt)
        ab_re, ab_im = mag * jnp.cos(li * dt), mag * jnp.sin(li * dt)
        den = lr * lr + li * li
        nr, ni = ab_re - 1.0, ab_im
        co_re = (nr * lr + ni * li) / den
        co_im = (ni * lr - nr * li) / den
        br, bi = bre_ref[d, pp], bim_ref[d, pp]
        bb_re = co_re * br - co_im * bi
        bb_im = co_re * bi + co_im * br
        pw = [(jnp.ones_like(ab_re), jnp.zeros_like(ab_re))]
        for _ in range(tc):
            pr, pi = pw[-1]
            pw.append((pr * ab_re - pi * ab_im, pr * ab_im + pi * ab_re))
        cr, ci = cre_ref[d, pp], cim_ref[d, pp]
        ca = [(cr * pr - ci * pi, cr * pi + ci * pr) for pr, pi in pw]
        taus = list(range(tc))[::-1] if d else list(range(tc))
        y_re = jnp.concatenate([ca[t][0] for t in taus], axis=0).astype(BF16)
        y_im = jnp.concatenate([ca[t][1] for t in taus], axis=0).astype(BF16)
        for j in range(2):
            x_re = jnp.where(in_group[j], bb_re, 0.0)
            x_im = jnp.where(in_group[j], bb_im, 0.0)
            kt = (lax.dot_general(x_re.astype(BF16), y_re, nt, preferred_element_type=F32)
                  - lax.dot_general(x_im.astype(BF16), y_im, nt, preferred_element_type=F32))
            for r in range(tc):
                sh = (r + 1) * g if d else r * g
                blk = pltpu.roll(kt, sh % w, 1) if sh % w else kt
                keep = (lane_w < sh) if d else (lane_w >= sh)
                toep_rows[j][r] = toep_rows[j][r] + jnp.where(keep, blk, 0.0)
        for r in range(tc):
            pr, pi = pw[r] if d else pw[tc - 1 - r]
            w_re = bb_re * pr - bb_im * pi
            w_im = bb_re * pi + bb_im * pr
            car, cai = ca[tc - r] if d else ca[r + 1]
            for j in range(2):
                rows = slice(j * w + r * g, j * w + (r + 1) * g)
                wsr_ref[d, pp, rows, :] = jnp.where(in_group[j], w_re, 0.0).astype(BF16)
                wsi_ref[d, pp, rows, :] = jnp.where(in_group[j], w_im, 0.0).astype(BF16)
                wor_ref[d, pp, rows, :] = jnp.where(in_group[j], car, 0.0).astype(BF16)
                woi_ref[d, pp, rows, :] = jnp.where(in_group[j], -cai, 0.0).astype(BF16)
        atr_ref[d, pp] = pw[tc][0]
        ati_ref[d, pp] = pw[tc][1]
    for j in range(2):
        toep_ref[pp, j] = jnp.concatenate(toep_rows[j], axis=0).astype(BF16)


S5_PREP_PAIRS = 4


def _s5_prep_kernel(*refs):
    for pp in range(S5_PREP_PAIRS):
        _s5_prep_pair(pp, *refs)


def _s5_prep(params):
    tc = S5_CHUNK
    wide = 2 * tc * S5_GROUP
    sl = 2 * S5_STATE

    def pair_lanes(v):
        return v.astype(F32).reshape(2, S5_PAIRS, 1, sl)

    def pair_rows(v):
        rows = v.shape[2]
        return v.astype(F32).reshape(2, S5_PAIRS, 2, rows, S5_STATE).transpose(0, 1, 3, 2, 4).reshape(
            2, S5_PAIRS, rows, sl)

    ldt = jnp.broadcast_to(params['s5_log_dt'].astype(F32)[:, :, None], (2, S5_GROUPS, S5_STATE))
    ins = [pair_lanes(params['s5_a_re']), pair_lanes(params['s5_a_im']), pair_lanes(ldt),
           pair_rows(params['s5_b_re'].transpose(0, 1, 3, 2)), pair_rows(params['s5_b_im'].transpose(0, 1, 3, 2)),
           pair_rows(params['s5_c_re']), pair_rows(params['s5_c_im'])]
    pp = S5_PREP_PAIRS
    vec_spec = pl.BlockSpec((2, pp, 1, sl), lambda k: (0, k, 0, 0))
    mat_spec = pl.BlockSpec((2, pp, S5_GROUP, sl), lambda k: (0, k, 0, 0))
    w_spec = pl.BlockSpec((2, pp, wide, sl), lambda k: (0, k, 0, 0))
    w_sds = jax.ShapeDtypeStruct((2, S5_PAIRS, wide, sl), BF16)
    a_sds = jax.ShapeDtypeStruct((2, S5_PAIRS, 1, sl), F32)
    return pl.pallas_call(
        _s5_prep_kernel,
        grid=(S5_PAIRS // pp,),
        in_specs=[vec_spec] * 3 + [mat_spec] * 4,
        out_specs=[pl.BlockSpec((pp, 2, wide // 2, wide // 2), lambda k: (k, 0, 0, 0))] + [w_spec] * 4 + [vec_spec] * 2,
        out_shape=[jax.ShapeDtypeStruct((S5_PAIRS, 2, wide // 2, wide // 2), BF16)] + [w_sds] * 4 + [a_sds] * 2,
        compiler_params=_cparams(1),
        name="s5_prep",
    )(*ins)


S5_STEP_PAIRS = 4
def _s5_state_rows(batch, n_lat, n_ctx):
    def pitch(n):
        p = -(-n // SUBLANES)
        return SUBLANES * (p + 1 - p % 2)

    lat_pitch, ctx_pitch = pitch(n_lat), pitch(n_ctx)
    ctx_base = batch * lat_pitch
    return lat_pitch, ctx_pitch, ctx_base, ctx_base + batch * ctx_pitch


def _s5_kernel(ul_ref, uc_ref, dsk_ref, *refs, batch, seq, ctx):
    toep_ref, wsr, wsi, wor, woi, atr, ati = refs[:7]
    o_ref = refs[7]
    wscr, zscr, upscr, sre, sim, hre, him = refs[8:]
    tc = S5_CHUNK
    n_lat, n_ctx = seq // tc, ctx // tc
    rows_lat = n_lat * batch
    lat_pitch, ctx_pitch, ctx_base, _ = _s5_state_rows(batch, n_lat, n_ctx)
    w = tc * S5_GROUP
    lanes = S5_STEP_PAIRS * 2 * S5_GROUP

    def scatter_tiles(xt, r, col0):
        for kk in range(S5_STEP_PAIRS):
            for j in range(2):
                ch = (2 * kk + j) * S5_GROUP
                wscr[kk, j * w + r * S5_GROUP:j * w + (r + 1) * S5_GROUP, col0:col0 + lanes] = xt[ch:ch + S5_GROUP, :]

    for r in range(tc):
        for b in range(batch):
            x = ul_ref[pl.ds(b * seq + r, n_lat, stride=tc), :]
            scatter_tiles(x.T, r, b * n_lat)
        xc = jnp.concatenate([uc_ref[pl.ds(b * ctx + r, n_ctx, stride=tc), :] for b in range(batch)]
                             + [jnp.zeros((lanes - batch * n_ctx, lanes), F32)], axis=0)
        scatter_tiles(xc.T, r, rows_lat)

    chains = [(kk, d) for kk in range(S5_STEP_PAIRS) for d in range(2)]
    for kk in range(S5_STEP_PAIRS):
        upscr[kk] = wscr[kk].T.astype(BF16)
        for d in range(2):
            for dst, wst in ((sre, wsr), (sim, wsi)):
                s = jnp.dot(upscr[kk], wst[d, kk], preferred_element_type=F32)
                for b in range(batch):
                    dst[kk, d, b * lat_pitch:b * lat_pitch + n_lat, :] = s[b * n_lat:(b + 1) * n_lat]
                    dst[kk, d, ctx_base + b * ctx_pitch:ctx_base + b * ctx_pitch + n_ctx, :] = (
                        s[rows_lat + b * n_ctx:rows_lat + (b + 1) * n_ctx])

    ctx_rows = [pl.ds(ctx_base + c, batch, stride=ctx_pitch) for c in range(n_ctx)]
    lat_rows = [pl.ds(c, batch, stride=lat_pitch) for c in range(n_lat)]
    order = (ctx_rows + lat_rows, ctx_rows[::-1] + lat_rows[::-1])
    coef = {(kk, d): (atr[d, kk], ati[d, kk]) for kk, d in chains}
    state = {ch: (jnp.zeros((batch, 2 * S5_STATE), F32), jnp.zeros((batch, 2 * S5_STATE), F32)) for ch in chains}
    for t in range(n_ctx + n_lat):
        for kk, d in chains:
            rows = order[d][t]
            (ar, ai), (h_re, h_im) = coef[kk, d], state[kk, d]
            hre[kk, d, rows, :] = h_re
            him[kk, d, rows, :] = h_im
            state[kk, d] = (ar * h_re - ai * h_im + sre[kk, d, rows, :],
                            ar * h_im + ai * h_re + sim[kk, d, rows, :])

    for kk in range(S5_STEP_PAIRS):
        ul = upscr[kk, :rows_lat, :]
        y = ul.astype(F32) * dsk_ref[kk]
        y = y + jnp.concatenate(
            [jnp.dot(ul[:, :w], toep_ref[kk, 0], preferred_element_type=F32),
             jnp.dot(ul[:, w:], toep_ref[kk, 1], preferred_element_type=F32)], axis=1)
        nt = (((1,), (1,)), ((), ()))
        for d in range(2):
            h_r = jnp.concatenate([hre[kk, d, b * lat_pitch:b * lat_pitch + n_lat, :] for b in range(batch)], axis=0)
            h_i = jnp.concatenate([him[kk, d, b * lat_pitch:b * lat_pitch + n_lat, :] for b in range(batch)], axis=0)
            y = y + lax.dot_general(h_r.astype(BF16), wor[d, kk], nt, preferred_element_type=F32)
            y = y + lax.dot_general(h_i.astype(BF16), woi[d, kk], nt, preferred_element_type=F32)
        yt = y.T
        for b in range(batch):
            for s in range(tc):
                for j in range(2):
                    ch = (2 * kk + j) * S5_GROUP
                    zscr[b * tc + s, ch:ch + S5_GROUP, :] = yt[j * w + s * S5_GROUP:j * w + (s + 1) * S5_GROUP,
                                                               b * n_lat:(b + 1) * n_lat]

    for b in range(batch):
        for s in range(tc):
            o_ref[pl.ds(b * seq + s, n_lat, stride=tc), :] = jax.nn.gelu(zscr[b * tc + s].T)


def _s5_mixer(u, params, batch, seq, ctx):
    tc = S5_CHUNK
    n_lat, n_ctx = seq // tc, ctx // tc
    sp = S5_STEP_PAIRS
    lanes = sp * 2 * S5_GROUP
    assert n_lat == lanes and batch * n_ctx <= lanes
    rows_lat = n_lat * batch
    rows_all = rows_lat + lanes
    state_rows = _s5_state_rows(batch, n_lat, n_ctx)[3]
    wide = 2 * tc * S5_GROUP
    dsk = jnp.broadcast_to(params['s5_d'].astype(F32).reshape(S5_PAIRS, 2, 1, S5_GROUP),
                           (S5_PAIRS, 2, tc, S5_GROUP)).reshape(S5_PAIRS, 1, wide)
    weights = _s5_prep(params)
    w_spec = pl.BlockSpec((2, sp, wide, 2 * S5_STATE), lambda k: (0, k, 0, 0))
    a_spec = pl.BlockSpec((2, sp, 1, 2 * S5_STATE), lambda k: (0, k, 0, 0))
    specs = [pl.BlockSpec((sp, 2, wide // 2, wide // 2), lambda k: (k, 0, 0, 0))] + [w_spec] * 4 + [a_spec] * 2
    return pl.pallas_call(
        functools.partial(_s5_kernel, batch=batch, seq=seq, ctx=ctx),
        grid=(S5_PAIRS // sp,),
        in_specs=[pl.BlockSpec((batch * seq, lanes), lambda k: (0, k)),
                  pl.BlockSpec((batch * ctx, lanes), lambda k: (seq // ctx, k)),
                  pl.BlockSpec((sp, 1, wide), lambda k: (k, 0, 0))] + specs,
        out_specs=pl.BlockSpec((batch * seq, lanes), lambda k: (0, k)),
        out_shape=jax.ShapeDtypeStruct((batch * seq, S5_WIDTH), F32),
        scratch_shapes=[pltpu.VMEM((sp, wide, rows_all), F32),
                        pltpu.VMEM((batch * tc, lanes, n_lat), F32),
                        pltpu.VMEM((sp, rows_all, wide), BF16)]
                       + [pltpu.VMEM((sp, 2, state_rows, 2 * S5_STATE), F32) for _ in range(4)],
        compiler_params=_cparams(1),
        name="s5",
    )(u, u, dsk, *weights)


def _merge_kernel(z5_ref, o_ref_in, gs_ref, gm_ref, wa_ref, wb_ref, wm_ref, out_ref, wa_s, wb_s, wm_s):
    @pl.when(pl.program_id(1) == 0)
    def _():
        wa_s[...] = wa_ref[...].astype(BF16)
        wb_s[...] = wb_ref[...].astype(BF16)
        wm_s[...] = wm_ref[...].astype(BF16)

    sub = min(MERGE_SUB_ROWS, out_ref.shape[0])
    for r0 in range(0, out_ref.shape[0], sub):
        rows = slice(r0, r0 + sub)
        z = z5_ref[rows, :].astype(BF16)
        a = jnp.dot(z, wa_s[...], preferred_element_type=F32)
        b = jnp.dot(z, wb_s[...], preferred_element_type=F32)
        mla = jnp.dot(o_ref_in[rows, :], wm_s[...], preferred_element_type=F32)
        merged = gs_ref[rows, :].astype(F32) * (a * jax.nn.sigmoid(b)) + gm_ref[rows, :].astype(F32) * mla
        out_ref[rows, :] = merged.astype(out_ref.dtype)


def _merge(y5, o_mla, gates, w_glu, w_mla_o, tm, tn):
    t = y5.shape[0]
    nj = D_MODEL // tn
    return pl.pallas_call(
        _merge_kernel,
        grid=(nj, t // tm),
        in_specs=[pl.BlockSpec((tm, S5_WIDTH), lambda j, i: (i, 0)),
                  pl.BlockSpec((tm, MLA_HEADS * V_DIM), lambda j, i: (i, 0)),
                  pl.BlockSpec((tm, tn), lambda j, i: (i, j)),
                  pl.BlockSpec((tm, tn), lambda j, i: (i, nj + j)),
                  pl.BlockSpec((S5_WIDTH, tn), lambda j, i: (0, j)),
                  pl.BlockSpec((S5_WIDTH, tn), lambda j, i: (0, nj + j)),
                  pl.BlockSpec((MLA_HEADS * V_DIM, tn), lambda j, i: (0, j))],
        out_specs=pl.BlockSpec((tm, tn), lambda j, i: (i, j)),
        out_shape=jax.ShapeDtypeStruct((t, D_MODEL), BF16),
        scratch_shapes=[pltpu.VMEM((S5_WIDTH, tn), BF16), pltpu.VMEM((S5_WIDTH, tn), BF16),
                        pltpu.VMEM((MLA_HEADS * V_DIM, tn), BF16)],
        compiler_params=_cparams(2),
        name="merge",
    )(y5, o_mla, gates, gates, w_glu, w_glu, w_mla_o)


def _out_proj_norm_kernel(a_ref, w_ref, x_ref, g1_ref, n2_ref, sc_ref, sh_ref, x1_ref, xn_ref, w_s):
    @pl.when(pl.program_id(0) == 0)
    def _():
        w_s[...] = w_ref[...].astype(BF16)

    sub = min(MERGE_SUB_ROWS, x1_ref.shape[0])
    for r0 in range(0, x1_ref.shape[0], sub):
        rows = slice(r0, r0 + sub)
        x1 = x_ref[rows, :] + g1_ref[0, 0] * jnp.dot(a_ref[rows, :], w_s[...], preferred_element_type=F32)
        x1_ref[rows, :] = x1
        xn_ref[rows, :] = (_rms(x1, n2_ref[...]) * (1.0 + sc_ref[0, 0]) + sh_ref[0, 0]).astype(xn_ref.dtype)


def _out_proj_norm(merged, w_out, x_res, m, gain2, rows_per_batch, tm):
    t, k = merged.shape
    tpb = rows_per_batch // tm
    row = pl.BlockSpec((tm, D_MODEL), lambda i: (i, 0))
    gate1, scale2, shift2 = (_mod_spec(D_MODEL, lambda i: i // tpb, which)
                             for which in (MOD_GATE1, MOD_SCALE2, MOD_SHIFT2))
    return pl.pallas_call(
        _out_proj_norm_kernel,
        grid=(t // tm,),
        in_specs=[pl.BlockSpec((tm, k), lambda i: (i, 0)),
                  pl.BlockSpec((k, D_MODEL), lambda i: (0, 0), pipeline_mode=pl.Buffered(1)),
                  row, gate1, pl.BlockSpec((1, D_MODEL), lambda i: (0, 0)), scale2, shift2],
        out_specs=[row, row],
        out_shape=[jax.ShapeDtypeStruct((t, D_MODEL), F32), jax.ShapeDtypeStruct((t, D_MODEL), BF16)],
        scratch_shapes=[pltpu.VMEM((k, D_MODEL), BF16)],
        compiler_params=_cparams(1),
        name="out_proj",
    )(merged, w_out, x_res, m, gain2.reshape(1, D_MODEL), m, m)


def _ffn_in(xn, w_ffn_in, tm, tn):
    def epi(accs, e_refs, o_refs, rows):
        o_refs[0][rows, :] = (jax.nn.silu(accs[0]) * accs[1]).astype(BF16)

    t = xn.shape[0]
    nj = D_FF // tn
    outs = [(jax.ShapeDtypeStruct((t, D_FF), BF16), pl.BlockSpec((tm, tn), lambda j, i: (i, j)))]
    return _fused_mm(xn, [(w_ffn_in, 0, "kn", tn), (w_ffn_in, nj, "kn", tn)], epi, [], outs,
                     tm=tm, nj=nj, name="ffn_in", sub_rows=MM_SUB_ROWS)[0]


def _ffn_out_kernel(h_ref, w_ref, x_ref, g2_ref, nf_ref, o_ref, w_s, *, last):
    @pl.when(pl.program_id(0) == 0)
    def _():
        w_s[...] = w_ref[...].astype(BF16)

    y = x_ref[...] + g2_ref[0, 0] * jnp.dot(h_ref[...], w_s[...], preferred_element_type=F32)
    o_ref[...] = _rms(y, nf_ref[...]) if last else y


def _ffn_out(hid, w_ffn_out, x_res, m, norm_f, rows_per_batch, tm):
    t, k = hid.shape
    kh = k // 2
    tpb = rows_per_batch // tm
    row = pl.BlockSpec((tm, D_MODEL), lambda i: (i, 0))
    y = x_res
    for half in range(2):
        y = pl.pallas_call(
            functools.partial(_ffn_out_kernel, last=half == 1),
            grid=(t // tm,),
            in_specs=[pl.BlockSpec((tm, kh), functools.partial(lambda i, half: (i, half), half=half)),
                      pl.BlockSpec((kh, D_MODEL), functools.partial(lambda i, half: (half, 0), half=half),
                                   pipeline_mode=pl.Buffered(1)),
                      row, _mod_spec(D_MODEL, lambda i: i // tpb, MOD_GATE2),
                      pl.BlockSpec((1, D_MODEL), lambda i: (0, 0))],
            out_specs=row,
            out_shape=jax.ShapeDtypeStruct((t, D_MODEL), F32),
            scratch_shapes=[pltpu.VMEM((kh, D_MODEL), BF16)],
            compiler_params=_cparams(1),
            name="ffn_out",
        )(hid, w_ffn_out, y, m, norm_f.reshape(1, D_MODEL))
    return y


def _rope_rot_cols(w):
    k = w.shape[0]
    ws = w.reshape(k, -1, 2, 2, QK_ROPE // 4)
    return jnp.stack([-ws[:, :, :, 1, :], ws[:, :, :, 0, :]], axis=3).reshape(k, -1)


def _rope_tables(n_tokens):
    rows = n_tokens // GRID_W
    row = np.repeat(np.arange(rows, dtype=np.float32), GRID_W)
    col = np.tile(np.arange(GRID_W, dtype=np.float32), rows)
    n_freq = QK_ROPE // 4
    inv = (np.float32(ROPE_BASE) ** (-np.arange(n_freq, dtype=np.float32) / np.float32(n_freq))).astype(np.float32)
    ang = np.stack([row[:, None] * inv, col[:, None] * inv], axis=1)
    cos = np.broadcast_to(np.cos(ang)[:, :, None, :], (n_tokens, 2, 2, n_freq)).reshape(n_tokens, QK_ROPE)
    sin = np.broadcast_to(np.sin(ang)[:, :, None, :], (n_tokens, 2, 2, n_freq)).reshape(n_tokens, QK_ROPE)
    return cos.astype(np.float32), sin.astype(np.float32)


def kernel(x, c, ctx, c_ctx, w_mod, b_mod, norm1, norm2, w_in, s5_a_re, s5_a_im, s5_log_dt, s5_b_re, s5_b_im,
           s5_c_re, s5_c_im, s5_d, w_glu, q_norm, kv_norm, w_uq, w_ukv, w_mla_o, w_out, w_ffn_in, w_ffn_out,
           norm_f):
    batch, seq, _ = x.shape
    n_ctx = ctx.shape[1]
    assert w_mod.shape[0] == 1, "single-layer block"
    p = dict(s5_a_re=s5_a_re[0], s5_a_im=s5_a_im[0], s5_log_dt=s5_log_dt[0], s5_b_re=s5_b_re[0],
             s5_b_im=s5_b_im[0], s5_c_re=s5_c_re[0], s5_c_im=s5_c_im[0], s5_d=s5_d[0])
    w_in_t = w_in.reshape(w_in.shape[1:]).T

    cv = jnp.concatenate([c, c_ctx[None], jnp.zeros((8 - batch - 1, D_MODEL), F32)], axis=0)
    m = _modulation(cv, w_mod[0], b_mod[0]).reshape(8, 6, 1, D_MODEL)

    lat_rows = batch * seq
    x2d = x.reshape(lat_rows, D_MODEL)
    c2d = ctx.reshape(batch * n_ctx, D_MODEL)
    xn, u = _norm_u_proj(x2d, c2d, norm1[0], m, w_in_t, seq, batch, 512)

    kv_lo = S5_WIDTH + Q_RANK
    w_kv_rows = w_in_t[kv_lo:kv_lo + KV_RANK + QK_ROPE]
    w_kv_t = jnp.concatenate([w_kv_rows, _rope_rot_cols(w_kv_rows[KV_RANK:].T).T], axis=0)
    wq = w_uq[0].reshape(Q_RANK, MLA_HEADS, QK_NOPE + QK_ROPE)
    wq_rope = wq[:, :, QK_NOPE:].reshape(Q_RANK, MLA_HEADS * QK_ROPE)
    wq2 = jnp.concatenate([wq[:, :, :QK_NOPE].reshape(Q_RANK, MLA_HEADS * QK_NOPE), wq_rope,
                           _rope_rot_cols(wq_rope)], axis=1).astype(BF16)
    w_ukv_bf = w_ukv[0].astype(BF16)
    tm = 1024
    cos, sin = _rope_tables(seq)
    identity = np.concatenate([np.ones((tm, QK_ROPE), np.float32), np.zeros((tm, QK_ROPE), np.float32)], axis=1)
    cos_sin_k = jnp.asarray(np.concatenate([np.concatenate([cos, sin], axis=1), identity], axis=0))
    cos2, sin2 = jnp.asarray(np.tile(cos, (1, 2))), jnp.asarray(np.tile(sin, (1, 2)))

    q = _q_path(xn, w_in_t, q_norm[0], wq2, cos2, sin2, tm, seq, lat_rows)
    kv, kr = _kv_path(xn, w_kv_t, kv_norm[0], w_ukv_bf, cos_sin_k, tm, seq, lat_rows)
    gates = _gates(xn, w_in_t, kv_lo + KV_RANK + QK_ROPE, tm, 1024, lat_rows)

    z5 = _s5_mixer(u, p, batch, seq, n_ctx)
    o_mla = _attention(q, kv, kr, batch, seq, n_ctx, 2048)

    merged = _merge(z5, o_mla, gates, w_glu[0], w_mla_o[0], 512, 1024)
    x1, xn2 = _out_proj_norm(merged, w_out[0], x2d, m, norm2[0], seq, 512)
    hid = _ffn_in(xn2, w_ffn_in[0], 2048, 512)
    return _ffn_out(hid, w_ffn_out[0], x1, m, norm_f, seq, 256).reshape(batch, seq, D_MODEL)
```

```python
import functools

import jax
import jax.numpy as jnp
import numpy as np
from jax import lax
from jax.experimental import pallas as pl
from jax.experimental.pallas import tpu as pltpu

F32 = jnp.float32
BF16 = jnp.bfloat16

D_MODEL = 2048
GRID_W = 64
EPS = 1e-6
S5_WIDTH = D_MODEL // 2
S5_GROUP = 16
S5_GROUPS = S5_WIDTH // S5_GROUP
S5_STATE = 64
S5_CHUNK = 16
S5_PAIRS = S5_GROUPS // 2
MLA_HEADS = 8
QK_NOPE = 128
QK_ROPE = 64
V_DIM = 128
Q_RANK = 512
KV_RANK = 256
ROPE_BASE = 10000.0
ATTN_SCALE = (QK_NOPE + QK_ROPE) ** -0.5
D_FF = -(-8 * D_MODEL // (3 * 256)) * 256

VMEM_LIMIT_BYTES = 56 * 1024 * 1024
SUBLANES = 8
MM_SUB_ROWS = 512
MERGE_SUB_ROWS = 256


def _cparams(n_axes):
    return pltpu.CompilerParams(dimension_semantics=("arbitrary",) * n_axes,
                                vmem_limit_bytes=VMEM_LIMIT_BYTES)


def _rms(x, g):
    return x * lax.rsqrt(jnp.mean(x * x, axis=-1, keepdims=True) + EPS) * g


def _mod_kernel(cv_ref, w_ref, b_ref, o_ref):
    s = jax.nn.silu(cv_ref[...]).astype(BF16)
    o_ref[...] = jnp.dot(s, w_ref[...].astype(BF16), preferred_element_type=F32) + b_ref[...]


def _modulation(cv, w_mod, b_mod):
    n = w_mod.shape[1]
    tn = 1536
    return pl.pallas_call(
        _mod_kernel,
        grid=(n // tn,),
        in_specs=[pl.BlockSpec((8, D_MODEL), lambda j: (0, 0)),
                  pl.BlockSpec((D_MODEL, tn), lambda j: (0, j)),
                  pl.BlockSpec((1, tn), lambda j: (0, j))],
        out_specs=pl.BlockSpec((8, tn), lambda j: (0, j)),
        out_shape=jax.ShapeDtypeStruct((8, n), F32),
        compiler_params=_cparams(1),
        name="mod",
    )(cv, w_mod, b_mod.reshape(1, n))


MOD_SHIFT1, MOD_SCALE1, MOD_GATE1, MOD_SHIFT2, MOD_SCALE2, MOD_GATE2 = range(6)


def _mod_spec(width, row_of, which, col_of=None):
    col_of = col_of or (lambda *ids: 0)
    return pl.BlockSpec((1, 1, 1, width), lambda *ids: (row_of(*ids), which, 0, col_of(*ids)))


def _norm_u_kernel(x_ref, c_ref, g_ref, sc_ref, sh_ref, w_ref, xn_ref, u_ref, w_s, *, lat_tiles):
    @pl.when(pl.program_id(0) == 0)
    def _():
        w_s[...] = w_ref[...].astype(BF16)

    is_lat = pl.program_id(0) < lat_tiles
    sub = min(MERGE_SUB_ROWS, xn_ref.shape[0])
    for r0 in range(0, xn_ref.shape[0], sub):
        rows = slice(r0, r0 + sub)
        src = jnp.where(is_lat, x_ref[rows, :], c_ref[rows, :])
        xn = (_rms(src, g_ref[...]) * (1.0 + sc_ref[0, 0]) + sh_ref[0, 0]).astype(xn_ref.dtype)
        xn_ref[rows, :] = xn
        u_ref[rows, :] = lax.dot_general(xn, w_s[...], (((1,), (1,)), ((), ())), preferred_element_type=F32)


def _norm_u_proj(x2d, c2d, gain, m, w_in_t, rows_per_batch, ctx_row, tm):
    lat_tiles, ctx_tiles = x2d.shape[0] // tm, c2d.shape[0] // tm
    tpb = rows_per_batch // tm
    t_all = x2d.shape[0] + c2d.shape[0]

    def row(i):
        return jnp.where(i < lat_tiles, i // tpb, ctx_row)

    once = pl.Buffered(1)
    return pl.pallas_call(
        functools.partial(_norm_u_kernel, lat_tiles=lat_tiles),
        grid=(lat_tiles + ctx_tiles,),
        in_specs=[pl.BlockSpec((tm, D_MODEL), lambda i: (jnp.minimum(i, lat_tiles - 1), 0)),
                  pl.BlockSpec((tm, D_MODEL), lambda i: (jnp.maximum(i - lat_tiles, 0), 0)),
                  pl.BlockSpec((1, D_MODEL), lambda i: (0, 0)),
                  _mod_spec(D_MODEL, row, MOD_SCALE1), _mod_spec(D_MODEL, row, MOD_SHIFT1),
                  pl.BlockSpec((S5_WIDTH, D_MODEL), lambda i: (0, 0), pipeline_mode=once)],
        out_specs=[pl.BlockSpec((tm, D_MODEL), lambda i: (i, 0)), pl.BlockSpec((tm, S5_WIDTH), lambda i: (i, 0))],
        out_shape=[jax.ShapeDtypeStruct((t_all, D_MODEL), BF16), jax.ShapeDtypeStruct((t_all, S5_WIDTH), F32)],
        scratch_shapes=[pltpu.VMEM((S5_WIDTH, D_MODEL), BF16)],
        compiler_params=_cparams(1),
        name="norm_u_proj",
    )(x2d, c2d, gain.reshape(1, D_MODEL), m, m, w_in_t)


def _fused_mm(a, weights, epilogue, extras, outs, *, tm, nj, name, rows=None, sub_rows=None):
    t, k = a.shape
    ni = (rows or t) // tm
    nw, ne, no = len(weights), len(extras), len(outs)
    need_cast = [w.dtype != BF16 for w, _, _, _ in weights]
    nt = (((1,), (1,)), ((), ()))

    def kernel(*refs):
        a_ref = refs[0]
        w_refs = refs[1:1 + nw]
        e_refs = refs[1 + nw:1 + nw + ne]
        o_refs = refs[1 + nw + ne:1 + nw + ne + no]
        s_refs = list(refs[1 + nw + ne + no:])
        staged = {idx: s_refs.pop(0) for idx in range(nw) if need_cast[idx]}
        if staged:
            @pl.when(pl.program_id(1) == 0)
            def _():
                for idx, s_ref in staged.items():
                    s_ref[...] = w_refs[idx][...].astype(BF16)

        for r0 in range(0, tm, sub_rows or tm):
            rows = slice(r0, r0 + (sub_rows or tm))
            av = a_ref[rows, :]
            accs = []
            for idx in range(nw):
                w_ref = staged.get(idx, w_refs[idx])
                if weights[idx][2] == "kn":
                    accs.append(jnp.dot(av, w_ref[...], preferred_element_type=F32))
                else:
                    accs.append(lax.dot_general(av, w_ref[...], nt, preferred_element_type=F32))
            epilogue(accs, e_refs, o_refs, rows)

    in_specs = [pl.BlockSpec((tm, k), lambda j, i: (i, 0))]
    scratch = []
    for (w, off, layout, width), cast in zip(weights, need_cast):
        if layout == "kn":
            shape = (k, width)
            in_specs.append(pl.BlockSpec(shape, functools.partial(lambda j, i, off: (0, off + j), off=off)))
        else:
            shape = (width, k)
            in_specs.append(pl.BlockSpec(
                (pl.Element(width), pl.Element(k)),
                functools.partial(lambda j, i, off, width: (pl.multiple_of(off + j * width, SUBLANES), 0),
                                  off=off, width=width)))
        if cast:
            scratch.append(pltpu.VMEM(shape, BF16))
    in_specs += [spec for _, spec in extras]
    return pl.pallas_call(
        kernel,
        grid=(nj, ni),
        in_specs=in_specs,
        out_specs=[spec for _, spec in outs],
        out_shape=[sds for sds, _ in outs],
        scratch_shapes=scratch,
        compiler_params=_cparams(2),
        name=name,
    )(a, *[w[0] for w in weights], *[e for e, _ in extras])


def _gates(xn, w_in_t, row0, tm, tn, rows):
    def epi(accs, e_refs, o_refs, rows):
        o_refs[0][rows, :] = jax.nn.sigmoid(accs[0]).astype(BF16)

    n = 2 * D_MODEL
    return _fused_mm(xn, [(w_in_t, row0, "nk", tn)], epi, [],
                     [(jax.ShapeDtypeStruct((rows, n), BF16), pl.BlockSpec((tm, tn), lambda j, i: (i, j)))],
                     tm=tm, nj=n // tn, name="gates", rows=rows)[0]


def _q_path(xn, w_in_t, q_norm, wq2, cos2, sin2, tm, seq, rows):
    nr = MLA_HEADS * QK_ROPE
    nn = MLA_HEADS * QK_NOPE
    lanes = 2 * QK_ROPE

    def epi(accs, e_refs, o_refs, rows):
        qn_ref, w2_ref, cos_ref, sin_ref = e_refs
        cq = _rms(accs[0], qn_ref[...]).astype(BF16)
        q = jnp.dot(cq, w2_ref[...], preferred_element_type=F32)
        o_refs[0][rows, :nn] = (q[:, :nn] * ATTN_SCALE).astype(BF16)
        cos, sin = cos_ref[rows, :] * ATTN_SCALE, sin_ref[rows, :] * ATTN_SCALE
        for c0 in range(0, nr, lanes):
            rope = q[:, nn + c0:nn + c0 + lanes] * cos + q[:, nn + nr + c0:nn + nr + c0 + lanes] * sin
            o_refs[0][rows, nn + c0:nn + c0 + lanes] = rope.astype(BF16)

    pos_tiles = seq // tm
    extras = [(q_norm.reshape(1, Q_RANK), pl.BlockSpec((1, Q_RANK), lambda j, i: (0, 0))),
              (wq2, pl.BlockSpec(wq2.shape, lambda j, i: (0, 0))),
              (cos2, pl.BlockSpec((tm, lanes), lambda j, i: (i % pos_tiles, 0))),
              (sin2, pl.BlockSpec((tm, lanes), lambda j, i: (i % pos_tiles, 0)))]
    outs = [(jax.ShapeDtypeStruct((rows, nn + nr), BF16), pl.BlockSpec((tm, nn + nr), lambda j, i: (i, 0)))]
    return _fused_mm(xn, [(w_in_t, S5_WIDTH, "nk", Q_RANK)], epi, extras, outs, tm=tm, nj=1, name="q_path",
                     rows=rows)[0]


def _kv_path(xn, w_kv_t, kv_norm, w_ukv_bf, cos_sin_k, tm, seq, lat_rows):
    nkv = w_ukv_bf.shape[1]

    def epi(accs, e_refs, o_refs, rows):
        acc = accs[0]
        ckv = _rms(acc[:, :KV_RANK], e_refs[0][...]).astype(BF16)
        o_refs[0][rows, :] = jnp.dot(ckv, e_refs[1][...], preferred_element_type=F32).astype(BF16)
        prod = acc[:, KV_RANK:] * e_refs[2][rows, :]
        o_refs[1][rows, :] = (prod + pltpu.roll(prod, QK_ROPE, 1))[:, :QK_ROPE].astype(BF16)

    t = xn.shape[0]
    pos_tiles, lat_tiles = seq // tm, lat_rows // tm
    extras = [(kv_norm.reshape(1, KV_RANK), pl.BlockSpec((1, KV_RANK), lambda j, i: (0, 0))),
              (w_ukv_bf, pl.BlockSpec(w_ukv_bf.shape, lambda j, i: (0, 0))),
              (cos_sin_k, pl.BlockSpec((tm, 2 * QK_ROPE),
                                       lambda j, i: (jnp.where(i < lat_tiles, i % pos_tiles, pos_tiles), 0)))]
    outs = [(jax.ShapeDtypeStruct((t, nkv), BF16), pl.BlockSpec((tm, nkv), lambda j, i: (i, 0))),
            (jax.ShapeDtypeStruct((t, QK_ROPE), BF16), pl.BlockSpec((tm, QK_ROPE), lambda j, i: (i, 0)))]
    return _fused_mm(xn, [(w_kv_t, 0, "nk", w_kv_t.shape[0])], epi, extras, outs, tm=tm, nj=1, name="kv_path")


ATTN_SUB_ROWS = 256


def _attn_kernel(qn_ref, qr_ref, kvl_ref, kvc_ref, krl_ref, krc_ref, o_ref, k_scr, v_scr, *, seq, ctx):
    dk = QK_NOPE + QK_ROPE

    @pl.when(pl.program_id(2) == 0)
    def _():
        for h in range(2):
            base = h * (QK_NOPE + V_DIM)
            k_scr[h, :seq, :QK_NOPE] = kvl_ref[:, base:base + QK_NOPE]
            k_scr[h, seq:, :QK_NOPE] = kvc_ref[:, base:base + QK_NOPE]
            k_scr[h, :seq, QK_NOPE:dk] = krl_ref[...]
            k_scr[h, seq:, QK_NOPE:dk] = krc_ref[...]
            v_scr[h, :seq, :V_DIM] = kvl_ref[:, base + QK_NOPE:base + QK_NOPE + V_DIM]
            v_scr[h, seq:, :V_DIM] = kvc_ref[:, base + QK_NOPE:base + QK_NOPE + V_DIM]
            ones_col = lax.broadcasted_iota(jnp.int32, (seq + ctx, V_DIM), 1) == 0
            v_scr[h, :, V_DIM:] = jnp.where(ones_col, 1.0, 0.0).astype(BF16)

    for r0 in range(0, qn_ref.shape[0], ATTN_SUB_ROWS):
        rows = slice(r0, r0 + ATTN_SUB_ROWS)
        for h in range(2):
            q = jnp.concatenate([qn_ref[rows, h * QK_NOPE:(h + 1) * QK_NOPE],
                                 qr_ref[rows, h * QK_ROPE:(h + 1) * QK_ROPE]], axis=1)
            s = lax.dot_general(q, k_scr[h], (((1,), (1,)), ((), ())), preferred_element_type=F32)
            m = jnp.max(s, axis=-1, keepdims=True)
            p = jnp.exp((s - m).astype(BF16))
            ol = jnp.dot(p, v_scr[h], preferred_element_type=F32)
            o_ref[rows, h * V_DIM:(h + 1) * V_DIM] = (ol[:, :V_DIM] / ol[:, V_DIM:V_DIM + 1]).astype(o_ref.dtype)


def _attention(q, kv, kr, batch, seq, ctx, tq):
    nq = seq // tq
    ctx0 = batch * seq // ctx
    nn_blocks = MLA_HEADS * QK_NOPE // (2 * QK_NOPE)
    dk = QK_NOPE + QK_ROPE
    hw = 2 * (QK_NOPE + V_DIM)
    return pl.pallas_call(
        functools.partial(_attn_kernel, seq=seq, ctx=ctx),
        grid=(batch, MLA_HEADS // 2, nq),
        in_specs=[pl.BlockSpec((tq, 2 * QK_NOPE), lambda b, hp, qi: (b * nq + qi, hp)),
                  pl.BlockSpec((tq, 2 * QK_ROPE), lambda b, hp, qi: (b * nq + qi, 2 * nn_blocks + hp)),
                  pl.BlockSpec((seq, hw), lambda b, hp, qi: (b, hp)),
                  pl.BlockSpec((ctx, hw), lambda b, hp, qi: (ctx0 + b, hp)),
                  pl.BlockSpec((seq, QK_ROPE), lambda b, hp, qi: (b, 0)),
                  pl.BlockSpec((ctx, QK_ROPE), lambda b, hp, qi: (ctx0 + b, 0))],
        out_specs=pl.BlockSpec((tq, 2 * V_DIM), lambda b, hp, qi: (b * nq + qi, hp)),
        out_shape=jax.ShapeDtypeStruct((batch * seq, MLA_HEADS * V_DIM), BF16),
        scratch_shapes=[pltpu.VMEM((2, seq + ctx, dk), BF16), pltpu.VMEM((2, seq + ctx, 2 * V_DIM), BF16)],
        compiler_params=_cparams(3),
        name="attention",
    )(q, q, kv, kv, kr, kr)


def _s5_prep_pair(pp, are_ref, aim_ref, ldt_ref, bre_ref, bim_ref, cre_ref, cim_ref,
                  toep_ref, wsr_ref, wsi_ref, wor_ref, woi_ref, atr_ref, ati_ref):
    tc, g = S5_CHUNK, S5_GROUP
    w = tc * g
    lane = lax.broadcasted_iota(jnp.int32, (g, 2 * S5_STATE), 1)
    in_group = (lane < S5_STATE, lane >= S5_STATE)
    lane_w = lax.broadcasted_iota(jnp.int32, (g, w), 1)
    nt = (((1,), (1,)), ((), ()))
    toep_rows = [[jnp.zeros((g, w), F32) for _ in range(tc)] for _ in range(2)]
    for d in range(2):
        lr, li = are_ref[d, pp], aim_ref[d, pp]
        dt = jnp.exp(ldt_ref[d, pp])
        mag = jnp.exp(lr * dt)
        ab_re, ab_im = mag * jnp.cos(li * dt), mag * jnp.sin(li * dt)
        den = lr * lr + li * li
        nr, ni = ab_re - 1.0, ab_im
        co_re = (nr * lr + ni * li) / den
        co_im = (ni * lr - nr * li) / den
        br, bi = bre_ref[d, pp], bim_ref[d, pp]
        bb_re = co_re * br - co_im * bi
        bb_im = co_re * bi + co_im * br
        pw = [(jnp.ones_like(ab_re), jnp.zeros_like(ab_re))]
        for _ in range(tc):
            pr, pi = pw[-1]
            pw.append((pr * ab_re - pi * ab_im, pr * ab_im + pi * ab_re))
        cr, ci = cre_ref[d, pp], cim_ref[d, pp]
        ca = [(cr * pr - ci * pi, cr * pi + ci * pr) for pr, pi in pw]
        taus = list(range(tc))[::-1] if d else list(range(tc))
        y_re = jnp.concatenate([ca[t][0] for t in taus], axis=0).astype(BF16)
        y_im = jnp.concatenate([ca[t][1] for t in taus], axis=0).astype(BF16)
        for j in range(2):
            x_re = jnp.where(in_group[j], bb_re, 0.0)
            x_im = jnp.where(in_group[j], bb_im, 0.0)
            kt = (lax.dot_general(x_re.astype(BF16), y_re, nt, preferred_element_type=F32)
                  - lax.dot_general(x_im.astype(BF16), y_im, nt, preferred_element_type=F32))
            for r in range(tc):
                sh = (r + 1) * g if d else r * g
                blk = pltpu.roll(kt, sh % w, 1) if sh % w else kt
                keep = (lane_w < sh) if d else (lane_w >= sh)
                toep_rows[j][r] = toep_rows[j][r] + jnp.where(keep, blk, 0.0)
        for r in range(tc):
            pr, pi = pw[r] if d else pw[tc - 1 - r]
            w_re = bb_re * pr - bb_im * pi
            w_im = bb_re * pi + bb_im * pr
            car, cai = ca[tc - r] if d else ca[r + 1]
            for j in range(2):
                rows = slice(j * w + r * g, j * w + (r + 1) * g)
                wsr_ref[d, pp, rows, :] = jnp.where(in_group[j], w_re, 0.0).astype(BF16)
                wsi_ref[d, pp, rows, :] = jnp.where(in_group[j], w_im, 0.0).astype(BF16)
                wor_ref[d, pp, rows, :] = jnp.where(in_group[j], car, 0.0).astype(BF16)
                woi_ref[d, pp, rows, :] = jnp.where(in_group[j], -cai, 0.0).astype(BF16)
        atr_ref[d, pp] = pw[tc][0]
        ati_ref[d, pp] = pw[tc][1]
    for j in range(2):
        toep_ref[pp, j] = jnp.concatenate(toep_rows[j], axis=0).astype(BF16)


S5_PREP_PAIRS = 8


def _s5_prep_kernel(*refs):
    for pp in range(S5_PREP_PAIRS):
        _s5_prep_pair(pp, *refs)


def _s5_prep(params):
    tc = S5_CHUNK
    wide = 2 * tc * S5_GROUP
    sl = 2 * S5_STATE

    def pair_lanes(v):
        return v.astype(F32).reshape(2, S5_PAIRS, 1, sl)

    def pair_rows(v):
        rows = v.shape[2]
        return v.astype(F32).reshape(2, S5_PAIRS, 2, rows, S5_STATE).transpose(0, 1, 3, 2, 4).reshape(
            2, S5_PAIRS, rows, sl)

    ldt = jnp.broadcast_to(params['s5_log_dt'].astype(F32)[:, :, None], (2, S5_GROUPS, S5_STATE))
    ins = [pair_lanes(params['s5_a_re']), pair_lanes(params['s5_a_im']), pair_lanes(ldt),
           pair_rows(params['s5_b_re'].transpose(0, 1, 3, 2)), pair_rows(params['s5_b_im'].transpose(0, 1, 3, 2)),
           pair_rows(params['s5_c_re']), pair_rows(params['s5_c_im'])]
    pp = S5_PREP_PAIRS
    vec_spec = pl.BlockSpec((2, pp, 1, sl), lambda k: (0, k, 0, 0))
    mat_spec = pl.BlockSpec((2, pp, S5_GROUP, sl), lambda k: (0, k, 0, 0))
    w_spec = pl.BlockSpec((2, pp, wide, sl), lambda k: (0, k, 0, 0))
    w_sds = jax.ShapeDtypeStruct((2, S5_PAIRS, wide, sl), BF16)
    a_sds = jax.ShapeDtypeStruct((2, S5_PAIRS, 1, sl), F32)
    return pl.pallas_call(
        _s5_prep_kernel,
        grid=(S5_PAIRS // pp,),
        in_specs=[vec_spec] * 3 + [mat_spec] * 4,
        out_specs=[pl.BlockSpec((pp, 2, wide // 2, wide // 2), lambda k: (k, 0, 0, 0))] + [w_spec] * 4 + [vec_spec] * 2,
        out_shape=[jax.ShapeDtypeStruct((S5_PAIRS, 2, wide // 2, wide // 2), BF16)] + [w_sds] * 4 + [a_sds] * 2,
        compiler_params=_cparams(1),
        name="s5_prep",
    )(*ins)


S5_STEP_PAIRS = 4
def _s5_state_rows(batch, n_lat, n_ctx):
    def pitch(n):
        p = -(-n // SUBLANES)
        return SUBLANES * (p + 1 - p % 2)

    lat_pitch, ctx_pitch = pitch(n_lat), pitch(n_ctx)
    ctx_base = batch * lat_pitch
    return lat_pitch, ctx_pitch, ctx_base, ctx_base + batch * ctx_pitch


def _s5_kernel(ul_ref, uc_ref, dsk_ref, *refs, batch, seq, ctx):
    toep_ref, wsr, wsi, wor, woi, atr, ati = refs[:7]
    o_ref = refs[7]
    wscr, zscr, upscr, sre, sim, hre, him = refs[8:]
    tc = S5_CHUNK
    n_lat, n_ctx = seq // tc, ctx // tc
    rows_lat = n_lat * batch
    lat_pitch, ctx_pitch, ctx_base, _ = _s5_state_rows(batch, n_lat, n_ctx)
    w = tc * S5_GROUP
    lanes = S5_STEP_PAIRS * 2 * S5_GROUP

    def scatter_tiles(xt, r, col0):
        for kk in range(S5_STEP_PAIRS):
            for j in range(2):
                ch = (2 * kk + j) * S5_GROUP
                wscr[kk, j * w + r * S5_GROUP:j * w + (r + 1) * S5_GROUP, col0:col0 + lanes] = xt[ch:ch + S5_GROUP, :]

    for r in range(tc):
        for b in range(batch):
            x = ul_ref[pl.ds(b * seq + r, n_lat, stride=tc), :]
            scatter_tiles(x.T, r, b * n_lat)
        xc = jnp.concatenate([uc_ref[pl.ds(b * ctx + r, n_ctx, stride=tc), :] for b in range(batch)]
                             + [jnp.zeros((lanes - batch * n_ctx, lanes), F32)], axis=0)
        scatter_tiles(xc.T, r, rows_lat)

    chains = [(kk, d) for kk in range(S5_STEP_PAIRS) for d in range(2)]
    for kk in range(S5_STEP_PAIRS):
        upscr[kk] = wscr[kk].T.astype(BF16)
        for d in range(2):
            for dst, wst in ((sre, wsr), (sim, wsi)):
                s = jnp.dot(upscr[kk], wst[d, kk], preferred_element_type=F32)
                for b in range(batch):
                    dst[kk, d, b * lat_pitch:b * lat_pitch + n_lat, :] = s[b * n_lat:(b + 1) * n_lat]
                    dst[kk, d, ctx_base + b * ctx_pitch:ctx_base + b * ctx_pitch + n_ctx, :] = (
                        s[rows_lat + b * n_ctx:rows_lat + (b + 1) * n_ctx])

    ctx_rows = [pl.ds(ctx_base + c, batch, stride=ctx_pitch) for c in range(n_ctx)]
    lat_rows = [pl.ds(c, batch, stride=lat_pitch) for c in range(n_lat)]
    order = (ctx_rows + lat_rows, ctx_rows[::-1] + lat_rows[::-1])
    coef = {(kk, d): (atr[d, kk], ati[d, kk]) for kk, d in chains}
    state = {ch: (jnp.zeros((batch, 2 * S5_STATE), F32), jnp.zeros((batch, 2 * S5_STATE), F32)) for ch in chains}
    for t in range(n_ctx + n_lat):
        for kk, d in chains:
            rows = order[d][t]
            (ar, ai), (h_re, h_im) = coef[kk, d], state[kk, d]
            hre[kk, d, rows, :] = h_re
            him[kk, d, rows, :] = h_im
            state[kk, d] = (ar * h_re - ai * h_im + sre[kk, d, rows, :],
                            ar * h_im + ai * h_re + sim[kk, d, rows, :])

    for kk in range(S5_STEP_PAIRS):
        ul = upscr[kk, :rows_lat, :]
        y = ul.astype(F32) * dsk_ref[kk]
        y = y + jnp.concatenate(
            [jnp.dot(ul[:, :w], toep_ref[kk, 0], preferred_element_type=F32),
             jnp.dot(ul[:, w:], toep_ref[kk, 1], preferred_element_type=F32)], axis=1)
        nt = (((1,), (1,)), ((), ()))
        for d in range(2):
            h_r = jnp.concatenate([hre[kk, d, b * lat_pitch:b * lat_pitch + n_lat, :] for b in range(batch)], axis=0)
            h_i = jnp.concatenate([him[kk, d, b * lat_pitch:b * lat_pitch + n_lat, :] for b in range(batch)], axis=0)
            y = y + lax.dot_general(h_r.astype(BF16), wor[d, kk], nt, preferred_element_type=F32)
            y = y + lax.dot_general(h_i.astype(BF16), woi[d, kk], nt, preferred_element_type=F32)
        yt = y.T
        for b in range(batch):
            for s in range(tc):
                for j in range(2):
                    ch = (2 * kk + j) * S5_GROUP
                    zscr[b * tc + s, ch:ch + S5_GROUP, :] = yt[j * w + s * S5_GROUP:j * w + (s + 1) * S5_GROUP,
                                                               b * n_lat:(b + 1) * n_lat]

    for b in range(batch):
        for s in range(tc):
            o_ref[pl.ds(b * seq + s, n_lat, stride=tc), :] = jax.nn.gelu(zscr[b * tc + s].T)


def _s5_mixer(u, params, batch, seq, ctx):
    tc = S5_CHUNK
    n_lat, n_ctx = seq // tc, ctx // tc
    sp = S5_STEP_PAIRS
    lanes = sp * 2 * S5_GROUP
    assert n_lat == lanes and batch * n_ctx <= lanes
    rows_lat = n_lat * batch
    rows_all = rows_lat + lanes
    state_rows = _s5_state_rows(batch, n_lat, n_ctx)[3]
    wide = 2 * tc * S5_GROUP
    dsk = jnp.broadcast_to(params['s5_d'].astype(F32).reshape(S5_PAIRS, 2, 1, S5_GROUP),
                           (S5_PAIRS, 2, tc, S5_GROUP)).reshape(S5_PAIRS, 1, wide)
    weights = _s5_prep(params)
    w_spec = pl.BlockSpec((2, sp, wide, 2 * S5_STATE), lambda k: (0, k, 0, 0))
    a_spec = pl.BlockSpec((2, sp, 1, 2 * S5_STATE), lambda k: (0, k, 0, 0))
    specs = [pl.BlockSpec((sp, 2, wide // 2, wide // 2), lambda k: (k, 0, 0, 0))] + [w_spec] * 4 + [a_spec] * 2
    return pl.pallas_call(
        functools.partial(_s5_kernel, batch=batch, seq=seq, ctx=ctx),
        grid=(S5_PAIRS // sp,),
        in_specs=[pl.BlockSpec((batch * seq, lanes), lambda k: (0, k)),
                  pl.BlockSpec((batch * ctx, lanes), lambda k: (seq // ctx, k)),
                  pl.BlockSpec((sp, 1, wide), lambda k: (k, 0, 0))] + specs,
        out_specs=pl.BlockSpec((batch * seq, lanes), lambda k: (0, k)),
        out_shape=jax.ShapeDtypeStruct((batch * seq, S5_WIDTH), F32),
        scratch_shapes=[pltpu.VMEM((sp, wide, rows_all), F32),
                        pltpu.VMEM((batch * tc, lanes, n_lat), F32),
                        pltpu.VMEM((sp, rows_all, wide), BF16)]
                       + [pltpu.VMEM((sp, 2, state_rows, 2 * S5_STATE), F32) for _ in range(4)],
        compiler_params=_cparams(1),
        name="s5",
    )(u, u, dsk, *weights)


def _merge_kernel(z5_ref, o_ref_in, gs_ref, gm_ref, wa_ref, wb_ref, wm_ref, out_ref, wa_s, wb_s, wm_s):
    @pl.when(pl.program_id(1) == 0)
    def _():
        wa_s[...] = wa_ref[...].astype(BF16)
        wb_s[...] = wb_ref[...].astype(BF16)
        wm_s[...] = wm_ref[...].astype(BF16)

    sub = min(MERGE_SUB_ROWS, out_ref.shape[0])
    for r0 in range(0, out_ref.shape[0], sub):
        rows = slice(r0, r0 + sub)
        z = z5_ref[rows, :].astype(BF16)
        a = jnp.dot(z, wa_s[...], preferred_element_type=F32)
        b = jnp.dot(z, wb_s[...], preferred_element_type=F32)
        mla = jnp.dot(o_ref_in[rows, :], wm_s[...], preferred_element_type=F32)
        merged = gs_ref[rows, :].astype(F32) * (a * jax.nn.sigmoid(b)) + gm_ref[rows, :].astype(F32) * mla
        out_ref[rows, :] = merged.astype(out_ref.dtype)


def _merge(z5, o_mla, gates, w_glu, w_mla_o, tm, tn):
    t = z5.shape[0]
    nj = D_MODEL // tn
    return pl.pallas_call(
        _merge_kernel,
        grid=(nj, t // tm),
        in_specs=[pl.BlockSpec((tm, S5_WIDTH), lambda j, i: (i, 0)),
                  pl.BlockSpec((tm, MLA_HEADS * V_DIM), lambda j, i: (i, 0)),
                  pl.BlockSpec((tm, tn), lambda j, i: (i, j)),
                  pl.BlockSpec((tm, tn), lambda j, i: (i, nj + j)),
                  pl.BlockSpec((S5_WIDTH, tn), lambda j, i: (0, j)),
                  pl.BlockSpec((S5_WIDTH, tn), lambda j, i: (0, nj + j)),
                  pl.BlockSpec((MLA_HEADS * V_DIM, tn), lambda j, i: (0, j))],
        out_specs=pl.BlockSpec((tm, tn), lambda j, i: (i, j)),
        out_shape=jax.ShapeDtypeStruct((t, D_MODEL), BF16),
        scratch_shapes=[pltpu.VMEM((S5_WIDTH, tn), BF16), pltpu.VMEM((S5_WIDTH, tn), BF16),
                        pltpu.VMEM((MLA_HEADS * V_DIM, tn), BF16)],
        compiler_params=_cparams(2),
        name="merge",
    )(z5, o_mla, gates, gates, w_glu, w_glu, w_mla_o)


def _out_proj_norm_kernel(a_ref, w_ref, x_ref, g1_ref, n2_ref, sc_ref, sh_ref, x1_ref, xn_ref, w_s):
    @pl.when(pl.program_id(0) == 0)
    def _():
        w_s[...] = w_ref[...].astype(BF16)

    sub = min(MERGE_SUB_ROWS, x1_ref.shape[0])
    for r0 in range(0, x1_ref.shape[0], sub):
        rows = slice(r0, r0 + sub)
        x1 = x_ref[rows, :] + g1_ref[0, 0] * jnp.dot(a_ref[rows, :], w_s[...], preferred_element_type=F32)
        x1_ref[rows, :] = x1
        xn_ref[rows, :] = (_rms(x1, n2_ref[...]) * (1.0 + sc_ref[0, 0]) + sh_ref[0, 0]).astype(xn_ref.dtype)


def _out_proj_norm(merged, w_out, x_res, m, gain2, rows_per_batch, tm):
    t, k = merged.shape
    tpb = rows_per_batch // tm
    row = pl.BlockSpec((tm, D_MODEL), lambda i: (i, 0))
    gate1, scale2, shift2 = (_mod_spec(D_MODEL, lambda i: i // tpb, which)
                             for which in (MOD_GATE1, MOD_SCALE2, MOD_SHIFT2))
    return pl.pallas_call(
        _out_proj_norm_kernel,
        grid=(t // tm,),
        in_specs=[pl.BlockSpec((tm, k), lambda i: (i, 0)),
                  pl.BlockSpec((k, D_MODEL), lambda i: (0, 0), pipeline_mode=pl.Buffered(1)),
                  row, gate1, pl.BlockSpec((1, D_MODEL), lambda i: (0, 0)), scale2, shift2],
        out_specs=[row, row],
        out_shape=[jax.ShapeDtypeStruct((t, D_MODEL), F32), jax.ShapeDtypeStruct((t, D_MODEL), BF16)],
        scratch_shapes=[pltpu.VMEM((k, D_MODEL), BF16)],
        compiler_params=_cparams(1),
        name="out_proj",
    )(merged, w_out, x_res, m, gain2.reshape(1, D_MODEL), m, m)


def _ffn_in(xn, w_ffn_in, tm, tn):
    def epi(accs, e_refs, o_refs, rows):
        o_refs[0][rows, :] = (jax.nn.silu(accs[0]) * accs[1]).astype(BF16)

    t = xn.shape[0]
    nj = D_FF // tn
    outs = [(jax.ShapeDtypeStruct((t, D_FF), BF16), pl.BlockSpec((tm, tn), lambda j, i: (i, j)))]
    return _fused_mm(xn, [(w_ffn_in, 0, "kn", tn), (w_ffn_in, nj, "kn", tn)], epi, [], outs,
                     tm=tm, nj=nj, name="ffn_in", sub_rows=MM_SUB_ROWS)[0]


def _ffn_out_kernel(h_ref, w_ref, x_ref, g2_ref, nf_ref, o_ref, w_s, *, last):
    @pl.when(pl.program_id(0) == 0)
    def _():
        w_s[...] = w_ref[...].astype(BF16)

    y = x_ref[...] + g2_ref[0, 0] * jnp.dot(h_ref[...], w_s[...], preferred_element_type=F32)
    o_ref[...] = _rms(y, nf_ref[...]) if last else y


def _ffn_out(hid, w_ffn_out, x_res, m, norm_f, rows_per_batch, tm):
    t, k = hid.shape
    kh = k // 2
    tpb = rows_per_batch // tm
    row = pl.BlockSpec((tm, D_MODEL), lambda i: (i, 0))
    y = x_res
    for half in range(2):
        y = pl.pallas_call(
            functools.partial(_ffn_out_kernel, last=half == 1),
            grid=(t // tm,),
            in_specs=[pl.BlockSpec((tm, kh), functools.partial(lambda i, half: (i, half), half=half)),
                      pl.BlockSpec((kh, D_MODEL), functools.partial(lambda i, half: (half, 0), half=half),
                                   pipeline_mode=pl.Buffered(1)),
                      row, _mod_spec(D_MODEL, lambda i: i // tpb, MOD_GATE2),
                      pl.BlockSpec((1, D_MODEL), lambda i: (0, 0))],
            out_specs=row,
            out_shape=jax.ShapeDtypeStruct((t, D_MODEL), F32),
            scratch_shapes=[pltpu.VMEM((kh, D_MODEL), BF16)],
            compiler_params=_cparams(1),
            name="ffn_out",
        )(hid, w_ffn_out, y, m, norm_f.reshape(1, D_MODEL))
    return y


def _rope_rot_cols(w):
    k = w.shape[0]
    ws = w.reshape(k, -1, 2, 2, QK_ROPE // 4)
    return jnp.stack([-ws[:, :, :, 1, :], ws[:, :, :, 0, :]], axis=3).reshape(k, -1)


def _rope_tables(n_tokens):
    rows = n_tokens // GRID_W
    row = np.repeat(np.arange(rows, dtype=np.float32), GRID_W)
    col = np.tile(np.arange(GRID_W, dtype=np.float32), rows)
    n_freq = QK_ROPE // 4
    inv = (np.float32(ROPE_BASE) ** (-np.arange(n_freq, dtype=np.float32) / np.float32(n_freq))).astype(np.float32)
    ang = np.stack([row[:, None] * inv, col[:, None] * inv], axis=1)
    cos = np.broadcast_to(np.cos(ang)[:, :, None, :], (n_tokens, 2, 2, n_freq)).reshape(n_tokens, QK_ROPE)
    sin = np.broadcast_to(np.sin(ang)[:, :, None, :], (n_tokens, 2, 2, n_freq)).reshape(n_tokens, QK_ROPE)
    return cos.astype(np.float32), sin.astype(np.float32)


def kernel(x, c, ctx, c_ctx, w_mod, b_mod, norm1, norm2, w_in, s5_a_re, s5_a_im, s5_log_dt, s5_b_re, s5_b_im,
           s5_c_re, s5_c_im, s5_d, w_glu, q_norm, kv_norm, w_uq, w_ukv, w_mla_o, w_out, w_ffn_in, w_ffn_out,
           norm_f):
    batch, seq, _ = x.shape
    n_ctx = ctx.shape[1]
    assert w_mod.shape[0] == 1, "single-layer block"
    p = dict(s5_a_re=s5_a_re[0], s5_a_im=s5_a_im[0], s5_log_dt=s5_log_dt[0], s5_b_re=s5_b_re[0],
             s5_b_im=s5_b_im[0], s5_c_re=s5_c_re[0], s5_c_im=s5_c_im[0], s5_d=s5_d[0])
    w_in_t = w_in.reshape(w_in.shape[1:]).T

    cv = jnp.concatenate([c, c_ctx[None], jnp.zeros((8 - batch - 1, D_MODEL), F32)], axis=0)
    m = _modulation(cv, w_mod[0], b_mod[0]).reshape(8, 6, 1, D_MODEL)

    lat_rows = batch * seq
    x2d = x.reshape(lat_rows, D_MODEL)
    c2d = ctx.reshape(batch * n_ctx, D_MODEL)
    xn, u = _norm_u_proj(x2d, c2d, norm1[0], m, w_in_t, seq, batch, 512)

    kv_lo = S5_WIDTH + Q_RANK
    w_kv_rows = w_in_t[kv_lo:kv_lo + KV_RANK + QK_ROPE]
    w_kv_t = jnp.concatenate([w_kv_rows, _rope_rot_cols(w_kv_rows[KV_RANK:].T).T], axis=0)
    wq = w_uq[0].reshape(Q_RANK, MLA_HEADS, QK_NOPE + QK_ROPE)
    wq_rope = wq[:, :, QK_NOPE:].reshape(Q_RANK, MLA_HEADS * QK_ROPE)
    wq2 = jnp.concatenate([wq[:, :, :QK_NOPE].reshape(Q_RANK, MLA_HEADS * QK_NOPE), wq_rope,
                           _rope_rot_cols(wq_rope)], axis=1).astype(BF16)
    w_ukv_bf = w_ukv[0].astype(BF16)
    tm = 1024
    cos, sin = _rope_tables(seq)
    identity = np.concatenate([np.ones((tm, QK_ROPE), np.float32), np.zeros((tm, QK_ROPE), np.float32)], axis=1)
    cos_sin_k = jnp.asarray(np.concatenate([np.concatenate([cos, sin], axis=1), identity], axis=0))
    cos2, sin2 = jnp.asarray(np.tile(cos, (1, 2))), jnp.asarray(np.tile(sin, (1, 2)))

    q = _q_path(xn, w_in_t, q_norm[0], wq2, cos2, sin2, tm, seq, lat_rows)
    kv, kr = _kv_path(xn, w_kv_t, kv_norm[0], w_ukv_bf, cos_sin_k, tm, seq, lat_rows)
    gates = _gates(xn, w_in_t, kv_lo + KV_RANK + QK_ROPE, tm, 1024, lat_rows)

    z5 = _s5_mixer(u, p, batch, seq, n_ctx)
    o_mla = _attention(q, kv, kr, batch, seq, n_ctx, 2048)

    merged = _merge(z5, o_mla, gates, w_glu[0], w_mla_o[0], 512, 1024)
    x1, xn2 = _out_proj_norm(merged, w_out[0], x2d, m, norm2[0], seq, 512)
    hid = _ffn_in(xn2, w_ffn_in[0], 2048, 512)
    return _ffn_out(hid, w_ffn_out[0], x1, m, norm_f, seq, 256).reshape(batch, seq, D_MODEL)
```

```python
import functools

import jax
import jax.numpy as jnp
import numpy as np
from jax import lax
from jax.experimental import pallas as pl
from jax.experimental.pallas import tpu as pltpu

F32 = jnp.float32
BF16 = jnp.bfloat16

D_MODEL = 2048
GRID_W = 64
EPS = 1e-6
S5_WIDTH = D_MODEL // 2
S5_GROUP = 16
S5_GROUPS = S5_WIDTH // S5_GROUP
S5_STATE = 64
S5_CHUNK = 16
S5_PAIRS = S5_GROUPS // 2
MLA_HEADS = 8
QK_NOPE = 128
QK_ROPE = 64
V_DIM = 128
Q_RANK = 512
KV_RANK = 256
ROPE_BASE = 10000.0
ATTN_SCALE = (QK_NOPE + QK_ROPE) ** -0.5
D_FF = -(-8 * D_MODEL // (3 * 256)) * 256

VMEM_LIMIT_BYTES = 56 * 1024 * 1024
SUBLANES = 8
MM_SUB_ROWS = 512
MERGE_SUB_ROWS = 256
ROW_STREAM_BUFFERS = pl.Buffered(3)


def _cparams(n_axes):
    return pltpu.CompilerParams(dimension_semantics=("arbitrary",) * n_axes,
                                vmem_limit_bytes=VMEM_LIMIT_BYTES)


def _rms(x, g):
    return x * lax.rsqrt(jnp.mean(x * x, axis=-1, keepdims=True) + EPS) * g


def _mod_kernel(cv_ref, w_ref, b_ref, o_ref):
    s = jax.nn.silu(cv_ref[...]).astype(BF16)
    o_ref[...] = jnp.dot(s, w_ref[...].astype(BF16), preferred_element_type=F32) + b_ref[...]


def _modulation(cv, w_mod, b_mod):
    n = w_mod.shape[1]
    tn = 1536
    return pl.pallas_call(
        _mod_kernel,
        grid=(n // tn,),
        in_specs=[pl.BlockSpec((8, D_MODEL), lambda j: (0, 0)),
                  pl.BlockSpec((D_MODEL, tn), lambda j: (0, j)),
                  pl.BlockSpec((1, tn), lambda j: (0, j))],
        out_specs=pl.BlockSpec((8, tn), lambda j: (0, j)),
        out_shape=jax.ShapeDtypeStruct((8, n), F32),
        compiler_params=_cparams(1),
        name="mod",
    )(cv, w_mod, b_mod.reshape(1, n))


MOD_SHIFT1, MOD_SCALE1, MOD_GATE1, MOD_SHIFT2, MOD_SCALE2, MOD_GATE2 = range(6)


def _mod_spec(width, row_of, which, col_of=None):
    col_of = col_of or (lambda *ids: 0)
    return pl.BlockSpec((1, 1, 1, width), lambda *ids: (row_of(*ids), which, 0, col_of(*ids)))


def _norm_u_kernel(x_ref, c_ref, g_ref, sc_ref, sh_ref, w_ref, xn_ref, u_ref, w_s, *, lat_tiles):
    @pl.when(pl.program_id(0) == 0)
    def _():
        w_s[...] = w_ref[...].astype(BF16)

    is_lat = pl.program_id(0) < lat_tiles
    sub = min(MERGE_SUB_ROWS, xn_ref.shape[0])
    for r0 in range(0, xn_ref.shape[0], sub):
        rows = slice(r0, r0 + sub)
        src = jnp.where(is_lat, x_ref[rows, :], c_ref[rows, :])
        xn = (_rms(src, g_ref[...]) * (1.0 + sc_ref[0, 0]) + sh_ref[0, 0]).astype(xn_ref.dtype)
        xn_ref[rows, :] = xn
        u_ref[rows, :] = lax.dot_general(xn, w_s[...], (((1,), (1,)), ((), ())), preferred_element_type=F32)


def _norm_u_proj(x2d, c2d, gain, m, w_in_t, rows_per_batch, ctx_row, tm):
    lat_tiles, ctx_tiles = x2d.shape[0] // tm, c2d.shape[0] // tm
    tpb = rows_per_batch // tm
    t_all = x2d.shape[0] + c2d.shape[0]

    def row(i):
        return jnp.where(i < lat_tiles, i // tpb, ctx_row)

    once = pl.Buffered(1)
    return pl.pallas_call(
        functools.partial(_norm_u_kernel, lat_tiles=lat_tiles),
        grid=(lat_tiles + ctx_tiles,),
        in_specs=[pl.BlockSpec((tm, D_MODEL), lambda i: (jnp.minimum(i, lat_tiles - 1), 0)),
                  pl.BlockSpec((tm, D_MODEL), lambda i: (jnp.maximum(i - lat_tiles, 0), 0)),
                  pl.BlockSpec((1, D_MODEL), lambda i: (0, 0)),
                  _mod_spec(D_MODEL, row, MOD_SCALE1), _mod_spec(D_MODEL, row, MOD_SHIFT1),
                  pl.BlockSpec((S5_WIDTH, D_MODEL), lambda i: (0, 0), pipeline_mode=once)],
        out_specs=[pl.BlockSpec((tm, D_MODEL), lambda i: (i, 0)), pl.BlockSpec((tm, S5_WIDTH), lambda i: (i, 0))],
        out_shape=[jax.ShapeDtypeStruct((t_all, D_MODEL), BF16), jax.ShapeDtypeStruct((t_all, S5_WIDTH), F32)],
        scratch_shapes=[pltpu.VMEM((S5_WIDTH, D_MODEL), BF16)],
        compiler_params=_cparams(1),
        name="norm_u_proj",
    )(x2d, c2d, gain.reshape(1, D_MODEL), m, m, w_in_t)


def _fused_mm(a, weights, epilogue, extras, outs, *, tm, nj, name, rows=None, sub_rows=None):
    t, k = a.shape
    ni = (rows or t) // tm
    nw, ne, no = len(weights), len(extras), len(outs)
    need_cast = [w.dtype != BF16 for w, _, _, _ in weights]
    nt = (((1,), (1,)), ((), ()))

    def kernel(*refs):
        a_ref = refs[0]
        w_refs = refs[1:1 + nw]
        e_refs = refs[1 + nw:1 + nw + ne]
        o_refs = refs[1 + nw + ne:1 + nw + ne + no]
        s_refs = list(refs[1 + nw + ne + no:])
        staged = {idx: s_refs.pop(0) for idx in range(nw) if need_cast[idx]}
        if staged:
            @pl.when(pl.program_id(1) == 0)
            def _():
                for idx, s_ref in staged.items():
                    s_ref[...] = w_refs[idx][...].astype(BF16)

        for r0 in range(0, tm, sub_rows or tm):
            rows = slice(r0, r0 + (sub_rows or tm))
            av = a_ref[rows, :]
            accs = []
            for idx in range(nw):
                w_ref = staged.get(idx, w_refs[idx])
                if weights[idx][2] == "kn":
                    accs.append(jnp.dot(av, w_ref[...], preferred_element_type=F32))
                else:
                    accs.append(lax.dot_general(av, w_ref[...], nt, preferred_element_type=F32))
            epilogue(accs, e_refs, o_refs, rows)

    in_specs = [pl.BlockSpec((tm, k), lambda j, i: (i, 0))]
    scratch = []
    for (w, off, layout, width), cast in zip(weights, need_cast):
        if layout == "kn":
            shape = (k, width)
            in_specs.append(pl.BlockSpec(shape, functools.partial(lambda j, i, off: (0, off + j), off=off)))
        else:
            shape = (width, k)
            in_specs.append(pl.BlockSpec(
                (pl.Element(width), pl.Element(k)),
                functools.partial(lambda j, i, off, width: (pl.multiple_of(off + j * width, SUBLANES), 0),
                                  off=off, width=width)))
        if cast:
            scratch.append(pltpu.VMEM(shape, BF16))
    in_specs += [spec for _, spec in extras]
    return pl.pallas_call(
        kernel,
        grid=(nj, ni),
        in_specs=in_specs,
        out_specs=[spec for _, spec in outs],
        out_shape=[sds for sds, _ in outs],
        scratch_shapes=scratch,
        compiler_params=_cparams(2),
        name=name,
    )(a, *[w[0] for w in weights], *[e for e, _ in extras])


def _gates(xn, w_in_t, row0, tm, tn, rows):
    def epi(accs, e_refs, o_refs, rows):
        o_refs[0][rows, :] = jax.nn.sigmoid(accs[0]).astype(BF16)

    n = 2 * D_MODEL
    return _fused_mm(xn, [(w_in_t, row0, "nk", tn)], epi, [],
                     [(jax.ShapeDtypeStruct((rows, n), BF16), pl.BlockSpec((tm, tn), lambda j, i: (i, j)))],
                     tm=tm, nj=n // tn, name="gates", rows=rows)[0]


def _q_path(xn, w_in_t, q_norm, wq2, cos2, sin2, tm, seq, rows):
    nr = MLA_HEADS * QK_ROPE
    nn = MLA_HEADS * QK_NOPE
    lanes = 2 * QK_ROPE

    def epi(accs, e_refs, o_refs, rows):
        qn_ref, w2_ref, cos_ref, sin_ref = e_refs
        cq = _rms(accs[0], qn_ref[...]).astype(BF16)
        q = jnp.dot(cq, w2_ref[...], preferred_element_type=F32)
        o_refs[0][rows, :nn] = (q[:, :nn] * ATTN_SCALE).astype(BF16)
        cos, sin = cos_ref[rows, :] * ATTN_SCALE, sin_ref[rows, :] * ATTN_SCALE
        for c0 in range(0, nr, lanes):
            rope = q[:, nn + c0:nn + c0 + lanes] * cos + q[:, nn + nr + c0:nn + nr + c0 + lanes] * sin
            o_refs[0][rows, nn + c0:nn + c0 + lanes] = rope.astype(BF16)

    pos_tiles = seq // tm
    extras = [(q_norm.reshape(1, Q_RANK), pl.BlockSpec((1, Q_RANK), lambda j, i: (0, 0))),
              (wq2, pl.BlockSpec(wq2.shape, lambda j, i: (0, 0))),
              (cos2, pl.BlockSpec((tm, lanes), lambda j, i: (i % pos_tiles, 0))),
              (sin2, pl.BlockSpec((tm, lanes), lambda j, i: (i % pos_tiles, 0)))]
    outs = [(jax.ShapeDtypeStruct((rows, nn + nr), BF16), pl.BlockSpec((tm, nn + nr), lambda j, i: (i, 0)))]
    return _fused_mm(xn, [(w_in_t, S5_WIDTH, "nk", Q_RANK)], epi, extras, outs, tm=tm, nj=1, name="q_path",
                     rows=rows)[0]


def _kv_path(xn, w_kv_t, kv_norm, w_ukv_bf, cos_sin_k, tm, seq, lat_rows):
    nkv = w_ukv_bf.shape[1]

    def epi(accs, e_refs, o_refs, rows):
        acc = accs[0]
        ckv = _rms(acc[:, :KV_RANK], e_refs[0][...]).astype(BF16)
        o_refs[0][rows, :] = jnp.dot(ckv, e_refs[1][...], preferred_element_type=F32).astype(BF16)
        prod = acc[:, KV_RANK:] * e_refs[2][rows, :]
        o_refs[1][rows, :] = (prod + pltpu.roll(prod, QK_ROPE, 1))[:, :QK_ROPE].astype(BF16)

    t = xn.shape[0]
    pos_tiles, lat_tiles = seq // tm, lat_rows // tm
    extras = [(kv_norm.reshape(1, KV_RANK), pl.BlockSpec((1, KV_RANK), lambda j, i: (0, 0))),
              (w_ukv_bf, pl.BlockSpec(w_ukv_bf.shape, lambda j, i: (0, 0))),
              (cos_sin_k, pl.BlockSpec((tm, 2 * QK_ROPE),
                                       lambda j, i: (jnp.where(i < lat_tiles, i % pos_tiles, pos_tiles), 0)))]
    outs = [(jax.ShapeDtypeStruct((t, nkv), BF16), pl.BlockSpec((tm, nkv), lambda j, i: (i, 0))),
            (jax.ShapeDtypeStruct((t, QK_ROPE), BF16), pl.BlockSpec((tm, QK_ROPE), lambda j, i: (i, 0)))]
    return _fused_mm(xn, [(w_kv_t, 0, "nk", w_kv_t.shape[0])], epi, extras, outs, tm=tm, nj=1, name="kv_path")


ATTN_SUB_ROWS = 256


def _attn_kernel(qn_ref, qr_ref, kvl_ref, kvc_ref, krl_ref, krc_ref, o_ref, k_scr, v_scr, *, seq, ctx):
    dk = QK_NOPE + QK_ROPE

    @pl.when(pl.program_id(2) == 0)
    def _():
        for h in range(2):
            base = h * (QK_NOPE + V_DIM)
            k_scr[h, :seq, :QK_NOPE] = kvl_ref[:, base:base + QK_NOPE]
            k_scr[h, seq:, :QK_NOPE] = kvc_ref[:, base:base + QK_NOPE]
            k_scr[h, :seq, QK_NOPE:dk] = krl_ref[...]
            k_scr[h, seq:, QK_NOPE:dk] = krc_ref[...]
            v_scr[h, :seq, :V_DIM] = kvl_ref[:, base + QK_NOPE:base + QK_NOPE + V_DIM]
            v_scr[h, seq:, :V_DIM] = kvc_ref[:, base + QK_NOPE:base + QK_NOPE + V_DIM]
            ones_col = lax.broadcasted_iota(jnp.int32, (seq + ctx, V_DIM), 1) == 0
            v_scr[h, :, V_DIM:] = jnp.where(ones_col, 1.0, 0.0).astype(BF16)

    for r0 in range(0, qn_ref.shape[0], ATTN_SUB_ROWS):
        rows = slice(r0, r0 + ATTN_SUB_ROWS)
        for h in range(2):
            q = jnp.concatenate([qn_ref[rows, h * QK_NOPE:(h + 1) * QK_NOPE],
                                 qr_ref[rows, h * QK_ROPE:(h + 1) * QK_ROPE]], axis=1)
            s = lax.dot_general(q, k_scr[h], (((1,), (1,)), ((), ())), preferred_element_type=F32)
            m = jnp.max(s, axis=-1, keepdims=True)
            p = jnp.exp((s - m).astype(BF16))
            ol = jnp.dot(p, v_scr[h], preferred_element_type=F32)
            o_ref[rows, h * V_DIM:(h + 1) * V_DIM] = (ol[:, :V_DIM] / ol[:, V_DIM:V_DIM + 1]).astype(o_ref.dtype)


def _attention(q, kv, kr, batch, seq, ctx, tq):
    nq = seq // tq
    ctx0 = batch * seq // ctx
    nn_blocks = MLA_HEADS * QK_NOPE // (2 * QK_NOPE)
    dk = QK_NOPE + QK_ROPE
    hw = 2 * (QK_NOPE + V_DIM)
    return pl.pallas_call(
        functools.partial(_attn_kernel, seq=seq, ctx=ctx),
        grid=(batch, MLA_HEADS // 2, nq),
        in_specs=[pl.BlockSpec((tq, 2 * QK_NOPE), lambda b, hp, qi: (b * nq + qi, hp)),
                  pl.BlockSpec((tq, 2 * QK_ROPE), lambda b, hp, qi: (b * nq + qi, 2 * nn_blocks + hp)),
                  pl.BlockSpec((seq, hw), lambda b, hp, qi: (b, hp)),
                  pl.BlockSpec((ctx, hw), lambda b, hp, qi: (ctx0 + b, hp)),
                  pl.BlockSpec((seq, QK_ROPE), lambda b, hp, qi: (b, 0)),
                  pl.BlockSpec((ctx, QK_ROPE), lambda b, hp, qi: (ctx0 + b, 0))],
        out_specs=pl.BlockSpec((tq, 2 * V_DIM), lambda b, hp, qi: (b * nq + qi, hp)),
        out_shape=jax.ShapeDtypeStruct((batch * seq, MLA_HEADS * V_DIM), BF16),
        scratch_shapes=[pltpu.VMEM((2, seq + ctx, dk), BF16), pltpu.VMEM((2, seq + ctx, 2 * V_DIM), BF16)],
        compiler_params=_cparams(3),
        name="attention",
    )(q, q, kv, kv, kr, kr)


def _s5_prep_pair(pp, are_ref, aim_ref, ldt_ref, bre_ref, bim_ref, cre_ref, cim_ref,
                  toep_ref, wsr_ref, wsi_ref, wor_ref, woi_ref, atr_ref, ati_ref):
    tc, g = S5_CHUNK, S5_GROUP
    w = tc * g
    lane = lax.broadcasted_iota(jnp.int32, (g, 2 * S5_STATE), 1)
    in_group = (lane < S5_STATE, lane >= S5_STATE)
    lane_w = lax.broadcasted_iota(jnp.int32, (g, w), 1)
    nt = (((1,), (1,)), ((), ()))
    toep_rows = [[jnp.zeros((g, w), F32) for _ in range(tc)] for _ in range(2)]
    for d in range(2):
        lr, li = are_ref[d, pp], aim_ref[d, pp]
        dt = jnp.exp(ldt_ref[d, pp])
        mag = jnp.exp(lr * dt)
        ab_re, ab_im = mag * jnp.cos(li * dt), mag * jnp.sin(li * dt)
        den = lr * lr + li * li
        nr, ni = ab_re - 1.0, ab_im
        co_re = (nr * lr + ni * li) / den
        co_im = (ni * lr - nr * li) / den
        br, bi = bre_ref[d, pp], bim_ref[d, pp]
        bb_re = co_re * br - co_im * bi
        bb_im = co_re * bi + co_im * br
        pw = [(jnp.ones_like(ab_re), jnp.zeros_like(ab_re))]
        for _ in range(tc):
            pr, pi = pw[-1]
            pw.append((pr * ab_re - pi * ab_im, pr * ab_im + pi * ab_re))
        cr, ci = cre_ref[d, pp], cim_ref[d, pp]
        ca = [(cr * pr - ci * pi, cr * pi + ci * pr) for pr, pi in pw]
        taus = list(range(tc))[::-1] if d else list(range(tc))
        y_re = jnp.concatenate([ca[t][0] for t in taus], axis=0).astype(BF16)
        y_im = jnp.concatenate([ca[t][1] for t in taus], axis=0).astype(BF16)
        for j in range(2):
            x_re = jnp.where(in_group[j], bb_re, 0.0)
            x_im = jnp.where(in_group[j], bb_im, 0.0)
            kt = (lax.dot_general(x_re.astype(BF16), y_re, nt, preferred_element_type=F32)
                  - lax.dot_general(x_im.astype(BF16), y_im, nt, preferred_element_type=F32))
            for r in range(tc):
                sh = (r + 1) * g if d else r * g
                blk = pltpu.roll(kt, sh % w, 1) if sh % w else kt
                keep = (lane_w < sh) if d else (lane_w >= sh)
                toep_rows[j][r] = toep_rows[j][r] + jnp.where(keep, blk, 0.0)
        for r in range(tc):
            pr, pi = pw[r] if d else pw[tc - 1 - r]
            w_re = bb_re * pr - bb_im * pi
            w_im = bb_re * pi + bb_im * pr
            car, cai = ca[tc - r] if d else ca[r + 1]
            for j in range(2):
                rows = slice(j * w + r * g, j * w + (r + 1) * g)
                wsr_ref[d, pp, rows, :] = jnp.where(in_group[j], w_re, 0.0).astype(BF16)
                wsi_ref[d, pp, rows, :] = jnp.where(in_group[j], w_im, 0.0).astype(BF16)
                wor_ref[d, pp, rows, :] = jnp.where(in_group[j], car, 0.0).astype(BF16)
                woi_ref[d, pp, rows, :] = jnp.where(in_group[j], -cai, 0.0).astype(BF16)
        atr_ref[d, pp] = pw[tc][0]
        ati_ref[d, pp] = pw[tc][1]
    for j in range(2):
        toep_ref[pp, j] = jnp.concatenate(toep_rows[j], axis=0).astype(BF16)


S5_PREP_PAIRS = 8


def _s5_prep_kernel(*refs):
    for pp in range(S5_PREP_PAIRS):
        _s5_prep_pair(pp, *refs)


def _s5_prep(params):
    tc = S5_CHUNK
    wide = 2 * tc * S5_GROUP
    sl = 2 * S5_STATE

    def pair_lanes(v):
        return v.astype(F32).reshape(2, S5_PAIRS, 1, sl)

    def pair_rows(v):
        rows = v.shape[2]
        return v.astype(F32).reshape(2, S5_PAIRS, 2, rows, S5_STATE).transpose(0, 1, 3, 2, 4).reshape(
            2, S5_PAIRS, rows, sl)

    ldt = jnp.broadcast_to(params['s5_log_dt'].astype(F32)[:, :, None], (2, S5_GROUPS, S5_STATE))
    ins = [pair_lanes(params['s5_a_re']), pair_lanes(params['s5_a_im']), pair_lanes(ldt),
           pair_rows(params['s5_b_re'].transpose(0, 1, 3, 2)), pair_rows(params['s5_b_im'].transpose(0, 1, 3, 2)),
           pair_rows(params['s5_c_re']), pair_rows(params['s5_c_im'])]
    pp = S5_PREP_PAIRS
    vec_spec = pl.BlockSpec((2, pp, 1, sl), lambda k: (0, k, 0, 0))
    mat_spec = pl.BlockSpec((2, pp, S5_GROUP, sl), lambda k: (0, k, 0, 0))
    w_spec = pl.BlockSpec((2, pp, wide, sl), lambda k: (0, k, 0, 0))
    w_sds = jax.ShapeDtypeStruct((2, S5_PAIRS, wide, sl), BF16)
    a_sds = jax.ShapeDtypeStruct((2, S5_PAIRS, 1, sl), F32)
    return pl.pallas_call(
        _s5_prep_kernel,
        grid=(S5_PAIRS // pp,),
        in_specs=[vec_spec] * 3 + [mat_spec] * 4,
        out_specs=[pl.BlockSpec((pp, 2, wide // 2, wide // 2), lambda k: (k, 0, 0, 0))] + [w_spec] * 4 + [vec_spec] * 2,
        out_shape=[jax.ShapeDtypeStruct((S5_PAIRS, 2, wide // 2, wide // 2), BF16)] + [w_sds] * 4 + [a_sds] * 2,
        compiler_params=_cparams(1),
        name="s5_prep",
    )(*ins)


S5_STEP_PAIRS = 4
def _s5_state_rows(batch, n_lat, n_ctx):
    def pitch(n):
        p = -(-n // SUBLANES)
        return SUBLANES * (p + 1 - p % 2)

    lat_pitch, ctx_pitch = pitch(n_lat), pitch(n_ctx)
    ctx_base = batch * lat_pitch
    return lat_pitch, ctx_pitch, ctx_base, ctx_base + batch * ctx_pitch


def _s5_kernel(ul_ref, uc_ref, dsk_ref, *refs, batch, seq, ctx):
    toep_ref, wsr, wsi, wor, woi, atr, ati = refs[:7]
    o_ref = refs[7]
    wscr, zscr, upscr, sre, sim, hre, him = refs[8:]
    tc = S5_CHUNK
    n_lat, n_ctx = seq // tc, ctx // tc
    rows_lat = n_lat * batch
    lat_pitch, ctx_pitch, ctx_base, _ = _s5_state_rows(batch, n_lat, n_ctx)
    w = tc * S5_GROUP
    lanes = S5_STEP_PAIRS * 2 * S5_GROUP

    def scatter_tiles(xt, r, col0):
        for kk in range(S5_STEP_PAIRS):
            for j in range(2):
                ch = (2 * kk + j) * S5_GROUP
                wscr[kk, j * w + r * S5_GROUP:j * w + (r + 1) * S5_GROUP, col0:col0 + lanes] = xt[ch:ch + S5_GROUP, :]

    for r in range(tc):
        for b in range(batch):
            x = ul_ref[pl.ds(b * seq + r, n_lat, stride=tc), :]
            scatter_tiles(x.T, r, b * n_lat)
        xc = jnp.concatenate([uc_ref[pl.ds(b * ctx + r, n_ctx, stride=tc), :] for b in range(batch)]
                             + [jnp.zeros((lanes - batch * n_ctx, lanes), F32)], axis=0)
        scatter_tiles(xc.T, r, rows_lat)

    chains = [(kk, d) for kk in range(S5_STEP_PAIRS) for d in range(2)]
    for kk in range(S5_STEP_PAIRS):
        upscr[kk] = wscr[kk].T.astype(BF16)
        for d in range(2):
            for dst, wst in ((sre, wsr), (sim, wsi)):
                s = jnp.dot(upscr[kk], wst[d, kk], preferred_element_type=F32)
                for b in range(batch):
                    dst[kk, d, b * lat_pitch:b * lat_pitch + n_lat, :] = s[b * n_lat:(b + 1) * n_lat]
                    dst[kk, d, ctx_base + b * ctx_pitch:ctx_base + b * ctx_pitch + n_ctx, :] = (
                        s[rows_lat + b * n_ctx:rows_lat + (b + 1) * n_ctx])

    ctx_rows = [pl.ds(ctx_base + c, batch, stride=ctx_pitch) for c in range(n_ctx)]
    lat_rows = [pl.ds(c, batch, stride=lat_pitch) for c in range(n_lat)]
    order = (ctx_rows + lat_rows, ctx_rows[::-1] + lat_rows[::-1])
    coef = {(kk, d): (atr[d, kk], ati[d, kk]) for kk, d in chains}
    state = {ch: (jnp.zeros((batch, 2 * S5_STATE), F32), jnp.zeros((batch, 2 * S5_STATE), F32)) for ch in chains}
    for t in range(n_ctx + n_lat):
        for kk, d in chains:
            rows = order[d][t]
            (ar, ai), (h_re, h_im) = coef[kk, d], state[kk, d]
            hre[kk, d, rows, :] = h_re
            him[kk, d, rows, :] = h_im
            state[kk, d] = (ar * h_re - ai * h_im + sre[kk, d, rows, :],
                            ar * h_im + ai * h_re + sim[kk, d, rows, :])

    for kk in range(S5_STEP_PAIRS):
        ul = upscr[kk, :rows_lat, :]
        y = ul.astype(F32) * dsk_ref[kk]
        y = y + jnp.concatenate(
            [jnp.dot(ul[:, :w], toep_ref[kk, 0], preferred_element_type=F32),
             jnp.dot(ul[:, w:], toep_ref[kk, 1], preferred_element_type=F32)], axis=1)
        nt = (((1,), (1,)), ((), ()))
        for d in range(2):
            h_r = jnp.concatenate([hre[kk, d, b * lat_pitch:b * lat_pitch + n_lat, :] for b in range(batch)], axis=0)
            h_i = jnp.concatenate([him[kk, d, b * lat_pitch:b * lat_pitch + n_lat, :] for b in range(batch)], axis=0)
            y = y + lax.dot_general(h_r.astype(BF16), wor[d, kk], nt, preferred_element_type=F32)
            y = y + lax.dot_general(h_i.astype(BF16), woi[d, kk], nt, preferred_element_type=F32)
        yt = y.T
        for b in range(batch):
            for s in range(tc):
                for j in range(2):
                    ch = (2 * kk + j) * S5_GROUP
                    zscr[b * tc + s, ch:ch + S5_GROUP, :] = yt[j * w + s * S5_GROUP:j * w + (s + 1) * S5_GROUP,
                                                               b * n_lat:(b + 1) * n_lat]

    for b in range(batch):
        for s in range(tc):
            o_ref[pl.ds(b * seq + s, n_lat, stride=tc), :] = jax.nn.gelu(zscr[b * tc + s].T)


def _s5_mixer(u, params, batch, seq, ctx):
    tc = S5_CHUNK
    n_lat, n_ctx = seq // tc, ctx // tc
    sp = S5_STEP_PAIRS
    lanes = sp * 2 * S5_GROUP
    assert n_lat == lanes and batch * n_ctx <= lanes
    rows_lat = n_lat * batch
    rows_all = rows_lat + lanes
    state_rows = _s5_state_rows(batch, n_lat, n_ctx)[3]
    wide = 2 * tc * S5_GROUP
    dsk = jnp.broadcast_to(params['s5_d'].astype(F32).reshape(S5_PAIRS, 2, 1, S5_GROUP),
                           (S5_PAIRS, 2, tc, S5_GROUP)).reshape(S5_PAIRS, 1, wide)
    weights = _s5_prep(params)
    w_spec = pl.BlockSpec((2, sp, wide, 2 * S5_STATE), lambda k: (0, k, 0, 0))
    a_spec = pl.BlockSpec((2, sp, 1, 2 * S5_STATE), lambda k: (0, k, 0, 0))
    specs = [pl.BlockSpec((sp, 2, wide // 2, wide // 2), lambda k: (k, 0, 0, 0))] + [w_spec] * 4 + [a_spec] * 2
    return pl.pallas_call(
        functools.partial(_s5_kernel, batch=batch, seq=seq, ctx=ctx),
        grid=(S5_PAIRS // sp,),
        in_specs=[pl.BlockSpec((batch * seq, lanes), lambda k: (0, k)),
                  pl.BlockSpec((batch * ctx, lanes), lambda k: (seq // ctx, k)),
                  pl.BlockSpec((sp, 1, wide), lambda k: (k, 0, 0))] + specs,
        out_specs=pl.BlockSpec((batch * seq, lanes), lambda k: (0, k)),
        out_shape=jax.ShapeDtypeStruct((batch * seq, S5_WIDTH), F32),
        scratch_shapes=[pltpu.VMEM((sp, wide, rows_all), F32),
                        pltpu.VMEM((batch * tc, lanes, n_lat), F32),
                        pltpu.VMEM((sp, rows_all, wide), BF16)]
                       + [pltpu.VMEM((sp, 2, state_rows, 2 * S5_STATE), F32) for _ in range(4)],
        compiler_params=_cparams(1),
        name="s5",
    )(u, u, dsk, *weights)


def _merge_kernel(z5_ref, o_ref_in, gs_ref, gm_ref, wa_ref, wb_ref, wm_ref, out_ref, wa_s, wb_s, wm_s):
    @pl.when(pl.program_id(1) == 0)
    def _():
        wa_s[...] = wa_ref[...].astype(BF16)
        wb_s[...] = wb_ref[...].astype(BF16)
        wm_s[...] = wm_ref[...].astype(BF16)

    sub = min(MERGE_SUB_ROWS, out_ref.shape[0])
    for r0 in range(0, out_ref.shape[0], sub):
        rows = slice(r0, r0 + sub)
        z = z5_ref[rows, :].astype(BF16)
        a = jnp.dot(z, wa_s[...], preferred_element_type=F32)
        b = jnp.dot(z, wb_s[...], preferred_element_type=F32)
        mla = jnp.dot(o_ref_in[rows, :], wm_s[...], preferred_element_type=F32)
        merged = gs_ref[rows, :].astype(F32) * (a * jax.nn.sigmoid(b)) + gm_ref[rows, :].astype(F32) * mla
        out_ref[rows, :] = merged.astype(out_ref.dtype)


def _merge(z5, o_mla, gates, w_glu, w_mla_o, tm, tn):
    t = z5.shape[0]
    nj = D_MODEL // tn
    return pl.pallas_call(
        _merge_kernel,
        grid=(nj, t // tm),
        in_specs=[pl.BlockSpec((tm, S5_WIDTH), lambda j, i: (i, 0)),
                  pl.BlockSpec((tm, MLA_HEADS * V_DIM), lambda j, i: (i, 0)),
                  pl.BlockSpec((tm, tn), lambda j, i: (i, j)),
                  pl.BlockSpec((tm, tn), lambda j, i: (i, nj + j)),
                  pl.BlockSpec((S5_WIDTH, tn), lambda j, i: (0, j)),
                  pl.BlockSpec((S5_WIDTH, tn), lambda j, i: (0, nj + j)),
                  pl.BlockSpec((MLA_HEADS * V_DIM, tn), lambda j, i: (0, j))],
        out_specs=pl.BlockSpec((tm, tn), lambda j, i: (i, j)),
        out_shape=jax.ShapeDtypeStruct((t, D_MODEL), BF16),
        scratch_shapes=[pltpu.VMEM((S5_WIDTH, tn), BF16), pltpu.VMEM((S5_WIDTH, tn), BF16),
                        pltpu.VMEM((MLA_HEADS * V_DIM, tn), BF16)],
        compiler_params=_cparams(2),
        name="merge",
    )(z5, o_mla, gates, gates, w_glu, w_glu, w_mla_o)


def _out_proj_norm(merged, w_out, x_res, m, gain2, rows_per_batch, tm):
    t, k = merged.shape
    tpb = rows_per_batch // tm
    row = pl.BlockSpec((tm, D_MODEL), lambda i: (i, 0))
    gate1, scale2, shift2 = (_mod_spec(D_MODEL, lambda i: i // tpb, which)
                             for which in (MOD_GATE1, MOD_SCALE2, MOD_SHIFT2))

    def kernel(a_hbm, w_ref, x_hbm, m_hbm, n2_ref, x1_hbm, xn_hbm, w_s):
        w_s[...] = w_ref[...].astype(BF16)

        def body(a_ref, x_ref, g1_ref, sc_ref, sh_ref, x1_ref, xn_ref):
            x1 = x_ref[...] + g1_ref[0, 0] * jnp.dot(a_ref[...], w_s[...], preferred_element_type=F32)
            x1_ref[...] = x1
            xn_ref[...] = (_rms(x1, n2_ref[...]) * (1.0 + sc_ref[0, 0]) + sh_ref[0, 0]).astype(xn_ref.dtype)

        pltpu.emit_pipeline(
            body, grid=(t // tm,),
            in_specs=[pl.BlockSpec((tm, k), lambda i: (i, 0), pipeline_mode=ROW_STREAM_BUFFERS),
                      pl.BlockSpec((tm, D_MODEL), lambda i: (i, 0), pipeline_mode=ROW_STREAM_BUFFERS),
                      gate1, scale2, shift2],
            out_specs=[row, row],
        )(a_hbm, x_hbm, m_hbm, m_hbm, m_hbm, x1_hbm, xn_hbm)

    hbm = pl.BlockSpec(memory_space=pl.ANY)
    vmem = pl.BlockSpec(memory_space=pltpu.VMEM)
    return pl.pallas_call(
        kernel,
        in_specs=[hbm, vmem, hbm, hbm, vmem],
        out_specs=[hbm, hbm],
        out_shape=[jax.ShapeDtypeStruct((t, D_MODEL), F32), jax.ShapeDtypeStruct((t, D_MODEL), BF16)],
        scratch_shapes=[pltpu.VMEM((k, D_MODEL), BF16)],
        compiler_params=pltpu.CompilerParams(vmem_limit_bytes=VMEM_LIMIT_BYTES),
        name="out_proj",
    )(merged, w_out, x_res, m, gain2.reshape(1, D_MODEL))


def _ffn_in(xn, w_ffn_in, tm, tn):
    def epi(accs, e_refs, o_refs, rows):
        o_refs[0][rows, :] = (jax.nn.silu(accs[0]) * accs[1]).astype(BF16)

    t = xn.shape[0]
    nj = D_FF // tn
    outs = [(jax.ShapeDtypeStruct((t, D_FF), BF16), pl.BlockSpec((tm, tn), lambda j, i: (i, j)))]
    return _fused_mm(xn, [(w_ffn_in, 0, "kn", tn), (w_ffn_in, nj, "kn", tn)], epi, [], outs,
                     tm=tm, nj=nj, name="ffn_in", sub_rows=MM_SUB_ROWS)[0]


def _ffn_out_kernel(h_ref, w_ref, x_ref, g2_ref, nf_ref, o_ref, w_s, *, last):
    @pl.when(pl.program_id(0) == 0)
    def _():
        w_s[...] = w_ref[...].astype(BF16)

    y = x_ref[...] + g2_ref[0, 0] * jnp.dot(h_ref[...], w_s[...], preferred_element_type=F32)
    o_ref[...] = _rms(y, nf_ref[...]) if last else y


def _ffn_out(hid, w_ffn_out, x_res, m, norm_f, rows_per_batch, tm):
    t, k = hid.shape
    kh = k // 2
    tpb = rows_per_batch // tm
    row = pl.BlockSpec((tm, D_MODEL), lambda i: (i, 0))
    y = x_res
    for half in range(2):
        y = pl.pallas_call(
            functools.partial(_ffn_out_kernel, last=half == 1),
            grid=(t // tm,),
            in_specs=[pl.BlockSpec((tm, kh), functools.partial(lambda i, half: (i, half), half=half)),
                      pl.BlockSpec((kh, D_MODEL), functools.partial(lambda i, half: (half, 0), half=half),
                                   pipeline_mode=pl.Buffered(1)),
                      row, _mod_spec(D_MODEL, lambda i: i // tpb, MOD_GATE2),
                      pl.BlockSpec((1, D_MODEL), lambda i: (0, 0))],
            out_specs=row,
            out_shape=jax.ShapeDtypeStruct((t, D_MODEL), F32),
            scratch_shapes=[pltpu.VMEM((kh, D_MODEL), BF16)],
            compiler_params=_cparams(1),
            name="ffn_out",
        )(hid, w_ffn_out, y, m, norm_f.reshape(1, D_MODEL))
    return y


def _rope_rot_cols(w):
    k = w.shape[0]
    ws = w.reshape(k, -1, 2, 2, QK_ROPE // 4)
    return jnp.stack([-ws[:, :, :, 1, :], ws[:, :, :, 0, :]], axis=3).reshape(k, -1)


def _rope_tables(n_tokens):
    rows = n_tokens // GRID_W
    row = np.repeat(np.arange(rows, dtype=np.float32), GRID_W)
    col = np.tile(np.arange(GRID_W, dtype=np.float32), rows)
    n_freq = QK_ROPE // 4
    inv = (np.float32(ROPE_BASE) ** (-np.arange(n_freq, dtype=np.float32) / np.float32(n_freq))).astype(np.float32)
    ang = np.stack([row[:, None] * inv, col[:, None] * inv], axis=1)
    cos = np.broadcast_to(np.cos(ang)[:, :, None, :], (n_tokens, 2, 2, n_freq)).reshape(n_tokens, QK_ROPE)
    sin = np.broadcast_to(np.sin(ang)[:, :, None, :], (n_tokens, 2, 2, n_freq)).reshape(n_tokens, QK_ROPE)
    return cos.astype(np.float32), sin.astype(np.float32)


def kernel(x, c, ctx, c_ctx, w_mod, b_mod, norm1, norm2, w_in, s5_a_re, s5_a_im, s5_log_dt, s5_b_re, s5_b_im,
           s5_c_re, s5_c_im, s5_d, w_glu, q_norm, kv_norm, w_uq, w_ukv, w_mla_o, w_out, w_ffn_in, w_ffn_out,
           norm_f):
    batch, seq, _ = x.shape
    n_ctx = ctx.shape[1]
    assert w_mod.shape[0] == 1, "single-layer block"
    p = dict(s5_a_re=s5_a_re[0], s5_a_im=s5_a_im[0], s5_log_dt=s5_log_dt[0], s5_b_re=s5_b_re[0],
             s5_b_im=s5_b_im[0], s5_c_re=s5_c_re[0], s5_c_im=s5_c_im[0], s5_d=s5_d[0])
    w_in_t = w_in.reshape(w_in.shape[1:]).T

    cv = jnp.concatenate([c, c_ctx[None], jnp.zeros((8 - batch - 1, D_MODEL), F32)], axis=0)
    m = _modulation(cv, w_mod[0], b_mod[0]).reshape(8, 6, 1, D_MODEL)

    lat_rows = batch * seq
    x2d = x.reshape(lat_rows, D_MODEL)
    c2d = ctx.reshape(batch * n_ctx, D_MODEL)
    xn, u = _norm_u_proj(x2d, c2d, norm1[0], m, w_in_t, seq, batch, 512)

    kv_lo = S5_WIDTH + Q_RANK
    w_kv_rows = w_in_t[kv_lo:kv_lo + KV_RANK + QK_ROPE]
    w_kv_t = jnp.concatenate([w_kv_rows, _rope_rot_cols(w_kv_rows[KV_RANK:].T).T], axis=0)
    wq = w_uq[0].reshape(Q_RANK, MLA_HEADS, QK_NOPE + QK_ROPE)
    wq_rope = wq[:, :, QK_NOPE:].reshape(Q_RANK, MLA_HEADS * QK_ROPE)
    wq2 = jnp.concatenate([wq[:, :, :QK_NOPE].reshape(Q_RANK, MLA_HEADS * QK_NOPE), wq_rope,
                           _rope_rot_cols(wq_rope)], axis=1).astype(BF16)
    w_ukv_bf = w_ukv[0].astype(BF16)
    tm = 1024
    cos, sin = _rope_tables(seq)
    identity = np.concatenate([np.ones((tm, QK_ROPE), np.float32), np.zeros((tm, QK_ROPE), np.float32)], axis=1)
    cos_sin_k = jnp.asarray(np.concatenate([np.concatenate([cos, sin], axis=1), identity], axis=0))
    cos2, sin2 = jnp.asarray(np.tile(cos, (1, 2))), jnp.asarray(np.tile(sin, (1, 2)))

    q = _q_path(xn, w_in_t, q_norm[0], wq2, cos2, sin2, tm, seq, lat_rows)
    kv, kr = _kv_path(xn, w_kv_t, kv_norm[0], w_ukv_bf, cos_sin_k, tm, seq, lat_rows)
    gates = _gates(xn, w_in_t, kv_lo + KV_RANK + QK_ROPE, tm, 1024, lat_rows)

    z5 = _s5_mixer(u, p, batch, seq, n_ctx)
    o_mla = _attention(q, kv, kr, batch, seq, n_ctx, 2048)

    merged = _merge(z5, o_mla, gates, w_glu[0], w_mla_o[0], 512, 1024)
    x1, xn2 = _out_proj_norm(merged, w_out[0], x2d, m, norm2[0], seq, 256)
    hid = _ffn_in(xn2, w_ffn_in[0], 2048, 512)
    return _ffn_out(hid, w_ffn_out[0], x1, m, norm_f, seq, 256).reshape(batch, seq, D_MODEL)
```

```python
import functools

import jax
import jax.numpy as jnp
import numpy as np
from jax import lax
from jax.experimental import pallas as pl
from jax.experimental.pallas import tpu as pltpu

F32 = jnp.float32
BF16 = jnp.bfloat16

D_MODEL = 2048
GRID_W = 64
EPS = 1e-6
S5_WIDTH = D_MODEL // 2
S5_GROUP = 16
S5_GROUPS = S5_WIDTH // S5_GROUP
S5_STATE = 64
S5_CHUNK = 16
S5_PAIRS = S5_GROUPS // 2
MLA_HEADS = 8
QK_NOPE = 128
QK_ROPE = 64
V_DIM = 128
Q_RANK = 512
KV_RANK = 256
ROPE_BASE = 10000.0
ATTN_SCALE = (QK_NOPE + QK_ROPE) ** -0.5
D_FF = -(-8 * D_MODEL // (3 * 256)) * 256

VMEM_LIMIT_BYTES = 56 * 1024 * 1024
SUBLANES = 8
MM_SUB_ROWS = 512
MERGE_SUB_ROWS = 256


def _cparams(n_axes, fuse_inputs=None):
    return pltpu.CompilerParams(dimension_semantics=("arbitrary",) * n_axes,
                                vmem_limit_bytes=VMEM_LIMIT_BYTES, allow_input_fusion=fuse_inputs)


def _rms(x, g):
    return x * lax.rsqrt(jnp.mean(x * x, axis=-1, keepdims=True) + EPS) * g


def _mod_kernel(cv_ref, w_ref, b_ref, o_ref):
    s = jax.nn.silu(cv_ref[...]).astype(BF16)
    o_ref[...] = jnp.dot(s, w_ref[...].astype(BF16), preferred_element_type=F32) + b_ref[...]


def _modulation(cv, w_mod, b_mod):
    n = w_mod.shape[1]
    tn = 1536
    return pl.pallas_call(
        _mod_kernel,
        grid=(n // tn,),
        in_specs=[pl.BlockSpec((8, D_MODEL), lambda j: (0, 0)),
                  pl.BlockSpec((D_MODEL, tn), lambda j: (0, j)),
                  pl.BlockSpec((1, tn), lambda j: (0, j))],
        out_specs=pl.BlockSpec((8, tn), lambda j: (0, j)),
        out_shape=jax.ShapeDtypeStruct((8, n), F32),
        compiler_params=_cparams(1),
        name="mod",
    )(cv, w_mod, b_mod.reshape(1, n))


MOD_SHIFT1, MOD_SCALE1, MOD_GATE1, MOD_SHIFT2, MOD_SCALE2, MOD_GATE2 = range(6)


def _mod_spec(width, row_of, which, col_of=None):
    col_of = col_of or (lambda *ids: 0)
    return pl.BlockSpec((1, 1, 1, width), lambda *ids: (row_of(*ids), which, 0, col_of(*ids)))


def _norm_u_kernel(x_ref, c_ref, g_ref, sc_ref, sh_ref, w_ref, xn_ref, u_ref, w_s, *, lat_tiles):
    @pl.when(pl.program_id(0) == 0)
    def _():
        w_s[...] = w_ref[...].astype(BF16)

    is_lat = pl.program_id(0) < lat_tiles
    sub = min(MERGE_SUB_ROWS, xn_ref.shape[0])
    for r0 in range(0, xn_ref.shape[0], sub):
        rows = slice(r0, r0 + sub)
        src = jnp.where(is_lat, x_ref[rows, :], c_ref[rows, :])
        xn = (_rms(src, g_ref[...]) * (1.0 + sc_ref[0, 0]) + sh_ref[0, 0]).astype(xn_ref.dtype)
        xn_ref[rows, :] = xn
        u_ref[rows, :] = lax.dot_general(xn, w_s[...], (((1,), (1,)), ((), ())), preferred_element_type=F32)


def _norm_u_proj(x2d, c2d, gain, m, w_in_t, rows_per_batch, ctx_row, tm):
    lat_tiles, ctx_tiles = x2d.shape[0] // tm, c2d.shape[0] // tm
    tpb = rows_per_batch // tm
    t_all = x2d.shape[0] + c2d.shape[0]

    def row(i):
        return jnp.where(i < lat_tiles, i // tpb, ctx_row)

    once = pl.Buffered(1)
    return pl.pallas_call(
        functools.partial(_norm_u_kernel, lat_tiles=lat_tiles),
        grid=(lat_tiles + ctx_tiles,),
        in_specs=[pl.BlockSpec((tm, D_MODEL), lambda i: (jnp.minimum(i, lat_tiles - 1), 0)),
                  pl.BlockSpec((tm, D_MODEL), lambda i: (jnp.maximum(i - lat_tiles, 0), 0)),
                  pl.BlockSpec((1, D_MODEL), lambda i: (0, 0)),
                  _mod_spec(D_MODEL, row, MOD_SCALE1), _mod_spec(D_MODEL, row, MOD_SHIFT1),
                  pl.BlockSpec((S5_WIDTH, D_MODEL), lambda i: (0, 0), pipeline_mode=once)],
        out_specs=[pl.BlockSpec((tm, D_MODEL), lambda i: (i, 0)), pl.BlockSpec((tm, S5_WIDTH), lambda i: (i, 0))],
        out_shape=[jax.ShapeDtypeStruct((t_all, D_MODEL), BF16), jax.ShapeDtypeStruct((t_all, S5_WIDTH), F32)],
        scratch_shapes=[pltpu.VMEM((S5_WIDTH, D_MODEL), BF16)],
        compiler_params=_cparams(1),
        name="norm_u_proj",
    )(x2d, c2d, gain.reshape(1, D_MODEL), m, m, w_in_t)


def _fused_mm(a, weights, epilogue, extras, outs, *, tm, nj, name, rows=None, sub_rows=None, fuse_inputs=()):
    t, k = a.shape
    ni = (rows or t) // tm
    nw, ne, no = len(weights), len(extras), len(outs)
    need_cast = [w.dtype != BF16 for w, _, _, _ in weights]
    nt = (((1,), (1,)), ((), ()))

    def kernel(*refs):
        a_ref = refs[0]
        w_refs = refs[1:1 + nw]
        e_refs = refs[1 + nw:1 + nw + ne]
        o_refs = refs[1 + nw + ne:1 + nw + ne + no]
        s_refs = list(refs[1 + nw + ne + no:])
        staged = {idx: s_refs.pop(0) for idx in range(nw) if need_cast[idx]}
        if staged:
            @pl.when(pl.program_id(1) == 0)
            def _():
                for idx, s_ref in staged.items():
                    s_ref[...] = w_refs[idx][...].astype(BF16)

        for r0 in range(0, tm, sub_rows or tm):
            rows = slice(r0, r0 + (sub_rows or tm))
            av = a_ref[rows, :]
            accs = []
            for idx in range(nw):
                w_ref = staged.get(idx, w_refs[idx])
                if weights[idx][2] == "kn":
                    accs.append(jnp.dot(av, w_ref[...], preferred_element_type=F32))
                else:
                    accs.append(lax.dot_general(av, w_ref[...], nt, preferred_element_type=F32))
            epilogue(accs, e_refs, o_refs, rows)

    in_specs = [pl.BlockSpec((tm, k), lambda j, i: (i, 0))]
    scratch = []
    for (w, off, layout, width), cast in zip(weights, need_cast):
        if layout == "kn":
            shape = (k, width)
            in_specs.append(pl.BlockSpec(shape, functools.partial(lambda j, i, off: (0, off + j), off=off)))
        else:
            shape = (width, k)
            in_specs.append(pl.BlockSpec(
                (pl.Element(width), pl.Element(k)),
                functools.partial(lambda j, i, off, width: (pl.multiple_of(off + j * width, SUBLANES), 0),
                                  off=off, width=width)))
        if cast:
            scratch.append(pltpu.VMEM(shape, BF16))
    in_specs += [spec for _, spec in extras]
    return pl.pallas_call(
        kernel,
        grid=(nj, ni),
        in_specs=in_specs,
        out_specs=[spec for _, spec in outs],
        out_shape=[sds for sds, _ in outs],
        scratch_shapes=scratch,
        compiler_params=_cparams(2, [idx in fuse_inputs for idx in range(1 + nw + ne)] if fuse_inputs else None),
        name=name,
    )(a, *[w[0] for w in weights], *[e for e, _ in extras])


def _gates(xn, w_in_t, row0, tm, tn, rows):
    def epi(accs, e_refs, o_refs, rows):
        o_refs[0][rows, :] = jax.nn.sigmoid(accs[0]).astype(BF16)

    n = 2 * D_MODEL
    return _fused_mm(xn, [(w_in_t, row0, "nk", tn)], epi, [],
                     [(jax.ShapeDtypeStruct((rows, n), BF16), pl.BlockSpec((tm, tn), lambda j, i: (i, j)))],
                     tm=tm, nj=n // tn, name="gates", rows=rows)[0]


def _q_path(xn, w_in_t, q_norm, wq2, cos2, sin2, tm, seq, rows):
    nr = MLA_HEADS * QK_ROPE
    nn = MLA_HEADS * QK_NOPE
    lanes = 2 * QK_ROPE

    def epi(accs, e_refs, o_refs, rows):
        qn_ref, w2_ref, cos_ref, sin_ref = e_refs
        cq = _rms(accs[0], qn_ref[...]).astype(BF16)
        q = jnp.dot(cq, w2_ref[...], preferred_element_type=F32)
        o_refs[0][rows, :nn] = (q[:, :nn] * ATTN_SCALE).astype(BF16)
        cos, sin = cos_ref[rows, :] * ATTN_SCALE, sin_ref[rows, :] * ATTN_SCALE
        for c0 in range(0, nr, lanes):
            rope = q[:, nn + c0:nn + c0 + lanes] * cos + q[:, nn + nr + c0:nn + nr + c0 + lanes] * sin
            o_refs[0][rows, nn + c0:nn + c0 + lanes] = rope.astype(BF16)

    pos_tiles = seq // tm
    extras = [(q_norm.reshape(1, Q_RANK), pl.BlockSpec((1, Q_RANK), lambda j, i: (0, 0))),
              (wq2, pl.BlockSpec(wq2.shape, lambda j, i: (0, 0))),
              (cos2, pl.BlockSpec((tm, lanes), lambda j, i: (i % pos_tiles, 0))),
              (sin2, pl.BlockSpec((tm, lanes), lambda j, i: (i % pos_tiles, 0)))]
    outs = [(jax.ShapeDtypeStruct((rows, nn + nr), BF16), pl.BlockSpec((tm, nn + nr), lambda j, i: (i, 0)))]
    return _fused_mm(xn, [(w_in_t, S5_WIDTH, "nk", Q_RANK)], epi, extras, outs, tm=tm, nj=1, name="q_path",
                     rows=rows, fuse_inputs=(3,))[0]


def _kv_path(xn, w_kv_t, kv_norm, w_ukv_bf, cos_sin_k, tm, seq, lat_rows):
    nkv = w_ukv_bf.shape[1]

    def epi(accs, e_refs, o_refs, rows):
        acc = accs[0]
        ckv = _rms(acc[:, :KV_RANK], e_refs[0][...]).astype(BF16)
        o_refs[0][rows, :] = jnp.dot(ckv, e_refs[1][...], preferred_element_type=F32).astype(BF16)
        prod = acc[:, KV_RANK:] * e_refs[2][rows, :]
        o_refs[1][rows, :] = (prod + pltpu.roll(prod, QK_ROPE, 1))[:, :QK_ROPE].astype(BF16)

    t = xn.shape[0]
    pos_tiles, lat_tiles = seq // tm, lat_rows // tm
    extras = [(kv_norm.reshape(1, KV_RANK), pl.BlockSpec((1, KV_RANK), lambda j, i: (0, 0))),
              (w_ukv_bf, pl.BlockSpec(w_ukv_bf.shape, lambda j, i: (0, 0))),
              (cos_sin_k, pl.BlockSpec((tm, 2 * QK_ROPE),
                                       lambda j, i: (jnp.where(i < lat_tiles, i % pos_tiles, pos_tiles), 0)))]
    outs = [(jax.ShapeDtypeStruct((t, nkv), BF16), pl.BlockSpec((tm, nkv), lambda j, i: (i, 0))),
            (jax.ShapeDtypeStruct((t, QK_ROPE), BF16), pl.BlockSpec((tm, QK_ROPE), lambda j, i: (i, 0)))]
    return _fused_mm(xn, [(w_kv_t, 0, "nk", w_kv_t.shape[0])], epi, extras, outs, tm=tm, nj=1, name="kv_path",
                     fuse_inputs=(3,))


ATTN_SUB_ROWS = 256


def _attn_kernel(qn_ref, qr_ref, kvl_ref, kvc_ref, krl_ref, krc_ref, o_ref, k_scr, v_scr, *, seq, ctx):
    dk = QK_NOPE + QK_ROPE

    @pl.when(pl.program_id(2) == 0)
    def _():
        for h in range(2):
            base = h * (QK_NOPE + V_DIM)
            k_scr[h, :seq, :QK_NOPE] = kvl_ref[:, base:base + QK_NOPE]
            k_scr[h, seq:, :QK_NOPE] = kvc_ref[:, base:base + QK_NOPE]
            k_scr[h, :seq, QK_NOPE:dk] = krl_ref[...]
            k_scr[h, seq:, QK_NOPE:dk] = krc_ref[...]
            v_scr[h, :seq, :V_DIM] = kvl_ref[:, base + QK_NOPE:base + QK_NOPE + V_DIM]
            v_scr[h, seq:, :V_DIM] = kvc_ref[:, base + QK_NOPE:base + QK_NOPE + V_DIM]
            ones_col = lax.broadcasted_iota(jnp.int32, (seq + ctx, V_DIM), 1) == 0
            v_scr[h, :, V_DIM:] = jnp.where(ones_col, 1.0, 0.0).astype(BF16)

    for r0 in range(0, qn_ref.shape[0], ATTN_SUB_ROWS):
        rows = slice(r0, r0 + ATTN_SUB_ROWS)
        for h in range(2):
            q = jnp.concatenate([qn_ref[rows, h * QK_NOPE:(h + 1) * QK_NOPE],
                                 qr_ref[rows, h * QK_ROPE:(h + 1) * QK_ROPE]], axis=1)
            s = lax.dot_general(q, k_scr[h], (((1,), (1,)), ((), ())), preferred_element_type=F32)
            m = jnp.max(s, axis=-1, keepdims=True)
            p = jnp.exp((s - m).astype(BF16))
            ol = jnp.dot(p, v_scr[h], preferred_element_type=F32)
            o_ref[rows, h * V_DIM:(h + 1) * V_DIM] = (ol[:, :V_DIM] / ol[:, V_DIM:V_DIM + 1]).astype(o_ref.dtype)


def _attention(q, kv, kr, batch, seq, ctx, tq):
    nq = seq // tq
    ctx0 = batch * seq // ctx
    nn_blocks = MLA_HEADS * QK_NOPE // (2 * QK_NOPE)
    dk = QK_NOPE + QK_ROPE
    hw = 2 * (QK_NOPE + V_DIM)
    return pl.pallas_call(
        functools.partial(_attn_kernel, seq=seq, ctx=ctx),
        grid=(batch, MLA_HEADS // 2, nq),
        in_specs=[pl.BlockSpec((tq, 2 * QK_NOPE), lambda b, hp, qi: (b * nq + qi, hp)),
                  pl.BlockSpec((tq, 2 * QK_ROPE), lambda b, hp, qi: (b * nq + qi, 2 * nn_blocks + hp)),
                  pl.BlockSpec((seq, hw), lambda b, hp, qi: (b, hp)),
                  pl.BlockSpec((ctx, hw), lambda b, hp, qi: (ctx0 + b, hp)),
                  pl.BlockSpec((seq, QK_ROPE), lambda b, hp, qi: (b, 0)),
                  pl.BlockSpec((ctx, QK_ROPE), lambda b, hp, qi: (ctx0 + b, 0))],
        out_specs=pl.BlockSpec((tq, 2 * V_DIM), lambda b, hp, qi: (b * nq + qi, hp)),
        out_shape=jax.ShapeDtypeStruct((batch * seq, MLA_HEADS * V_DIM), BF16),
        scratch_shapes=[pltpu.VMEM((2, seq + ctx, dk), BF16), pltpu.VMEM((2, seq + ctx, 2 * V_DIM), BF16)],
        compiler_params=_cparams(3),
        name="attention",
    )(q, q, kv, kv, kr, kr)


def _s5_prep_pair(pp, are_ref, aim_ref, ldt_ref, bre_ref, bim_ref, cre_ref, cim_ref,
                  toep_ref, wsr_ref, wsi_ref, wor_ref, woi_ref, atr_ref, ati_ref):
    tc, g = S5_CHUNK, S5_GROUP
    w = tc * g
    lane = lax.broadcasted_iota(jnp.int32, (g, 2 * S5_STATE), 1)
    in_group = (lane < S5_STATE, lane >= S5_STATE)
    lane_w = lax.broadcasted_iota(jnp.int32, (g, w), 1)
    nt = (((1,), (1,)), ((), ()))
    toep_rows = [[jnp.zeros((g, w), F32) for _ in range(tc)] for _ in range(2)]
    for d in range(2):
        lr, li = are_ref[d, pp], aim_ref[d, pp]
        dt = jnp.exp(ldt_ref[d, pp])
        mag = jnp.exp(lr * dt)
        ab_re, ab_im = mag * jnp.cos(li * dt), mag * jnp.sin(li * dt)
        den = lr * lr + li * li
        nr, ni = ab_re - 1.0, ab_im
        co_re = (nr * lr + ni * li) / den
        co_im = (ni * lr - nr * li) / den
        br, bi = bre_ref[d, pp], bim_ref[d, pp]
        bb_re = co_re * br - co_im * bi
        bb_im = co_re * bi + co_im * br
        pw = [(jnp.ones_like(ab_re), jnp.zeros_like(ab_re))]
        for _ in range(tc):
            pr, pi = pw[-1]
            pw.append((pr * ab_re - pi * ab_im, pr * ab_im + pi * ab_re))
        cr, ci = cre_ref[d, pp], cim_ref[d, pp]
        ca = [(cr * pr - ci * pi, cr * pi + ci * pr) for pr, pi in pw]
        taus = list(range(tc))[::-1] if d else list(range(tc))
        y_re = jnp.concatenate([ca[t][0] for t in taus], axis=0).astype(BF16)
        y_im = jnp.concatenate([ca[t][1] for t in taus], axis=0).astype(BF16)
        for j in range(2):
            x_re = jnp.where(in_group[j], bb_re, 0.0)
            x_im = jnp.where(in_group[j], bb_im, 0.0)
            kt = (lax.dot_general(x_re.astype(BF16), y_re, nt, preferred_element_type=F32)
                  - lax.dot_general(x_im.astype(BF16), y_im, nt, preferred_element_type=F32))
            for r in range(tc):
                sh = (r + 1) * g if d else r * g
                blk = pltpu.roll(kt, sh % w, 1) if sh % w else kt
                keep = (lane_w < sh) if d else (lane_w >= sh)
                toep_rows[j][r] = toep_rows[j][r] + jnp.where(keep, blk, 0.0)
        for r in range(tc):
            pr, pi = pw[r] if d else pw[tc - 1 - r]
            w_re = bb_re * pr - bb_im * pi
            w_im = bb_re * pi + bb_im * pr
            car, cai = ca[tc - r] if d else ca[r + 1]
            for j in range(2):
                rows = slice(j * w + r * g, j * w + (r + 1) * g)
                wsr_ref[d, pp, rows, :] = jnp.where(in_group[j], w_re, 0.0).astype(BF16)
                wsi_ref[d, pp, rows, :] = jnp.where(in_group[j], w_im, 0.0).astype(BF16)
                wor_ref[d, pp, rows, :] = jnp.where(in_group[j], car, 0.0).astype(BF16)
                woi_ref[d, pp, rows, :] = jnp.where(in_group[j], -cai, 0.0).astype(BF16)
        atr_ref[d, pp] = pw[tc][0]
        ati_ref[d, pp] = pw[tc][1]
    for j in range(2):
        toep_ref[pp, j] = jnp.concatenate(toep_rows[j], axis=0).astype(BF16)


S5_PREP_PAIRS = 8


def _s5_prep_kernel(*refs):
    for pp in range(S5_PREP_PAIRS):
        _s5_prep_pair(pp, *refs)


def _s5_prep(params):
    tc = S5_CHUNK
    wide = 2 * tc * S5_GROUP
    sl = 2 * S5_STATE

    def pair_lanes(v):
        return v.astype(F32).reshape(2, S5_PAIRS, 1, sl)

    def pair_rows(v):
        rows = v.shape[2]
        return v.astype(F32).reshape(2, S5_PAIRS, 2, rows, S5_STATE).transpose(0, 1, 3, 2, 4).reshape(
            2, S5_PAIRS, rows, sl)

    ldt = jnp.broadcast_to(params['s5_log_dt'].astype(F32)[:, :, None], (2, S5_GROUPS, S5_STATE))
    ins = [pair_lanes(params['s5_a_re']), pair_lanes(params['s5_a_im']), pair_lanes(ldt),
           pair_rows(params['s5_b_re'].transpose(0, 1, 3, 2)), pair_rows(params['s5_b_im'].transpose(0, 1, 3, 2)),
           pair_rows(params['s5_c_re']), pair_rows(params['s5_c_im'])]
    pp = S5_PREP_PAIRS
    vec_spec = pl.BlockSpec((2, pp, 1, sl), lambda k: (0, k, 0, 0))
    mat_spec = pl.BlockSpec((2, pp, S5_GROUP, sl), lambda k: (0, k, 0, 0))
    w_spec = pl.BlockSpec((2, pp, wide, sl), lambda k: (0, k, 0, 0))
    w_sds = jax.ShapeDtypeStruct((2, S5_PAIRS, wide, sl), BF16)
    a_sds = jax.ShapeDtypeStruct((2, S5_PAIRS, 1, sl), F32)
    return pl.pallas_call(
        _s5_prep_kernel,
        grid=(S5_PAIRS // pp,),
        in_specs=[vec_spec] * 3 + [mat_spec] * 4,
        out_specs=[pl.BlockSpec((pp, 2, wide // 2, wide // 2), lambda k: (k, 0, 0, 0))] + [w_spec] * 4 + [vec_spec] * 2,
        out_shape=[jax.ShapeDtypeStruct((S5_PAIRS, 2, wide // 2, wide // 2), BF16)] + [w_sds] * 4 + [a_sds] * 2,
        compiler_params=_cparams(1),
        name="s5_prep",
    )(*ins)


S5_STEP_PAIRS = 4
def _s5_state_rows(batch, n_lat, n_ctx):
    def pitch(n):
        p = -(-n // SUBLANES)
        return SUBLANES * (p + 1 - p % 2)

    lat_pitch, ctx_pitch = pitch(n_lat), pitch(n_ctx)
    ctx_base = batch * lat_pitch
    return lat_pitch, ctx_pitch, ctx_base, ctx_base + batch * ctx_pitch


def _s5_kernel(ul_ref, uc_ref, dsk_ref, *refs, batch, seq, ctx):
    toep_ref, wsr, wsi, wor, woi, atr, ati = refs[:7]
    o_ref = refs[7]
    wscr, zscr, upscr, sre, sim, hre, him = refs[8:]
    tc = S5_CHUNK
    n_lat, n_ctx = seq // tc, ctx // tc
    rows_lat = n_lat * batch
    lat_pitch, ctx_pitch, ctx_base, _ = _s5_state_rows(batch, n_lat, n_ctx)
    w = tc * S5_GROUP
    lanes = S5_STEP_PAIRS * 2 * S5_GROUP

    def scatter_tiles(xt, r, col0):
        for kk in range(S5_STEP_PAIRS):
            for j in range(2):
                ch = (2 * kk + j) * S5_GROUP
                wscr[kk, j * w + r * S5_GROUP:j * w + (r + 1) * S5_GROUP, col0:col0 + lanes] = xt[ch:ch + S5_GROUP, :]

    for r in range(tc):
        for b in range(batch):
            x = ul_ref[pl.ds(b * seq + r, n_lat, stride=tc), :]
            scatter_tiles(x.T, r, b * n_lat)
        xc = jnp.concatenate([uc_ref[pl.ds(b * ctx + r, n_ctx, stride=tc), :] for b in range(batch)]
                             + [jnp.zeros((lanes - batch * n_ctx, lanes), F32)], axis=0)
        scatter_tiles(xc.T, r, rows_lat)

    chains = [(kk, d) for kk in range(S5_STEP_PAIRS) for d in range(2)]
    for kk in range(S5_STEP_PAIRS):
        upscr[kk] = wscr[kk].T.astype(BF16)
        for d in range(2):
            for dst, wst in ((sre, wsr), (sim, wsi)):
                s = jnp.dot(upscr[kk], wst[d, kk], preferred_element_type=F32)
                for b in range(batch):
                    dst[kk, d, b * lat_pitch:b * lat_pitch + n_lat, :] = s[b * n_lat:(b + 1) * n_lat]
                    dst[kk, d, ctx_base + b * ctx_pitch:ctx_base + b * ctx_pitch + n_ctx, :] = (
                        s[rows_lat + b * n_ctx:rows_lat + (b + 1) * n_ctx])

    ctx_rows = [pl.ds(ctx_base + c, batch, stride=ctx_pitch) for c in range(n_ctx)]
    lat_rows = [pl.ds(c, batch, stride=lat_pitch) for c in range(n_lat)]
    order = (ctx_rows + lat_rows, ctx_rows[::-1] + lat_rows[::-1])
    coef = {(kk, d): (atr[d, kk], ati[d, kk]) for kk, d in chains}
    state = {ch: (jnp.zeros((batch, 2 * S5_STATE), F32), jnp.zeros((batch, 2 * S5_STATE), F32)) for ch in chains}
    for t in range(n_ctx + n_lat):
        for kk, d in chains:
            rows = order[d][t]
            (ar, ai), (h_re, h_im) = coef[kk, d], state[kk, d]
            hre[kk, d, rows, :] = h_re
            him[kk, d, rows, :] = h_im
            state[kk, d] = (ar * h_re - ai * h_im + sre[kk, d, rows, :],
                            ar * h_im + ai * h_re + sim[kk, d, rows, :])

    for kk in range(S5_STEP_PAIRS):
        ul = upscr[kk, :rows_lat, :]
        y = ul.astype(F32) * dsk_ref[kk]
        y = y + jnp.concatenate(
            [jnp.dot(ul[:, :w], toep_ref[kk, 0], preferred_element_type=F32),
             jnp.dot(ul[:, w:], toep_ref[kk, 1], preferred_element_type=F32)], axis=1)
        nt = (((1,), (1,)), ((), ()))
        for d in range(2):
            h_r = jnp.concatenate([hre[kk, d, b * lat_pitch:b * lat_pitch + n_lat, :] for b in range(batch)], axis=0)
            h_i = jnp.concatenate([him[kk, d, b * lat_pitch:b * lat_pitch + n_lat, :] for b in range(batch)], axis=0)
            y = y + lax.dot_general(h_r.astype(BF16), wor[d, kk], nt, preferred_element_type=F32)
            y = y + lax.dot_general(h_i.astype(BF16), woi[d, kk], nt, preferred_element_type=F32)
        yt = y.T
        for b in range(batch):
            for s in range(tc):
                for j in range(2):
                    ch = (2 * kk + j) * S5_GROUP
                    zscr[b * tc + s, ch:ch + S5_GROUP, :] = yt[j * w + s * S5_GROUP:j * w + (s + 1) * S5_GROUP,
                                                               b * n_lat:(b + 1) * n_lat]

    for b in range(batch):
        for s in range(tc):
            o_ref[pl.ds(b * seq + s, n_lat, stride=tc), :] = jax.nn.gelu(zscr[b * tc + s].T)


def _s5_mixer(u, params, batch, seq, ctx):
    tc = S5_CHUNK
    n_lat, n_ctx = seq // tc, ctx // tc
    sp = S5_STEP_PAIRS
    lanes = sp * 2 * S5_GROUP
    assert n_lat == lanes and batch * n_ctx <= lanes
    rows_lat = n_lat * batch
    rows_all = rows_lat + lanes
    state_rows = _s5_state_rows(batch, n_lat, n_ctx)[3]
    wide = 2 * tc * S5_GROUP
    dsk = jnp.broadcast_to(params['s5_d'].astype(F32).reshape(S5_PAIRS, 2, 1, S5_GROUP),
                           (S5_PAIRS, 2, tc, S5_GROUP)).reshape(S5_PAIRS, 1, wide)
    weights = _s5_prep(params)
    w_spec = pl.BlockSpec((2, sp, wide, 2 * S5_STATE), lambda k: (0, k, 0, 0))
    a_spec = pl.BlockSpec((2, sp, 1, 2 * S5_STATE), lambda k: (0, k, 0, 0))
    specs = [pl.BlockSpec((sp, 2, wide // 2, wide // 2), lambda k: (k, 0, 0, 0))] + [w_spec] * 4 + [a_spec] * 2
    return pl.pallas_call(
        functools.partial(_s5_kernel, batch=batch, seq=seq, ctx=ctx),
        grid=(S5_PAIRS // sp,),
        in_specs=[pl.BlockSpec((batch * seq, lanes), lambda k: (0, k)),
                  pl.BlockSpec((batch * ctx, lanes), lambda k: (seq // ctx, k)),
                  pl.BlockSpec((sp, 1, wide), lambda k: (k, 0, 0))] + specs,
        out_specs=pl.BlockSpec((batch * seq, lanes), lambda k: (0, k)),
        out_shape=jax.ShapeDtypeStruct((batch * seq, S5_WIDTH), F32),
        scratch_shapes=[pltpu.VMEM((sp, wide, rows_all), F32),
                        pltpu.VMEM((batch * tc, lanes, n_lat), F32),
                        pltpu.VMEM((sp, rows_all, wide), BF16)]
                       + [pltpu.VMEM((sp, 2, state_rows, 2 * S5_STATE), F32) for _ in range(4)],
        compiler_params=_cparams(1),
        name="s5",
    )(u, u, dsk, *weights)


def _merge_kernel(z5_ref, o_ref_in, gs_ref, gm_ref, wa_ref, wb_ref, wm_ref, out_ref, wa_s, wb_s, wm_s):
    @pl.when(pl.program_id(1) == 0)
    def _():
        wa_s[...] = wa_ref[...].astype(BF16)
        wb_s[...] = wb_ref[...].astype(BF16)
        wm_s[...] = wm_ref[...].astype(BF16)

    sub = min(MERGE_SUB_ROWS, out_ref.shape[0])
    for r0 in range(0, out_ref.shape[0], sub):
        rows = slice(r0, r0 + sub)
        z = z5_ref[rows, :].astype(BF16)
        a = jnp.dot(z, wa_s[...], preferred_element_type=F32)
        b = jnp.dot(z, wb_s[...], preferred_element_type=F32)
        mla = jnp.dot(o_ref_in[rows, :], wm_s[...], preferred_element_type=F32)
        merged = gs_ref[rows, :].astype(F32) * (a * jax.nn.sigmoid(b)) + gm_ref[rows, :].astype(F32) * mla
        out_ref[rows, :] = merged.astype(out_ref.dtype)


def _merge(z5, o_mla, gates, w_glu, w_mla_o, tm, tn):
    t = z5.shape[0]
    nj = D_MODEL // tn
    return pl.pallas_call(
        _merge_kernel,
        grid=(nj, t // tm),
        in_specs=[pl.BlockSpec((tm, S5_WIDTH), lambda j, i: (i, 0)),
                  pl.BlockSpec((tm, MLA_HEADS * V_DIM), lambda j, i: (i, 0)),
                  pl.BlockSpec((tm, tn), lambda j, i: (i, j)),
                  pl.BlockSpec((tm, tn), lambda j, i: (i, nj + j)),
                  pl.BlockSpec((S5_WIDTH, tn), lambda j, i: (0, j)),
                  pl.BlockSpec((S5_WIDTH, tn), lambda j, i: (0, nj + j)),
                  pl.BlockSpec((MLA_HEADS * V_DIM, tn), lambda j, i: (0, j))],
        out_specs=pl.BlockSpec((tm, tn), lambda j, i: (i, j)),
        out_shape=jax.ShapeDtypeStruct((t, D_MODEL), BF16),
        scratch_shapes=[pltpu.VMEM((S5_WIDTH, tn), BF16), pltpu.VMEM((S5_WIDTH, tn), BF16),
                        pltpu.VMEM((MLA_HEADS * V_DIM, tn), BF16)],
        compiler_params=_cparams(2),
        name="merge",
    )(z5, o_mla, gates, gates, w_glu, w_glu, w_mla_o)


def _out_proj_norm_kernel(a_ref, w_ref, x_ref, g1_ref, n2_ref, sc_ref, sh_ref, x1_ref, xn_ref, w_s):
    @pl.when(pl.program_id(0) == 0)
    def _():
        w_s[...] = w_ref[...].astype(BF16)

    sub = min(MERGE_SUB_ROWS, x1_ref.shape[0])
    for r0 in range(0, x1_ref.shape[0], sub):
        rows = slice(r0, r0 + sub)
        x1 = x_ref[rows, :] + g1_ref[0, 0] * jnp.dot(a_ref[rows, :], w_s[...], preferred_element_type=F32)
        x1_ref[rows, :] = x1
        xn_ref[rows, :] = (_rms(x1, n2_ref[...]) * (1.0 + sc_ref[0, 0]) + sh_ref[0, 0]).astype(xn_ref.dtype)


def _out_proj_norm(merged, w_out, x_res, m, gain2, rows_per_batch, tm):
    t, k = merged.shape
    tpb = rows_per_batch // tm
    row = pl.BlockSpec((tm, D_MODEL), lambda i: (i, 0))
    gate1, scale2, shift2 = (_mod_spec(D_MODEL, lambda i: i // tpb, which)
                             for which in (MOD_GATE1, MOD_SCALE2, MOD_SHIFT2))
    return pl.pallas_call(
        _out_proj_norm_kernel,
        grid=(t // tm,),
        in_specs=[pl.BlockSpec((tm, k), lambda i: (i, 0)),
                  pl.BlockSpec((k, D_MODEL), lambda i: (0, 0), pipeline_mode=pl.Buffered(1)),
                  row, gate1, pl.BlockSpec((1, D_MODEL), lambda i: (0, 0)), scale2, shift2],
        out_specs=[row, row],
        out_shape=[jax.ShapeDtypeStruct((t, D_MODEL), F32), jax.ShapeDtypeStruct((t, D_MODEL), BF16)],
        scratch_shapes=[pltpu.VMEM((k, D_MODEL), BF16)],
        compiler_params=_cparams(1),
        name="out_proj",
    )(merged, w_out, x_res, m, gain2.reshape(1, D_MODEL), m, m)


def _ffn_in(xn, w_ffn_in, tm, tn):
    def epi(accs, e_refs, o_refs, rows):
        o_refs[0][rows, :] = (jax.nn.silu(accs[0]) * accs[1]).astype(BF16)

    t = xn.shape[0]
    nj = D_FF // tn
    outs = [(jax.ShapeDtypeStruct((t, D_FF), BF16), pl.BlockSpec((tm, tn), lambda j, i: (i, j)))]
    return _fused_mm(xn, [(w_ffn_in, 0, "kn", tn), (w_ffn_in, nj, "kn", tn)], epi, [], outs,
                     tm=tm, nj=nj, name="ffn_in", sub_rows=MM_SUB_ROWS)[0]


def _ffn_out_kernel(h_ref, w_ref, x_ref, g2_ref, nf_ref, o_ref, w_s, *, last):
    @pl.when(pl.program_id(0) == 0)
    def _():
        w_s[...] = w_ref[...].astype(BF16)

    y = x_ref[...] + g2_ref[0, 0] * jnp.dot(h_ref[...], w_s[...], preferred_element_type=F32)
    o_ref[...] = _rms(y, nf_ref[...]) if last else y


def _ffn_out(hid, w_ffn_out, x_res, m, norm_f, rows_per_batch, tm):
    t, k = hid.shape
    kh = k // 2
    tpb = rows_per_batch // tm
    row = pl.BlockSpec((tm, D_MODEL), lambda i: (i, 0))
    y = x_res
    for half in range(2):
        y = pl.pallas_call(
            functools.partial(_ffn_out_kernel, last=half == 1),
            grid=(t // tm,),
            in_specs=[pl.BlockSpec((tm, kh), functools.partial(lambda i, half: (i, half), half=half)),
                      pl.BlockSpec((kh, D_MODEL), functools.partial(lambda i, half: (half, 0), half=half),
                                   pipeline_mode=pl.Buffered(1)),
                      row, _mod_spec(D_MODEL, lambda i: i // tpb, MOD_GATE2),
                      pl.BlockSpec((1, D_MODEL), lambda i: (0, 0))],
            out_specs=row,
            out_shape=jax.ShapeDtypeStruct((t, D_MODEL), F32),
            scratch_shapes=[pltpu.VMEM((kh, D_MODEL), BF16)],
            compiler_params=_cparams(1),
            name="ffn_out",
        )(hid, w_ffn_out, y, m, norm_f.reshape(1, D_MODEL))
    return y


def _rope_rot_cols(w):
    k = w.shape[0]
    ws = w.reshape(k, -1, 2, 2, QK_ROPE // 4)
    return jnp.stack([-ws[:, :, :, 1, :], ws[:, :, :, 0, :]], axis=3).reshape(k, -1)


def _rope_tables(n_tokens):
    rows = n_tokens // GRID_W
    row = np.repeat(np.arange(rows, dtype=np.float32), GRID_W)
    col = np.tile(np.arange(GRID_W, dtype=np.float32), rows)
    n_freq = QK_ROPE // 4
    inv = (np.float32(ROPE_BASE) ** (-np.arange(n_freq, dtype=np.float32) / np.float32(n_freq))).astype(np.float32)
    ang = np.stack([row[:, None] * inv, col[:, None] * inv], axis=1)
    cos = np.broadcast_to(np.cos(ang)[:, :, None, :], (n_tokens, 2, 2, n_freq)).reshape(n_tokens, QK_ROPE)
    sin = np.broadcast_to(np.sin(ang)[:, :, None, :], (n_tokens, 2, 2, n_freq)).reshape(n_tokens, QK_ROPE)
    return cos.astype(np.float32), sin.astype(np.float32)


def kernel(x, c, ctx, c_ctx, w_mod, b_mod, norm1, norm2, w_in, s5_a_re, s5_a_im, s5_log_dt, s5_b_re, s5_b_im,
           s5_c_re, s5_c_im, s5_d, w_glu, q_norm, kv_norm, w_uq, w_ukv, w_mla_o, w_out, w_ffn_in, w_ffn_out,
           norm_f):
    batch, seq, _ = x.shape
    n_ctx = ctx.shape[1]
    assert w_mod.shape[0] == 1, "single-layer block"
    p = dict(s5_a_re=s5_a_re[0], s5_a_im=s5_a_im[0], s5_log_dt=s5_log_dt[0], s5_b_re=s5_b_re[0],
             s5_b_im=s5_b_im[0], s5_c_re=s5_c_re[0], s5_c_im=s5_c_im[0], s5_d=s5_d[0])
    w_in_t = w_in.reshape(w_in.shape[1:]).T

    cv = jnp.concatenate([c, c_ctx[None], jnp.zeros((8 - batch - 1, D_MODEL), F32)], axis=0)
    m = _modulation(cv, w_mod[0], b_mod[0]).reshape(8, 6, 1, D_MODEL)

    lat_rows = batch * seq
    x2d = x.reshape(lat_rows, D_MODEL)
    c2d = ctx.reshape(batch * n_ctx, D_MODEL)
    xn, u = _norm_u_proj(x2d, c2d, norm1[0], m, w_in_t, seq, batch, 512)

    kv_lo = S5_WIDTH + Q_RANK
    w_kv_rows = w_in_t[kv_lo:kv_lo + KV_RANK + QK_ROPE]
    w_kv_t = jnp.concatenate([w_kv_rows, _rope_rot_cols(w_kv_rows[KV_RANK:].T).T], axis=0)
    wq = w_uq[0].reshape(Q_RANK, MLA_HEADS, QK_NOPE + QK_ROPE)
    wq_rope = wq[:, :, QK_NOPE:].reshape(Q_RANK, MLA_HEADS * QK_ROPE)
    wq2 = jnp.concatenate([wq[:, :, :QK_NOPE].reshape(Q_RANK, MLA_HEADS * QK_NOPE), wq_rope,
                           _rope_rot_cols(wq_rope)], axis=1).astype(BF16)
    w_ukv_bf = w_ukv[0].astype(BF16)
    tm = 1024
    cos, sin = _rope_tables(seq)
    identity = np.concatenate([np.ones((tm, QK_ROPE), np.float32), np.zeros((tm, QK_ROPE), np.float32)], axis=1)
    cos_sin_k = jnp.asarray(np.concatenate([np.concatenate([cos, sin], axis=1), identity], axis=0))
    cos2, sin2 = jnp.asarray(np.tile(cos, (1, 2))), jnp.asarray(np.tile(sin, (1, 2)))

    q = _q_path(xn, w_in_t, q_norm[0], wq2, cos2, sin2, tm, seq, lat_rows)
    kv, kr = _kv_path(xn, w_kv_t, kv_norm[0], w_ukv_bf, cos_sin_k, tm, seq, lat_rows)
    gates = _gates(xn, w_in_t, kv_lo + KV_RANK + QK_ROPE, tm, 1024, lat_rows)

    z5 = _s5_mixer(u, p, batch, seq, n_ctx)
    o_mla = _attention(q, kv, kr, batch, seq, n_ctx, 2048)

    merged = _merge(z5, o_mla, gates, w_glu[0], w_mla_o[0], 512, 1024)
    x1, xn2 = _out_proj_norm(merged, w_out[0], x2d, m, norm2[0], seq, 512)
    hid = _ffn_in(xn2, w_ffn_in[0], 2048, 512)
    return _ffn_out(hid, w_ffn_out[0], x1, m, norm_f, seq, 256).reshape(batch, seq, D_MODEL)
```
